```python
import jax, jax.numpy as jnp
from jax import lax
import numpy as np

D_MODEL = 1024
BATCH = 8
SEQ = 8192
DEPTH = 1
DEC_BATCH = 8
DEC_SEQ = 4096
PAST_LEN = 128

HEAD_DIM = 64
ATT_WINDOWS = ((128, 1), (512, 4), (2048, 16))
ATT_HEADS_PER_GROUP = 4
N_ATT_HEADS = ATT_HEADS_PER_GROUP * len(ATT_WINDOWS)
ATT_WIDTH = N_ATT_HEADS * HEAD_DIM
ATT_OUT_WIDTH = ATT_HEADS_PER_GROUP * HEAD_DIM
SGU_WIDTH = D_MODEL // 4
SGU_GROUP_DIM = 64
SGU_GROUPS = SGU_WIDTH // SGU_GROUP_DIM
SGU_CHUNK = 128
IN_WIDTH = 3 * ATT_WIDTH + 2 * SGU_WIDTH
MIX_OUT_WIDTH = ATT_OUT_WIDTH + SGU_WIDTH
ROT_DIM = HEAD_DIM // 4
ROPE_THETA = 500000.0
N_EXPERTS = 32
TOP_K = 4
D_FF = D_MODEL
SWIGLU_LIMIT = 7.0
SWIGLU_ALPHA = 1.702
MOE_BLOCK = 128
NORM_EPS = 1e-6
LN_EPS = 1e-5
NEG_INF = -1e30

kernel_name = "hybrid_sgu_dilated_moe_encoder"


def rms_norm(x, w):
    xf = x.astype(jnp.float32)
    y = xf * lax.rsqrt(jnp.mean(xf * xf, axis=-1, keepdims=True) + NORM_EPS)
    return (y * w.astype(jnp.float32)).astype(x.dtype)


def layer_norm(x, w, b):
    xf = x.astype(jnp.float32)
    mu = jnp.mean(xf, axis=-1, keepdims=True)
    var = jnp.mean(jnp.square(xf - mu), axis=-1, keepdims=True)
    y = (xf - mu) * lax.rsqrt(var + LN_EPS) * w.astype(jnp.float32) + b.astype(jnp.float32)
    return y.astype(x.dtype)


def partial_rope(t, pos):
    half = ROT_DIM // 2
    inv_freq = jnp.power(ROPE_THETA, -2.0 * jnp.arange(half, dtype=jnp.float32) / ROT_DIM)
    ang = pos[:, None] * inv_freq[None, :]
    cos = jnp.cos(ang)[None, :, None, :]
    sin = jnp.sin(ang)[None, :, None, :]
    tf = t.astype(jnp.float32)
    x1 = tf[..., :half]
    x2 = tf[..., half:ROT_DIM]
    out = jnp.concatenate([x1 * cos - x2 * sin, x2 * cos + x1 * sin, tf[..., ROT_DIM:]], axis=-1)
    return out.astype(t.dtype)


def dilated_window_attention(q, k, v, dil, steps):
    B, S, H, Dh = q.shape
    L = S // dil
    nblk = -(-L // steps)
    Lp = nblk * steps

    def strided(t):
        t = t.astype(jnp.float32).reshape(B, L, dil, H, Dh)
        return jnp.pad(t, ((0, 0), (0, Lp - L), (0, 0), (0, 0), (0, 0)))

    def banded(t):
        tp = jnp.pad(strided(t), ((0, 0), (steps, steps), (0, 0), (0, 0), (0, 0)))
        tp = tp.reshape(B, nblk + 2, steps, dil, H, Dh)
        return jnp.concatenate([tp[:, :-2], tp[:, 1:-1], tp[:, 2:]], axis=2)

    qb = strided(q).reshape(B, nblk, steps, dil, H, Dh)
    kb = banded(k)
    vb = banded(v)
    s = jnp.einsum('bnqrhd,bnkrhd->bnrhqk', qb, kb) * (Dh ** -0.5)
    qi = jnp.arange(steps)
    ki = jnp.arange(3 * steps)
    rel = ki[None, :] - steps - qi[:, None]
    band = jnp.abs(rel) <= steps
    kidx = jnp.arange(nblk)[:, None] * steps + ki[None, :] - steps
    valid = (kidx >= 0) & (kidx < L)
    mask = band[None, :, :] & valid[:, None, :]
    s = jnp.where(mask[None, :, None, None], s, NEG_INF)
    m = jnp.max(s, axis=-1, keepdims=True)
    p = jnp.exp(s - m)
    den = jnp.sum(p, axis=-1)
    o = jnp.einsum('bnrhqk,bnkrhd->bnqrhd', p, vb) / jnp.transpose(den, (0, 1, 4, 2, 3))[..., None]
    lse = jnp.transpose(m[..., 0] + jnp.log(den), (0, 1, 4, 2, 3))
    o = o.reshape(B, Lp, dil, H, Dh)[:, :L].reshape(B, S, H, Dh)
    lse = lse.reshape(B, Lp, dil, H)[:, :L].reshape(B, S, H)
    return o, lse


def moe_ffn(h, router_w, router_b, w_gate, b_gate, w_up, b_up, w_down, b_down):
    T, D = h.shape
    logits = jnp.einsum('td,de->te', h, router_w).astype(jnp.float32) + router_b.astype(jnp.float32)
    top_v, top_i = lax.top_k(logits, TOP_K)
    gates = jax.nn.softmax(top_v, axis=-1)
    n_assign = T * TOP_K
    flat_e = top_i.reshape(-1)
    flat_w = gates.reshape(-1)
    flat_tok = jnp.arange(n_assign, dtype=jnp.int32) // TOP_K
    order = jnp.argsort(flat_e)
    se = flat_e[order]
    counts = jnp.bincount(flat_e, length=N_EXPERTS)
    starts = jnp.cumsum(counts) - counts
    pcounts = (counts + MOE_BLOCK - 1) // MOE_BLOCK * MOE_BLOCK
    pends = jnp.cumsum(pcounts)
    pstarts = pends - pcounts
    dest = pstarts[se] + (jnp.arange(n_assign, dtype=jnp.int32) - starts[se])
    P = -(-(n_assign + N_EXPERTS * (MOE_BLOCK - 1)) // MOE_BLOCK) * MOE_BLOCK
    nb = P // MOE_BLOCK
    row_tok = jnp.zeros((P,), jnp.int32).at[dest].set(flat_tok[order])
    row_w = jnp.zeros((P,), jnp.float32).at[dest].set(flat_w[order])
    blk_e = jnp.minimum(jnp.searchsorted(pends, jnp.arange(nb) * MOE_BLOCK, side='right'), N_EXPERTS - 1)

    def body(y, xs):
        e, tok, w = xs
        xe = h[tok]
        g = jnp.dot(xe, w_gate[e]) + b_gate[e]
        u = jnp.dot(xe, w_up[e]) + b_up[e]
        g = jnp.minimum(g, SWIGLU_LIMIT)
        u = jnp.clip(u, -SWIGLU_LIMIT, SWIGLU_LIMIT)
        act = (u + 1.0) * (g * jax.nn.sigmoid(SWIGLU_ALPHA * g))
        o = jnp.dot(act, w_down[e]) + b_down[e]
        y = y.at[tok].add(o.astype(jnp.float32) * w[:, None])
        return y, None

    y0 = jnp.zeros((T, D), jnp.float32)
    y, _ = lax.scan(body, y0, (blk_e, row_tok.reshape(nb, MOE_BLOCK), row_w.reshape(nb, MOE_BLOCK)))
    return y.astype(h.dtype)


def encoder_layer(x, attn_norm_w, w_in, q_norm_w, k_norm_w, sgu_ln_w, sgu_ln_b, sgu_w, sgu_b,
                  att_out_norm_w, sgu_out_norm_w, w_out, ffn_norm_w, router_w, router_b,
                  w_gate, b_gate, w_up, b_up, w_down, b_down):
    B, S, D = x.shape
    h = rms_norm(x, attn_norm_w)
    z = jnp.einsum('bsd,dn->bsn', h, w_in)
    q, k, v, zu, zv = jnp.split(z, [ATT_WIDTH, 2 * ATT_WIDTH, 3 * ATT_WIDTH, 3 * ATT_WIDTH + SGU_WIDTH], axis=-1)
    pos = jnp.arange(S, dtype=jnp.float32)
    q = partial_rope(rms_norm(q.reshape(B, S, N_ATT_HEADS, HEAD_DIM), q_norm_w), pos)
    k = partial_rope(rms_norm(k.reshape(B, S, N_ATT_HEADS, HEAD_DIM), k_norm_w), pos)
    v = v.reshape(B, S, N_ATT_HEADS, HEAD_DIM)
    outs = []
    lses = []
    for g, (window, dil) in enumerate(ATT_WINDOWS):
        sl = slice(g * ATT_HEADS_PER_GROUP, (g + 1) * ATT_HEADS_PER_GROUP)
        o_g, lse_g = dilated_window_attention(q[:, :, sl], k[:, :, sl], v[:, :, sl], dil, window // (2 * dil))
        outs.append(o_g)
        lses.append(lse_g)
    o = jnp.stack(outs, axis=0)
    mix_w = jax.nn.softmax(jnp.stack(lses, axis=0), axis=0)
    att = jnp.sum(mix_w[..., None] * o, axis=0).reshape(B, S, ATT_OUT_WIDTH).astype(x.dtype)
    zu = jax.nn.gelu(zu, approximate=False)
    zv = layer_norm(jax.nn.gelu(zv, approximate=False), sgu_ln_w, sgu_ln_b)
    vc = zv.reshape(B, S // SGU_CHUNK, SGU_CHUNK, SGU_GROUPS, SGU_GROUP_DIM)
    gate = jnp.einsum('gpq,bcqgd->bcpgd', sgu_w, vc) + jnp.transpose(sgu_b)[:, :, None]
    sgu = zu * gate.reshape(B, S, SGU_WIDTH).astype(zu.dtype)
    mixed = jnp.concatenate([rms_norm(att, att_out_norm_w), rms_norm(sgu, sgu_out_norm_w)], axis=-1)
    x = x + jnp.einsum('bsm,md->bsd', mixed, w_out)
    h2 = rms_norm(x, ffn_norm_w).reshape(B * S, D)
    y = moe_ffn(h2, router_w, router_b, w_gate, b_gate, w_up, b_up, w_down, b_down).reshape(B, S, D)
    return x + y


def trunk(x, attn_norm_w, w_in, q_norm_w, k_norm_w, sgu_ln_w, sgu_ln_b, sgu_w, sgu_b,
          att_out_norm_w, sgu_out_norm_w, w_out, ffn_norm_w, router_w, router_b,
          w_gate, b_gate, w_up, b_up, w_down, b_down):
    for l in range(DEPTH):
        x = encoder_layer(x, attn_norm_w[l], w_in[l], q_norm_w[l], k_norm_w[l], sgu_ln_w[l], sgu_ln_b[l],
                          sgu_w[l], sgu_b[l], att_out_norm_w[l], sgu_out_norm_w[l], w_out[l], ffn_norm_w[l],
                          router_w[l], router_b[l], w_gate[l], b_gate[l], w_up[l], b_up[l], w_down[l], b_down[l])
    return x


def setup_inputs(seed: int = 0) -> dict:
    key = jax.random.key(seed)
    ks = jax.random.split(key, 22)

    def nrm(k, shape, scale):
        return jax.random.normal(k, shape, jnp.float32) * scale

    def gain(k, shape):
        return 1.0 + 0.05 * jax.random.normal(k, shape, jnp.float32)

    return {
        "x_prompt": nrm(ks[0], (BATCH, SEQ, D_MODEL), 1.0),
        "x_sample": nrm(ks[1], (DEC_BATCH, DEC_SEQ, D_MODEL), 1.0),
        "attn_norm_w": gain(ks[2], (DEPTH, D_MODEL)),
        "w_in": nrm(ks[3], (DEPTH, D_MODEL, IN_WIDTH), D_MODEL ** -0.5),
        "q_norm_w": gain(ks[4], (DEPTH, HEAD_DIM)),
        "k_norm_w": gain(ks[5], (DEPTH, HEAD_DIM)),
        "sgu_ln_w": gain(ks[6], (DEPTH, SGU_WIDTH)),
        "sgu_ln_b": nrm(ks[7], (DEPTH, SGU_WIDTH), 0.02),
        "sgu_w": nrm(ks[8], (DEPTH, SGU_GROUPS, SGU_CHUNK, SGU_CHUNK), SGU_CHUNK ** -0.5),
        "sgu_b": 1.0 + nrm(ks[9], (DEPTH, SGU_GROUPS, SGU_CHUNK), 0.1),
        "att_out_norm_w": gain(ks[10], (DEPTH, ATT_OUT_WIDTH)),
        "sgu_out_norm_w": gain(ks[11], (DEPTH, SGU_WIDTH)),
        "w_out": nrm(ks[12], (DEPTH, MIX_OUT_WIDTH, D_MODEL), MIX_OUT_WIDTH ** -0.5),
        "ffn_norm_w": gain(ks[13], (DEPTH, D_MODEL)),
        "router_w": nrm(ks[14], (DEPTH, D_MODEL, N_EXPERTS), D_MODEL ** -0.5),
        "router_b": nrm(ks[15], (DEPTH, N_EXPERTS), 0.01),
        "w_gate": nrm(ks[16], (DEPTH, N_EXPERTS, D_MODEL, D_FF), D_MODEL ** -0.5),
        "b_gate": nrm(ks[17], (DEPTH, N_EXPERTS, D_FF), 0.02),
        "w_up": nrm(ks[18], (DEPTH, N_EXPERTS, D_MODEL, D_FF), D_MODEL ** -0.5),
        "b_up": nrm(ks[19], (DEPTH, N_EXPERTS, D_FF), 0.02),
        "w_down": nrm(ks[20], (DEPTH, N_EXPERTS, D_FF, D_MODEL), D_FF ** -0.5),
        "b_down": nrm(ks[21], (DEPTH, N_EXPERTS, D_MODEL), 0.02),
    }


def reference(x_prompt, x_sample, attn_norm_w, w_in, q_norm_w, k_norm_w, sgu_ln_w, sgu_ln_b, sgu_w, sgu_b,
              att_out_norm_w, sgu_out_norm_w, w_out, ffn_norm_w, router_w, router_b,
              w_gate, b_gate, w_up, b_up, w_down, b_down):
    y_prompt = trunk(x_prompt, attn_norm_w, w_in, q_norm_w, k_norm_w, sgu_ln_w, sgu_ln_b, sgu_w, sgu_b,
                     att_out_norm_w, sgu_out_norm_w, w_out, ffn_norm_w, router_w, router_b,
                     w_gate, b_gate, w_up, b_up, w_down, b_down)
    y_sample = trunk(x_sample, attn_norm_w, w_in, q_norm_w, k_norm_w, sgu_ln_w, sgu_ln_b, sgu_w, sgu_b,
                     att_out_norm_w, sgu_out_norm_w, w_out, ffn_norm_w, router_w, router_b,
                     w_gate, b_gate, w_up, b_up, w_down, b_down)
    return (y_prompt, y_sample)
```

```python
import functools

import numpy as np
import jax
import jax.numpy as jnp
from jax import lax
from jax.experimental import pallas as pl
from jax.experimental.pallas import tpu as pltpu

F32 = jnp.float32
BF16 = jnp.bfloat16
I32 = jnp.int32

HEAD_DIM = 64
ATT_WINDOWS = ((128, 1), (512, 4), (2048, 16))
HEADS_PER_GROUP = 4
GROUP_WIDTH = HEADS_PER_GROUP * HEAD_DIM
ATT_WIDTH = len(ATT_WINDOWS) * GROUP_WIDTH
SGU_WIDTH = 256
SGU_GROUP_DIM = 64
SGU_GROUPS = SGU_WIDTH // SGU_GROUP_DIM
SGU_CHUNK = 128
ROT_DIM = HEAD_DIM // 4
ROPE_THETA = 500000.0
N_EXPERTS = 32
TOP_K = 4
SWIGLU_LIMIT = 7.0
SWIGLU_ALPHA = 1.702
NORM_EPS = 1e-6
LN_EPS = 1e-5
NEG_INF = -1e30

LANES = 128
MXU_DIM = 256
TOKEN_TILE = 512
ATT_Q_BLOCK = 256
MOE_BLOCK = 256
ROW_TILE = 256
VMEM_LIMIT = 48 * 1024 * 1024


def _dot(a, b):
    return jnp.dot(a, b, preferred_element_type=F32)


def _dot_nt(a, b, precision=None):
    return lax.dot_general(a, b, (((1,), (1,)), ((), ())), precision=precision, preferred_element_type=F32)


def _rms(x, w):
    return x * lax.rsqrt(jnp.mean(x * x, axis=-1, keepdims=True) + NORM_EPS) * w


def _gelu(x):
    return 0.5 * x * (1.0 + lax.erf(x * np.float32(np.sqrt(0.5))))


def _in_proj_kernel(x_ref, anw_ref, w_ref, qnw_ref, knw_ref, ones_ref, cos_ref, sa_ref, sb_ref, lnw_ref, lnb_ref,
                    q_ref, k_ref, v_ref, zu_ref, zv_ref):
    h = _rms(x_ref[...], anw_ref[...]).astype(BF16)
    reps = ATT_WIDTH // LANES
    cos = jnp.concatenate([cos_ref[...]] * reps, axis=1)
    sa = jnp.concatenate([sa_ref[...]] * reps, axis=1)
    sb = jnp.concatenate([sb_ref[...]] * reps, axis=1)
    ones = ones_ref[...]

    def head_norm_rope(t, nw):
        sq = t * t
        hi = sq.astype(BF16)
        lo = (sq - hi.astype(F32)).astype(BF16)
        parts = []
        for j in range(ATT_WIDTH // MXU_DIM):
            sl = slice(j * MXU_DIM, (j + 1) * MXU_DIM)
            parts.append(_dot(hi[:, sl], ones) + _dot(lo[:, sl], ones))
        ssum = jnp.concatenate(parts, axis=1)
        t = t * lax.rsqrt(ssum * (1.0 / HEAD_DIM) + NORM_EPS) * nw
        half = ROT_DIM // 2
        return t * cos + pltpu.roll(t, ATT_WIDTH - half, 1) * sa + pltpu.roll(t, half, 1) * sb

    q = head_norm_rope(_dot(h, w_ref[:, 0:ATT_WIDTH]), qnw_ref[...])
    q_ref[...] = (q * (HEAD_DIM ** -0.5)).astype(BF16)
    k = head_norm_rope(_dot(h, w_ref[:, ATT_WIDTH:2 * ATT_WIDTH]), knw_ref[...])
    k_ref[...] = k.astype(BF16)
    v_ref[...] = _dot(h, w_ref[:, 2 * ATT_WIDTH:3 * ATT_WIDTH]).astype(BF16)
    z = _dot(h, w_ref[:, 3 * ATT_WIDTH:3 * ATT_WIDTH + 2 * SGU_WIDTH])
    zu_ref[...] = _gelu(z[:, :SGU_WIDTH])
    gv = _gelu(z[:, SGU_WIDTH:])
    mu = jnp.mean(gv, axis=-1, keepdims=True)
    var = jnp.mean(jnp.square(gv - mu), axis=-1, keepdims=True)
    zv_ref[...] = ((gv - mu) * lax.rsqrt(var + LN_EPS) * lnw_ref[...] + lnb_ref[...]).astype(BF16)


def _in_proj(x2, seq, p):
    T, D = x2.shape
    tm = TOKEN_TILE
    n_seq_tiles = seq // tm
    in_width = p["w_in"].shape[1]
    const = lambda shape: pl.BlockSpec(shape, lambda i: (0,) * len(shape))
    rope = pl.BlockSpec((tm, LANES), lambda i: (i % n_seq_tiles, 0))
    row = lambda w: pl.BlockSpec((tm, w), lambda i: (i, 0))
    return pl.pallas_call(
        _in_proj_kernel,
        grid=(T // tm,),
        in_specs=[row(D), const((1, D)), const((D, in_width)), const((1, ATT_WIDTH)), const((1, ATT_WIDTH)),
                  const((MXU_DIM, MXU_DIM)), rope, rope, rope, const((1, SGU_WIDTH)), const((1, SGU_WIDTH))],
        out_specs=[row(ATT_WIDTH), row(ATT_WIDTH), row(ATT_WIDTH), row(SGU_WIDTH), row(SGU_WIDTH)],
        out_shape=[jax.ShapeDtypeStruct((T, ATT_WIDTH), BF16)] * 3
        + [jax.ShapeDtypeStruct((T, SGU_WIDTH), F32), jax.ShapeDtypeStruct((T, SGU_WIDTH), BF16)],
        compiler_params=pltpu.CompilerParams(dimension_semantics=("arbitrary",), vmem_limit_bytes=VMEM_LIMIT),
        name="in_proj",
    )(x2, p["attn_norm_w"], p["w_in"], p["q_norm_w"], p["k_norm_w"], p["head_ones"],
      p["rope_cos"], p["rope_sa"], p["rope_sb"], p["sgu_ln_w"], p["sgu_ln_b"])


def _attn_kernel(q_ref, kp_ref, km_ref, kn_ref, vp_ref, vm_ref, vn_ref, o_ref, lse_ref, *, sub_len, steps):
    lq = q_ref.shape[0]
    lk = lq + 2 * steps
    lb = pl.program_id(2)
    q = q_ref[...]
    kk = jnp.concatenate([kp_ref[...], km_ref[...], kn_ref[...]], axis=0)
    vv = jnp.concatenate([vp_ref[...], vm_ref[...], vn_ref[...]], axis=0)
    qi = lax.broadcasted_iota(I32, (lq, 1), 0)
    kj = lax.broadcasted_iota(I32, (lq, lk), 1)
    lo = jnp.maximum(qi, steps - lb * lq)
    hi = jnp.minimum(qi + 2 * steps, sub_len - 1 + steps - lb * lq)
    mask = (kj >= lo) & (kj <= hi)
    head = lax.broadcasted_iota(I32, (1, GROUP_WIDTH), 1) // HEAD_DIM
    acc = jnp.zeros((lq, GROUP_WIDTH), F32)
    lse = jnp.zeros((lq, GROUP_WIDTH), F32)
    for h in range(HEADS_PER_GROUP):
        hm = (head == h).astype(F32)
        s = _dot_nt(q * hm.astype(BF16), kk)
        s = jnp.where(mask, s, NEG_INF)
        m = jnp.max(s, axis=-1, keepdims=True)
        pr = jnp.exp(s - m)
        den = jnp.sum(pr, axis=-1, keepdims=True)
        oh = _dot(pr.astype(BF16), vv) / den
        acc = acc + oh * hm
        lse = lse + (m + jnp.log(den)) * hm
    o_ref[...] = acc
    lse_ref[...] = lse


def _attention(q, k, v, batch, seq, group):
    window, dil = ATT_WINDOWS[group]
    steps = window // (2 * dil)
    n_groups = len(ATT_WINDOWS)
    sub_len = seq // dil
    assert sub_len % steps == 0
    lq = min(ATT_Q_BLOCK, sub_len)
    assert lq % steps == 0 and sub_len % lq == 0
    per_q = lq // steps
    n_halo = sub_len // steps
    view = lambda t: t.reshape(batch, sub_len, dil * ATT_WIDTH)
    col = lambda r: r * n_groups + group
    main = pl.BlockSpec((None, lq, GROUP_WIDTH), lambda b, r, i: (b, i, col(r)))
    prev = pl.BlockSpec((None, steps, GROUP_WIDTH), lambda b, r, i: (b, jnp.maximum(i * per_q - 1, 0), col(r)))
    nxt = pl.BlockSpec((None, steps, GROUP_WIDTH),
                       lambda b, r, i: (b, jnp.minimum((i + 1) * per_q, n_halo - 1), col(r)))
    out = pl.BlockSpec((None, lq, GROUP_WIDTH), lambda b, r, i: (b, i, r))
    o, lse = pl.pallas_call(
        functools.partial(_attn_kernel, sub_len=sub_len, steps=steps),
        grid=(batch, dil, sub_len // lq),
        in_specs=[main, prev, main, nxt, prev, main, nxt],
        out_specs=[out, out],
        out_shape=[jax.ShapeDtypeStruct((batch, sub_len, dil * GROUP_WIDTH), F32)] * 2,
        compiler_params=pltpu.CompilerParams(dimension_semantics=("arbitrary",) * 3, vmem_limit_bytes=VMEM_LIMIT),
        name=f"attention_g{group}",
    )(view(q), view(k), view(k), view(k), view(v), view(v), view(v))
    return o.reshape(batch * seq, GROUP_WIDTH), lse.reshape(batch * seq, GROUP_WIDTH)


def _post_mix_kernel(x_ref, o0_ref, o1_ref, o2_ref, l0_ref, l1_ref, l2_ref, zu_ref, zv_ref, sguw_ref, sgub_ref,
                     aonw_ref, sonw_ref, wout_ref, fnw_ref, rwt_ref, rb_ref, tri_ref,
                     x1_ref, h2_ref, gates_ref, eidx_ref, rank_ref, cnt_ref):
    tm = x_ref.shape[0]
    l0, l1, l2 = l0_ref[...], l1_ref[...], l2_ref[...]
    lmax = jnp.maximum(jnp.maximum(l0, l1), l2)
    e0, e1, e2 = jnp.exp(l0 - lmax), jnp.exp(l1 - lmax), jnp.exp(l2 - lmax)
    esum = e0 + e1 + e2
    att = (e0 / esum) * o0_ref[...] + (e1 / esum) * o1_ref[...] + (e2 / esum) * o2_ref[...]
    att_n = _rms(att, aonw_ref[...]).astype(BF16)
    cgrp = lax.broadcasted_iota(I32, (1, SGU_WIDTH), 1) // SGU_GROUP_DIM
    sguw = sguw_ref[...]
    gates = []
    for c in range(tm // SGU_CHUNK):
        r = _dot(sguw, zv_ref[c * SGU_CHUNK:(c + 1) * SGU_CHUNK, :])
        g = sgub_ref[...]
        for grp in range(SGU_GROUPS):
            g = g + r[grp * SGU_CHUNK:(grp + 1) * SGU_CHUNK, :] * (cgrp == grp).astype(F32)
        gates.append(g)
    sgu = zu_ref[...] * jnp.concatenate(gates, axis=0)
    sgu_n = _rms(sgu, sonw_ref[...]).astype(BF16)
    x1 = x_ref[...] + _dot(att_n, wout_ref[0:GROUP_WIDTH, :]) + _dot(sgu_n, wout_ref[GROUP_WIDTH:, :])
    x1_ref[...] = x1
    h2 = _rms(x1, fnw_ref[...])
    h2_ref[...] = h2
    logits = _dot_nt(rwt_ref[...], h2, precision=lax.Precision.HIGHEST) + rb_ref[...]
    eiota = lax.broadcasted_iota(I32, (N_EXPERTS, tm), 0)
    vals, idxs = [], []
    for _ in range(TOP_K):
        m = jnp.max(logits, axis=0, keepdims=True)
        idx = jnp.min(jnp.where(logits == m, eiota, N_EXPERTS), axis=0, keepdims=True)
        vals.append(m)
        idxs.append(idx)
        logits = jnp.where(eiota == idx, -jnp.inf, logits)
    exps = [jnp.exp(v - vals[0]) for v in vals]
    den = exps[0] + exps[1] + exps[2] + exps[3]
    gates_ref[...] = jnp.concatenate([e / den for e in exps], axis=0)
    eidx_ref[...] = jnp.concatenate(idxs, axis=0)
    onehots = [(eiota == idx).astype(F32) for idx in idxs]
    chosen = onehots[0] + onehots[1] + onehots[2] + onehots[3]
    before = _dot(chosen.astype(BF16), tri_ref[...])
    rank_ref[...] = jnp.concatenate(
        [jnp.sum(oh * before, axis=0, keepdims=True) for oh in onehots], axis=0).astype(I32)
    cnt_ref[...] = jnp.broadcast_to(jnp.sum(chosen, axis=1, keepdims=True), (N_EXPERTS, LANES)).astype(I32)


def _post_mix(x2, o, lse, zu, zv, p):
    T, D = x2.shape
    tm = TOKEN_TILE
    n_tiles = T // tm
    const = lambda shape: pl.BlockSpec(shape, lambda i: (0,) * len(shape))
    row = lambda w: pl.BlockSpec((tm, w), lambda i: (i, 0))
    colt = pl.BlockSpec((TOP_K, tm), lambda i: (0, i))
    return pl.pallas_call(
        _post_mix_kernel,
        grid=(n_tiles,),
        in_specs=[row(D)] + [row(GROUP_WIDTH)] * 6 + [row(SGU_WIDTH), row(SGU_WIDTH),
                  const((SGU_GROUPS * SGU_CHUNK, SGU_CHUNK)), const((SGU_CHUNK, SGU_WIDTH)),
                  const((1, GROUP_WIDTH)), const((1, SGU_WIDTH)), const((GROUP_WIDTH + SGU_WIDTH, D)),
                  const((1, D)), const((N_EXPERTS, D)), const((N_EXPERTS, 1)), const((tm, tm))],
        out_specs=[row(D), row(D), colt, colt, colt, pl.BlockSpec((None, N_EXPERTS, LANES), lambda i: (i, 0, 0))],
        out_shape=[jax.ShapeDtypeStruct((T, D), F32), jax.ShapeDtypeStruct((T, D), F32),
                   jax.ShapeDtypeStruct((TOP_K, T), F32), jax.ShapeDtypeStruct((TOP_K, T), I32),
                   jax.ShapeDtypeStruct((TOP_K, T), I32), jax.ShapeDtypeStruct((n_tiles, N_EXPERTS, LANES), I32)],
        compiler_params=pltpu.CompilerParams(dimension_semantics=("arbitrary",), vmem_limit_bytes=VMEM_LIMIT),
        name="post_mix",
    )(x2, o[0], o[1], o[2], lse[0], lse[1], lse[2], zu, zv, p["sgu_w"], p["sgu_b"], p["att_out_norm_w"],
      p["sgu_out_norm_w"], p["w_out"], p["ffn_norm_w"], p["router_wt"], p["router_b"], p["tri"])


def _pos_copy(pos_hbm, pos_smem, sem, tile, slot):
    n = pos_smem.shape[1]
    return pltpu.make_async_copy(pos_hbm.at[pl.ds(pl.multiple_of(tile * n, n), n)], pos_smem.at[slot], sem.at[slot])


def _dispatch_kernel(pos_hbm, h_ref, xs_init_hbm, xs_hbm, pos_smem, pos_sem, row_sem):
    del xs_init_hbm
    tm = h_ref.shape[0]
    i = pl.program_id(0)
    n = pl.num_programs(0)
    slot = i % 2

    @pl.when(i == 0)
    def _():
        _pos_copy(pos_hbm, pos_smem, pos_sem, 0, 0).start()

    _pos_copy(pos_hbm, pos_smem, pos_sem, i, slot).wait()

    @pl.when(i + 1 < n)
    def _():
        _pos_copy(pos_hbm, pos_smem, pos_sem, i + 1, 1 - slot).start()

    def row_copy(j, dst_row):
        return pltpu.make_async_copy(h_ref.at[pl.ds(j, 1)], xs_hbm.at[pl.ds(dst_row, 1)], row_sem)

    def issue(j, carry):
        for k in range(TOP_K):
            row_copy(j, pos_smem[slot, k * tm + j]).start()
        return carry

    lax.fori_loop(0, tm, issue, 0, unroll=8)
    for _ in range(TOP_K):
        pltpu.make_async_copy(h_ref, xs_hbm.at[pl.ds(0, tm)], row_sem).wait()


def _dispatch(h2, pos_tiles, n_rows):
    T, D = h2.shape
    tm = ROW_TILE
    xs0 = jnp.zeros((n_rows, D), h2.dtype)
    return pl.pallas_call(
        _dispatch_kernel,
        grid=(T // tm,),
        in_specs=[pl.BlockSpec(memory_space=pl.ANY), pl.BlockSpec((tm, D), lambda i: (i, 0)),
                  pl.BlockSpec(memory_space=pl.ANY)],
        out_specs=pl.BlockSpec(memory_space=pl.ANY),
        out_shape=jax.ShapeDtypeStruct((n_rows, D), h2.dtype),
        scratch_shapes=[pltpu.SMEM((2, TOP_K * tm), I32), pltpu.SemaphoreType.DMA((2,)), pltpu.SemaphoreType.DMA],
        input_output_aliases={2: 0},
        compiler_params=pltpu.CompilerParams(dimension_semantics=("arbitrary",), vmem_limit_bytes=VMEM_LIMIT),
        name="dispatch",
    )(pos_tiles, h2, xs0)


def _experts_kernel(blk_e_ref, n_used_ref, x_ref, wg_ref, bg_ref, wu_ref, bu_ref, wd_ref, bd_ref, o_ref):
    del blk_e_ref
    used = pl.program_id(0) < n_used_ref[0]

    @pl.when(jnp.logical_not(used))
    def _():
        o_ref[...] = jnp.zeros_like(o_ref)

    @pl.when(used)
    def _():
        x = x_ref[...].astype(BF16)
        g = _dot(x, wg_ref[...]) + bg_ref[...]
        u = _dot(x, wu_ref[...]) + bu_ref[...]
        g = jnp.minimum(g, SWIGLU_LIMIT)
        u = jnp.clip(u, -SWIGLU_LIMIT, SWIGLU_LIMIT)
        act = (u + 1.0) * (g * jax.nn.sigmoid(SWIGLU_ALPHA * g))
        o_ref[...] = _dot(act.astype(BF16), wd_ref[...]) + bd_ref[...]


def _experts(xs, blk_e, n_used, p):
    n_rows, D = xs.shape
    bm = MOE_BLOCK
    d_ff = p["w_gate"].shape[2]
    rows = pl.BlockSpec((bm, D), lambda i, be, nu: (jnp.minimum(i, nu[0] - 1), 0))
    wspec = lambda a, b: pl.BlockSpec((None, a, b), lambda i, be, nu: (be[i], 0, 0))
    return pl.pallas_call(
        _experts_kernel,
        grid_spec=pltpu.PrefetchScalarGridSpec(
            num_scalar_prefetch=2,
            grid=(n_rows // bm,),
            in_specs=[rows, wspec(D, d_ff), wspec(1, d_ff), wspec(D, d_ff), wspec(1, d_ff), wspec(d_ff, D),
                      wspec(1, D)],
            out_specs=pl.BlockSpec((bm, D), lambda i, be, nu: (i, 0)),
        ),
        out_shape=jax.ShapeDtypeStruct((n_rows, D), F32),
        compiler_params=pltpu.CompilerParams(dimension_semantics=("arbitrary",), vmem_limit_bytes=VMEM_LIMIT),
        name="experts",
    )(blk_e, n_used, xs, p["w_gate"], p["b_gate"], p["w_up"], p["b_up"], p["w_down"], p["b_down"])


def _combine_kernel(pos_hbm, gates_ref, x1_ref, ys_hbm, out_ref, pos_smem, buf, pos_sem, row_sem):
    tm = x1_ref.shape[0]
    i = pl.program_id(0)
    n = pl.num_programs(0)
    slot = i % 2

    def issue_rows(s):
        def body(j, carry):
            for k in range(TOP_K):
                r = k * tm + j
                pltpu.make_async_copy(ys_hbm.at[pl.ds(pos_smem[s, r], 1)], buf.at[s, pl.ds(r, 1)],
                                      row_sem.at[s]).start()
            return carry
        lax.fori_loop(0, tm, body, 0, unroll=8)

    @pl.when(i == 0)
    def _():
        _pos_copy(pos_hbm, pos_smem, pos_sem, 0, 0).start()
        _pos_copy(pos_hbm, pos_smem, pos_sem, 0, 0).wait()
        issue_rows(0)

        @pl.when(n > 1)
        def _():
            _pos_copy(pos_hbm, pos_smem, pos_sem, 1, 1).start()

    @pl.when(i + 1 < n)
    def _():
        _pos_copy(pos_hbm, pos_smem, pos_sem, i + 1, 1 - slot).wait()
        issue_rows(1 - slot)

    @pl.when(i + 2 < n)
    def _():
        _pos_copy(pos_hbm, pos_smem, pos_sem, i + 2, slot).start()

    pltpu.make_async_copy(ys_hbm.at[pl.ds(0, TOP_K * tm)], buf.at[slot], row_sem.at[slot]).wait()
    y = gates_ref[:, 0:1] * buf[slot, 0:tm, :]
    for k in range(1, TOP_K):
        y = y + gates_ref[:, k:k + 1] * buf[slot, k * tm:(k + 1) * tm, :]
    out_ref[...] = x1_ref[...] + y


def _combine(x1, gates_tok, ys, pos_tiles):
    T, D = x1.shape
    tm = ROW_TILE
    return pl.pallas_call(
        _combine_kernel,
        grid=(T // tm,),
        in_specs=[pl.BlockSpec(memory_space=pl.ANY), pl.BlockSpec((tm, TOP_K), lambda i: (i, 0)),
                  pl.BlockSpec((tm, D), lambda i: (i, 0)), pl.BlockSpec(memory_space=pl.ANY)],
        out_specs=pl.BlockSpec((tm, D), lambda i: (i, 0)),
        out_shape=jax.ShapeDtypeStruct((T, D), F32),
        scratch_shapes=[pltpu.SMEM((2, TOP_K * tm), I32), pltpu.VMEM((2, TOP_K * tm, D), F32),
                        pltpu.SemaphoreType.DMA((2,)), pltpu.SemaphoreType.DMA((2,))],
        compiler_params=pltpu.CompilerParams(dimension_semantics=("arbitrary",), vmem_limit_bytes=VMEM_LIMIT),
        name="combine",
    )(pos_tiles, gates_tok, x1, ys)


def _route(eidx, rank, tile_counts):
    T = eidx.shape[1]
    bm = MOE_BLOCK
    tc = tile_counts[:, :, 0]
    counts = jnp.sum(tc, axis=0)
    tile_off = jnp.cumsum(tc, axis=0) - tc
    pcounts = (counts + bm - 1) // bm * bm
    pends = jnp.cumsum(pcounts)
    base = (pends - pcounts)[None, :] + tile_off
    tile_of_tok = jnp.arange(T, dtype=I32) // TOKEN_TILE
    pos = base.reshape(-1)[tile_of_tok[None, :] * N_EXPERTS + eidx] + rank
    n_blocks = (TOP_K * T) // bm + N_EXPERTS
    blk_start = jnp.arange(n_blocks, dtype=I32) * bm
    blk_e = jnp.minimum(jnp.sum((pends[None, :] <= blk_start[:, None]).astype(I32), axis=1), N_EXPERTS - 1)
    n_used = (pends[-1] // bm).astype(I32).reshape(1)
    pos_tiles = pos.reshape(TOP_K, T // ROW_TILE, ROW_TILE).transpose(1, 0, 2).reshape(-1)
    return pos_tiles, blk_e, n_used, n_blocks * bm


def _layer(x, p):
    batch, seq, D = x.shape
    x2 = x.reshape(batch * seq, D)
    q, k, v, zu, zv = _in_proj(x2, seq, p)
    o, lse = zip(*[_attention(q, k, v, batch, seq, g) for g in range(len(ATT_WINDOWS))])
    x1, h2, gates, eidx, rank, tile_counts = _post_mix(x2, o, lse, zu, zv, p)
    pos_tiles, blk_e, n_used, n_rows = _route(eidx, rank, tile_counts)
    xs = _dispatch(h2, pos_tiles, n_rows)
    ys = _experts(xs, blk_e, n_used, p)
    out = _combine(x1, gates.T, ys, pos_tiles)
    return out.reshape(batch, seq, D)


def _rope_tables(seq):
    half = ROT_DIM // 2
    inv_freq = jnp.power(ROPE_THETA, -2.0 * jnp.arange(half, dtype=F32) / ROT_DIM)
    ang = jnp.arange(seq, dtype=F32)[:, None] * inv_freq[None, :]
    cos, sin = jnp.cos(ang), jnp.sin(ang)
    zeros = jnp.zeros((seq, HEAD_DIM - ROT_DIM), F32)
    zh = jnp.zeros((seq, half), F32)
    cos_t = jnp.concatenate([cos, cos, zeros + 1.0], axis=1)
    sa = jnp.concatenate([-sin, zh, zeros], axis=1)
    sb = jnp.concatenate([zh, sin, zeros], axis=1)
    rep = LANES // HEAD_DIM
    return tuple(jnp.tile(t, (1, rep)) for t in (cos_t, sa, sb))


def kernel(x_prompt, x_sample, attn_norm_w, w_in, q_norm_w, k_norm_w, sgu_ln_w, sgu_ln_b, sgu_w, sgu_b,
           att_out_norm_w, sgu_out_norm_w, w_out, ffn_norm_w, router_w, router_b,
           w_gate, b_gate, w_up, b_up, w_down, b_down):
    depth = w_in.shape[0]
    n_heads = ATT_WIDTH // HEAD_DIM
    blk = np.arange(MXU_DIM) // HEAD_DIM
    tri = np.arange(TOKEN_TILE)
    outs = []
    for x in (x_prompt, x_sample):
        seq = x.shape[1]
        cos_t, sa, sb = _rope_tables(seq)
        for l in range(depth):
            p = dict(
                attn_norm_w=attn_norm_w[l][None], w_in=w_in[l].astype(BF16),
                q_norm_w=jnp.tile(q_norm_w[l], n_heads)[None], k_norm_w=jnp.tile(k_norm_w[l], n_heads)[None],
                head_ones=jnp.asarray(blk[:, None] == blk[None, :], BF16),
                rope_cos=cos_t, rope_sa=sa, rope_sb=sb,
                sgu_ln_w=sgu_ln_w[l][None], sgu_ln_b=sgu_ln_b[l][None],
                sgu_w=sgu_w[l].reshape(SGU_GROUPS * SGU_CHUNK, SGU_CHUNK).astype(BF16),
                sgu_b=jnp.repeat(sgu_b[l].T, SGU_GROUP_DIM, axis=1),
                att_out_norm_w=att_out_norm_w[l][None], sgu_out_norm_w=sgu_out_norm_w[l][None],
                w_out=w_out[l].astype(BF16), ffn_norm_w=ffn_norm_w[l][None],
                router_wt=router_w[l].T, router_b=router_b[l][:, None],
                tri=jnp.asarray(tri[:, None] < tri[None, :], BF16),
                w_gate=w_gate[l].astype(BF16), b_gate=b_gate[l][:, None, :],
                w_up=w_up[l].astype(BF16), b_up=b_up[l][:, None, :],
                w_down=w_down[l].astype(BF16), b_down=b_down[l][:, None, :],
            )
            x = _layer(x, p)
        outs.append(x)
    return tuple(outs)
```

```python
import functools

import numpy as np
import jax
import jax.numpy as jnp
from jax import lax
from jax.experimental import pallas as pl
from jax.experimental.pallas import tpu as pltpu

F32 = jnp.float32
BF16 = jnp.bfloat16
I32 = jnp.int32

HEAD_DIM = 64
ATT_WINDOWS = ((128, 1), (512, 4), (2048, 16))
N_GROUPS = len(ATT_WINDOWS)
HEADS_PER_GROUP = 4
GROUP_WIDTH = HEADS_PER_GROUP * HEAD_DIM
ATT_WIDTH = N_GROUPS * GROUP_WIDTH
SGU_WIDTH = 256
SGU_GROUP_DIM = 64
SGU_GROUPS = SGU_WIDTH // SGU_GROUP_DIM
SGU_CHUNK = 128
ROT_DIM = HEAD_DIM // 4
ROPE_THETA = 500000.0
N_EXPERTS = 32
TOP_K = 4
SWIGLU_LIMIT = 7.0
SWIGLU_ALPHA = 1.702
NORM_EPS = 1e-6
LN_EPS = 1e-5
NEG_INF = -1e30

LANES = 128
SUBLANES = 8
MXU_DIM = 256
TOKEN_TILE = 512
ATT_Q_BLOCK = 256
MOE_BLOCK = 256
ROW_TILE = 256
VMEM_LIMIT = 48 * 1024 * 1024


def _dot(a, b):
    return jnp.dot(a, b, preferred_element_type=F32)


def _dot_nt(a, b, precision=None):
    return lax.dot_general(a, b, (((1,), (1,)), ((), ())), precision=precision, preferred_element_type=F32)


def _rms(x, w):
    return x * lax.rsqrt(jnp.mean(x * x, axis=-1, keepdims=True) + NORM_EPS) * w


def _gelu(x):
    return 0.5 * x * (1.0 + lax.erf(x * np.float32(np.sqrt(0.5))))


def _to_tiled(ref, idx, value):
    rows = value.shape[0]
    for c in range(value.shape[1] // LANES):
        ref[idx + (slice(None), c)] = value[:, c * LANES:(c + 1) * LANES].reshape(rows // SUBLANES, SUBLANES, LANES)


def _from_tiled(ref, idx, row_tiles):
    chunks = ref.shape[-3]
    n = (row_tiles.stop - row_tiles.start) * SUBLANES
    return jnp.concatenate([ref[idx + (row_tiles, c)].reshape(n, LANES) for c in range(chunks)], axis=1)


def _in_proj_kernel(x_ref, anw_ref, w_ref, qnw_ref, knw_ref, ones_ref, cos_ref, sa_ref, sb_ref, lnw_ref, lnb_ref,
                    *refs):
    qkv_refs = refs[:3 * N_GROUPS]
    zu_ref, zv_ref, stage = refs[3 * N_GROUPS:]
    tm = x_ref.shape[0]
    h = _rms(x_ref[...], anw_ref[...]).astype(BF16)
    reps = ATT_WIDTH // LANES
    cos = jnp.concatenate([cos_ref[...]] * reps, axis=1)
    sa = jnp.concatenate([sa_ref[...]] * reps, axis=1)
    sb = jnp.concatenate([sb_ref[...]] * reps, axis=1)
    ones = ones_ref[...]

    def head_norm_rope(t, nw):
        sq = t * t
        hi = sq.astype(BF16)
        lo = (sq - hi.astype(F32)).astype(BF16)
        parts = []
        for j in range(ATT_WIDTH // MXU_DIM):
            sl = slice(j * MXU_DIM, (j + 1) * MXU_DIM)
            parts.append(_dot(hi[:, sl], ones) + _dot(lo[:, sl], ones))
        ssum = jnp.concatenate(parts, axis=1)
        t = t * lax.rsqrt(ssum * (1.0 / HEAD_DIM) + NORM_EPS) * nw
        half = ROT_DIM // 2
        return t * cos + pltpu.roll(t, ATT_WIDTH - half, 1) * sa + pltpu.roll(t, half, 1) * sb

    def emit(which, t):
        n_chunks = ATT_WIDTH // LANES
        per_group = GROUP_WIDTH // LANES
        for c in range(n_chunks):
            stage[which * n_chunks + c] = t[:, c * LANES:(c + 1) * LANES]
        for g, (_, dil) in enumerate(ATT_WINDOWS):
            out = qkv_refs[which * N_GROUPS + g]
            first = which * n_chunks + g * per_group
            for r in range(dil):
                rows = pl.ds(r, tm // dil, stride=dil)
                out[r] = jnp.concatenate([stage[first + c, rows, :] for c in range(per_group)], axis=1).astype(BF16)

    q = head_norm_rope(_dot(h, w_ref[:, 0:ATT_WIDTH]), qnw_ref[...])
    emit(0, q * (HEAD_DIM ** -0.5))
    emit(1, head_norm_rope(_dot(h, w_ref[:, ATT_WIDTH:2 * ATT_WIDTH]), knw_ref[...]))
    emit(2, _dot(h, w_ref[:, 2 * ATT_WIDTH:3 * ATT_WIDTH]))
    z = _dot(h, w_ref[:, 3 * ATT_WIDTH:3 * ATT_WIDTH + 2 * SGU_WIDTH])
    zu_ref[...] = _gelu(z[:, :SGU_WIDTH])
    gv = _gelu(z[:, SGU_WIDTH:])
    mu = jnp.mean(gv, axis=-1, keepdims=True)
    var = jnp.mean(jnp.square(gv - mu), axis=-1, keepdims=True)
    zv_ref[...] = ((gv - mu) * lax.rsqrt(var + LN_EPS) * lnw_ref[...] + lnb_ref[...]).astype(BF16)


def _phase_spec(tm, dil, n_seq_tiles):
    return pl.BlockSpec((None, dil, tm // dil, GROUP_WIDTH), lambda i: (i // n_seq_tiles, 0, i % n_seq_tiles, 0))


def _in_proj(x2, batch, seq, p):
    T, D = x2.shape
    tm = TOKEN_TILE
    n_seq_tiles = seq // tm
    in_width = p["w_in"].shape[1]
    const = lambda shape: pl.BlockSpec(shape, lambda i: (0,) * len(shape))
    rope = pl.BlockSpec((tm, LANES), lambda i: (i % n_seq_tiles, 0))
    row = lambda w: pl.BlockSpec((tm, w), lambda i: (i, 0))
    qkv_specs = [_phase_spec(tm, dil, n_seq_tiles) for _ in range(3) for _, dil in ATT_WINDOWS]
    qkv_shapes = [jax.ShapeDtypeStruct((batch, dil, seq // dil, GROUP_WIDTH), BF16)
                  for _ in range(3) for _, dil in ATT_WINDOWS]
    outs = pl.pallas_call(
        _in_proj_kernel,
        grid=(T // tm,),
        in_specs=[row(D), const((1, D)), const((D, in_width)), const((1, ATT_WIDTH)), const((1, ATT_WIDTH)),
                  const((MXU_DIM, MXU_DIM)), rope, rope, rope, const((1, SGU_WIDTH)), const((1, SGU_WIDTH))],
        out_specs=qkv_specs + [row(SGU_WIDTH), row(SGU_WIDTH)],
        out_shape=qkv_shapes + [jax.ShapeDtypeStruct((T, SGU_WIDTH), F32), jax.ShapeDtypeStruct((T, SGU_WIDTH), BF16)],
        scratch_shapes=[pltpu.VMEM((3 * ATT_WIDTH // LANES, tm, LANES), F32)],
        compiler_params=pltpu.CompilerParams(dimension_semantics=("arbitrary",), vmem_limit_bytes=VMEM_LIMIT),
        name="in_proj",
    )(x2, p["attn_norm_w"], p["w_in"], p["q_norm_w"], p["k_norm_w"], p["head_ones"],
      p["rope_cos"], p["rope_sa"], p["rope_sb"], p["sgu_ln_w"], p["sgu_ln_b"])
    q, k, v = (outs[i * N_GROUPS:(i + 1) * N_GROUPS] for i in range(3))
    return q, k, v, outs[-2], outs[-1]


def _attn_kernel(q_ref, kp_ref, km_ref, kn_ref, vp_ref, vm_ref, vn_ref, o_ref, lse_ref, *, sub_len, steps):
    lq = q_ref.shape[0]
    lk = lq + 2 * steps
    lb = pl.program_id(2)
    q = q_ref[...]
    kk = jnp.concatenate([kp_ref[...], km_ref[...], kn_ref[...]], axis=0)
    vv = jnp.concatenate([vp_ref[...], vm_ref[...], vn_ref[...]], axis=0)
    qi = lax.broadcasted_iota(I32, (lq, 1), 0)
    kj = lax.broadcasted_iota(I32, (lq, lk), 1)
    lo = jnp.maximum(qi, steps - lb * lq)
    hi = jnp.minimum(qi + 2 * steps, sub_len - 1 + steps - lb * lq)
    mask = (kj >= lo) & (kj <= hi)
    head = lax.broadcasted_iota(I32, (1, GROUP_WIDTH), 1) // HEAD_DIM
    acc = jnp.zeros((lq, GROUP_WIDTH), F32)
    lse = jnp.zeros((lq, GROUP_WIDTH), F32)
    for h in range(HEADS_PER_GROUP):
        hm = (head == h).astype(F32)
        s = _dot_nt(q * hm.astype(BF16), kk)
        s = jnp.where(mask, s, NEG_INF)
        m = jnp.max(s, axis=-1, keepdims=True)
        pr = jnp.exp(s - m)
        den = jnp.sum(pr, axis=-1, keepdims=True)
        oh = _dot(pr.astype(BF16), vv) / den
        acc = acc + oh * hm
        lse = lse + (m + jnp.log(den)) * hm
    o_ref[...] = acc
    lse_ref[...] = lse


def _attention(q, k, v, group):
    window, dil = ATT_WINDOWS[group]
    steps = window // (2 * dil)
    batch, _, sub_len, _ = q.shape
    assert sub_len % steps == 0
    lq = min(ATT_Q_BLOCK, sub_len)
    assert lq % steps == 0 and sub_len % lq == 0
    per_q = lq // steps
    n_halo = sub_len // steps
    main = pl.BlockSpec((None, None, lq, GROUP_WIDTH), lambda b, r, i: (b, r, i, 0))
    prev = pl.BlockSpec((None, None, steps, GROUP_WIDTH), lambda b, r, i: (b, r, jnp.maximum(i * per_q - 1, 0), 0))
    nxt = pl.BlockSpec((None, None, steps, GROUP_WIDTH),
                       lambda b, r, i: (b, r, jnp.minimum((i + 1) * per_q, n_halo - 1), 0))
    return pl.pallas_call(
        functools.partial(_attn_kernel, sub_len=sub_len, steps=steps),
        grid=(batch, dil, sub_len // lq),
        in_specs=[main, prev, main, nxt, prev, main, nxt],
        out_specs=[main, main],
        out_shape=[jax.ShapeDtypeStruct(q.shape, F32)] * 2,
        compiler_params=pltpu.CompilerParams(dimension_semantics=("arbitrary",) * 3, vmem_limit_bytes=VMEM_LIMIT),
        name=f"attention_g{group}",
    )(q, k, k, k, v, v, v)


def _post_mix_kernel(x_ref, *refs):
    o_refs = refs[:N_GROUPS]
    l_refs = refs[N_GROUPS:2 * N_GROUPS]
    (zu_ref, zv_ref, sguw_ref, sgub_ref, aonw_ref, sonw_ref, wout_ref, fnw_ref, rwt_ref, rb_ref, tri_ref,
     x1_ref, h2_ref, gates_ref, eidx_ref, rank_ref, cnt_ref, stage) = refs[2 * N_GROUPS:]
    tm = x_ref.shape[0]

    def token_order(ref, slot):
        dil = ref.shape[0]
        if dil == 1:
            return ref[0]
        per_group = GROUP_WIDTH // LANES
        for r in range(dil):
            for c in range(per_group):
                stage[slot * per_group + c, pl.ds(r, tm // dil, stride=dil), :] = ref[r, :, c * LANES:(c + 1) * LANES]
        return jnp.concatenate([stage[slot * per_group + c] for c in range(per_group)], axis=1)

    o = [token_order(ref, g) for g, ref in enumerate(o_refs)]
    l = [token_order(ref, N_GROUPS + g) for g, ref in enumerate(l_refs)]
    lmax = jnp.maximum(jnp.maximum(l[0], l[1]), l[2])
    e = [jnp.exp(lg - lmax) for lg in l]
    esum = e[0] + e[1] + e[2]
    att = (e[0] / esum) * o[0] + (e[1] / esum) * o[1] + (e[2] / esum) * o[2]
    att_n = _rms(att, aonw_ref[...]).astype(BF16)
    cgrp = lax.broadcasted_iota(I32, (1, SGU_WIDTH), 1) // SGU_GROUP_DIM
    sguw = sguw_ref[...]
    gates = []
    for c in range(tm // SGU_CHUNK):
        r = _dot(sguw, zv_ref[c * SGU_CHUNK:(c + 1) * SGU_CHUNK, :])
        g = sgub_ref[...]
        for grp in range(SGU_GROUPS):
            g = g + r[grp * SGU_CHUNK:(grp + 1) * SGU_CHUNK, :] * (cgrp == grp).astype(F32)
        gates.append(g)
    sgu = zu_ref[...] * jnp.concatenate(gates, axis=0)
    sgu_n = _rms(sgu, sonw_ref[...]).astype(BF16)
    x1 = x_ref[...] + _dot(att_n, wout_ref[0:GROUP_WIDTH, :]) + _dot(sgu_n, wout_ref[GROUP_WIDTH:, :])
    x1_ref[...] = x1
    h2 = _rms(x1, fnw_ref[...])
    _to_tiled(h2_ref, (), h2)
    logits = _dot_nt(rwt_ref[...], h2, precision=lax.Precision.HIGHEST) + rb_ref[...]
    eiota = lax.broadcasted_iota(I32, (N_EXPERTS, tm), 0)
    vals, idxs = [], []
    for _ in range(TOP_K):
        m = jnp.max(logits, axis=0, keepdims=True)
        idx = jnp.min(jnp.where(logits == m, eiota, N_EXPERTS), axis=0, keepdims=True)
        vals.append(m)
        idxs.append(idx)
        logits = jnp.where(eiota == idx, -jnp.inf, logits)
    exps = [jnp.exp(v - vals[0]) for v in vals]
    den = exps[0] + exps[1] + exps[2] + exps[3]
    gates_ref[...] = jnp.concatenate([ex / den for ex in exps], axis=0)
    eidx_ref[...] = jnp.concatenate(idxs, axis=0)
    onehots = [(eiota == idx).astype(F32) for idx in idxs]
    chosen = onehots[0] + onehots[1] + onehots[2] + onehots[3]
    before = _dot(chosen.astype(BF16), tri_ref[...])
    rank_ref[...] = jnp.concatenate(
        [jnp.sum(oh * before, axis=0, keepdims=True) for oh in onehots], axis=0).astype(I32)
    cnt_ref[...] = jnp.broadcast_to(jnp.sum(chosen, axis=1, keepdims=True), (N_EXPERTS, LANES)).astype(I32)


def _post_mix(x2, seq, o, lse, zu, zv, p):
    T, D = x2.shape
    tm = TOKEN_TILE
    n_tiles = T // tm
    n_seq_tiles = seq // tm
    chunks = D // LANES
    const = lambda shape: pl.BlockSpec(shape, lambda i: (0,) * len(shape))
    row = lambda w: pl.BlockSpec((tm, w), lambda i: (i, 0))
    colt = pl.BlockSpec((TOP_K, tm), lambda i: (0, i))
    phase = [_phase_spec(tm, dil, n_seq_tiles) for _, dil in ATT_WINDOWS]
    return pl.pallas_call(
        _post_mix_kernel,
        grid=(n_tiles,),
        in_specs=[row(D)] + phase + phase + [row(SGU_WIDTH), row(SGU_WIDTH),
                  const((SGU_GROUPS * SGU_CHUNK, SGU_CHUNK)), const((SGU_CHUNK, SGU_WIDTH)),
                  const((1, GROUP_WIDTH)), const((1, SGU_WIDTH)), const((GROUP_WIDTH + SGU_WIDTH, D)),
                  const((1, D)), const((N_EXPERTS, D)), const((N_EXPERTS, 1)), const((tm, tm))],
        out_specs=[row(D), pl.BlockSpec((tm // SUBLANES, chunks, SUBLANES, LANES), lambda i: (i, 0, 0, 0)),
                   colt, colt, colt, pl.BlockSpec((None, N_EXPERTS, LANES), lambda i: (i, 0, 0))],
        out_shape=[jax.ShapeDtypeStruct((T, D), F32),
                   jax.ShapeDtypeStruct((T // SUBLANES, chunks, SUBLANES, LANES), F32),
                   jax.ShapeDtypeStruct((TOP_K, T), F32), jax.ShapeDtypeStruct((TOP_K, T), I32),
                   jax.ShapeDtypeStruct((TOP_K, T), I32), jax.ShapeDtypeStruct((n_tiles, N_EXPERTS, LANES), I32)],
        scratch_shapes=[pltpu.VMEM((2 * N_GROUPS * GROUP_WIDTH // LANES, tm, LANES), F32)],
        compiler_params=pltpu.CompilerParams(dimension_semantics=("arbitrary",), vmem_limit_bytes=VMEM_LIMIT),
        name="post_mix",
    )(x2, *o, *lse, zu, zv, p["sgu_w"], p["sgu_b"], p["att_out_norm_w"],
      p["sgu_out_norm_w"], p["w_out"], p["ffn_norm_w"], p["router_wt"], p["router_b"], p["tri"])


def _pos_copy(pos_hbm, pos_smem, sem, tile, slot):
    n = pos_smem.shape[1]
    return pltpu.make_async_copy(pos_hbm.at[pl.ds(pl.multiple_of(tile * n, n), n)], pos_smem.at[slot], sem.at[slot])


def _tile_row(ref, idx, row):
    return ref.at[idx + (row // SUBLANES, slice(None), row % SUBLANES, slice(None))]


def _dispatch_kernel(seg_ref, pos_hbm, h_ref, xs_hbm, pos_smem, zeros, pos_sem, row_sem, zero_sem):
    tm = h_ref.shape[0] * SUBLANES
    bm = zeros.shape[0]
    i = pl.program_id(0)
    n = pl.num_programs(0)
    slot = i % 2

    @pl.when(i == 0)
    def _():
        _pos_copy(pos_hbm, pos_smem, pos_sem, 0, 0).start()
        zeros[...] = jnp.zeros_like(zeros)

        def tail_copy(e):
            return pltpu.make_async_copy(
                zeros, xs_hbm.at[pl.ds(pl.multiple_of(seg_ref[e + 1] - bm, bm), bm)], zero_sem)

        for e in range(N_EXPERTS):
            pl.when(seg_ref[e + 1] > seg_ref[e])(lambda e=e: tail_copy(e).start())
        for e in range(N_EXPERTS):
            pl.when(seg_ref[e + 1] > seg_ref[e])(lambda e=e: tail_copy(e).wait())

        def fill_block(b, carry):
            fill = pltpu.make_async_copy(zeros, xs_hbm.at[pl.ds(pl.multiple_of(b * bm, bm), bm)], zero_sem)
            fill.start()
            fill.wait()
            return carry

        lax.fori_loop(seg_ref[N_EXPERTS] // bm, xs_hbm.shape[0] // bm, fill_block, 0)

    _pos_copy(pos_hbm, pos_smem, pos_sem, i, slot).wait()

    @pl.when(i + 1 < n)
    def _():
        _pos_copy(pos_hbm, pos_smem, pos_sem, i + 1, 1 - slot).start()

    for r in range(tm):
        src = _tile_row(h_ref, (), r)
        for k in range(TOP_K):
            pltpu.make_async_copy(src, xs_hbm.at[pos_smem[slot, k * tm + r]], row_sem).start()
    for _ in range(TOP_K * tm // bm):
        pltpu.make_async_copy(zeros, xs_hbm.at[pl.ds(0, bm)], row_sem).wait()


def _dispatch(h2t, pos_tiles, seg, n_rows):
    _, chunks, _, _ = h2t.shape
    tm = ROW_TILE
    bm = MOE_BLOCK
    return pl.pallas_call(
        _dispatch_kernel,
        grid_spec=pltpu.PrefetchScalarGridSpec(
            num_scalar_prefetch=1,
            grid=(h2t.shape[0] * SUBLANES // tm,),
            in_specs=[pl.BlockSpec(memory_space=pl.ANY),
                      pl.BlockSpec((tm // SUBLANES, chunks, SUBLANES, LANES), lambda i, seg: (i, 0, 0, 0))],
            out_specs=pl.BlockSpec(memory_space=pl.ANY),
            scratch_shapes=[pltpu.SMEM((2, TOP_K * tm), I32), pltpu.VMEM((bm, chunks, LANES), F32),
                            pltpu.SemaphoreType.DMA((2,)), pltpu.SemaphoreType.DMA, pltpu.SemaphoreType.DMA],
        ),
        out_shape=jax.ShapeDtypeStruct((n_rows, chunks, LANES), F32),
        compiler_params=pltpu.CompilerParams(dimension_semantics=("arbitrary",), vmem_limit_bytes=VMEM_LIMIT),
        name="dispatch",
    )(seg, pos_tiles, h2t)


def _experts_kernel(blk_e_ref, n_used_ref, xs_hbm, wg_ref, bg_ref, wu_ref, bu_ref, wd_ref, bd_ref, ys_hbm,
                    xbuf, obuf, in_sem, out_sem):
    del blk_e_ref
    bt = xbuf.shape[1]
    i = pl.program_id(0)
    n = pl.num_programs(0)
    n_used = n_used_ref[0]
    slot = i % 2

    def in_copies(blk, s):
        return [pltpu.make_async_copy(xs_hbm.at[pl.ds(blk * bt, bt), j], xbuf.at[s, :, :, j, :], in_sem.at[s])
                for j in range(SUBLANES)]

    def out_copies(blk, s):
        return [pltpu.make_async_copy(obuf.at[s, :, :, j, :], ys_hbm.at[pl.ds(blk * bt, bt), j], out_sem.at[s])
                for j in range(SUBLANES)]

    @pl.when(i == 0)
    def _():
        for c in in_copies(0, 0):
            c.start()

    @pl.when(i + 1 < n_used)
    def _():
        for c in in_copies(i + 1, 1 - slot):
            c.start()

    @pl.when(i >= 2)
    def _():
        for c in out_copies(i - 2, slot):
            c.wait()

    @pl.when(i >= n_used)
    def _():
        obuf[slot] = jnp.zeros(obuf.shape[1:], F32)

    @pl.when(i < n_used)
    def _():
        for c in in_copies(i, slot):
            c.wait()
        x = _from_tiled(xbuf, (slot,), slice(0, bt)).astype(BF16)
        g = _dot(x, wg_ref[...]) + bg_ref[...]
        u = _dot(x, wu_ref[...]) + bu_ref[...]
        g = jnp.minimum(g, SWIGLU_LIMIT)
        u = jnp.clip(u, -SWIGLU_LIMIT, SWIGLU_LIMIT)
        act = (u + 1.0) * (g * jax.nn.sigmoid(SWIGLU_ALPHA * g))
        _to_tiled(obuf, (slot,), _dot(act.astype(BF16), wd_ref[...]) + bd_ref[...])

    for c in out_copies(i, slot):
        c.start()

    @pl.when(i == n - 1)
    def _():
        for c in out_copies(i - 1, 1 - slot) + out_copies(i, slot):
            c.wait()


def _experts(xs, blk_e, n_used, p):
    n_rows, chunks, _ = xs.shape
    D = chunks * LANES
    bm = MOE_BLOCK
    d_ff = p["w_gate"].shape[2]
    assert n_rows // bm >= 2
    wspec = lambda a, b: pl.BlockSpec((None, a, b), lambda i, be, nu: (be[i], 0, 0))
    by_tile = (n_rows // SUBLANES, SUBLANES, chunks, LANES)
    ys = pl.pallas_call(
        _experts_kernel,
        grid_spec=pltpu.PrefetchScalarGridSpec(
            num_scalar_prefetch=2,
            grid=(n_rows // bm,),
            in_specs=[pl.BlockSpec(memory_space=pl.ANY), wspec(D, d_ff), wspec(1, d_ff), wspec(D, d_ff),
                      wspec(1, d_ff), wspec(d_ff, D), wspec(1, D)],
            out_specs=pl.BlockSpec(memory_space=pl.ANY),
            scratch_shapes=[pltpu.VMEM((2, bm // SUBLANES, chunks, SUBLANES, LANES), F32),
                            pltpu.VMEM((2, bm // SUBLANES, chunks, SUBLANES, LANES), F32),
                            pltpu.SemaphoreType.DMA((2,)), pltpu.SemaphoreType.DMA((2,))],
        ),
        out_shape=jax.ShapeDtypeStruct(by_tile, F32),
        compiler_params=pltpu.CompilerParams(dimension_semantics=("arbitrary",), vmem_limit_bytes=VMEM_LIMIT),
        name="experts",
    )(blk_e, n_used, xs.reshape(by_tile), p["w_gate"], p["b_gate"], p["w_up"], p["b_up"], p["w_down"], p["b_down"])
    return ys.reshape(n_rows, chunks, LANES)


def _combine_kernel(pos_hbm, gates_ref, x1_ref, ys_hbm, out_ref, pos_smem, buf, pos_sem, row_sem):
    tm = x1_ref.shape[0]
    j = pl.program_id(0)
    n = pl.num_programs(0) - 1
    slot = j % 2

    @pl.when(j == 0)
    def _():
        _pos_copy(pos_hbm, pos_smem, pos_sem, 0, 0).start()

    @pl.when(j < n)
    def _():
        _pos_copy(pos_hbm, pos_smem, pos_sem, j, slot).wait()

        @pl.when(j + 1 < n)
        def _():
            _pos_copy(pos_hbm, pos_smem, pos_sem, j + 1, 1 - slot).start()

        for r in range(TOP_K * tm):
            pltpu.make_async_copy(ys_hbm.at[pos_smem[slot, r]], _tile_row(buf, (slot,), r), row_sem.at[slot]).start()

    @pl.when(j >= 1)
    def _():
        done = 1 - slot
        pltpu.make_async_copy(buf.at[done], buf.at[done], row_sem.at[done]).wait()
        bt = tm // SUBLANES
        y = gates_ref[:, 0:1] * _from_tiled(buf, (done,), slice(0, bt))
        for k in range(1, TOP_K):
            y = y + gates_ref[:, k:k + 1] * _from_tiled(buf, (done,), slice(k * bt, (k + 1) * bt))
        out_ref[...] = x1_ref[...] + y


def _combine(x1, gates_tok, ys, pos_tiles):
    T, D = x1.shape
    tm = ROW_TILE
    n = T // tm
    chunks = D // LANES
    lag = lambda i: (jnp.maximum(i - 1, 0), 0)
    return pl.pallas_call(
        _combine_kernel,
        grid=(n + 1,),
        in_specs=[pl.BlockSpec(memory_space=pl.ANY), pl.BlockSpec((tm, TOP_K), lag),
                  pl.BlockSpec((tm, D), lag), pl.BlockSpec(memory_space=pl.ANY)],
        out_specs=pl.BlockSpec((tm, D), lag),
        out_shape=jax.ShapeDtypeStruct((T, D), F32),
        scratch_shapes=[pltpu.SMEM((2, TOP_K * tm), I32),
                        pltpu.VMEM((2, TOP_K * tm // SUBLANES, chunks, SUBLANES, LANES), F32),
                        pltpu.SemaphoreType.DMA((2,)), pltpu.SemaphoreType.DMA((2,))],
        compiler_params=pltpu.CompilerParams(dimension_semantics=("arbitrary",), vmem_limit_bytes=VMEM_LIMIT),
        name="combine",
    )(pos_tiles, gates_tok, x1, ys)


def _route(eidx, rank, tile_counts):
    T = eidx.shape[1]
    bm = MOE_BLOCK
    n_tiles = T // TOKEN_TILE
    tc = tile_counts[:, :, 0]
    counts = jnp.sum(tc, axis=0)
    tile_off = jnp.cumsum(tc, axis=0) - tc
    pcounts = (counts + bm - 1) // bm * bm
    pends = jnp.cumsum(pcounts)
    base = (pends - pcounts)[None, :] + tile_off
    sel = eidx.reshape(TOP_K, n_tiles, 1, TOKEN_TILE) == jnp.arange(N_EXPERTS, dtype=I32)[None, None, :, None]
    pos = jnp.sum(jnp.where(sel, base[None, :, :, None], 0), axis=2).reshape(TOP_K, T) + rank
    n_blocks = (TOP_K * T) // bm + N_EXPERTS
    blk_start = jnp.arange(n_blocks, dtype=I32) * bm
    blk_e = jnp.minimum(jnp.sum((pends[None, :] <= blk_start[:, None]).astype(I32), axis=1), N_EXPERTS - 1)
    n_used = (pends[-1] // bm).astype(I32).reshape(1)
    seg = jnp.concatenate([jnp.zeros((1,), I32), pends.astype(I32)])
    pos_tiles = pos.reshape(TOP_K, T // ROW_TILE, ROW_TILE).transpose(1, 0, 2).reshape(-1)
    return pos_tiles, blk_e, n_used, seg, n_blocks * bm


def _layer(x, p):
    batch, seq, D = x.shape
    x2 = x.reshape(batch * seq, D)
    q, k, v, zu, zv = _in_proj(x2, batch, seq, p)
    o, lse = zip(*[_attention(q[g], k[g], v[g], g) for g in range(N_GROUPS)])
    x1, h2t, gates, eidx, rank, tile_counts = _post_mix(x2, seq, o, lse, zu, zv, p)
    pos_tiles, blk_e, n_used, seg, n_rows = _route(eidx, rank, tile_counts)
    xs = _dispatch(h2t, pos_tiles, seg, n_rows)
    ys = _experts(xs, blk_e, n_used, p)
    out = _combine(x1, gates.T, ys, pos_tiles)
    return out.reshape(batch, seq, D)


def _rope_tables(seq):
    half = ROT_DIM // 2
    inv_freq = jnp.power(ROPE_THETA, -2.0 * jnp.arange(half, dtype=F32) / ROT_DIM)
    ang = jnp.arange(seq, dtype=F32)[:, None] * inv_freq[None, :]
    cos, sin = jnp.cos(ang), jnp.sin(ang)
    zeros = jnp.zeros((seq, HEAD_DIM - ROT_DIM), F32)
    zh = jnp.zeros((seq, half), F32)
    cos_t = jnp.concatenate([cos, cos, zeros + 1.0], axis=1)
    sa = jnp.concatenate([-sin, zh, zeros], axis=1)
    sb = jnp.concatenate([zh, sin, zeros], axis=1)
    rep = LANES // HEAD_DIM
    return tuple(jnp.tile(t, (1, rep)) for t in (cos_t, sa, sb))


def kernel(x_prompt, x_sample, attn_norm_w, w_in, q_norm_w, k_norm_w, sgu_ln_w, sgu_ln_b, sgu_w, sgu_b,
           att_out_norm_w, sgu_out_norm_w, w_out, ffn_norm_w, router_w, router_b,
           w_gate, b_gate, w_up, b_up, w_down, b_down):
    depth = w_in.shape[0]
    n_heads = ATT_WIDTH // HEAD_DIM
    blk = np.arange(MXU_DIM) // HEAD_DIM
    tri = np.arange(TOKEN_TILE)
    outs = []
    for x in (x_prompt, x_sample):
        seq = x.shape[1]
        cos_t, sa, sb = _rope_tables(seq)
        for l in range(depth):
            p = dict(
                attn_norm_w=attn_norm_w[l][None], w_in=w_in[l].astype(BF16),
                q_norm_w=jnp.tile(q_norm_w[l], n_heads)[None], k_norm_w=jnp.tile(k_norm_w[l], n_heads)[None],
                head_ones=jnp.asarray(blk[:, None] == blk[None, :], BF16),
                rope_cos=cos_t, rope_sa=sa, rope_sb=sb,
                sgu_ln_w=sgu_ln_w[l][None], sgu_ln_b=sgu_ln_b[l][None],
                sgu_w=sgu_w[l].reshape(SGU_GROUPS * SGU_CHUNK, SGU_CHUNK).astype(BF16),
                sgu_b=jnp.repeat(sgu_b[l].T, SGU_GROUP_DIM, axis=1),
                att_out_norm_w=att_out_norm_w[l][None], sgu_out_norm_w=sgu_out_norm_w[l][None],
                w_out=w_out[l].astype(BF16), ffn_norm_w=ffn_norm_w[l][None],
                router_wt=router_w[l].T, router_b=router_b[l][:, None],
                tri=jnp.asarray(tri[:, None] < tri[None, :], BF16),
                w_gate=w_gate[l].astype(BF16), b_gate=b_gate[l][:, None, :],
                w_up=w_up[l].astype(BF16), b_up=b_up[l][:, None, :],
                w_down=w_down[l].astype(BF16), b_down=b_down[l][:, None, :],
            )
            x = _layer(x, p)
        outs.append(x)
    return tuple(outs)
```

```python
import functools

import numpy as np
import jax
import jax.numpy as jnp
from jax import lax
from jax.experimental import pallas as pl
from jax.experimental.pallas import tpu as pltpu

F32 = jnp.float32
BF16 = jnp.bfloat16
I32 = jnp.int32

HEAD_DIM = 64
ATT_WINDOWS = ((128, 1), (512, 4), (2048, 16))
N_GROUPS = len(ATT_WINDOWS)
HEADS_PER_GROUP = 4
GROUP_WIDTH = HEADS_PER_GROUP * HEAD_DIM
ATT_WIDTH = N_GROUPS * GROUP_WIDTH
SGU_WIDTH = 256
SGU_GROUP_DIM = 64
SGU_GROUPS = SGU_WIDTH // SGU_GROUP_DIM
SGU_CHUNK = 128
ROT_DIM = HEAD_DIM // 4
ROPE_THETA = 500000.0
N_EXPERTS = 32
TOP_K = 4
SWIGLU_LIMIT = 7.0
SWIGLU_ALPHA = 1.702
NORM_EPS = 1e-6
LN_EPS = 1e-5
NEG_INF = -1e30

LANES = 128
SUBLANES = 8
MXU_DIM = 256
TOKEN_TILE = 512
ATT_Q_BLOCK = 256
ATT_SUB_Q = 128
MOE_BLOCK = 512
ROW_TILE = 256
VMEM_LIMIT = 48 * 1024 * 1024


def _dot(a, b):
    return jnp.dot(a, b, preferred_element_type=F32)


def _dot_nt(a, b, precision=None):
    return lax.dot_general(a, b, (((1,), (1,)), ((), ())), precision=precision, preferred_element_type=F32)


def _rms(x, w):
    return x * lax.rsqrt(jnp.mean(x * x, axis=-1, keepdims=True) + NORM_EPS) * w


def _gelu(x):
    return 0.5 * x * (1.0 + lax.erf(x * np.float32(np.sqrt(0.5))))


def _to_tiled(ref, idx, value):
    rows = value.shape[0]
    for c in range(value.shape[1] // LANES):
        ref[idx + (slice(None), c)] = value[:, c * LANES:(c + 1) * LANES].reshape(rows // SUBLANES, SUBLANES, LANES)


def _from_tiled(ref, idx, row_tiles):
    chunks = ref.shape[-3]
    n = (row_tiles.stop - row_tiles.start) * SUBLANES
    return jnp.concatenate([ref[idx + (row_tiles, c)].reshape(n, LANES) for c in range(chunks)], axis=1)


def _in_proj_kernel(x_ref, anw_ref, w_ref, qnw_ref, knw_ref, ones_ref, cos_ref, sa_ref, sb_ref, lnw_ref, lnb_ref,
                    *refs):
    qkv_refs = refs[:3 * N_GROUPS]
    zu_ref, zv_ref, stage = refs[3 * N_GROUPS:]
    tm = x_ref.shape[0]
    h = _rms(x_ref[...], anw_ref[...]).astype(BF16)
    reps = ATT_WIDTH // LANES
    cos = jnp.concatenate([cos_ref[...]] * reps, axis=1)
    sa = jnp.concatenate([sa_ref[...]] * reps, axis=1)
    sb = jnp.concatenate([sb_ref[...]] * reps, axis=1)
    ones = ones_ref[...]

    def head_norm_rope(t, nw):
        sq = t * t
        hi = sq.astype(BF16)
        lo = (sq - hi.astype(F32)).astype(BF16)
        parts = []
        for j in range(ATT_WIDTH // MXU_DIM):
            sl = slice(j * MXU_DIM, (j + 1) * MXU_DIM)
            parts.append(_dot(hi[:, sl], ones) + _dot(lo[:, sl], ones))
        ssum = jnp.concatenate(parts, axis=1)
        t = t * lax.rsqrt(ssum * (1.0 / HEAD_DIM) + NORM_EPS) * nw
        half = ROT_DIM // 2
        return t * cos + pltpu.roll(t, ATT_WIDTH - half, 1) * sa + pltpu.roll(t, half, 1) * sb

    def emit(which, t):
        n_chunks = ATT_WIDTH // LANES
        per_group = GROUP_WIDTH // LANES
        for c in range(n_chunks):
            stage[which * n_chunks + c] = t[:, c * LANES:(c + 1) * LANES]
        for g, (_, dil) in enumerate(ATT_WINDOWS):
            out = qkv_refs[which * N_GROUPS + g]
            first = which * n_chunks + g * per_group
            for r in range(dil):
                rows = pl.ds(r, tm // dil, stride=dil)
                out[r] = jnp.concatenate([stage[first + c, rows, :] for c in range(per_group)], axis=1).astype(BF16)

    q = head_norm_rope(_dot(h, w_ref[:, 0:ATT_WIDTH]), qnw_ref[...])
    emit(0, q * (HEAD_DIM ** -0.5))
    emit(1, head_norm_rope(_dot(h, w_ref[:, ATT_WIDTH:2 * ATT_WIDTH]), knw_ref[...]))
    emit(2, _dot(h, w_ref[:, 2 * ATT_WIDTH:3 * ATT_WIDTH]))
    z = _dot(h, w_ref[:, 3 * ATT_WIDTH:3 * ATT_WIDTH + 2 * SGU_WIDTH])
    zu_ref[...] = _gelu(z[:, :SGU_WIDTH])
    gv = _gelu(z[:, SGU_WIDTH:])
    mu = jnp.mean(gv, axis=-1, keepdims=True)
    var = jnp.mean(jnp.square(gv - mu), axis=-1, keepdims=True)
    zv_ref[...] = ((gv - mu) * lax.rsqrt(var + LN_EPS) * lnw_ref[...] + lnb_ref[...]).astype(BF16)


def _phase_spec(tm, dil, n_seq_tiles):
    return pl.BlockSpec((None, dil, tm // dil, GROUP_WIDTH), lambda i: (i // n_seq_tiles, 0, i % n_seq_tiles, 0))


def _in_proj(x2, batch, seq, p):
    T, D = x2.shape
    tm = TOKEN_TILE
    n_seq_tiles = seq // tm
    in_width = p["w_in"].shape[1]
    const = lambda shape: pl.BlockSpec(shape, lambda i: (0,) * len(shape))
    rope = pl.BlockSpec((tm, LANES), lambda i: (i % n_seq_tiles, 0))
    row = lambda w: pl.BlockSpec((tm, w), lambda i: (i, 0))
    qkv_specs = [_phase_spec(tm, dil, n_seq_tiles) for _ in range(3) for _, dil in ATT_WINDOWS]
    qkv_shapes = [jax.ShapeDtypeStruct((batch, dil, seq // dil, GROUP_WIDTH), BF16)
                  for _ in range(3) for _, dil in ATT_WINDOWS]
    outs = pl.pallas_call(
        _in_proj_kernel,
        grid=(T // tm,),
        in_specs=[row(D), const((1, D)), const((D, in_width)), const((1, ATT_WIDTH)), const((1, ATT_WIDTH)),
                  const((MXU_DIM, MXU_DIM)), rope, rope, rope, const((1, SGU_WIDTH)), const((1, SGU_WIDTH))],
        out_specs=qkv_specs + [row(SGU_WIDTH), row(SGU_WIDTH)],
        out_shape=qkv_shapes + [jax.ShapeDtypeStruct((T, SGU_WIDTH), F32), jax.ShapeDtypeStruct((T, SGU_WIDTH), BF16)],
        scratch_shapes=[pltpu.VMEM((3 * ATT_WIDTH // LANES, tm, LANES), F32)],
        compiler_params=pltpu.CompilerParams(dimension_semantics=("arbitrary",), vmem_limit_bytes=VMEM_LIMIT),
        name="in_proj",
    )(x2, p["attn_norm_w"], p["w_in"], p["q_norm_w"], p["k_norm_w"], p["head_ones"],
      p["rope_cos"], p["rope_sa"], p["rope_sb"], p["sgu_ln_w"], p["sgu_ln_b"])
    q, k, v = (outs[i * N_GROUPS:(i + 1) * N_GROUPS] for i in range(3))
    return q, k, v, outs[-2], outs[-1]


def _attn_kernel(q_ref, kp_ref, km_ref, kn_ref, vp_ref, vm_ref, vn_ref, o_ref, lse_ref, *, sub_len, steps):
    lq = q_ref.shape[0]
    sq = min(lq, ATT_SUB_Q)
    lk = sq + 2 * steps
    nh = HEADS_PER_GROUP
    lb = pl.program_id(2)
    kk = jnp.concatenate([kp_ref[...], km_ref[...], kn_ref[...]], axis=0)
    vv = jnp.concatenate([vp_ref[...], vm_ref[...], vn_ref[...]], axis=0)
    head = lax.broadcasted_iota(I32, (1, GROUP_WIDTH), 1) // HEAD_DIM
    hm_f = [(head == h).astype(F32) for h in range(nh)]
    hm_b = [m.astype(BF16) for m in hm_f]
    qi = lax.broadcasted_iota(I32, (sq, 1), 0)
    kj = lax.broadcasted_iota(I32, (sq, lk), 1)
    for j in range(lq // sq):
        first = lb * lq + j * sq
        lo = jnp.maximum(qi, steps - first)
        hi = jnp.minimum(qi + 2 * steps, sub_len - 1 + steps - first)
        mask = ((kj - lo).astype(jnp.uint32) <= (hi - lo).astype(jnp.uint32))[None]
        qj = q_ref[j * sq:(j + 1) * sq, :]
        qs = jnp.concatenate([qj * hm_b[h] for h in range(nh)], axis=0)
        s = _dot_nt(qs, kk[j * sq:j * sq + lk]).reshape(nh, sq, lk)
        s = jnp.where(mask, s, NEG_INF)
        m = jnp.max(s, axis=-1, keepdims=True)
        pr = jnp.exp(s - m)
        den = jnp.sum(pr, axis=-1, keepdims=True)
        oh = _dot(pr.reshape(nh * sq, lk).astype(BF16), vv[j * sq:j * sq + lk]).reshape(nh, sq, GROUP_WIDTH) / den
        lh = m + jnp.log(den)
        o_ref[j * sq:(j + 1) * sq, :] = sum(oh[h] * hm_f[h] for h in range(nh))
        lse_ref[j * sq:(j + 1) * sq, :] = sum(lh[h] * hm_f[h] for h in range(nh))


def _attention(q, k, v, group):
    window, dil = ATT_WINDOWS[group]
    steps = window // (2 * dil)
    batch, _, sub_len, _ = q.shape
    assert sub_len % steps == 0
    lq = min(ATT_Q_BLOCK, sub_len)
    assert lq % steps == 0 and sub_len % lq == 0
    per_q = lq // steps
    n_halo = sub_len // steps
    main = pl.BlockSpec((None, None, lq, GROUP_WIDTH), lambda b, r, i: (b, r, i, 0))
    prev = pl.BlockSpec((None, None, steps, GROUP_WIDTH), lambda b, r, i: (b, r, jnp.maximum(i * per_q - 1, 0), 0))
    nxt = pl.BlockSpec((None, None, steps, GROUP_WIDTH),
                       lambda b, r, i: (b, r, jnp.minimum((i + 1) * per_q, n_halo - 1), 0))
    return pl.pallas_call(
        functools.partial(_attn_kernel, sub_len=sub_len, steps=steps),
        grid=(batch, dil, sub_len // lq),
        in_specs=[main, prev, main, nxt, prev, main, nxt],
        out_specs=[main, main],
        out_shape=[jax.ShapeDtypeStruct(q.shape, F32)] * 2,
        compiler_params=pltpu.CompilerParams(dimension_semantics=("arbitrary",) * 3, vmem_limit_bytes=VMEM_LIMIT),
        name=f"attention_g{group}",
    )(q, k, k, k, v, v, v)


def _post_mix_kernel(x_ref, *refs):
    o_refs = refs[:N_GROUPS]
    l_refs = refs[N_GROUPS:2 * N_GROUPS]
    (zu_ref, zv_ref, sguw_ref, sgub_ref, aonw_ref, sonw_ref, wout_ref, fnw_ref, rwt_ref, rb_ref, tri_ref,
     x1_ref, h2_ref, gates_ref, eidx_ref, rank_ref, cnt_ref, stage) = refs[2 * N_GROUPS:]
    tm = x_ref.shape[0]

    def token_order(ref, slot):
        dil = ref.shape[0]
        if dil == 1:
            return ref[0]
        per_group = GROUP_WIDTH // LANES
        for r in range(dil):
            for c in range(per_group):
                stage[slot * per_group + c, pl.ds(r, tm // dil, stride=dil), :] = ref[r, :, c * LANES:(c + 1) * LANES]
        return jnp.concatenate([stage[slot * per_group + c] for c in range(per_group)], axis=1)

    o = [token_order(ref, g) for g, ref in enumerate(o_refs)]
    l = [token_order(ref, N_GROUPS + g) for g, ref in enumerate(l_refs)]
    lmax = jnp.maximum(jnp.maximum(l[0], l[1]), l[2])
    e = [jnp.exp(lg - lmax) for lg in l]
    esum = e[0] + e[1] + e[2]
    att = (e[0] / esum) * o[0] + (e[1] / esum) * o[1] + (e[2] / esum) * o[2]
    att_n = _rms(att, aonw_ref[...]).astype(BF16)
    cgrp = lax.broadcasted_iota(I32, (1, SGU_WIDTH), 1) // SGU_GROUP_DIM
    sguw = sguw_ref[...]
    gates = []
    for c in range(tm // SGU_CHUNK):
        r = _dot(sguw, zv_ref[c * SGU_CHUNK:(c + 1) * SGU_CHUNK, :])
        g = sgub_ref[...]
        for grp in range(SGU_GROUPS):
            g = g + r[grp * SGU_CHUNK:(grp + 1) * SGU_CHUNK, :] * (cgrp == grp).astype(F32)
        gates.append(g)
    sgu = zu_ref[...] * jnp.concatenate(gates, axis=0)
    sgu_n = _rms(sgu, sonw_ref[...]).astype(BF16)
    x1 = x_ref[...] + _dot(att_n, wout_ref[0:GROUP_WIDTH, :]) + _dot(sgu_n, wout_ref[GROUP_WIDTH:, :])
    x1_ref[...] = x1
    h2 = _rms(x1, fnw_ref[...])
    _to_tiled(h2_ref, (), h2)
    logits = _dot_nt(rwt_ref[...], h2, precision=lax.Precision.HIGHEST) + rb_ref[...]
    eiota = lax.broadcasted_iota(I32, (N_EXPERTS, tm), 0)
    vals, idxs = [], []
    for _ in range(TOP_K):
        m = jnp.max(logits, axis=0, keepdims=True)
        idx = jnp.min(jnp.where(logits == m, eiota, N_EXPERTS), axis=0, keepdims=True)
        vals.append(m)
        idxs.append(idx)
        logits = jnp.where(eiota == idx, -jnp.inf, logits)
    exps = [jnp.exp(v - vals[0]) for v in vals]
    den = exps[0] + exps[1] + exps[2] + exps[3]
    gates_ref[...] = jnp.concatenate([ex / den for ex in exps], axis=0)
    eidx_ref[...] = jnp.concatenate(idxs, axis=0)
    onehots = [(eiota == idx).astype(F32) for idx in idxs]
    chosen = onehots[0] + onehots[1] + onehots[2] + onehots[3]
    before = _dot(chosen.astype(BF16), tri_ref[...])
    rank_ref[...] = jnp.concatenate(
        [jnp.sum(oh * before, axis=0, keepdims=True) for oh in onehots], axis=0).astype(I32)
    cnt_ref[...] = jnp.broadcast_to(jnp.sum(chosen, axis=1, keepdims=True), (N_EXPERTS, LANES)).astype(I32)


def _post_mix(x2, seq, o, lse, zu, zv, p):
    T, D = x2.shape
    tm = TOKEN_TILE
    n_tiles = T // tm
    n_seq_tiles = seq // tm
    chunks = D // LANES
    const = lambda shape: pl.BlockSpec(shape, lambda i: (0,) * len(shape))
    row = lambda w: pl.BlockSpec((tm, w), lambda i: (i, 0))
    colt = pl.BlockSpec((TOP_K, tm), lambda i: (0, i))
    phase = [_phase_spec(tm, dil, n_seq_tiles) for _, dil in ATT_WINDOWS]
    return pl.pallas_call(
        _post_mix_kernel,
        grid=(n_tiles,),
        in_specs=[row(D)] + phase + phase + [row(SGU_WIDTH), row(SGU_WIDTH),
                  const((SGU_GROUPS * SGU_CHUNK, SGU_CHUNK)), const((SGU_CHUNK, SGU_WIDTH)),
                  const((1, GROUP_WIDTH)), const((1, SGU_WIDTH)), const((GROUP_WIDTH + SGU_WIDTH, D)),
                  const((1, D)), const((N_EXPERTS, D)), const((N_EXPERTS, 1)), const((tm, tm))],
        out_specs=[row(D), pl.BlockSpec((tm // SUBLANES, chunks, SUBLANES, LANES), lambda i: (i, 0, 0, 0)),
                   colt, colt, colt, pl.BlockSpec((None, N_EXPERTS, LANES), lambda i: (i, 0, 0))],
        out_shape=[jax.ShapeDtypeStruct((T, D), F32),
                   jax.ShapeDtypeStruct((T // SUBLANES, chunks, SUBLANES, LANES), F32),
                   jax.ShapeDtypeStruct((TOP_K, T), F32), jax.ShapeDtypeStruct((TOP_K, T), I32),
                   jax.ShapeDtypeStruct((TOP_K, T), I32), jax.ShapeDtypeStruct((n_tiles, N_EXPERTS, LANES), I32)],
        scratch_shapes=[pltpu.VMEM((2 * N_GROUPS * GROUP_WIDTH // LANES, tm, LANES), F32)],
        compiler_params=pltpu.CompilerParams(dimension_semantics=("arbitrary",), vmem_limit_bytes=VMEM_LIMIT),
        name="post_mix",
    )(x2, *o, *lse, zu, zv, p["sgu_w"], p["sgu_b"], p["att_out_norm_w"],
      p["sgu_out_norm_w"], p["w_out"], p["ffn_norm_w"], p["router_wt"], p["router_b"], p["tri"])


def _pos_copy(pos_hbm, pos_smem, sem, tile, slot):
    n = pos_smem.shape[1]
    return pltpu.make_async_copy(pos_hbm.at[pl.ds(pl.multiple_of(tile * n, n), n)], pos_smem.at[slot], sem.at[slot])


def _tile_row(ref, idx, row):
    return ref.at[idx + (row // SUBLANES, slice(None), row % SUBLANES, slice(None))]


def _dispatch_kernel(seg_ref, pos_hbm, h_ref, xs_hbm, pos_smem, zeros, pos_sem, row_sem, zero_sem):
    tm = h_ref.shape[0] * SUBLANES
    bm = zeros.shape[0]
    i = pl.program_id(0)
    n = pl.num_programs(0)
    slot = i % 2

    @pl.when(i == 0)
    def _():
        _pos_copy(pos_hbm, pos_smem, pos_sem, 0, 0).start()
        zeros[...] = jnp.zeros_like(zeros)

        def tail_copy(e):
            return pltpu.make_async_copy(
                zeros, xs_hbm.at[pl.ds(pl.multiple_of(seg_ref[e + 1] - bm, bm), bm)], zero_sem)

        for e in range(N_EXPERTS):
            pl.when(seg_ref[e + 1] > seg_ref[e])(lambda e=e: tail_copy(e).start())
        for e in range(N_EXPERTS):
            pl.when(seg_ref[e + 1] > seg_ref[e])(lambda e=e: tail_copy(e).wait())

        def fill_block(b, carry):
            fill = pltpu.make_async_copy(zeros, xs_hbm.at[pl.ds(pl.multiple_of(b * bm, bm), bm)], zero_sem)
            fill.start()
            fill.wait()
            return carry

        lax.fori_loop(seg_ref[N_EXPERTS] // bm, xs_hbm.shape[0] // bm, fill_block, 0)

    _pos_copy(pos_hbm, pos_smem, pos_sem, i, slot).wait()

    @pl.when(i + 1 < n)
    def _():
        _pos_copy(pos_hbm, pos_smem, pos_sem, i + 1, 1 - slot).start()

    for r in range(tm):
        src = _tile_row(h_ref, (), r)
        for k in range(TOP_K):
            pltpu.make_async_copy(src, xs_hbm.at[pos_smem[slot, k * tm + r]], row_sem).start(priority=k % 2)
    for _ in range(TOP_K * tm // bm):
        pltpu.make_async_copy(zeros, xs_hbm.at[pl.ds(0, bm)], row_sem).wait()


def _dispatch(h2t, pos_tiles, seg, n_rows):
    _, chunks, _, _ = h2t.shape
    tm = ROW_TILE
    bm = MOE_BLOCK
    return pl.pallas_call(
        _dispatch_kernel,
        grid_spec=pltpu.PrefetchScalarGridSpec(
            num_scalar_prefetch=1,
            grid=(h2t.shape[0] * SUBLANES // tm,),
            in_specs=[pl.BlockSpec(memory_space=pl.ANY),
                      pl.BlockSpec((tm // SUBLANES, chunks, SUBLANES, LANES), lambda i, seg: (i, 0, 0, 0))],
            out_specs=pl.BlockSpec(memory_space=pl.ANY),
            scratch_shapes=[pltpu.SMEM((2, TOP_K * tm), I32), pltpu.VMEM((bm, chunks, LANES), F32),
                            pltpu.SemaphoreType.DMA((2,)), pltpu.SemaphoreType.DMA, pltpu.SemaphoreType.DMA],
        ),
        out_shape=jax.ShapeDtypeStruct((n_rows, chunks, LANES), F32),
        compiler_params=pltpu.CompilerParams(dimension_semantics=("arbitrary",), vmem_limit_bytes=VMEM_LIMIT),
        name="dispatch",
    )(seg, pos_tiles, h2t)


def _experts_kernel(blk_e_ref, n_used_ref, xs_hbm, wg_ref, bg_ref, wu_ref, bu_ref, wd_ref, bd_ref, ys_hbm,
                    xbuf, obuf, in_sem, out_sem):
    del blk_e_ref
    bt = xbuf.shape[1]
    i = pl.program_id(0)
    n = pl.num_programs(0)
    n_used = n_used_ref[0]
    slot = i % 2

    def in_copies(blk, s):
        return [pltpu.make_async_copy(xs_hbm.at[pl.ds(blk * bt, bt), j], xbuf.at[s, :, :, j, :], in_sem.at[s])
                for j in range(SUBLANES)]

    def out_copies(blk, s):
        return [pltpu.make_async_copy(obuf.at[s, :, :, j, :], ys_hbm.at[pl.ds(blk * bt, bt), j], out_sem.at[s])
                for j in range(SUBLANES)]

    @pl.when(i == 0)
    def _():
        for c in in_copies(0, 0):
            c.start()

    @pl.when(i + 1 < n_used)
    def _():
        for c in in_copies(i + 1, 1 - slot):
            c.start()

    @pl.when(i >= 2)
    def _():
        for c in out_copies(i - 2, slot):
            c.wait()

    @pl.when(i >= n_used)
    def _():
        obuf[slot] = jnp.zeros(obuf.shape[1:], F32)

    @pl.when(i < n_used)
    def _():
        for c in in_copies(i, slot):
            c.wait()
        x = _from_tiled(xbuf, (slot,), slice(0, bt)).astype(BF16)
        g = _dot(x, wg_ref[...]) + bg_ref[...]
        u = _dot(x, wu_ref[...]) + bu_ref[...]
        g = jnp.minimum(g, SWIGLU_LIMIT)
        u = jnp.clip(u, -SWIGLU_LIMIT, SWIGLU_LIMIT)
        act = (u + 1.0) * (g * jax.nn.sigmoid(SWIGLU_ALPHA * g))
        _to_tiled(obuf, (slot,), _dot(act.astype(BF16), wd_ref[...]) + bd_ref[...])

    for c in out_copies(i, slot):
        c.start()

    @pl.when(i == n - 1)
    def _():
        for c in out_copies(i - 1, 1 - slot) + out_copies(i, slot):
            c.wait()


def _experts(xs, blk_e, n_used, p):
    n_rows, chunks, _ = xs.shape
    D = chunks * LANES
    bm = MOE_BLOCK
    d_ff = p["w_gate"].shape[2]
    assert n_rows // bm >= 2
    wspec = lambda a, b: pl.BlockSpec((None, a, b), lambda i, be, nu: (be[i], 0, 0))
    by_tile = (n_rows // SUBLANES, SUBLANES, chunks, LANES)
    ys = pl.pallas_call(
        _experts_kernel,
        grid_spec=pltpu.PrefetchScalarGridSpec(
            num_scalar_prefetch=2,
            grid=(n_rows // bm,),
            in_specs=[pl.BlockSpec(memory_space=pl.ANY), wspec(D, d_ff), wspec(1, d_ff), wspec(D, d_ff),
                      wspec(1, d_ff), wspec(d_ff, D), wspec(1, D)],
            out_specs=pl.BlockSpec(memory_space=pl.ANY),
            scratch_shapes=[pltpu.VMEM((2, bm // SUBLANES, chunks, SUBLANES, LANES), F32),
                            pltpu.VMEM((2, bm // SUBLANES, chunks, SUBLANES, LANES), F32),
                            pltpu.SemaphoreType.DMA((2,)), pltpu.SemaphoreType.DMA((2,))],
        ),
        out_shape=jax.ShapeDtypeStruct(by_tile, F32),
        compiler_params=pltpu.CompilerParams(dimension_semantics=("arbitrary",), vmem_limit_bytes=VMEM_LIMIT),
        name="experts",
    )(blk_e, n_used, xs.reshape(by_tile), p["w_gate"], p["b_gate"], p["w_up"], p["b_up"], p["w_down"], p["b_down"])
    return ys.reshape(n_rows, chunks, LANES)


def _combine_kernel(pos_hbm, gates_ref, x1_ref, ys_hbm, out_ref, pos_smem, buf, pos_sem, row_sem):
    tm = x1_ref.shape[0]
    j = pl.program_id(0)
    n = pl.num_programs(0) - 1
    slot = j % 2

    @pl.when(j == 0)
    def _():
        _pos_copy(pos_hbm, pos_smem, pos_sem, 0, 0).start()

    @pl.when(j < n)
    def _():
        _pos_copy(pos_hbm, pos_smem, pos_sem, j, slot).wait()

        @pl.when(j + 1 < n)
        def _():
            _pos_copy(pos_hbm, pos_smem, pos_sem, j + 1, 1 - slot).start()

        for r in range(TOP_K * tm):
            pltpu.make_async_copy(ys_hbm.at[pos_smem[slot, r]], _tile_row(buf, (slot,), r),
                                  row_sem.at[slot]).start(priority=r % 2)

    @pl.when(j >= 1)
    def _():
        done = 1 - slot
        pltpu.make_async_copy(buf.at[done], buf.at[done], row_sem.at[done]).wait()
        bt = tm // SUBLANES
        y = gates_ref[:, 0:1] * _from_tiled(buf, (done,), slice(0, bt))
        for k in range(1, TOP_K):
            y = y + gates_ref[:, k:k + 1] * _from_tiled(buf, (done,), slice(k * bt, (k + 1) * bt))
        out_ref[...] = x1_ref[...] + y


def _combine(x1, gates_tok, ys, pos_tiles):
    T, D = x1.shape
    tm = ROW_TILE
    n = T // tm
    chunks = D // LANES
    lag = lambda i: (jnp.maximum(i - 1, 0), 0)
    return pl.pallas_call(
        _combine_kernel,
        grid=(n + 1,),
        in_specs=[pl.BlockSpec(memory_space=pl.ANY), pl.BlockSpec((tm, TOP_K), lag),
                  pl.BlockSpec((tm, D), lag), pl.BlockSpec(memory_space=pl.ANY)],
        out_specs=pl.BlockSpec((tm, D), lag),
        out_shape=jax.ShapeDtypeStruct((T, D), F32),
        scratch_shapes=[pltpu.SMEM((2, TOP_K * tm), I32),
                        pltpu.VMEM((2, TOP_K * tm // SUBLANES, chunks, SUBLANES, LANES), F32),
                        pltpu.SemaphoreType.DMA((2,)), pltpu.SemaphoreType.DMA((2,))],
        compiler_params=pltpu.CompilerParams(dimension_semantics=("arbitrary",), vmem_limit_bytes=VMEM_LIMIT),
        name="combine",
    )(pos_tiles, gates_tok, x1, ys)


def _route(eidx, rank, tile_counts):
    T = eidx.shape[1]
    bm = MOE_BLOCK
    n_tiles = T // TOKEN_TILE
    tc = tile_counts[:, :, 0]
    counts = jnp.sum(tc, axis=0)
    tile_off = jnp.cumsum(tc, axis=0) - tc
    pcounts = (counts + bm - 1) // bm * bm
    pends = jnp.cumsum(pcounts)
    base = (pends - pcounts)[None, :] + tile_off
    sel = eidx.reshape(TOP_K, n_tiles, 1, TOKEN_TILE) == jnp.arange(N_EXPERTS, dtype=I32)[None, None, :, None]
    pos = jnp.sum(jnp.where(sel, base[None, :, :, None], 0), axis=2).reshape(TOP_K, T) + rank
    n_blocks = (TOP_K * T) // bm + N_EXPERTS
    blk_start = jnp.arange(n_blocks, dtype=I32) * bm
    blk_e = jnp.minimum(jnp.sum((pends[None, :] <= blk_start[:, None]).astype(I32), axis=1), N_EXPERTS - 1)
    n_used = (pends[-1] // bm).astype(I32).reshape(1)
    seg = jnp.concatenate([jnp.zeros((1,), I32), pends.astype(I32)])
    pos_tiles = pos.reshape(TOP_K, T // ROW_TILE, ROW_TILE).transpose(1, 0, 2).reshape(-1)
    return pos_tiles, blk_e, n_used, seg, n_blocks * bm


def _layer(x, p):
    batch, seq, D = x.shape
    x2 = x.reshape(batch * seq, D)
    q, k, v, zu, zv = _in_proj(x2, batch, seq, p)
    o, lse = zip(*[_attention(q[g], k[g], v[g], g) for g in range(N_GROUPS)])
    x1, h2t, gates, eidx, rank, tile_counts = _post_mix(x2, seq, o, lse, zu, zv, p)
    pos_tiles, blk_e, n_used, seg, n_rows = _route(eidx, rank, tile_counts)
    xs = _dispatch(h2t, pos_tiles, seg, n_rows)
    ys = _experts(xs, blk_e, n_used, p)
    out = _combine(x1, gates.T, ys, pos_tiles)
    return out.reshape(batch, seq, D)


def _rope_tables(seq):
    half = ROT_DIM // 2
    inv_freq = jnp.power(ROPE_THETA, -2.0 * jnp.arange(half, dtype=F32) / ROT_DIM)
    ang = jnp.arange(seq, dtype=F32)[:, None] * inv_freq[None, :]
    cos, sin = jnp.cos(ang), jnp.sin(ang)
    zeros = jnp.zeros((seq, HEAD_DIM - ROT_DIM), F32)
    zh = jnp.zeros((seq, half), F32)
    cos_t = jnp.concatenate([cos, cos, zeros + 1.0], axis=1)
    sa = jnp.concatenate([-sin, zh, zeros], axis=1)
    sb = jnp.concatenate([zh, sin, zeros], axis=1)
    rep = LANES // HEAD_DIM
    return tuple(jnp.tile(t, (1, rep)) for t in (cos_t, sa, sb))


def kernel(x_prompt, x_sample, attn_norm_w, w_in, q_norm_w, k_norm_w, sgu_ln_w, sgu_ln_b, sgu_w, sgu_b,
           att_out_norm_w, sgu_out_norm_w, w_out, ffn_norm_w, router_w, router_b,
           w_gate, b_gate, w_up, b_up, w_down, b_down):
    depth = w_in.shape[0]
    n_heads = ATT_WIDTH // HEAD_DIM
    blk = np.arange(MXU_DIM) // HEAD_DIM
    tri = np.arange(TOKEN_TILE)
    outs = []
    for x in (x_prompt, x_sample):
        seq = x.shape[1]
        cos_t, sa, sb = _rope_tables(seq)
        for l in range(depth):
            p = dict(
                attn_norm_w=attn_norm_w[l][None], w_in=w_in[l].astype(BF16),
                q_norm_w=jnp.tile(q_norm_w[l], n_heads)[None], k_norm_w=jnp.tile(k_norm_w[l], n_heads)[None],
                head_ones=jnp.asarray(blk[:, None] == blk[None, :], BF16),
                rope_cos=cos_t, rope_sa=sa, rope_sb=sb,
                sgu_ln_w=sgu_ln_w[l][None], sgu_ln_b=sgu_ln_b[l][None],
                sgu_w=sgu_w[l].reshape(SGU_GROUPS * SGU_CHUNK, SGU_CHUNK).astype(BF16),
                sgu_b=jnp.repeat(sgu_b[l].T, SGU_GROUP_DIM, axis=1),
                att_out_norm_w=att_out_norm_w[l][None], sgu_out_norm_w=sgu_out_norm_w[l][None],
                w_out=w_out[l].astype(BF16), ffn_norm_w=ffn_norm_w[l][None],
                router_wt=router_w[l].T, router_b=router_b[l][:, None],
                tri=jnp.asarray(tri[:, None] < tri[None, :], BF16),
                w_gate=w_gate[l].astype(BF16), b_gate=b_gate[l][:, None, :],
                w_up=w_up[l].astype(BF16), b_up=b_up[l][:, None, :],
                w_down=w_down[l].astype(BF16), b_down=b_down[l][:, None, :],
            )
            x = _layer(x, p)
        outs.append(x)
    return tuple(outs)
```

```python
import functools

import numpy as np
import jax
import jax.numpy as jnp
from jax import lax
from jax.experimental import pallas as pl
from jax.experimental.pallas import tpu as pltpu

F32 = jnp.float32
BF16 = jnp.bfloat16
I32 = jnp.int32

HEAD_DIM = 64
ATT_WINDOWS = ((128, 1), (512, 4), (2048, 16))
N_GROUPS = len(ATT_WINDOWS)
HEADS_PER_GROUP = 4
GROUP_WIDTH = HEADS_PER_GROUP * HEAD_DIM
ATT_WIDTH = N_GROUPS * GROUP_WIDTH
SGU_WIDTH = 256
SGU_GROUP_DIM = 64
SGU_GROUPS = SGU_WIDTH // SGU_GROUP_DIM
SGU_CHUNK = 128
ROT_DIM = HEAD_DIM // 4
ROPE_THETA = 500000.0
N_EXPERTS = 32
TOP_K = 4
SWIGLU_LIMIT = 7.0
SWIGLU_ALPHA = 1.702
NORM_EPS = 1e-6
LN_EPS = 1e-5
NEG_INF = -1e30

LANES = 128
SUBLANES = 8
MXU_DIM = 256
TOKEN_TILE = 512
ATT_Q_BLOCK = 256
ATT_SUB_Q = 128
MOE_BLOCK = 512
ROW_TILE = 256
VMEM_LIMIT = 48 * 1024 * 1024


def _dot(a, b):
    return jnp.dot(a, b, preferred_element_type=F32)


def _dot_nt(a, b, precision=None):
    return lax.dot_general(a, b, (((1,), (1,)), ((), ())), precision=precision, preferred_element_type=F32)


def _rms(x, w):
    return x * lax.rsqrt(jnp.mean(x * x, axis=-1, keepdims=True) + NORM_EPS) * w


def _gelu(x):
    return 0.5 * x * (1.0 + lax.erf(x * np.float32(np.sqrt(0.5))))


def _to_tiled(ref, idx, value):
    rows = value.shape[0]
    for c in range(value.shape[1] // LANES):
        ref[idx + (slice(None), c)] = value[:, c * LANES:(c + 1) * LANES].reshape(rows // SUBLANES, SUBLANES, LANES)


def _from_tiled(ref, idx, row_tiles):
    chunks = ref.shape[-3]
    n = (row_tiles.stop - row_tiles.start) * SUBLANES
    return jnp.concatenate([ref[idx + (row_tiles, c)].reshape(n, LANES) for c in range(chunks)], axis=1)


def _in_proj_kernel(x_ref, anw_ref, w_ref, qnw_ref, knw_ref, ones_ref, cos_ref, sa_ref, sb_ref, lnw_ref, lnb_ref,
                    *refs):
    qkv_refs = refs[:3 * N_GROUPS]
    zu_ref, zv_ref, stage = refs[3 * N_GROUPS:]
    tm = x_ref.shape[0]
    h = _rms(x_ref[...], anw_ref[...]).astype(BF16)
    reps = ATT_WIDTH // LANES
    cos = jnp.concatenate([cos_ref[...]] * reps, axis=1)
    sa = jnp.concatenate([sa_ref[...]] * reps, axis=1)
    sb = jnp.concatenate([sb_ref[...]] * reps, axis=1)
    ones = ones_ref[...]

    def head_norm_rope(t, nw):
        sq = (t * t).astype(BF16)
        parts = []
        for j in range(ATT_WIDTH // MXU_DIM):
            sl = slice(j * MXU_DIM, (j + 1) * MXU_DIM)
            parts.append(_dot(sq[:, sl], ones))
        ssum = jnp.concatenate(parts, axis=1)
        t = t * lax.rsqrt(ssum * (1.0 / HEAD_DIM) + NORM_EPS) * nw
        half = ROT_DIM // 2
        return t * cos + pltpu.roll(t, ATT_WIDTH - half, 1) * sa + pltpu.roll(t, half, 1) * sb

    def emit(which, t):
        n_chunks = ATT_WIDTH // LANES
        per_group = GROUP_WIDTH // LANES
        for c in range(n_chunks):
            stage[which * n_chunks + c] = t[:, c * LANES:(c + 1) * LANES]
        for g, (_, dil) in enumerate(ATT_WINDOWS):
            out = qkv_refs[which * N_GROUPS + g]
            first = which * n_chunks + g * per_group
            for r in range(dil):
                rows = pl.ds(r, tm // dil, stride=dil)
                out[r] = jnp.concatenate([stage[first + c, rows, :] for c in range(per_group)], axis=1).astype(BF16)

    q = head_norm_rope(_dot(h, w_ref[:, 0:ATT_WIDTH]), qnw_ref[...])
    emit(0, q * (HEAD_DIM ** -0.5))
    emit(1, head_norm_rope(_dot(h, w_ref[:, ATT_WIDTH:2 * ATT_WIDTH]), knw_ref[...]))
    emit(2, _dot(h, w_ref[:, 2 * ATT_WIDTH:3 * ATT_WIDTH]))
    z = _dot(h, w_ref[:, 3 * ATT_WIDTH:3 * ATT_WIDTH + 2 * SGU_WIDTH])
    zu_ref[...] = _gelu(z[:, :SGU_WIDTH])
    gv = _gelu(z[:, SGU_WIDTH:])
    mu = jnp.mean(gv, axis=-1, keepdims=True)
    var = jnp.mean(jnp.square(gv - mu), axis=-1, keepdims=True)
    zv_ref[...] = ((gv - mu) * lax.rsqrt(var + LN_EPS) * lnw_ref[...] + lnb_ref[...]).astype(BF16)


def _phase_spec(tm, dil, n_seq_tiles):
    return pl.BlockSpec((None, dil, tm // dil, GROUP_WIDTH), lambda i: (i // n_seq_tiles, 0, i % n_seq_tiles, 0))


def _in_proj(x2, batch, seq, p):
    T, D = x2.shape
    tm = TOKEN_TILE
    n_seq_tiles = seq // tm
    in_width = p["w_in"].shape[1]
    const = lambda shape: pl.BlockSpec(shape, lambda i: (0,) * len(shape))
    rope = pl.BlockSpec((tm, LANES), lambda i: (i % n_seq_tiles, 0))
    row = lambda w: pl.BlockSpec((tm, w), lambda i: (i, 0))
    qkv_specs = [_phase_spec(tm, dil, n_seq_tiles) for _ in range(3) for _, dil in ATT_WINDOWS]
    qkv_shapes = [jax.ShapeDtypeStruct((batch, dil, seq // dil, GROUP_WIDTH), BF16)
                  for _ in range(3) for _, dil in ATT_WINDOWS]
    outs = pl.pallas_call(
        _in_proj_kernel,
        grid=(T // tm,),
        in_specs=[row(D), const((1, D)), const((D, in_width)), const((1, ATT_WIDTH)), const((1, ATT_WIDTH)),
                  const((MXU_DIM, MXU_DIM)), rope, rope, rope, const((1, SGU_WIDTH)), const((1, SGU_WIDTH))],
        out_specs=qkv_specs + [row(SGU_WIDTH), row(SGU_WIDTH)],
        out_shape=qkv_shapes + [jax.ShapeDtypeStruct((T, SGU_WIDTH), F32), jax.ShapeDtypeStruct((T, SGU_WIDTH), BF16)],
        scratch_shapes=[pltpu.VMEM((3 * ATT_WIDTH // LANES, tm, LANES), F32)],
        compiler_params=pltpu.CompilerParams(dimension_semantics=("arbitrary",), vmem_limit_bytes=VMEM_LIMIT),
        name="in_proj",
    )(x2, p["attn_norm_w"], p["w_in"], p["q_norm_w"], p["k_norm_w"], p["head_ones"],
      *p["rope"][seq], p["sgu_ln_w"], p["sgu_ln_b"])
    q, k, v = (outs[i * N_GROUPS:(i + 1) * N_GROUPS] for i in range(3))
    return q, k, v, outs[-2], outs[-1]


def _attn_kernel(q_ref, kp_ref, km_ref, kn_ref, vp_ref, vm_ref, vn_ref, o_ref, lse_ref, *, sub_len, steps):
    lq = q_ref.shape[0]
    sq = min(lq, ATT_SUB_Q)
    lk = sq + 2 * steps
    nh = HEADS_PER_GROUP
    lb = pl.program_id(2)
    kk = jnp.concatenate([kp_ref[...], km_ref[...], kn_ref[...]], axis=0)
    vv = jnp.concatenate([vp_ref[...], vm_ref[...], vn_ref[...]], axis=0)
    head = lax.broadcasted_iota(I32, (1, GROUP_WIDTH), 1) // HEAD_DIM
    hm_f = [(head == h).astype(F32) for h in range(nh)]
    hm_b = [m.astype(BF16) for m in hm_f]
    qi = lax.broadcasted_iota(I32, (sq, 1), 0)
    kj = lax.broadcasted_iota(I32, (sq, lk), 1)
    for j in range(lq // sq):
        first = lb * lq + j * sq
        lo = jnp.maximum(qi, steps - first)
        hi = jnp.minimum(qi + 2 * steps, sub_len - 1 + steps - first)
        mask = ((kj - lo).astype(jnp.uint32) <= (hi - lo).astype(jnp.uint32))[None]
        qj = q_ref[j * sq:(j + 1) * sq, :]
        qs = jnp.concatenate([qj * hm_b[h] for h in range(nh)], axis=0)
        s = _dot_nt(qs, kk[j * sq:j * sq + lk]).reshape(nh, sq, lk)
        s = jnp.where(mask, s, NEG_INF)
        m = jnp.max(s, axis=-1, keepdims=True)
        pr = jnp.exp(s - m)
        den = jnp.sum(pr, axis=-1, keepdims=True)
        oh = _dot(pr.reshape(nh * sq, lk).astype(BF16), vv[j * sq:j * sq + lk]).reshape(nh, sq, GROUP_WIDTH) / den
        lh = m + jnp.log(den)
        o_ref[j * sq:(j + 1) * sq, :] = sum(oh[h] * hm_f[h] for h in range(nh))
        lse_ref[j * sq:(j + 1) * sq, :] = sum(lh[h] * hm_f[h] for h in range(nh))


def _attention(q, k, v, group):
    window, dil = ATT_WINDOWS[group]
    steps = window // (2 * dil)
    batch, _, sub_len, _ = q.shape
    assert sub_len % steps == 0
    lq = min(ATT_Q_BLOCK, sub_len)
    assert lq % steps == 0 and sub_len % lq == 0
    per_q = lq // steps
    n_halo = sub_len // steps
    main = pl.BlockSpec((None, None, lq, GROUP_WIDTH), lambda b, r, i: (b, r, i, 0))
    prev = pl.BlockSpec((None, None, steps, GROUP_WIDTH), lambda b, r, i: (b, r, jnp.maximum(i * per_q - 1, 0), 0))
    nxt = pl.BlockSpec((None, None, steps, GROUP_WIDTH),
                       lambda b, r, i: (b, r, jnp.minimum((i + 1) * per_q, n_halo - 1), 0))
    return pl.pallas_call(
        functools.partial(_attn_kernel, sub_len=sub_len, steps=steps),
        grid=(batch, dil, sub_len // lq),
        in_specs=[main, prev, main, nxt, prev, main, nxt],
        out_specs=[main, main],
        out_shape=[jax.ShapeDtypeStruct(q.shape, F32)] * 2,
        compiler_params=pltpu.CompilerParams(dimension_semantics=("arbitrary",) * 3, vmem_limit_bytes=VMEM_LIMIT),
        name=f"attention_g{group}",
    )(q, k, k, k, v, v, v)


def _post_mix_kernel(x_ref, *refs):
    o_refs = refs[:N_GROUPS]
    l_refs = refs[N_GROUPS:2 * N_GROUPS]
    (zu_ref, zv_ref, sguw_ref, sgub_ref, aonw_ref, sonw_ref, wout_ref, fnw_ref, rwt_ref, rb_ref, tri_ref,
     x1_ref, h2_ref, gates_ref, eidx_ref, rank_ref, cnt_ref, stage) = refs[2 * N_GROUPS:]
    tm = x_ref.shape[0]

    def token_order(ref, slot):
        dil = ref.shape[0]
        if dil == 1:
            return ref[0]
        per_group = GROUP_WIDTH // LANES
        for r in range(dil):
            for c in range(per_group):
                stage[slot * per_group + c, pl.ds(r, tm // dil, stride=dil), :] = ref[r, :, c * LANES:(c + 1) * LANES]
        return jnp.concatenate([stage[slot * per_group + c] for c in range(per_group)], axis=1)

    o = [token_order(ref, g) for g, ref in enumerate(o_refs)]
    l = [token_order(ref, N_GROUPS + g) for g, ref in enumerate(l_refs)]
    lmax = jnp.maximum(jnp.maximum(l[0], l[1]), l[2])
    e = [jnp.exp(lg - lmax) for lg in l]
    esum = e[0] + e[1] + e[2]
    att = (e[0] / esum) * o[0] + (e[1] / esum) * o[1] + (e[2] / esum) * o[2]
    att_n = _rms(att, aonw_ref[...]).astype(BF16)
    cgrp = lax.broadcasted_iota(I32, (1, SGU_WIDTH), 1) // SGU_GROUP_DIM
    sguw = sguw_ref[...]
    gates = []
    for c in range(tm // SGU_CHUNK):
        r = _dot(sguw, zv_ref[c * SGU_CHUNK:(c + 1) * SGU_CHUNK, :])
        g = sgub_ref[...]
        for grp in range(SGU_GROUPS):
            g = g + r[grp * SGU_CHUNK:(grp + 1) * SGU_CHUNK, :] * (cgrp == grp).astype(F32)
        gates.append(g)
    sgu = zu_ref[...] * jnp.concatenate(gates, axis=0)
    sgu_n = _rms(sgu, sonw_ref[...]).astype(BF16)
    x1 = x_ref[...] + _dot(att_n, wout_ref[0:GROUP_WIDTH, :]) + _dot(sgu_n, wout_ref[GROUP_WIDTH:, :])
    x1_ref[...] = x1
    h2 = _rms(x1, fnw_ref[...])
    _to_tiled(h2_ref, (), h2)
    logits = _dot_nt(rwt_ref[...], h2, precision=lax.Precision.HIGHEST) + rb_ref[...]
    eiota = lax.broadcasted_iota(I32, (N_EXPERTS, tm), 0)
    vals, idxs = [], []
    for _ in range(TOP_K):
        m = jnp.max(logits, axis=0, keepdims=True)
        idx = jnp.min(jnp.where(logits == m, eiota, N_EXPERTS), axis=0, keepdims=True)
        vals.append(m)
        idxs.append(idx)
        logits = jnp.where(eiota == idx, -jnp.inf, logits)
    exps = [jnp.exp(v - vals[0]) for v in vals]
    den = exps[0] + exps[1] + exps[2] + exps[3]
    gates_ref[...] = jnp.concatenate([ex / den for ex in exps], axis=0)
    eidx_ref[...] = jnp.concatenate(idxs, axis=0)
    onehots = [(eiota == idx).astype(F32) for idx in idxs]
    chosen = onehots[0] + onehots[1] + onehots[2] + onehots[3]
    before = _dot(chosen.astype(BF16), tri_ref[...])
    rank_ref[...] = jnp.concatenate(
        [jnp.sum(oh * before, axis=0, keepdims=True) for oh in onehots], axis=0).astype(I32)
    cnt_ref[...] = jnp.broadcast_to(jnp.sum(chosen, axis=1, keepdims=True), (N_EXPERTS, LANES)).astype(I32)


def _post_mix(x2, seq, o, lse, zu, zv, p):
    T, D = x2.shape
    tm = TOKEN_TILE
    n_tiles = T // tm
    n_seq_tiles = seq // tm
    chunks = D // LANES
    const = lambda shape: pl.BlockSpec(shape, lambda i: (0,) * len(shape))
    row = lambda w: pl.BlockSpec((tm, w), lambda i: (i, 0))
    colt = pl.BlockSpec((TOP_K, tm), lambda i: (0, i))
    phase = [_phase_spec(tm, dil, n_seq_tiles) for _, dil in ATT_WINDOWS]
    return pl.pallas_call(
        _post_mix_kernel,
        grid=(n_tiles,),
        in_specs=[row(D)] + phase + phase + [row(SGU_WIDTH), row(SGU_WIDTH),
                  const((SGU_GROUPS * SGU_CHUNK, SGU_CHUNK)), const((SGU_CHUNK, SGU_WIDTH)),
                  const((1, GROUP_WIDTH)), const((1, SGU_WIDTH)), const((GROUP_WIDTH + SGU_WIDTH, D)),
                  const((1, D)), const((N_EXPERTS, D)), const((N_EXPERTS, 1)), const((tm, tm))],
        out_specs=[row(D), pl.BlockSpec((tm // SUBLANES, chunks, SUBLANES, LANES), lambda i: (i, 0, 0, 0)),
                   colt, colt, colt, pl.BlockSpec((None, N_EXPERTS, LANES), lambda i: (i, 0, 0))],
        out_shape=[jax.ShapeDtypeStruct((T, D), F32),
                   jax.ShapeDtypeStruct((T // SUBLANES, chunks, SUBLANES, LANES), F32),
                   jax.ShapeDtypeStruct((TOP_K, T), F32), jax.ShapeDtypeStruct((TOP_K, T), I32),
                   jax.ShapeDtypeStruct((TOP_K, T), I32), jax.ShapeDtypeStruct((n_tiles, N_EXPERTS, LANES), I32)],
        scratch_shapes=[pltpu.VMEM((2 * N_GROUPS * GROUP_WIDTH // LANES, tm, LANES), F32)],
        compiler_params=pltpu.CompilerParams(dimension_semantics=("arbitrary",), vmem_limit_bytes=VMEM_LIMIT),
        name="post_mix",
    )(x2, *o, *lse, zu, zv, p["sgu_w"], p["sgu_b"], p["att_out_norm_w"],
      p["sgu_out_norm_w"], p["w_out"], p["ffn_norm_w"], p["router_wt"], p["router_b"], p["tri"])


def _pos_copy(pos_hbm, pos_smem, sem, tile, slot):
    n = pos_smem.shape[1]
    return pltpu.make_async_copy(pos_hbm.at[pl.ds(pl.multiple_of(tile * n, n), n)], pos_smem.at[slot], sem.at[slot])


def _tile_row(ref, idx, row):
    return ref.at[idx + (row // SUBLANES, slice(None), row % SUBLANES, slice(None))]


def _scatter_issue(i, n, seg_ref, pos_hbm, h_ref, xs_hbm, pos_smem, zeros, pos_sem, row_sem, zero_sem):
    tm = h_ref.shape[0] * SUBLANES
    bm = zeros.shape[0]
    slot = i % 2

    @pl.when(i == 0)
    def _():
        _pos_copy(pos_hbm, pos_smem, pos_sem, 0, 0).start()
        zeros[...] = jnp.zeros_like(zeros)

        def tail_copy(e):
            return pltpu.make_async_copy(
                zeros, xs_hbm.at[pl.ds(pl.multiple_of(seg_ref[e + 1] - bm, bm), bm)], zero_sem)

        for e in range(N_EXPERTS):
            pl.when(seg_ref[e + 1] > seg_ref[e])(lambda e=e: tail_copy(e).start())
        for e in range(N_EXPERTS):
            pl.when(seg_ref[e + 1] > seg_ref[e])(lambda e=e: tail_copy(e).wait())

        def fill_block(b, carry):
            fill = pltpu.make_async_copy(zeros, xs_hbm.at[pl.ds(pl.multiple_of(b * bm, bm), bm)], zero_sem)
            fill.start()
            fill.wait()
            return carry

        lax.fori_loop(seg_ref[N_EXPERTS] // bm, xs_hbm.shape[0] // bm, fill_block, 0)

    _pos_copy(pos_hbm, pos_smem, pos_sem, i, slot).wait()

    @pl.when(i + 1 < n)
    def _():
        _pos_copy(pos_hbm, pos_smem, pos_sem, i + 1, 1 - slot).start()

    for r in range(tm):
        src = _tile_row(h_ref, (), r)
        for k in range(TOP_K):
            pltpu.make_async_copy(src, xs_hbm.at[pos_smem[slot, k * tm + r]], row_sem).start(priority=k % 2)


def _scatter_finish(h_ref, xs_hbm, zeros, row_sem):
    tm = h_ref.shape[0] * SUBLANES
    bm = zeros.shape[0]
    for _ in range(TOP_K * tm // bm):
        pltpu.make_async_copy(zeros, xs_hbm.at[pl.ds(0, bm)], row_sem).wait()


def _dispatch_kernel(seg_ref, pos_hbm, h_ref, xs_hbm, pos_smem, zeros, pos_sem, row_sem, zero_sem):
    _scatter_issue(pl.program_id(0), pl.num_programs(0), seg_ref, pos_hbm, h_ref, xs_hbm, pos_smem, zeros,
                   pos_sem, row_sem, zero_sem)
    _scatter_finish(h_ref, xs_hbm, zeros, row_sem)


def _row_block_spec(chunks, index_map):
    return pl.BlockSpec((ROW_TILE // SUBLANES, chunks, SUBLANES, LANES), index_map)


def _scatter_scratch(chunks):
    return [pltpu.SMEM((2, TOP_K * ROW_TILE), I32), pltpu.VMEM((MOE_BLOCK, chunks, LANES), F32),
            pltpu.SemaphoreType.DMA((2,)), pltpu.SemaphoreType.DMA, pltpu.SemaphoreType.DMA]


def _dispatch(h2t, pos_tiles, seg, n_rows):
    _, chunks, _, _ = h2t.shape
    return pl.pallas_call(
        _dispatch_kernel,
        grid_spec=pltpu.PrefetchScalarGridSpec(
            num_scalar_prefetch=1,
            grid=(h2t.shape[0] * SUBLANES // ROW_TILE,),
            in_specs=[pl.BlockSpec(memory_space=pl.ANY), _row_block_spec(chunks, lambda i, seg: (i, 0, 0, 0))],
            out_specs=pl.BlockSpec(memory_space=pl.ANY),
            scratch_shapes=_scatter_scratch(chunks),
        ),
        out_shape=jax.ShapeDtypeStruct((n_rows, chunks, LANES), F32),
        compiler_params=pltpu.CompilerParams(dimension_semantics=("arbitrary",), vmem_limit_bytes=VMEM_LIMIT),
        name="dispatch",
    )(seg, pos_tiles, h2t)


def _experts_kernel(*refs, scatter_tiles, gather_tiles):
    refs = list(refs)
    take = lambda k: [refs.pop(0) for _ in range(k)]
    blk_e_ref, n_used_ref = take(2)
    seg_ref = take(1)[0] if scatter_tiles else None
    xs_hbm, wg_ref, bg_ref, wu_ref, bu_ref, wd_ref, bd_ref = take(7)
    s_pos_hbm, s_h_ref = take(2) if scatter_tiles else (None, None)
    g_pos_hbm, g_gates_ref, g_x1_ref, g_ys_hbm = take(4) if gather_tiles else (None,) * 4
    ys_hbm = take(1)[0]
    s_xs_hbm = take(1)[0] if scatter_tiles else None
    g_out_ref = take(1)[0] if gather_tiles else None
    xbuf, obuf, in_sem, out_sem = take(4)
    s_pos_smem, s_zeros, s_pos_sem, s_row_sem, s_zero_sem = take(5) if scatter_tiles else (None,) * 5
    g_pos_smem, g_buf, g_pos_sem, g_row_sem = take(4) if gather_tiles else (None,) * 4
    assert not refs
    del blk_e_ref
    bt = xbuf.shape[1]
    i = pl.program_id(0)
    n = pl.num_programs(0)
    n_used = n_used_ref[0]
    slot = i % 2

    def in_copies(blk, s):
        return [pltpu.make_async_copy(xs_hbm.at[pl.ds(blk * bt, bt), j], xbuf.at[s, :, :, j, :], in_sem.at[s])
                for j in range(SUBLANES)]

    def out_copies(blk, s):
        return [pltpu.make_async_copy(obuf.at[s, :, :, j, :], ys_hbm.at[pl.ds(blk * bt, bt), j], out_sem.at[s])
                for j in range(SUBLANES)]

    @pl.when(i == 0)
    def _():
        for c in in_copies(0, 0):
            c.start()

    @pl.when(i + 1 < n_used)
    def _():
        for c in in_copies(i + 1, 1 - slot):
            c.start()

    @pl.when(i >= 2)
    def _():
        for c in out_copies(i - 2, slot):
            c.wait()

    if scatter_tiles:
        pl.when(i < scatter_tiles)(lambda: _scatter_issue(
            i, scatter_tiles, seg_ref, s_pos_hbm, s_h_ref, s_xs_hbm, s_pos_smem, s_zeros, s_pos_sem, s_row_sem,
            s_zero_sem))
    if gather_tiles:
        pl.when(i < gather_tiles)(lambda: _gather_issue(
            i, gather_tiles, g_pos_hbm, g_ys_hbm, g_pos_smem, g_buf, g_pos_sem, g_row_sem))

    @pl.when(i >= n_used)
    def _():
        obuf[slot] = jnp.zeros(obuf.shape[1:], F32)

    @pl.when(i < n_used)
    def _():
        for c in in_copies(i, slot):
            c.wait()
        x = _from_tiled(xbuf, (slot,), slice(0, bt)).astype(BF16)
        g = _dot(x, wg_ref[...]) + bg_ref[...]
        u = _dot(x, wu_ref[...]) + bu_ref[...]
        g = jnp.minimum(g, SWIGLU_LIMIT)
        u = jnp.clip(u, -SWIGLU_LIMIT, SWIGLU_LIMIT)
        act = (u + 1.0) * (g * jax.nn.sigmoid(SWIGLU_ALPHA * g))
        _to_tiled(obuf, (slot,), _dot(act.astype(BF16), wd_ref[...]) + bd_ref[...])

    for c in out_copies(i, slot):
        c.start()

    if scatter_tiles:
        pl.when(i < scatter_tiles)(lambda: _scatter_finish(s_h_ref, s_xs_hbm, s_zeros, s_row_sem))
    if gather_tiles:
        pl.when((i >= 1) & (i <= gather_tiles))(lambda: _gather_finish(
            i, g_gates_ref, g_x1_ref, g_out_ref, g_buf, g_row_sem))

    @pl.when(i == n - 1)
    def _():
        for c in out_copies(i - 1, 1 - slot) + out_copies(i, slot):
            c.wait()


def _experts(xs, blk_e, n_used, p, scatter=None, gather=None):
    n_rows, chunks, _ = xs.shape
    D = chunks * LANES
    bm = MOE_BLOCK
    n_blocks = n_rows // bm
    d_ff = p["w_gate"].shape[2]
    assert n_blocks >= 2
    hbm = pl.BlockSpec(memory_space=pl.ANY)
    wspec = lambda a, b: pl.BlockSpec((None, a, b), lambda i, be, *_: (be[i], 0, 0))
    by_tile = (n_rows // SUBLANES, SUBLANES, chunks, LANES)
    prefetch = [blk_e, n_used]
    in_specs = [hbm, wspec(D, d_ff), wspec(1, d_ff), wspec(D, d_ff), wspec(1, d_ff), wspec(d_ff, D), wspec(1, D)]
    args = [xs.reshape(by_tile), p["w_gate"], p["b_gate"], p["w_up"], p["b_up"], p["w_down"], p["b_down"]]
    out_specs = [hbm]
    out_shape = [jax.ShapeDtypeStruct(by_tile, F32)]
    block_buf = pltpu.VMEM((2, bm // SUBLANES, chunks, SUBLANES, LANES), F32)
    scratch = [block_buf, block_buf, pltpu.SemaphoreType.DMA((2,)), pltpu.SemaphoreType.DMA((2,))]
    scatter_tiles = gather_tiles = 0
    if scatter is not None:
        h2t, s_pos, seg, s_rows = scatter
        scatter_tiles = h2t.shape[0] * SUBLANES // ROW_TILE
        assert scatter_tiles <= n_blocks
        prefetch.append(seg)
        in_specs += [hbm, _row_block_spec(chunks, lambda i, *_: (jnp.minimum(i, scatter_tiles - 1), 0, 0, 0))]
        args += [s_pos, h2t]
        out_specs.append(hbm)
        out_shape.append(jax.ShapeDtypeStruct((s_rows, chunks, LANES), F32))
    if gather is not None:
        x1, gates_tok, g_ys, g_pos = gather
        gather_tiles = x1.shape[0] // ROW_TILE
        assert gather_tiles + 1 <= n_blocks
        lag = lambda i, *_: (jnp.clip(i - 1, 0, gather_tiles - 1), 0)
        in_specs += [hbm, pl.BlockSpec((ROW_TILE, TOP_K), lag), pl.BlockSpec((ROW_TILE, D), lag), hbm]
        args += [g_pos, gates_tok, x1, g_ys]
        out_specs.append(pl.BlockSpec((ROW_TILE, D), lag))
        out_shape.append(jax.ShapeDtypeStruct(x1.shape, F32))
    if scatter is not None:
        scratch += _scatter_scratch(chunks)
    if gather is not None:
        scratch += _gather_scratch(chunks)
    outs = pl.pallas_call(
        functools.partial(_experts_kernel, scatter_tiles=scatter_tiles, gather_tiles=gather_tiles),
        grid_spec=pltpu.PrefetchScalarGridSpec(
            num_scalar_prefetch=len(prefetch), grid=(n_blocks,), in_specs=in_specs, out_specs=out_specs,
            scratch_shapes=scratch),
        out_shape=out_shape,
        compiler_params=pltpu.CompilerParams(dimension_semantics=("arbitrary",), vmem_limit_bytes=VMEM_LIMIT),
        name="experts" + ("_dispatch" if scatter is not None else "") + ("_combine" if gather is not None else ""),
    )(*prefetch, *args)
    outs = list(outs)
    ys = outs.pop(0).reshape(n_rows, chunks, LANES)
    xs_other = outs.pop(0) if scatter is not None else None
    out_other = outs.pop(0) if gather is not None else None
    return ys, xs_other, out_other


def _gather_issue(j, n, pos_hbm, ys_hbm, pos_smem, buf, pos_sem, row_sem):
    slot = j % 2

    @pl.when(j == 0)
    def _():
        _pos_copy(pos_hbm, pos_smem, pos_sem, 0, 0).start()

    _pos_copy(pos_hbm, pos_smem, pos_sem, j, slot).wait()

    @pl.when(j + 1 < n)
    def _():
        _pos_copy(pos_hbm, pos_smem, pos_sem, j + 1, 1 - slot).start()

    for r in range(buf.shape[1] * SUBLANES):
        pltpu.make_async_copy(ys_hbm.at[pos_smem[slot, r]], _tile_row(buf, (slot,), r),
                              row_sem.at[slot]).start(priority=r % 2)


def _gather_finish(j, gates_ref, x1_ref, out_ref, buf, row_sem):
    tm = x1_ref.shape[0]
    done = 1 - j % 2
    pltpu.make_async_copy(buf.at[done], buf.at[done], row_sem.at[done]).wait()
    bt = tm // SUBLANES
    y = gates_ref[:, 0:1] * _from_tiled(buf, (done,), slice(0, bt))
    for k in range(1, TOP_K):
        y = y + gates_ref[:, k:k + 1] * _from_tiled(buf, (done,), slice(k * bt, (k + 1) * bt))
    out_ref[...] = x1_ref[...] + y


def _combine_kernel(pos_hbm, gates_ref, x1_ref, ys_hbm, out_ref, pos_smem, buf, pos_sem, row_sem):
    j = pl.program_id(0)
    n = pl.num_programs(0) - 1
    pl.when(j < n)(lambda: _gather_issue(j, n, pos_hbm, ys_hbm, pos_smem, buf, pos_sem, row_sem))
    pl.when(j >= 1)(lambda: _gather_finish(j, gates_ref, x1_ref, out_ref, buf, row_sem))


def _gather_scratch(chunks):
    return [pltpu.SMEM((2, TOP_K * ROW_TILE), I32),
            pltpu.VMEM((2, TOP_K * ROW_TILE // SUBLANES, chunks, SUBLANES, LANES), F32),
            pltpu.SemaphoreType.DMA((2,)), pltpu.SemaphoreType.DMA((2,))]


def _combine(x1, gates_tok, ys, pos_tiles):
    T, D = x1.shape
    tm = ROW_TILE
    n = T // tm
    lag = lambda i: (jnp.maximum(i - 1, 0), 0)
    return pl.pallas_call(
        _combine_kernel,
        grid=(n + 1,),
        in_specs=[pl.BlockSpec(memory_space=pl.ANY), pl.BlockSpec((tm, TOP_K), lag),
                  pl.BlockSpec((tm, D), lag), pl.BlockSpec(memory_space=pl.ANY)],
        out_specs=pl.BlockSpec((tm, D), lag),
        out_shape=jax.ShapeDtypeStruct((T, D), F32),
        scratch_shapes=_gather_scratch(D // LANES),
        compiler_params=pltpu.CompilerParams(dimension_semantics=("arbitrary",), vmem_limit_bytes=VMEM_LIMIT),
        name="combine",
    )(pos_tiles, gates_tok, x1, ys)


def _route(eidx, rank, tile_counts):
    T = eidx.shape[1]
    bm = MOE_BLOCK
    n_tiles = T // TOKEN_TILE
    tc = tile_counts[:, :, 0]
    counts = jnp.sum(tc, axis=0)
    tile_off = jnp.cumsum(tc, axis=0) - tc
    pcounts = (counts + bm - 1) // bm * bm
    pends = jnp.cumsum(pcounts)
    base = (pends - pcounts)[None, :] + tile_off
    sel = eidx.reshape(TOP_K, n_tiles, 1, TOKEN_TILE) == jnp.arange(N_EXPERTS, dtype=I32)[None, None, :, None]
    pos = jnp.sum(jnp.where(sel, base[None, :, :, None], 0), axis=2).reshape(TOP_K, T) + rank
    n_blocks = (TOP_K * T) // bm + N_EXPERTS
    blk_start = jnp.arange(n_blocks, dtype=I32) * bm
    blk_e = jnp.minimum(jnp.sum((pends[None, :] <= blk_start[:, None]).astype(I32), axis=1), N_EXPERTS - 1)
    n_used = (pends[-1] // bm).astype(I32).reshape(1)
    seg = jnp.concatenate([jnp.zeros((1,), I32), pends.astype(I32)])
    pos_tiles = pos.reshape(TOP_K, T // ROW_TILE, ROW_TILE).transpose(1, 0, 2).reshape(-1)
    return pos_tiles, blk_e, n_used, seg, n_blocks * bm


def _front(x, p):
    batch, seq, D = x.shape
    x2 = x.reshape(batch * seq, D)
    q, k, v, zu, zv = _in_proj(x2, batch, seq, p)
    o, lse = zip(*[_attention(q[g], k[g], v[g], g) for g in range(N_GROUPS)])
    x1, h2t, gates, eidx, rank, tile_counts = _post_mix(x2, seq, o, lse, zu, zv, p)
    pos_tiles, blk_e, n_used, seg, n_rows = _route(eidx, rank, tile_counts)
    return dict(x1=x1, h2t=h2t, gates_tok=gates.T, pos_tiles=pos_tiles, blk_e=blk_e, n_used=n_used, seg=seg,
                n_rows=n_rows)


def _layer(xa, xb, p):
    a, b = _front(xa, p), _front(xb, p)
    xs_a = _dispatch(a["h2t"], a["pos_tiles"], a["seg"], a["n_rows"])
    ys_a, xs_b, _ = _experts(xs_a, a["blk_e"], a["n_used"], p,
                             scatter=(b["h2t"], b["pos_tiles"], b["seg"], b["n_rows"]))
    ys_b, _, out_a = _experts(xs_b, b["blk_e"], b["n_used"], p,
                              gather=(a["x1"], a["gates_tok"], ys_a, a["pos_tiles"]))
    out_b = _combine(b["x1"], b["gates_tok"], ys_b, b["pos_tiles"])
    return out_a.reshape(xa.shape), out_b.reshape(xb.shape)


def _rope_tables(seq):
    half = ROT_DIM // 2
    inv_freq = jnp.power(ROPE_THETA, -2.0 * jnp.arange(half, dtype=F32) / ROT_DIM)
    ang = jnp.arange(seq, dtype=F32)[:, None] * inv_freq[None, :]
    cos, sin = jnp.cos(ang), jnp.sin(ang)
    zeros = jnp.zeros((seq, HEAD_DIM - ROT_DIM), F32)
    zh = jnp.zeros((seq, half), F32)
    cos_t = jnp.concatenate([cos, cos, zeros + 1.0], axis=1)
    sa = jnp.concatenate([-sin, zh, zeros], axis=1)
    sb = jnp.concatenate([zh, sin, zeros], axis=1)
    rep = LANES // HEAD_DIM
    return tuple(jnp.tile(t, (1, rep)) for t in (cos_t, sa, sb))


def kernel(x_prompt, x_sample, attn_norm_w, w_in, q_norm_w, k_norm_w, sgu_ln_w, sgu_ln_b, sgu_w, sgu_b,
           att_out_norm_w, sgu_out_norm_w, w_out, ffn_norm_w, router_w, router_b,
           w_gate, b_gate, w_up, b_up, w_down, b_down):
    depth = w_in.shape[0]
    n_heads = ATT_WIDTH // HEAD_DIM
    blk = np.arange(MXU_DIM) // HEAD_DIM
    tri = np.arange(TOKEN_TILE)
    xa, xb = x_prompt, x_sample
    rope = {x.shape[1]: _rope_tables(x.shape[1]) for x in (xa, xb)}
    for l in range(depth):
        shared = dict(
            attn_norm_w=attn_norm_w[l][None], w_in=w_in[l].astype(BF16),
            q_norm_w=jnp.tile(q_norm_w[l], n_heads)[None], k_norm_w=jnp.tile(k_norm_w[l], n_heads)[None],
            head_ones=jnp.asarray(blk[:, None] == blk[None, :], BF16),
            sgu_ln_w=sgu_ln_w[l][None], sgu_ln_b=sgu_ln_b[l][None],
            sgu_w=sgu_w[l].reshape(SGU_GROUPS * SGU_CHUNK, SGU_CHUNK).astype(BF16),
            sgu_b=jnp.repeat(sgu_b[l].T, SGU_GROUP_DIM, axis=1),
            att_out_norm_w=att_out_norm_w[l][None], sgu_out_norm_w=sgu_out_norm_w[l][None],
            w_out=w_out[l].astype(BF16), ffn_norm_w=ffn_norm_w[l][None],
            router_wt=router_w[l].T, router_b=router_b[l][:, None],
            tri=jnp.asarray(tri[:, None] < tri[None, :], BF16),
            w_gate=w_gate[l].astype(BF16), b_gate=b_gate[l][:, None, :],
            w_up=w_up[l].astype(BF16), b_up=b_up[l][:, None, :],
            w_down=w_down[l].astype(BF16), b_down=b_down[l][:, None, :],
        )
        xa, xb = _layer(xa, xb, dict(shared, rope=rope))
    return xa, xb
```

```python
import functools

import numpy as np
import jax
import jax.numpy as jnp
from jax import lax
from jax.experimental import pallas as pl
from jax.experimental.pallas import tpu as pltpu
from jax.experimental.pallas import tpu_sc as plsc

F32 = jnp.float32
BF16 = jnp.bfloat16
I32 = jnp.int32

HEAD_DIM = 64
ATT_WINDOWS = ((128, 1), (512, 4), (2048, 16))
N_GROUPS = len(ATT_WINDOWS)
HEADS_PER_GROUP = 4
GROUP_WIDTH = HEADS_PER_GROUP * HEAD_DIM
ATT_WIDTH = N_GROUPS * GROUP_WIDTH
SGU_WIDTH = 256
SGU_GROUP_DIM = 64
SGU_GROUPS = SGU_WIDTH // SGU_GROUP_DIM
SGU_CHUNK = 128
ROT_DIM = HEAD_DIM // 4
ROPE_THETA = 500000.0
N_EXPERTS = 32
TOP_K = 4
SWIGLU_LIMIT = 7.0
SWIGLU_ALPHA = 1.702
NORM_EPS = 1e-6
LN_EPS = 1e-5
NEG_INF = -1e30

LANES = 128
SUBLANES = 8
MXU_DIM = 256
TOKEN_TILE = 512
ATT_Q_BLOCK = 256
ATT_SUB_Q = 128
MOE_BLOCK = 512
ROW_TILE = 256
SC_WINDOW = 64
VMEM_LIMIT = 48 * 1024 * 1024


def _dot(a, b):
    return jnp.dot(a, b, preferred_element_type=F32)


def _dot_nt(a, b, precision=None):
    return lax.dot_general(a, b, (((1,), (1,)), ((), ())), precision=precision, preferred_element_type=F32)


def _rms(x, w):
    return x * lax.rsqrt(jnp.mean(x * x, axis=-1, keepdims=True) + NORM_EPS) * w


def _gelu(x):
    return 0.5 * x * (1.0 + lax.erf(x * np.float32(np.sqrt(0.5))))


def _to_tiled(ref, idx, value):
    rows = value.shape[0]
    for c in range(value.shape[1] // LANES):
        ref[idx + (slice(None), c)] = value[:, c * LANES:(c + 1) * LANES].reshape(rows // SUBLANES, SUBLANES, LANES)


def _from_tiled(ref, idx, row_tiles):
    chunks = ref.shape[-3]
    n = (row_tiles.stop - row_tiles.start) * SUBLANES
    return jnp.concatenate([ref[idx + (row_tiles, c)].reshape(n, LANES) for c in range(chunks)], axis=1)


def _in_proj_kernel(x_ref, anw_ref, w_ref, qnw_ref, knw_ref, ones_ref, cos_ref, sa_ref, sb_ref, lnw_ref, lnb_ref,
                    *refs):
    qkv_refs = refs[:3 * N_GROUPS]
    zu_ref, zv_ref, stage = refs[3 * N_GROUPS:]
    tm = x_ref.shape[0]
    h = _rms(x_ref[...], anw_ref[...]).astype(BF16)
    reps = ATT_WIDTH // LANES
    cos = jnp.concatenate([cos_ref[...]] * reps, axis=1)
    sa = jnp.concatenate([sa_ref[...]] * reps, axis=1)
    sb = jnp.concatenate([sb_ref[...]] * reps, axis=1)
    ones = ones_ref[...]

    def head_norm_rope(t, nw):
        sq = (t * t).astype(BF16)
        parts = []
        for j in range(ATT_WIDTH // MXU_DIM):
            sl = slice(j * MXU_DIM, (j + 1) * MXU_DIM)
            parts.append(_dot(sq[:, sl], ones))
        ssum = jnp.concatenate(parts, axis=1)
        t = t * lax.rsqrt(ssum * (1.0 / HEAD_DIM) + NORM_EPS) * nw
        half = ROT_DIM // 2
        return t * cos + pltpu.roll(t, ATT_WIDTH - half, 1) * sa + pltpu.roll(t, half, 1) * sb

    def emit(which, t):
        n_chunks = ATT_WIDTH // LANES
        per_group = GROUP_WIDTH // LANES
        for c in range(n_chunks):
            stage[which * n_chunks + c] = t[:, c * LANES:(c + 1) * LANES]
        for g, (_, dil) in enumerate(ATT_WINDOWS):
            out = qkv_refs[which * N_GROUPS + g]
            first = which * n_chunks + g * per_group
            for r in range(dil):
                rows = pl.ds(r, tm // dil, stride=dil)
                out[r] = jnp.concatenate([stage[first + c, rows, :] for c in range(per_group)], axis=1).astype(BF16)

    q = head_norm_rope(_dot(h, w_ref[:, 0:ATT_WIDTH]), qnw_ref[...])
    emit(0, q * (HEAD_DIM ** -0.5))
    emit(1, head_norm_rope(_dot(h, w_ref[:, ATT_WIDTH:2 * ATT_WIDTH]), knw_ref[...]))
    emit(2, _dot(h, w_ref[:, 2 * ATT_WIDTH:3 * ATT_WIDTH]))
    z = _dot(h, w_ref[:, 3 * ATT_WIDTH:3 * ATT_WIDTH + 2 * SGU_WIDTH])
    zu_ref[...] = _gelu(z[:, :SGU_WIDTH])
    gv = _gelu(z[:, SGU_WIDTH:])
    mu = jnp.mean(gv, axis=-1, keepdims=True)
    var = jnp.mean(jnp.square(gv - mu), axis=-1, keepdims=True)
    zv_ref[...] = ((gv - mu) * lax.rsqrt(var + LN_EPS) * lnw_ref[...] + lnb_ref[...]).astype(BF16)


def _phase_spec(tm, dil, n_seq_tiles):
    return pl.BlockSpec((None, dil, tm // dil, GROUP_WIDTH), lambda i: (i // n_seq_tiles, 0, i % n_seq_tiles, 0))


def _in_proj(x2, batch, seq, p):
    T, D = x2.shape
    tm = TOKEN_TILE
    n_seq_tiles = seq // tm
    in_width = p["w_in"].shape[1]
    const = lambda shape: pl.BlockSpec(shape, lambda i: (0,) * len(shape))
    rope = pl.BlockSpec((tm, LANES), lambda i: (i % n_seq_tiles, 0))
    row = lambda w: pl.BlockSpec((tm, w), lambda i: (i, 0))
    qkv_specs = [_phase_spec(tm, dil, n_seq_tiles) for _ in range(3) for _, dil in ATT_WINDOWS]
    qkv_shapes = [jax.ShapeDtypeStruct((batch, dil, seq // dil, GROUP_WIDTH), BF16)
                  for _ in range(3) for _, dil in ATT_WINDOWS]
    outs = pl.pallas_call(
        _in_proj_kernel,
        grid=(T // tm,),
        in_specs=[row(D), const((1, D)), const((D, in_width)), const((1, ATT_WIDTH)), const((1, ATT_WIDTH)),
                  const((MXU_DIM, MXU_DIM)), rope, rope, rope, const((1, SGU_WIDTH)), const((1, SGU_WIDTH))],
        out_specs=qkv_specs + [row(SGU_WIDTH), row(SGU_WIDTH)],
        out_shape=qkv_shapes + [jax.ShapeDtypeStruct((T, SGU_WIDTH), F32), jax.ShapeDtypeStruct((T, SGU_WIDTH), BF16)],
        scratch_shapes=[pltpu.VMEM((3 * ATT_WIDTH // LANES, tm, LANES), F32)],
        compiler_params=pltpu.CompilerParams(dimension_semantics=("arbitrary",), vmem_limit_bytes=VMEM_LIMIT),
        name="in_proj",
    )(x2, p["attn_norm_w"], p["w_in"], p["q_norm_w"], p["k_norm_w"], p["head_ones"],
      *p["rope"][seq], p["sgu_ln_w"], p["sgu_ln_b"])
    q, k, v = (outs[i * N_GROUPS:(i + 1) * N_GROUPS] for i in range(3))
    return q, k, v, outs[-2], outs[-1]


def _attn_kernel(q_ref, kp_ref, km_ref, kn_ref, vp_ref, vm_ref, vn_ref, o_ref, lse_ref, *, sub_len, steps):
    lq = q_ref.shape[0]
    sq = min(lq, ATT_SUB_Q)
    lk = sq + 2 * steps
    nh = HEADS_PER_GROUP
    lb = pl.program_id(2)
    kk = jnp.concatenate([kp_ref[...], km_ref[...], kn_ref[...]], axis=0)
    vv = jnp.concatenate([vp_ref[...], vm_ref[...], vn_ref[...]], axis=0)
    head = lax.broadcasted_iota(I32, (1, GROUP_WIDTH), 1) // HEAD_DIM
    hm_f = [(head == h).astype(F32) for h in range(nh)]
    hm_b = [m.astype(BF16) for m in hm_f]
    qi = lax.broadcasted_iota(I32, (sq, 1), 0)
    kj = lax.broadcasted_iota(I32, (sq, lk), 1)
    for j in range(lq // sq):
        first = lb * lq + j * sq
        lo = jnp.maximum(qi, steps - first)
        hi = jnp.minimum(qi + 2 * steps, sub_len - 1 + steps - first)
        mask = ((kj - lo).astype(jnp.uint32) <= (hi - lo).astype(jnp.uint32))[None]
        qj = q_ref[j * sq:(j + 1) * sq, :]
        qs = jnp.concatenate([qj * hm_b[h] for h in range(nh)], axis=0)
        s = _dot_nt(qs, kk[j * sq:j * sq + lk]).reshape(nh, sq, lk)
        s = jnp.where(mask, s, NEG_INF)
        m = jnp.max(s, axis=-1, keepdims=True)
        pr = jnp.exp(s - m)
        den = jnp.sum(pr, axis=-1, keepdims=True)
        oh = _dot(pr.reshape(nh * sq, lk).astype(BF16), vv[j * sq:j * sq + lk]).reshape(nh, sq, GROUP_WIDTH) / den
        lh = m + jnp.log(den)
        o_ref[j * sq:(j + 1) * sq, :] = sum(oh[h] * hm_f[h] for h in range(nh))
        lse_ref[j * sq:(j + 1) * sq, :] = sum(lh[h] * hm_f[h] for h in range(nh))


def _attention(q, k, v, group):
    window, dil = ATT_WINDOWS[group]
    steps = window // (2 * dil)
    batch, _, sub_len, _ = q.shape
    assert sub_len % steps == 0
    lq = min(ATT_Q_BLOCK, sub_len)
    assert lq % steps == 0 and sub_len % lq == 0
    per_q = lq // steps
    n_halo = sub_len // steps
    main = pl.BlockSpec((None, None, lq, GROUP_WIDTH), lambda b, r, i: (b, r, i, 0))
    prev = pl.BlockSpec((None, None, steps, GROUP_WIDTH), lambda b, r, i: (b, r, jnp.maximum(i * per_q - 1, 0), 0))
    nxt = pl.BlockSpec((None, None, steps, GROUP_WIDTH),
                       lambda b, r, i: (b, r, jnp.minimum((i + 1) * per_q, n_halo - 1), 0))
    return pl.pallas_call(
        functools.partial(_attn_kernel, sub_len=sub_len, steps=steps),
        grid=(batch, dil, sub_len // lq),
        in_specs=[main, prev, main, nxt, prev, main, nxt],
        out_specs=[main, main],
        out_shape=[jax.ShapeDtypeStruct(q.shape, F32)] * 2,
        compiler_params=pltpu.CompilerParams(dimension_semantics=("arbitrary",) * 3, vmem_limit_bytes=VMEM_LIMIT),
        name=f"attention_g{group}",
    )(q, k, k, k, v, v, v)


def _post_mix_kernel(x_ref, *refs):
    o_refs = refs[:N_GROUPS]
    l_refs = refs[N_GROUPS:2 * N_GROUPS]
    (zu_ref, zv_ref, sguw_ref, sgub_ref, aonw_ref, sonw_ref, wout_ref, fnw_ref, rwt_ref, rb_ref, tri_ref,
     x1_ref, h2_hbm, gates_ref, eidx_ref, rank_ref, cnt_ref, stage, hstage, hsem) = refs[2 * N_GROUPS:]
    tm = x_ref.shape[0]

    def token_order(ref, slot):
        dil = ref.shape[0]
        if dil == 1:
            return ref[0]
        per_group = GROUP_WIDTH // LANES
        for r in range(dil):
            for c in range(per_group):
                stage[slot * per_group + c, pl.ds(r, tm // dil, stride=dil), :] = ref[r, :, c * LANES:(c + 1) * LANES]
        return jnp.concatenate([stage[slot * per_group + c] for c in range(per_group)], axis=1)

    o = [token_order(ref, g) for g, ref in enumerate(o_refs)]
    l = [token_order(ref, N_GROUPS + g) for g, ref in enumerate(l_refs)]
    lmax = jnp.maximum(jnp.maximum(l[0], l[1]), l[2])
    e = [jnp.exp(lg - lmax) for lg in l]
    esum = e[0] + e[1] + e[2]
    att = (e[0] / esum) * o[0] + (e[1] / esum) * o[1] + (e[2] / esum) * o[2]
    att_n = _rms(att, aonw_ref[...]).astype(BF16)
    cgrp = lax.broadcasted_iota(I32, (1, SGU_WIDTH), 1) // SGU_GROUP_DIM
    sguw = sguw_ref[...]
    gates = []
    for c in range(tm // SGU_CHUNK):
        r = _dot(sguw, zv_ref[c * SGU_CHUNK:(c + 1) * SGU_CHUNK, :])
        g = sgub_ref[...]
        for grp in range(SGU_GROUPS):
            g = g + r[grp * SGU_CHUNK:(grp + 1) * SGU_CHUNK, :] * (cgrp == grp).astype(F32)
        gates.append(g)
    sgu = zu_ref[...] * jnp.concatenate(gates, axis=0)
    sgu_n = _rms(sgu, sonw_ref[...]).astype(BF16)
    x1 = x_ref[...] + _dot(att_n, wout_ref[0:GROUP_WIDTH, :]) + _dot(sgu_n, wout_ref[GROUP_WIDTH:, :])
    x1_ref[...] = x1
    h2 = _rms(x1, fnw_ref[...])
    _to_tiled(hstage, (), h2)
    bt = tm // SUBLANES
    h2_copies = [pltpu.make_async_copy(hstage.at[:, :, j, :], h2_hbm.at[pl.ds(pl.program_id(0) * bt, bt), j], hsem)
                 for j in range(SUBLANES)]
    for c in h2_copies:
        c.start()
    logits = _dot_nt(rwt_ref[...], h2, precision=lax.Precision.HIGHEST) + rb_ref[...]
    eiota = lax.broadcasted_iota(I32, (N_EXPERTS, tm), 0)
    vals, idxs = [], []
    for _ in range(TOP_K):
        m = jnp.max(logits, axis=0, keepdims=True)
        idx = jnp.min(jnp.where(logits == m, eiota, N_EXPERTS), axis=0, keepdims=True)
        vals.append(m)
        idxs.append(idx)
        logits = jnp.where(eiota == idx, -jnp.inf, logits)
    exps = [jnp.exp(v - vals[0]) for v in vals]
    den = exps[0] + exps[1] + exps[2] + exps[3]
    gates_ref[...] = jnp.concatenate([ex / den for ex in exps], axis=0)
    eidx_ref[...] = jnp.concatenate(idxs, axis=0)
    onehots = [(eiota == idx).astype(F32) for idx in idxs]
    chosen = onehots[0] + onehots[1] + onehots[2] + onehots[3]
    before = _dot(chosen.astype(BF16), tri_ref[...])
    rank_ref[...] = jnp.concatenate(
        [jnp.sum(oh * before, axis=0, keepdims=True) for oh in onehots], axis=0).astype(I32)
    cnt_ref[...] = jnp.broadcast_to(jnp.sum(chosen, axis=1, keepdims=True), (N_EXPERTS, LANES)).astype(I32)
    for c in h2_copies:
        c.wait()


def _post_mix(x2, seq, o, lse, zu, zv, p):
    T, D = x2.shape
    tm = TOKEN_TILE
    n_tiles = T // tm
    n_seq_tiles = seq // tm
    chunks = D // LANES
    const = lambda shape: pl.BlockSpec(shape, lambda i: (0,) * len(shape))
    row = lambda w: pl.BlockSpec((tm, w), lambda i: (i, 0))
    colt = pl.BlockSpec((TOP_K, tm), lambda i: (0, i))
    phase = [_phase_spec(tm, dil, n_seq_tiles) for _, dil in ATT_WINDOWS]
    return pl.pallas_call(
        _post_mix_kernel,
        grid=(n_tiles,),
        in_specs=[row(D)] + phase + phase + [row(SGU_WIDTH), row(SGU_WIDTH),
                  const((SGU_GROUPS * SGU_CHUNK, SGU_CHUNK)), const((SGU_CHUNK, SGU_WIDTH)),
                  const((1, GROUP_WIDTH)), const((1, SGU_WIDTH)), const((GROUP_WIDTH + SGU_WIDTH, D)),
                  const((1, D)), const((N_EXPERTS, D)), const((N_EXPERTS, 1)), const((tm, tm))],
        out_specs=[row(D), pl.BlockSpec(memory_space=pl.ANY),
                   colt, colt, colt, pl.BlockSpec((None, N_EXPERTS, LANES), lambda i: (i, 0, 0))],
        out_shape=[jax.ShapeDtypeStruct((T, D), F32),
                   jax.ShapeDtypeStruct((T // SUBLANES, SUBLANES, chunks, LANES), F32),
                   jax.ShapeDtypeStruct((TOP_K, T), F32), jax.ShapeDtypeStruct((TOP_K, T), I32),
                   jax.ShapeDtypeStruct((TOP_K, T), I32), jax.ShapeDtypeStruct((n_tiles, N_EXPERTS, LANES), I32)],
        scratch_shapes=[pltpu.VMEM((2 * N_GROUPS * GROUP_WIDTH // LANES, tm, LANES), F32),
                        pltpu.VMEM((tm // SUBLANES, chunks, SUBLANES, LANES), F32), pltpu.SemaphoreType.DMA],
        compiler_params=pltpu.CompilerParams(dimension_semantics=("arbitrary",), vmem_limit_bytes=VMEM_LIMIT),
        name="post_mix",
    )(x2, *o, *lse, zu, zv, p["sgu_w"], p["sgu_b"], p["att_out_norm_w"],
      p["sgu_out_norm_w"], p["w_out"], p["ffn_norm_w"], p["router_wt"], p["router_b"], p["tri"])


def _pos_copy(pos_hbm, pos_smem, sem, tile, slot):
    n = pos_smem.shape[1]
    return pltpu.make_async_copy(pos_hbm.at[pl.ds(pl.multiple_of(tile * n, n), n)], pos_smem.at[slot], sem.at[slot])


def _tile_row(ref, idx, row):
    return ref.at[idx + (row // SUBLANES, slice(None), row % SUBLANES, slice(None))]


def _scatter_issue(i, n, seg_ref, pos_hbm, h_ref, xs_hbm, pos_smem, zeros, pos_sem, row_sem, zero_sem):
    tm = h_ref.shape[0] * SUBLANES
    bm = zeros.shape[0]
    slot = i % 2

    @pl.when(i == 0)
    def _():
        _pos_copy(pos_hbm, pos_smem, pos_sem, 0, 0).start()
        zeros[...] = jnp.zeros_like(zeros)

        def tail_copy(e):
            return pltpu.make_async_copy(
                zeros, xs_hbm.at[pl.ds(pl.multiple_of(seg_ref[e + 1] - bm, bm), bm)], zero_sem)

        for e in range(N_EXPERTS):
            pl.when(seg_ref[e + 1] > seg_ref[e])(lambda e=e: tail_copy(e).start())
        for e in range(N_EXPERTS):
            pl.when(seg_ref[e + 1] > seg_ref[e])(lambda e=e: tail_copy(e).wait())

        def fill_block(b, carry):
            fill = pltpu.make_async_copy(zeros, xs_hbm.at[pl.ds(pl.multiple_of(b * bm, bm), bm)], zero_sem)
            fill.start()
            fill.wait()
            return carry

        lax.fori_loop(seg_ref[N_EXPERTS] // bm, xs_hbm.shape[0] // bm, fill_block, 0)

    _pos_copy(pos_hbm, pos_smem, pos_sem, i, slot).wait()

    @pl.when(i + 1 < n)
    def _():
        _pos_copy(pos_hbm, pos_smem, pos_sem, i + 1, 1 - slot).start()

    for r in range(tm):
        src = _tile_row(h_ref, (), r)
        for k in range(TOP_K):
            pltpu.make_async_copy(src, xs_hbm.at[pos_smem[slot, k * tm + r]], row_sem).start(priority=k % 2)


def _scatter_finish(h_ref, xs_hbm, zeros, row_sem):
    tm = h_ref.shape[0] * SUBLANES
    bm = zeros.shape[0]
    for _ in range(TOP_K * tm // bm):
        pltpu.make_async_copy(zeros, xs_hbm.at[pl.ds(0, bm)], row_sem).wait()


def _dispatch_kernel(seg_ref, pos_hbm, h_ref, xs_hbm, pos_smem, zeros, pos_sem, row_sem, zero_sem):
    _scatter_issue(pl.program_id(0), pl.num_programs(0), seg_ref, pos_hbm, h_ref, xs_hbm, pos_smem, zeros,
                   pos_sem, row_sem, zero_sem)
    _scatter_finish(h_ref, xs_hbm, zeros, row_sem)


def _row_block_spec(chunks, index_map):
    return pl.BlockSpec((ROW_TILE // SUBLANES, chunks, SUBLANES, LANES), index_map)


def _scatter_scratch(chunks):
    return [pltpu.SMEM((2, TOP_K * ROW_TILE), I32), pltpu.VMEM((MOE_BLOCK, chunks, LANES), F32),
            pltpu.SemaphoreType.DMA((2,)), pltpu.SemaphoreType.DMA, pltpu.SemaphoreType.DMA]


def _dispatch(h2t, pos_tiles, seg, n_rows):
    _, chunks, _, _ = h2t.shape
    return pl.pallas_call(
        _dispatch_kernel,
        grid_spec=pltpu.PrefetchScalarGridSpec(
            num_scalar_prefetch=1,
            grid=(h2t.shape[0] * SUBLANES // ROW_TILE,),
            in_specs=[pl.BlockSpec(memory_space=pl.ANY), _row_block_spec(chunks, lambda i, seg: (i, 0, 0, 0))],
            out_specs=pl.BlockSpec(memory_space=pl.ANY),
            scratch_shapes=_scatter_scratch(chunks),
        ),
        out_shape=jax.ShapeDtypeStruct((n_rows, chunks, LANES), F32),
        compiler_params=pltpu.CompilerParams(dimension_semantics=("arbitrary",), vmem_limit_bytes=VMEM_LIMIT),
        name="dispatch",
    )(seg, pos_tiles, h2t)


def _dispatch_sc(h2p, pos, n_rows):
    T, chunks, _ = h2p.shape
    info = plsc.get_sparse_core_info()
    n_workers = info.num_cores * info.num_subcores
    win = SC_WINDOW
    per_worker = T // n_workers
    assert per_worker % win == 0
    mesh = plsc.VectorSubcoreMesh(core_axis_name="c", subcore_axis_name="s")

    @functools.partial(
        pl.kernel, mesh=mesh, out_type=jax.ShapeDtypeStruct((n_rows, chunks, LANES), F32),
        scratch_types=[pltpu.VMEM((win,), I32), pltpu.VMEM((win, chunks, LANES), F32)],
        name="dispatch_sc")
    def scatter_rows(h_hbm, pos_hbm, xs_hbm, idx_v, rows_v):
        worker = lax.axis_index("s") * info.num_cores + lax.axis_index("c")

        @pl.loop(0, per_worker // win)
        def _(w):
            t0 = worker * per_worker + w * win
            pltpu.sync_copy(h_hbm.at[pl.ds(t0, win)], rows_v)
            for k in range(TOP_K):
                pltpu.sync_copy(pos_hbm.at[pl.ds(k * T + t0, win)], idx_v)
                pltpu.sync_copy(rows_v, xs_hbm.at[idx_v])

    return scatter_rows(h2p, pos)


def _experts_kernel(*refs, scatter_tiles, gather_tiles):
    refs = list(refs)
    take = lambda k: [refs.pop(0) for _ in range(k)]
    blk_e_ref, n_used_ref = take(2)
    seg_ref = take(1)[0] if scatter_tiles else None
    xs_hbm, wg_ref, bg_ref, wu_ref, bu_ref, wd_ref, bd_ref = take(7)
    s_pos_hbm, s_h_ref = take(2) if scatter_tiles else (None, None)
    g_pos_hbm, g_gates_ref, g_x1_ref, g_ys_hbm = take(4) if gather_tiles else (None,) * 4
    ys_hbm = take(1)[0]
    s_xs_hbm = take(1)[0] if scatter_tiles else None
    g_out_ref = take(1)[0] if gather_tiles else None
    xbuf, obuf, in_sem, out_sem = take(4)
    s_pos_smem, s_zeros, s_pos_sem, s_row_sem, s_zero_sem = take(5) if scatter_tiles else (None,) * 5
    g_pos_smem, g_buf, g_pos_sem, g_row_sem = take(4) if gather_tiles else (None,) * 4
    assert not refs
    del blk_e_ref
    bt = xbuf.shape[1]
    i = pl.program_id(0)
    n = pl.num_programs(0)
    n_used = n_used_ref[0]
    slot = i % 2

    def in_copies(blk, s):
        return [pltpu.make_async_copy(xs_hbm.at[pl.ds(blk * bt, bt), j], xbuf.at[s, :, :, j, :], in_sem.at[s])
                for j in range(SUBLANES)]

    def out_copies(blk, s):
        return [pltpu.make_async_copy(obuf.at[s, :, :, j, :], ys_hbm.at[pl.ds(blk * bt, bt), j], out_sem.at[s])
                for j in range(SUBLANES)]

    @pl.when(i == 0)
    def _():
        for c in in_copies(0, 0):
            c.start()

    @pl.when(i + 1 < n_used)
    def _():
        for c in in_copies(i + 1, 1 - slot):
            c.start()

    @pl.when(i >= 2)
    def _():
        for c in out_copies(i - 2, slot):
            c.wait()

    if scatter_tiles:
        pl.when(i < scatter_tiles)(lambda: _scatter_issue(
            i, scatter_tiles, seg_ref, s_pos_hbm, s_h_ref, s_xs_hbm, s_pos_smem, s_zeros, s_pos_sem, s_row_sem,
            s_zero_sem))
    if gather_tiles:
        pl.when(i < gather_tiles)(lambda: _gather_issue(
            i, gather_tiles, g_pos_hbm, g_ys_hbm, g_pos_smem, g_buf, g_pos_sem, g_row_sem))

    @pl.when(i >= n_used)
    def _():
        obuf[slot] = jnp.zeros(obuf.shape[1:], F32)

    @pl.when(i < n_used)
    def _():
        for c in in_copies(i, slot):
            c.wait()
        x = _from_tiled(xbuf, (slot,), slice(0, bt)).astype(BF16)
        g = _dot(x, wg_ref[...]) + bg_ref[...]
        u = _dot(x, wu_ref[...]) + bu_ref[...]
        g = jnp.minimum(g, SWIGLU_LIMIT)
        u = jnp.clip(u, -SWIGLU_LIMIT, SWIGLU_LIMIT)
        act = (u + 1.0) * (g * jax.nn.sigmoid(SWIGLU_ALPHA * g))
        _to_tiled(obuf, (slot,), _dot(act.astype(BF16), wd_ref[...]) + bd_ref[...])

    for c in out_copies(i, slot):
        c.start()

    if scatter_tiles:
        pl.when(i < scatter_tiles)(lambda: _scatter_finish(s_h_ref, s_xs_hbm, s_zeros, s_row_sem))
    if gather_tiles:
        pl.when((i >= 1) & (i <= gather_tiles))(lambda: _gather_finish(
            i, g_gates_ref, g_x1_ref, g_out_ref, g_buf, g_row_sem))

    @pl.when(i == n - 1)
    def _():
        for c in out_copies(i - 1, 1 - slot) + out_copies(i, slot):
            c.wait()


def _experts(xs, blk_e, n_used, p, scatter=None, gather=None):
    n_rows, chunks, _ = xs.shape
    D = chunks * LANES
    bm = MOE_BLOCK
    n_blocks = n_rows // bm
    d_ff = p["w_gate"].shape[2]
    assert n_blocks >= 2
    hbm = pl.BlockSpec(memory_space=pl.ANY)
    wspec = lambda a, b: pl.BlockSpec((None, a, b), lambda i, be, *_: (be[i], 0, 0))
    by_tile = (n_rows // SUBLANES, SUBLANES, chunks, LANES)
    prefetch = [blk_e, n_used]
    in_specs = [hbm, wspec(D, d_ff), wspec(1, d_ff), wspec(D, d_ff), wspec(1, d_ff), wspec(d_ff, D), wspec(1, D)]
    args = [xs.reshape(by_tile), p["w_gate"], p["b_gate"], p["w_up"], p["b_up"], p["w_down"], p["b_down"]]
    out_specs = [hbm]
    out_shape = [jax.ShapeDtypeStruct(by_tile, F32)]
    block_buf = pltpu.VMEM((2, bm // SUBLANES, chunks, SUBLANES, LANES), F32)
    scratch = [block_buf, block_buf, pltpu.SemaphoreType.DMA((2,)), pltpu.SemaphoreType.DMA((2,))]
    scatter_tiles = gather_tiles = 0
    if scatter is not None:
        h2t, s_pos, seg, s_rows = scatter
        scatter_tiles = h2t.shape[0] * SUBLANES // ROW_TILE
        assert scatter_tiles <= n_blocks
        prefetch.append(seg)
        in_specs += [hbm, _row_block_spec(chunks, lambda i, *_: (jnp.minimum(i, scatter_tiles - 1), 0, 0, 0))]
        args += [s_pos, h2t]
        out_specs.append(hbm)
        out_shape.append(jax.ShapeDtypeStruct((s_rows, chunks, LANES), F32))
    if gather is not None:
        x1, gates_tok, g_ys, g_pos = gather
        gather_tiles = x1.shape[0] // ROW_TILE
        assert gather_tiles + 1 <= n_blocks
        lag = lambda i, *_: (jnp.clip(i - 1, 0, gather_tiles - 1), 0)
        in_specs += [hbm, pl.BlockSpec((ROW_TILE, TOP_K), lag), pl.BlockSpec((ROW_TILE, D), lag), hbm]
        args += [g_pos, gates_tok, x1, g_ys]
        out_specs.append(pl.BlockSpec((ROW_TILE, D), lag))
        out_shape.append(jax.ShapeDtypeStruct(x1.shape, F32))
    if scatter is not None:
        scratch += _scatter_scratch(chunks)
    if gather is not None:
        scratch += _gather_scratch(chunks)
    outs = pl.pallas_call(
        functools.partial(_experts_kernel, scatter_tiles=scatter_tiles, gather_tiles=gather_tiles),
        grid_spec=pltpu.PrefetchScalarGridSpec(
            num_scalar_prefetch=len(prefetch), grid=(n_blocks,), in_specs=in_specs, out_specs=out_specs,
            scratch_shapes=scratch),
        out_shape=out_shape,
        compiler_params=pltpu.CompilerParams(dimension_semantics=("arbitrary",), vmem_limit_bytes=VMEM_LIMIT),
        name="experts" + ("_dispatch" if scatter is not None else "") + ("_combine" if gather is not None else ""),
    )(*prefetch, *args)
    outs = list(outs)
    ys = outs.pop(0).reshape(n_rows, chunks, LANES)
    xs_other = outs.pop(0) if scatter is not None else None
    out_other = outs.pop(0) if gather is not None else None
    return ys, xs_other, out_other


def _gather_issue(j, n, pos_hbm, ys_hbm, pos_smem, buf, pos_sem, row_sem):
    slot = j % 2

    @pl.when(j == 0)
    def _():
        _pos_copy(pos_hbm, pos_smem, pos_sem, 0, 0).start()

    _pos_copy(pos_hbm, pos_smem, pos_sem, j, slot).wait()

    @pl.when(j + 1 < n)
    def _():
        _pos_copy(pos_hbm, pos_smem, pos_sem, j + 1, 1 - slot).start()

    for r in range(buf.shape[1] * SUBLANES):
        pltpu.make_async_copy(ys_hbm.at[pos_smem[slot, r]], _tile_row(buf, (slot,), r),
                              row_sem.at[slot]).start(priority=r % 2)


def _gather_finish(j, gates_ref, x1_ref, out_ref, buf, row_sem):
    tm = x1_ref.shape[0]
    done = 1 - j % 2
    pltpu.make_async_copy(buf.at[done], buf.at[done], row_sem.at[done]).wait()
    bt = tm // SUBLANES
    y = gates_ref[:, 0:1] * _from_tiled(buf, (done,), slice(0, bt))
    for k in range(1, TOP_K):
        y = y + gates_ref[:, k:k + 1] * _from_tiled(buf, (done,), slice(k * bt, (k + 1) * bt))
    out_ref[...] = x1_ref[...] + y


def _combine_kernel(pos_hbm, gates_ref, x1_ref, ys_hbm, out_ref, pos_smem, buf, pos_sem, row_sem):
    j = pl.program_id(0)
    n = pl.num_programs(0) - 1
    pl.when(j < n)(lambda: _gather_issue(j, n, pos_hbm, ys_hbm, pos_smem, buf, pos_sem, row_sem))
    pl.when(j >= 1)(lambda: _gather_finish(j, gates_ref, x1_ref, out_ref, buf, row_sem))


def _gather_scratch(chunks):
    return [pltpu.SMEM((2, TOP_K * ROW_TILE), I32),
            pltpu.VMEM((2, TOP_K * ROW_TILE // SUBLANES, chunks, SUBLANES, LANES), F32),
            pltpu.SemaphoreType.DMA((2,)), pltpu.SemaphoreType.DMA((2,))]


def _combine(x1, gates_tok, ys, pos_tiles):
    T, D = x1.shape
    tm = ROW_TILE
    n = T // tm
    lag = lambda i: (jnp.maximum(i - 1, 0), 0)
    return pl.pallas_call(
        _combine_kernel,
        grid=(n + 1,),
        in_specs=[pl.BlockSpec(memory_space=pl.ANY), pl.BlockSpec((tm, TOP_K), lag),
                  pl.BlockSpec((tm, D), lag), pl.BlockSpec(memory_space=pl.ANY)],
        out_specs=pl.BlockSpec((tm, D), lag),
        out_shape=jax.ShapeDtypeStruct((T, D), F32),
        scratch_shapes=_gather_scratch(D // LANES),
        compiler_params=pltpu.CompilerParams(dimension_semantics=("arbitrary",), vmem_limit_bytes=VMEM_LIMIT),
        name="combine",
    )(pos_tiles, gates_tok, x1, ys)


def _route(eidx, rank, tile_counts):
    T = eidx.shape[1]
    bm = MOE_BLOCK
    n_tiles = T // TOKEN_TILE
    tc = tile_counts[:, :, 0]
    counts = jnp.sum(tc, axis=0)
    tile_off = jnp.cumsum(tc, axis=0) - tc
    pcounts = (counts + bm - 1) // bm * bm
    pends = jnp.cumsum(pcounts)
    base = (pends - pcounts)[None, :] + tile_off
    sel = eidx.reshape(TOP_K, n_tiles, 1, TOKEN_TILE) == jnp.arange(N_EXPERTS, dtype=I32)[None, None, :, None]
    pos = jnp.sum(jnp.where(sel, base[None, :, :, None], 0), axis=2).reshape(TOP_K, T) + rank
    n_blocks = (TOP_K * T) // bm + N_EXPERTS
    blk_start = jnp.arange(n_blocks, dtype=I32) * bm
    blk_e = jnp.minimum(jnp.sum((pends[None, :] <= blk_start[:, None]).astype(I32), axis=1), N_EXPERTS - 1)
    n_used = (pends[-1] // bm).astype(I32).reshape(1)
    seg = jnp.concatenate([jnp.zeros((1,), I32), pends.astype(I32)])
    pos_tiles = pos.reshape(TOP_K, T // ROW_TILE, ROW_TILE).transpose(1, 0, 2).reshape(-1)
    return pos.reshape(-1), pos_tiles, blk_e, n_used, seg, n_blocks * bm


def _layer(x, p):
    batch, seq, D = x.shape
    x2 = x.reshape(batch * seq, D)
    q, k, v, zu, zv = _in_proj(x2, batch, seq, p)
    o, lse = zip(*[_attention(q[g], k[g], v[g], g) for g in range(N_GROUPS)])
    x1, h2p, gates, eidx, rank, tile_counts = _post_mix(x2, seq, o, lse, zu, zv, p)
    pos, pos_tiles, blk_e, n_used, seg, n_rows = _route(eidx, rank, tile_counts)
    xs = _dispatch_sc(h2p.reshape(batch * seq, D // LANES, LANES), pos, n_rows)
    ys, _, _ = _experts(xs, blk_e, n_used, p)
    out = _combine(x1, gates.T, ys, pos_tiles)
    return out.reshape(batch, seq, D)


def _rope_tables(seq):
    half = ROT_DIM // 2
    inv_freq = jnp.power(ROPE_THETA, -2.0 * jnp.arange(half, dtype=F32) / ROT_DIM)
    ang = jnp.arange(seq, dtype=F32)[:, None] * inv_freq[None, :]
    cos, sin = jnp.cos(ang), jnp.sin(ang)
    zeros = jnp.zeros((seq, HEAD_DIM - ROT_DIM), F32)
    zh = jnp.zeros((seq, half), F32)
    cos_t = jnp.concatenate([cos, cos, zeros + 1.0], axis=1)
    sa = jnp.concatenate([-sin, zh, zeros], axis=1)
    sb = jnp.concatenate([zh, sin, zeros], axis=1)
    rep = LANES // HEAD_DIM
    return tuple(jnp.tile(t, (1, rep)) for t in (cos_t, sa, sb))


def kernel(x_prompt, x_sample, attn_norm_w, w_in, q_norm_w, k_norm_w, sgu_ln_w, sgu_ln_b, sgu_w, sgu_b,
           att_out_norm_w, sgu_out_norm_w, w_out, ffn_norm_w, router_w, router_b,
           w_gate, b_gate, w_up, b_up, w_down, b_down):
    depth = w_in.shape[0]
    n_heads = ATT_WIDTH // HEAD_DIM
    blk = np.arange(MXU_DIM) // HEAD_DIM
    tri = np.arange(TOKEN_TILE)
    xa, xb = x_prompt, x_sample
    rope = {x.shape[1]: _rope_tables(x.shape[1]) for x in (xa, xb)}
    for l in range(depth):
        shared = dict(
            attn_norm_w=attn_norm_w[l][None], w_in=w_in[l].astype(BF16),
            q_norm_w=jnp.tile(q_norm_w[l], n_heads)[None], k_norm_w=jnp.tile(k_norm_w[l], n_heads)[None],
            head_ones=jnp.asarray(blk[:, None] == blk[None, :], BF16),
            sgu_ln_w=sgu_ln_w[l][None], sgu_ln_b=sgu_ln_b[l][None],
            sgu_w=sgu_w[l].reshape(SGU_GROUPS * SGU_CHUNK, SGU_CHUNK).astype(BF16),
            sgu_b=jnp.repeat(sgu_b[l].T, SGU_GROUP_DIM, axis=1),
            att_out_norm_w=att_out_norm_w[l][None], sgu_out_norm_w=sgu_out_norm_w[l][None],
            w_out=w_out[l].astype(BF16), ffn_norm_w=ffn_norm_w[l][None],
            router_wt=router_w[l].T, router_b=router_b[l][:, None],
            tri=jnp.asarray(tri[:, None] < tri[None, :], BF16),
            w_gate=w_gate[l].astype(BF16), b_gate=b_gate[l][:, None, :],
            w_up=w_up[l].astype(BF16), b_up=b_up[l][:, None, :],
            w_down=w_down[l].astype(BF16), b_down=b_down[l][:, None, :],
        )
        xa, xb = (_layer(x, dict(shared, rope=rope)) for x in (xa, xb))
    return xa, xb
```

```python
import functools

import numpy as np
import jax
import jax.numpy as jnp
from jax import lax
from jax.experimental import pallas as pl
from jax.experimental.pallas import tpu as pltpu
from jax.experimental.pallas import tpu_sc as plsc

F32 = jnp.float32
BF16 = jnp.bfloat16
I32 = jnp.int32

HEAD_DIM = 64
ATT_WINDOWS = ((128, 1), (512, 4), (2048, 16))
N_GROUPS = len(ATT_WINDOWS)
HEADS_PER_GROUP = 4
GROUP_WIDTH = HEADS_PER_GROUP * HEAD_DIM
ATT_WIDTH = N_GROUPS * GROUP_WIDTH
SGU_WIDTH = 256
SGU_GROUP_DIM = 64
SGU_GROUPS = SGU_WIDTH // SGU_GROUP_DIM
SGU_CHUNK = 128
ROT_DIM = HEAD_DIM // 4
ROPE_THETA = 500000.0
N_EXPERTS = 32
TOP_K = 4
SWIGLU_LIMIT = 7.0
SWIGLU_ALPHA = 1.702
NORM_EPS = 1e-6
LN_EPS = 1e-5
NEG_INF = -1e30

LANES = 128
SUBLANES = 8
MXU_DIM = 256
TOKEN_TILE = 512
ATT_Q_BLOCK = 256
ATT_SUB_Q = 128
MOE_BLOCK = 512
ROW_TILE = 256
SC_WINDOW = 64
VMEM_LIMIT = 48 * 1024 * 1024


def _dot(a, b):
    return jnp.dot(a, b, preferred_element_type=F32)


def _dot_nt(a, b, precision=None):
    return lax.dot_general(a, b, (((1,), (1,)), ((), ())), precision=precision, preferred_element_type=F32)


def _rms(x, w):
    return x * lax.rsqrt(jnp.mean(x * x, axis=-1, keepdims=True) + NORM_EPS) * w


def _gelu(x):
    return 0.5 * x * (1.0 + lax.erf(x * np.float32(np.sqrt(0.5))))


def _to_tiled(ref, idx, value):
    rows = value.shape[0]
    for c in range(value.shape[1] // LANES):
        ref[idx + (slice(None), c)] = value[:, c * LANES:(c + 1) * LANES].reshape(rows // SUBLANES, SUBLANES, LANES)


def _from_tiled(ref, idx, row_tiles):
    chunks = ref.shape[-3]
    n = (row_tiles.stop - row_tiles.start) * SUBLANES
    return jnp.concatenate([ref[idx + (row_tiles, c)].reshape(n, LANES) for c in range(chunks)], axis=1)


def _in_proj_kernel(x_ref, anw_ref, w_ref, qnw_ref, knw_ref, ones_ref, cos_ref, sa_ref, sb_ref, lnw_ref, lnb_ref,
                    *refs):
    qkv_refs = refs[:3 * N_GROUPS]
    zu_ref, zv_ref, stage = refs[3 * N_GROUPS:]
    tm = x_ref.shape[0]
    h = _rms(x_ref[...], anw_ref[...]).astype(BF16)
    reps = ATT_WIDTH // LANES
    cos = jnp.concatenate([cos_ref[...]] * reps, axis=1)
    sa = jnp.concatenate([sa_ref[...]] * reps, axis=1)
    sb = jnp.concatenate([sb_ref[...]] * reps, axis=1)
    ones = ones_ref[...]

    def head_norm_rope(t, nw):
        sq = (t * t).astype(BF16)
        parts = []
        for j in range(ATT_WIDTH // MXU_DIM):
            sl = slice(j * MXU_DIM, (j + 1) * MXU_DIM)
            parts.append(_dot(sq[:, sl], ones))
        ssum = jnp.concatenate(parts, axis=1)
        t = t * lax.rsqrt(ssum * (1.0 / HEAD_DIM) + NORM_EPS) * nw
        half = ROT_DIM // 2
        return t * cos + pltpu.roll(t, ATT_WIDTH - half, 1) * sa + pltpu.roll(t, half, 1) * sb

    def emit(which, t):
        n_chunks = ATT_WIDTH // LANES
        per_group = GROUP_WIDTH // LANES
        for c in range(n_chunks):
            stage[which * n_chunks + c] = t[:, c * LANES:(c + 1) * LANES]
        for g, (_, dil) in enumerate(ATT_WINDOWS):
            out = qkv_refs[which * N_GROUPS + g]
            first = which * n_chunks + g * per_group
            for r in range(dil):
                rows = pl.ds(r, tm // dil, stride=dil)
                out[r] = jnp.concatenate([stage[first + c, rows, :] for c in range(per_group)], axis=1).astype(BF16)

    q = head_norm_rope(_dot(h, w_ref[:, 0:ATT_WIDTH]), qnw_ref[...])
    emit(0, q * (HEAD_DIM ** -0.5))
    emit(1, head_norm_rope(_dot(h, w_ref[:, ATT_WIDTH:2 * ATT_WIDTH]), knw_ref[...]))
    emit(2, _dot(h, w_ref[:, 2 * ATT_WIDTH:3 * ATT_WIDTH]))
    z = _dot(h, w_ref[:, 3 * ATT_WIDTH:3 * ATT_WIDTH + 2 * SGU_WIDTH])
    zu_ref[...] = _gelu(z[:, :SGU_WIDTH])
    gv = _gelu(z[:, SGU_WIDTH:])
    mu = jnp.mean(gv, axis=-1, keepdims=True)
    var = jnp.mean(jnp.square(gv - mu), axis=-1, keepdims=True)
    zv_ref[...] = ((gv - mu) * lax.rsqrt(var + LN_EPS) * lnw_ref[...] + lnb_ref[...]).astype(BF16)


def _phase_spec(tm, dil, n_seq_tiles):
    return pl.BlockSpec((None, dil, tm // dil, GROUP_WIDTH), lambda i: (i // n_seq_tiles, 0, i % n_seq_tiles, 0))


def _in_proj(x2, batch, seq, p):
    T, D = x2.shape
    tm = TOKEN_TILE
    n_seq_tiles = seq // tm
    in_width = p["w_in"].shape[1]
    const = lambda shape: pl.BlockSpec(shape, lambda i: (0,) * len(shape))
    rope = pl.BlockSpec((tm, LANES), lambda i: (i % n_seq_tiles, 0))
    row = lambda w: pl.BlockSpec((tm, w), lambda i: (i, 0))
    qkv_specs = [_phase_spec(tm, dil, n_seq_tiles) for _ in range(3) for _, dil in ATT_WINDOWS]
    qkv_shapes = [jax.ShapeDtypeStruct((batch, dil, seq // dil, GROUP_WIDTH), BF16)
                  for _ in range(3) for _, dil in ATT_WINDOWS]
    outs = pl.pallas_call(
        _in_proj_kernel,
        grid=(T // tm,),
        in_specs=[row(D), const((1, D)), const((D, in_width)), const((1, ATT_WIDTH)), const((1, ATT_WIDTH)),
                  const((MXU_DIM, MXU_DIM)), rope, rope, rope, const((1, SGU_WIDTH)), const((1, SGU_WIDTH))],
        out_specs=qkv_specs + [row(SGU_WIDTH), row(SGU_WIDTH)],
        out_shape=qkv_shapes + [jax.ShapeDtypeStruct((T, SGU_WIDTH), F32), jax.ShapeDtypeStruct((T, SGU_WIDTH), BF16)],
        scratch_shapes=[pltpu.VMEM((3 * ATT_WIDTH // LANES, tm, LANES), F32)],
        compiler_params=pltpu.CompilerParams(dimension_semantics=("arbitrary",), vmem_limit_bytes=VMEM_LIMIT),
        name="in_proj",
    )(x2, p["attn_norm_w"], p["w_in"], p["q_norm_w"], p["k_norm_w"], p["head_ones"],
      *p["rope"][seq], p["sgu_ln_w"], p["sgu_ln_b"])
    q, k, v = (outs[i * N_GROUPS:(i + 1) * N_GROUPS] for i in range(3))
    return q, k, v, outs[-2], outs[-1]


def _attn_kernel(q_ref, kp_ref, km_ref, kn_ref, vp_ref, vm_ref, vn_ref, o_ref, lse_ref, *, sub_len, steps):
    lq = q_ref.shape[0]
    sq = min(lq, ATT_SUB_Q)
    lk = sq + 2 * steps
    nh = HEADS_PER_GROUP
    lb = pl.program_id(2)
    kk = jnp.concatenate([kp_ref[...], km_ref[...], kn_ref[...]], axis=0)
    vv = jnp.concatenate([vp_ref[...], vm_ref[...], vn_ref[...]], axis=0)
    head = lax.broadcasted_iota(I32, (1, GROUP_WIDTH), 1) // HEAD_DIM
    hm_f = [(head == h).astype(F32) for h in range(nh)]
    hm_b = [m.astype(BF16) for m in hm_f]
    qi = lax.broadcasted_iota(I32, (sq, 1), 0)
    kj = lax.broadcasted_iota(I32, (sq, lk), 1)
    for j in range(lq // sq):
        first = lb * lq + j * sq
        lo = jnp.maximum(qi, steps - first)
        hi = jnp.minimum(qi + 2 * steps, sub_len - 1 + steps - first)
        mask = ((kj - lo).astype(jnp.uint32) <= (hi - lo).astype(jnp.uint32))[None]
        qj = q_ref[j * sq:(j + 1) * sq, :]
        qs = jnp.concatenate([qj * hm_b[h] for h in range(nh)], axis=0)
        s = _dot_nt(qs, kk[j * sq:j * sq + lk]).reshape(nh, sq, lk)
        s = jnp.where(mask, s, NEG_INF)
        m = jnp.max(s, axis=-1, keepdims=True)
        pr = jnp.exp(s - m)
        den = jnp.sum(pr, axis=-1, keepdims=True)
        oh = _dot(pr.reshape(nh * sq, lk).astype(BF16), vv[j * sq:j * sq + lk]).reshape(nh, sq, GROUP_WIDTH) / den
        lh = m + jnp.log(den)
        o_ref[j * sq:(j + 1) * sq, :] = sum(oh[h] * hm_f[h] for h in range(nh))
        lse_ref[j * sq:(j + 1) * sq, :] = sum(lh[h] * hm_f[h] for h in range(nh))


def _attention(q, k, v, group):
    window, dil = ATT_WINDOWS[group]
    steps = window // (2 * dil)
    batch, _, sub_len, _ = q.shape
    assert sub_len % steps == 0
    lq = min(ATT_Q_BLOCK, sub_len)
    assert lq % steps == 0 and sub_len % lq == 0
    per_q = lq // steps
    n_halo = sub_len // steps
    main = pl.BlockSpec((None, None, lq, GROUP_WIDTH), lambda b, r, i: (b, r, i, 0))
    prev = pl.BlockSpec((None, None, steps, GROUP_WIDTH), lambda b, r, i: (b, r, jnp.maximum(i * per_q - 1, 0), 0))
    nxt = pl.BlockSpec((None, None, steps, GROUP_WIDTH),
                       lambda b, r, i: (b, r, jnp.minimum((i + 1) * per_q, n_halo - 1), 0))
    return pl.pallas_call(
        functools.partial(_attn_kernel, sub_len=sub_len, steps=steps),
        grid=(batch, dil, sub_len // lq),
        in_specs=[main, prev, main, nxt, prev, main, nxt],
        out_specs=[main, main],
        out_shape=[jax.ShapeDtypeStruct(q.shape, F32)] * 2,
        compiler_params=pltpu.CompilerParams(dimension_semantics=("arbitrary",) * 3, vmem_limit_bytes=VMEM_LIMIT),
        name=f"attention_g{group}",
    )(q, k, k, k, v, v, v)


def _post_mix_kernel(x_ref, *refs):
    o_refs = refs[:N_GROUPS]
    l_refs = refs[N_GROUPS:2 * N_GROUPS]
    (zu_ref, zv_ref, sguw_ref, sgub_ref, aonw_ref, sonw_ref, wout_ref, fnw_ref, rwt_ref, rb_ref, tri_ref,
     x1_ref, h2_hbm, gates_ref, eidx_ref, rank_ref, cnt_ref, stage, hstage, hsem) = refs[2 * N_GROUPS:]
    tm = x_ref.shape[0]

    def token_order(ref, slot):
        dil = ref.shape[0]
        if dil == 1:
            return ref[0]
        per_group = GROUP_WIDTH // LANES
        for r in range(dil):
            for c in range(per_group):
                stage[slot * per_group + c, pl.ds(r, tm // dil, stride=dil), :] = ref[r, :, c * LANES:(c + 1) * LANES]
        return jnp.concatenate([stage[slot * per_group + c] for c in range(per_group)], axis=1)

    o = [token_order(ref, g) for g, ref in enumerate(o_refs)]
    l = [token_order(ref, N_GROUPS + g) for g, ref in enumerate(l_refs)]
    lmax = jnp.maximum(jnp.maximum(l[0], l[1]), l[2])
    e = [jnp.exp(lg - lmax) for lg in l]
    esum = e[0] + e[1] + e[2]
    att = (e[0] / esum) * o[0] + (e[1] / esum) * o[1] + (e[2] / esum) * o[2]
    att_n = _rms(att, aonw_ref[...]).astype(BF16)
    cgrp = lax.broadcasted_iota(I32, (1, SGU_WIDTH), 1) // SGU_GROUP_DIM
    sguw = sguw_ref[...]
    gates = []
    for c in range(tm // SGU_CHUNK):
        r = _dot(sguw, zv_ref[c * SGU_CHUNK:(c + 1) * SGU_CHUNK, :])
        g = sgub_ref[...]
        for grp in range(SGU_GROUPS):
            g = g + r[grp * SGU_CHUNK:(grp + 1) * SGU_CHUNK, :] * (cgrp == grp).astype(F32)
        gates.append(g)
    sgu = zu_ref[...] * jnp.concatenate(gates, axis=0)
    sgu_n = _rms(sgu, sonw_ref[...]).astype(BF16)
    x1 = x_ref[...] + _dot(att_n, wout_ref[0:GROUP_WIDTH, :]) + _dot(sgu_n, wout_ref[GROUP_WIDTH:, :])
    x1_ref[...] = x1
    h2 = _rms(x1, fnw_ref[...])
    _to_tiled(hstage, (), h2)
    bt = tm // SUBLANES
    h2_copies = [pltpu.make_async_copy(hstage.at[:, :, j, :], h2_hbm.at[pl.ds(pl.program_id(0) * bt, bt), j], hsem)
                 for j in range(SUBLANES)]
    for c in h2_copies:
        c.start()
    logits = _dot_nt(rwt_ref[...], h2, precision=lax.Precision.HIGHEST) + rb_ref[...]
    eiota = lax.broadcasted_iota(I32, (N_EXPERTS, tm), 0)
    vals, idxs = [], []
    for _ in range(TOP_K):
        m = jnp.max(logits, axis=0, keepdims=True)
        idx = jnp.min(jnp.where(logits == m, eiota, N_EXPERTS), axis=0, keepdims=True)
        vals.append(m)
        idxs.append(idx)
        logits = jnp.where(eiota == idx, -jnp.inf, logits)
    exps = [jnp.exp(v - vals[0]) for v in vals]
    den = exps[0] + exps[1] + exps[2] + exps[3]
    gates_ref[...] = jnp.concatenate([ex / den for ex in exps], axis=0)
    eidx_ref[...] = jnp.concatenate(idxs, axis=0)
    onehots = [(eiota == idx).astype(F32) for idx in idxs]
    chosen = onehots[0] + onehots[1] + onehots[2] + onehots[3]
    before = _dot(chosen.astype(BF16), tri_ref[...])
    rank_ref[...] = jnp.concatenate(
        [jnp.sum(oh * before, axis=0, keepdims=True) for oh in onehots], axis=0).astype(I32)
    cnt_ref[...] = jnp.broadcast_to(jnp.sum(chosen, axis=1, keepdims=True), (N_EXPERTS, LANES)).astype(I32)
    for c in h2_copies:
        c.wait()


def _post_mix(x2, seq, o, lse, zu, zv, p):
    T, D = x2.shape
    tm = TOKEN_TILE
    n_tiles = T // tm
    n_seq_tiles = seq // tm
    chunks = D // LANES
    const = lambda shape: pl.BlockSpec(shape, lambda i: (0,) * len(shape))
    row = lambda w: pl.BlockSpec((tm, w), lambda i: (i, 0))
    colt = pl.BlockSpec((TOP_K, tm), lambda i: (0, i))
    phase = [_phase_spec(tm, dil, n_seq_tiles) for _, dil in ATT_WINDOWS]
    return pl.pallas_call(
        _post_mix_kernel,
        grid=(n_tiles,),
        in_specs=[row(D)] + phase + phase + [row(SGU_WIDTH), row(SGU_WIDTH),
                  const((SGU_GROUPS * SGU_CHUNK, SGU_CHUNK)), const((SGU_CHUNK, SGU_WIDTH)),
                  const((1, GROUP_WIDTH)), const((1, SGU_WIDTH)), const((GROUP_WIDTH + SGU_WIDTH, D)),
                  const((1, D)), const((N_EXPERTS, D)), const((N_EXPERTS, 1)), const((tm, tm))],
        out_specs=[row(D), pl.BlockSpec(memory_space=pl.ANY),
                   colt, colt, colt, pl.BlockSpec((None, N_EXPERTS, LANES), lambda i: (i, 0, 0))],
        out_shape=[jax.ShapeDtypeStruct((T, D), F32),
                   jax.ShapeDtypeStruct((T // SUBLANES, SUBLANES, chunks, LANES), F32),
                   jax.ShapeDtypeStruct((TOP_K, T), F32), jax.ShapeDtypeStruct((TOP_K, T), I32),
                   jax.ShapeDtypeStruct((TOP_K, T), I32), jax.ShapeDtypeStruct((n_tiles, N_EXPERTS, LANES), I32)],
        scratch_shapes=[pltpu.VMEM((2 * N_GROUPS * GROUP_WIDTH // LANES, tm, LANES), F32),
                        pltpu.VMEM((tm // SUBLANES, chunks, SUBLANES, LANES), F32), pltpu.SemaphoreType.DMA],
        compiler_params=pltpu.CompilerParams(dimension_semantics=("arbitrary",), vmem_limit_bytes=VMEM_LIMIT),
        name="post_mix",
    )(x2, *o, *lse, zu, zv, p["sgu_w"], p["sgu_b"], p["att_out_norm_w"],
      p["sgu_out_norm_w"], p["w_out"], p["ffn_norm_w"], p["router_wt"], p["router_b"], p["tri"])


def _sc_workers():
    info = plsc.get_sparse_core_info()
    return info.num_cores, info.num_cores * info.num_subcores


def _dispatch_sc(h2p, pos, pad_pos, n_rows):
    T, chunks, _ = h2p.shape
    n_cores, n_workers = _sc_workers()
    win = SC_WINDOW
    per_worker = T // n_workers
    pad_per_worker = pad_pos.shape[0] // n_workers
    assert per_worker % win == 0 and pad_per_worker % win == 0
    mesh = plsc.VectorSubcoreMesh(core_axis_name="c", subcore_axis_name="s")

    @functools.partial(
        pl.kernel, mesh=mesh, out_type=jax.ShapeDtypeStruct((n_rows, chunks, LANES), F32),
        scratch_types=[pltpu.VMEM((win,), I32), pltpu.VMEM((win, chunks, LANES), F32)],
        name="dispatch_sc")
    def scatter_rows(h_hbm, pos_hbm, pad_hbm, zeros_hbm, xs_hbm, idx_v, rows_v):
        worker = lax.axis_index("s") * n_cores + lax.axis_index("c")

        @pl.loop(0, per_worker // win)
        def _(w):
            t0 = worker * per_worker + w * win
            pltpu.sync_copy(h_hbm.at[pl.ds(t0, win)], rows_v)
            for k in range(TOP_K):
                pltpu.sync_copy(pos_hbm.at[pl.ds(k * T + t0, win)], idx_v)
                pltpu.sync_copy(rows_v, xs_hbm.at[idx_v])

        pltpu.sync_copy(zeros_hbm, rows_v)

        @pl.loop(0, pad_per_worker // win)
        def _(w):
            pltpu.sync_copy(pad_hbm.at[pl.ds(worker * pad_per_worker + w * win, win)], idx_v)
            pltpu.sync_copy(rows_v, xs_hbm.at[idx_v])

    return scatter_rows(h2p, pos, pad_pos, jnp.zeros((win, chunks, LANES), F32))


def _gather_sc(ys, pos):
    _, chunks, _ = ys.shape
    n_assign = pos.shape[0]
    n_cores, n_workers = _sc_workers()
    win = SC_WINDOW
    per_worker = n_assign // n_workers
    assert per_worker % win == 0
    mesh = plsc.VectorSubcoreMesh(core_axis_name="c", subcore_axis_name="s")

    @functools.partial(
        pl.kernel, mesh=mesh, out_type=jax.ShapeDtypeStruct((n_assign, chunks, LANES), F32),
        scratch_types=[pltpu.VMEM((win,), I32), pltpu.VMEM((win, chunks, LANES), F32)],
        name="gather_sc")
    def gather_rows(ys_hbm, pos_hbm, out_hbm, idx_v, rows_v):
        worker = lax.axis_index("s") * n_cores + lax.axis_index("c")

        @pl.loop(0, per_worker // win)
        def _(w):
            a0 = worker * per_worker + w * win
            pltpu.sync_copy(pos_hbm.at[pl.ds(a0, win)], idx_v)
            pltpu.sync_copy(ys_hbm.at[idx_v], rows_v)
            pltpu.sync_copy(rows_v, out_hbm.at[pl.ds(a0, win)])

    return gather_rows(ys, pos)


def _experts_kernel(blk_e_ref, n_used_ref, xs_hbm, wg_ref, bg_ref, wu_ref, bu_ref, wd_ref, bd_ref, ys_hbm,
                    xbuf, obuf, in_sem, out_sem):
    del blk_e_ref
    bt = xbuf.shape[1]
    i = pl.program_id(0)
    n = pl.num_programs(0)
    n_used = n_used_ref[0]
    slot = i % 2

    def in_copies(blk, s):
        return [pltpu.make_async_copy(xs_hbm.at[pl.ds(blk * bt, bt), j], xbuf.at[s, :, :, j, :], in_sem.at[s])
                for j in range(SUBLANES)]

    def out_copies(blk, s):
        return [pltpu.make_async_copy(obuf.at[s, :, :, j, :], ys_hbm.at[pl.ds(blk * bt, bt), j], out_sem.at[s])
                for j in range(SUBLANES)]

    @pl.when(i == 0)
    def _():
        for c in in_copies(0, 0):
            c.start()

    @pl.when(i + 1 < n_used)
    def _():
        for c in in_copies(i + 1, 1 - slot):
            c.start()

    @pl.when(i >= 2)
    def _():
        for c in out_copies(i - 2, slot):
            c.wait()

    @pl.when(i >= n_used)
    def _():
        obuf[slot] = jnp.zeros(obuf.shape[1:], F32)

    @pl.when(i < n_used)
    def _():
        for c in in_copies(i, slot):
            c.wait()
        x = _from_tiled(xbuf, (slot,), slice(0, bt)).astype(BF16)
        g = _dot(x, wg_ref[...]) + bg_ref[...]
        u = _dot(x, wu_ref[...]) + bu_ref[...]
        g = jnp.minimum(g, SWIGLU_LIMIT)
        u = jnp.clip(u, -SWIGLU_LIMIT, SWIGLU_LIMIT)
        act = (u + 1.0) * (g * jax.nn.sigmoid(SWIGLU_ALPHA * g))
        _to_tiled(obuf, (slot,), _dot(act.astype(BF16), wd_ref[...]) + bd_ref[...])

    for c in out_copies(i, slot):
        c.start()

    @pl.when(i == n - 1)
    def _():
        for c in out_copies(i - 1, 1 - slot) + out_copies(i, slot):
            c.wait()


def _experts(xs, blk_e, n_used, p):
    n_rows, chunks, _ = xs.shape
    D = chunks * LANES
    bm = MOE_BLOCK
    d_ff = p["w_gate"].shape[2]
    assert n_rows // bm >= 2
    wspec = lambda a, b: pl.BlockSpec((None, a, b), lambda i, be, nu: (be[i], 0, 0))
    by_tile = (n_rows // SUBLANES, SUBLANES, chunks, LANES)
    block_buf = pltpu.VMEM((2, bm // SUBLANES, chunks, SUBLANES, LANES), F32)
    ys = pl.pallas_call(
        _experts_kernel,
        grid_spec=pltpu.PrefetchScalarGridSpec(
            num_scalar_prefetch=2,
            grid=(n_rows // bm,),
            in_specs=[pl.BlockSpec(memory_space=pl.ANY), wspec(D, d_ff), wspec(1, d_ff), wspec(D, d_ff),
                      wspec(1, d_ff), wspec(d_ff, D), wspec(1, D)],
            out_specs=pl.BlockSpec(memory_space=pl.ANY),
            scratch_shapes=[block_buf, block_buf, pltpu.SemaphoreType.DMA((2,)), pltpu.SemaphoreType.DMA((2,))],
        ),
        out_shape=jax.ShapeDtypeStruct(by_tile, F32),
        compiler_params=pltpu.CompilerParams(dimension_semantics=("arbitrary",), vmem_limit_bytes=VMEM_LIMIT),
        name="experts",
    )(blk_e, n_used, xs.reshape(by_tile), p["w_gate"], p["b_gate"], p["w_up"], p["b_up"], p["w_down"], p["b_down"])
    return ys.reshape(n_rows, chunks, LANES)


def _combine_kernel(gates_ref, x1_ref, yg_hbm, out_ref, buf, sem):
    tm = x1_ref.shape[0]
    bt = tm // SUBLANES
    j = pl.program_id(0)
    n = pl.num_programs(0) - 1
    tiles_per_k = yg_hbm.shape[0] // TOP_K

    def copies(tile, s):
        return [pltpu.make_async_copy(yg_hbm.at[pl.ds(k * tiles_per_k + tile * bt, bt), r],
                                      buf.at[s, pl.ds(k * bt, bt), :, r, :], sem.at[s])
                for k in range(TOP_K) for r in range(SUBLANES)]

    @pl.when(j < n)
    def _():
        for c in copies(j, j % 2):
            c.start()

    @pl.when(j >= 1)
    def _():
        done = 1 - j % 2
        for c in copies(j - 1, done):
            c.wait()
        y = gates_ref[:, 0:1] * _from_tiled(buf, (done,), slice(0, bt))
        for k in range(1, TOP_K):
            y = y + gates_ref[:, k:k + 1] * _from_tiled(buf, (done,), slice(k * bt, (k + 1) * bt))
        out_ref[...] = x1_ref[...] + y


def _combine(x1, gates_tok, yg):
    T, D = x1.shape
    tm = ROW_TILE
    n = T // tm
    chunks = D // LANES
    lag = lambda i: (jnp.maximum(i - 1, 0), 0)
    return pl.pallas_call(
        _combine_kernel,
        grid=(n + 1,),
        in_specs=[pl.BlockSpec((tm, TOP_K), lag), pl.BlockSpec((tm, D), lag), pl.BlockSpec(memory_space=pl.ANY)],
        out_specs=pl.BlockSpec((tm, D), lag),
        out_shape=jax.ShapeDtypeStruct((T, D), F32),
        scratch_shapes=[pltpu.VMEM((2, TOP_K * tm // SUBLANES, chunks, SUBLANES, LANES), F32),
                        pltpu.SemaphoreType.DMA((2,))],
        compiler_params=pltpu.CompilerParams(dimension_semantics=("arbitrary",), vmem_limit_bytes=VMEM_LIMIT),
        name="combine",
    )(gates_tok, x1, yg.reshape(TOP_K * T // SUBLANES, SUBLANES, chunks, LANES))


def _route(eidx, rank, tile_counts):
    T = eidx.shape[1]
    bm = MOE_BLOCK
    n_tiles = T // TOKEN_TILE
    tc = tile_counts[:, :, 0]
    counts = jnp.sum(tc, axis=0)
    tile_off = jnp.cumsum(tc, axis=0) - tc
    pcounts = (counts + bm - 1) // bm * bm
    pends = jnp.cumsum(pcounts)
    pstarts = pends - pcounts
    base = pstarts[None, :] + tile_off
    sel = eidx.reshape(TOP_K, n_tiles, 1, TOKEN_TILE) == jnp.arange(N_EXPERTS, dtype=I32)[None, None, :, None]
    pos = jnp.sum(jnp.where(sel, base[None, :, :, None], 0), axis=2).reshape(TOP_K, T) + rank
    n_blocks = (TOP_K * T) // bm + N_EXPERTS
    n_rows = n_blocks * bm
    blk_start = jnp.arange(n_blocks, dtype=I32) * bm
    blk_e = jnp.minimum(jnp.sum((pends[None, :] <= blk_start[:, None]).astype(I32), axis=1), N_EXPERTS - 1)
    n_used = (pends[-1] // bm).astype(I32).reshape(1)
    lane = jnp.arange(bm, dtype=I32)[None, :]
    tails = (pstarts + counts)[:, None] + lane
    tails = jnp.where(tails < pends[:, None], tails, n_rows - 1)
    rest = jnp.minimum(pends[-1] + jnp.arange(N_EXPERTS * bm, dtype=I32), n_rows - 1)
    pad_pos = jnp.concatenate([tails.reshape(-1), rest]).astype(I32)
    return pos.reshape(-1).astype(I32), pad_pos, blk_e.astype(I32), n_used, n_rows


def _layer(x, p):
    batch, seq, D = x.shape
    x2 = x.reshape(batch * seq, D)
    q, k, v, zu, zv = _in_proj(x2, batch, seq, p)
    o, lse = zip(*[_attention(q[g], k[g], v[g], g) for g in range(N_GROUPS)])
    x1, h2p, gates, eidx, rank, tile_counts = _post_mix(x2, seq, o, lse, zu, zv, p)
    pos, pad_pos, blk_e, n_used, n_rows = _route(eidx, rank, tile_counts)
    xs = _dispatch_sc(h2p.reshape(batch * seq, D // LANES, LANES), pos, pad_pos, n_rows)
    ys = _experts(xs, blk_e, n_used, p)
    out = _combine(x1, gates.T, _gather_sc(ys, pos))
    return out.reshape(batch, seq, D)


def _rope_tables(seq):
    half = ROT_DIM // 2
    inv_freq = jnp.power(ROPE_THETA, -2.0 * jnp.arange(half, dtype=F32) / ROT_DIM)
    ang = jnp.arange(seq, dtype=F32)[:, None] * inv_freq[None, :]
    cos, sin = jnp.cos(ang), jnp.sin(ang)
    zeros = jnp.zeros((seq, HEAD_DIM - ROT_DIM), F32)
    zh = jnp.zeros((seq, half), F32)
    cos_t = jnp.concatenate([cos, cos, zeros + 1.0], axis=1)
    sa = jnp.concatenate([-sin, zh, zeros], axis=1)
    sb = jnp.concatenate([zh, sin, zeros], axis=1)
    rep = LANES // HEAD_DIM
    return tuple(jnp.tile(t, (1, rep)) for t in (cos_t, sa, sb))


def kernel(x_prompt, x_sample, attn_norm_w, w_in, q_norm_w, k_norm_w, sgu_ln_w, sgu_ln_b, sgu_w, sgu_b,
           att_out_norm_w, sgu_out_norm_w, w_out, ffn_norm_w, router_w, router_b,
           w_gate, b_gate, w_up, b_up, w_down, b_down):
    depth = w_in.shape[0]
    n_heads = ATT_WIDTH // HEAD_DIM
    blk = np.arange(MXU_DIM) // HEAD_DIM
    tri = np.arange(TOKEN_TILE)
    xa, xb = x_prompt, x_sample
    rope = {x.shape[1]: _rope_tables(x.shape[1]) for x in (xa, xb)}
    for l in range(depth):
        shared = dict(
            attn_norm_w=attn_norm_w[l][None], w_in=w_in[l].astype(BF16),
            q_norm_w=jnp.tile(q_norm_w[l], n_heads)[None], k_norm_w=jnp.tile(k_norm_w[l], n_heads)[None],
            head_ones=jnp.asarray(blk[:, None] == blk[None, :], BF16),
            sgu_ln_w=sgu_ln_w[l][None], sgu_ln_b=sgu_ln_b[l][None],
            sgu_w=sgu_w[l].reshape(SGU_GROUPS * SGU_CHUNK, SGU_CHUNK).astype(BF16),
            sgu_b=jnp.repeat(sgu_b[l].T, SGU_GROUP_DIM, axis=1),
            att_out_norm_w=att_out_norm_w[l][None], sgu_out_norm_w=sgu_out_norm_w[l][None],
            w_out=w_out[l].astype(BF16), ffn_norm_w=ffn_norm_w[l][None],
            router_wt=router_w[l].T, router_b=router_b[l][:, None],
            tri=jnp.asarray(tri[:, None] < tri[None, :], BF16),
            w_gate=w_gate[l].astype(BF16), b_gate=b_gate[l][:, None, :],
            w_up=w_up[l].astype(BF16), b_up=b_up[l][:, None, :],
            w_down=w_down[l].astype(BF16), b_down=b_down[l][:, None, :],
        )
        xa, xb = (_layer(x, dict(shared, rope=rope)) for x in (xa, xb))
    return xa, xb
```

```python
import functools

import numpy as np
import jax
import jax.numpy as jnp
from jax import lax
from jax.experimental import pallas as pl
from jax.experimental.pallas import tpu as pltpu
from jax.experimental.pallas import tpu_sc as plsc

F32 = jnp.float32
BF16 = jnp.bfloat16
I32 = jnp.int32

HEAD_DIM = 64
ATT_WINDOWS = ((128, 1), (512, 4), (2048, 16))
N_GROUPS = len(ATT_WINDOWS)
HEADS_PER_GROUP = 4
GROUP_WIDTH = HEADS_PER_GROUP * HEAD_DIM
ATT_WIDTH = N_GROUPS * GROUP_WIDTH
SGU_WIDTH = 256
SGU_GROUP_DIM = 64
SGU_GROUPS = SGU_WIDTH // SGU_GROUP_DIM
SGU_CHUNK = 128
ROT_DIM = HEAD_DIM // 4
ROPE_THETA = 500000.0
N_EXPERTS = 32
TOP_K = 4
SWIGLU_LIMIT = 7.0
SWIGLU_ALPHA = 1.702
NORM_EPS = 1e-6
LN_EPS = 1e-5
NEG_INF = -1e30

LANES = 128
SUBLANES = 8
MXU_DIM = 256
TOKEN_TILE = 512
ATT_Q_BLOCK = 256
ATT_SUB_Q = 128
MOE_BLOCK = 512
ROW_TILE = 256
SC_WINDOW = 64
VMEM_LIMIT = 48 * 1024 * 1024
EXPERTS_VMEM_LIMIT = 56 * 1024 * 1024


def _dot(a, b):
    return jnp.dot(a, b, preferred_element_type=F32)


def _dot_nt(a, b, precision=None):
    return lax.dot_general(a, b, (((1,), (1,)), ((), ())), precision=precision, preferred_element_type=F32)


def _rms(x, w):
    return x * lax.rsqrt(jnp.mean(x * x, axis=-1, keepdims=True) + NORM_EPS) * w


def _gelu(x):
    return 0.5 * x * (1.0 + lax.erf(x * np.float32(np.sqrt(0.5))))


def _to_tiled(ref, idx, value):
    rows = value.shape[0]
    for c in range(value.shape[1] // LANES):
        ref[idx + (slice(None), c)] = value[:, c * LANES:(c + 1) * LANES].reshape(rows // SUBLANES, SUBLANES, LANES)


def _from_tiled(ref, idx, row_tiles):
    chunks = ref.shape[-3]
    n = (row_tiles.stop - row_tiles.start) * SUBLANES
    return jnp.concatenate([ref[idx + (row_tiles, c)].reshape(n, LANES) for c in range(chunks)], axis=1)


def _in_proj_kernel(x_ref, anw_ref, w_ref, qnw_ref, knw_ref, ones_ref, cos_ref, sa_ref, sb_ref, lnw_ref, lnb_ref,
                    *refs):
    qkv_refs = refs[:3 * N_GROUPS]
    zu_ref, zv_ref, stage = refs[3 * N_GROUPS:]
    tm = x_ref.shape[0]
    h = _rms(x_ref[...], anw_ref[...]).astype(BF16)
    reps = ATT_WIDTH // LANES
    cos = jnp.concatenate([cos_ref[...]] * reps, axis=1)
    sa = jnp.concatenate([sa_ref[...]] * reps, axis=1)
    sb = jnp.concatenate([sb_ref[...]] * reps, axis=1)
    ones = ones_ref[...]

    def head_norm_rope(t, nw):
        sq = (t * t).astype(BF16)
        parts = []
        for j in range(ATT_WIDTH // MXU_DIM):
            sl = slice(j * MXU_DIM, (j + 1) * MXU_DIM)
            parts.append(_dot(sq[:, sl], ones))
        ssum = jnp.concatenate(parts, axis=1)
        t = t * lax.rsqrt(ssum * (1.0 / HEAD_DIM) + NORM_EPS) * nw
        half = ROT_DIM // 2
        return t * cos + pltpu.roll(t, ATT_WIDTH - half, 1) * sa + pltpu.roll(t, half, 1) * sb

    def emit(which, t):
        n_chunks = ATT_WIDTH // LANES
        per_group = GROUP_WIDTH // LANES
        for c in range(n_chunks):
            stage[which * n_chunks + c] = t[:, c * LANES:(c + 1) * LANES]
        for g, (_, dil) in enumerate(ATT_WINDOWS):
            out = qkv_refs[which * N_GROUPS + g]
            first = which * n_chunks + g * per_group
            for r in range(dil):
                rows = pl.ds(r, tm // dil, stride=dil)
                out[r] = jnp.concatenate([stage[first + c, rows, :] for c in range(per_group)], axis=1).astype(BF16)

    q = head_norm_rope(_dot(h, w_ref[:, 0:ATT_WIDTH]), qnw_ref[...])
    emit(0, q * (HEAD_DIM ** -0.5))
    emit(1, head_norm_rope(_dot(h, w_ref[:, ATT_WIDTH:2 * ATT_WIDTH]), knw_ref[...]))
    emit(2, _dot(h, w_ref[:, 2 * ATT_WIDTH:3 * ATT_WIDTH]))
    z = _dot(h, w_ref[:, 3 * ATT_WIDTH:3 * ATT_WIDTH + 2 * SGU_WIDTH])
    zu_ref[...] = _gelu(z[:, :SGU_WIDTH])
    gv = _gelu(z[:, SGU_WIDTH:])
    mu = jnp.mean(gv, axis=-1, keepdims=True)
    var = jnp.mean(jnp.square(gv - mu), axis=-1, keepdims=True)
    zv_ref[...] = ((gv - mu) * lax.rsqrt(var + LN_EPS) * lnw_ref[...] + lnb_ref[...]).astype(BF16)


def _phase_spec(tm, dil, n_seq_tiles):
    return pl.BlockSpec((None, dil, tm // dil, GROUP_WIDTH), lambda i: (i // n_seq_tiles, 0, i % n_seq_tiles, 0))


def _in_proj(x2, batch, seq, p):
    T, D = x2.shape
    tm = TOKEN_TILE
    n_seq_tiles = seq // tm
    in_width = p["w_in"].shape[1]
    const = lambda shape: pl.BlockSpec(shape, lambda i: (0,) * len(shape))
    rope = pl.BlockSpec((tm, LANES), lambda i: (i % n_seq_tiles, 0))
    row = lambda w: pl.BlockSpec((tm, w), lambda i: (i, 0))
    qkv_specs = [_phase_spec(tm, dil, n_seq_tiles) for _ in range(3) for _, dil in ATT_WINDOWS]
    qkv_shapes = [jax.ShapeDtypeStruct((batch, dil, seq // dil, GROUP_WIDTH), BF16)
                  for _ in range(3) for _, dil in ATT_WINDOWS]
    outs = pl.pallas_call(
        _in_proj_kernel,
        grid=(T // tm,),
        in_specs=[row(D), const((1, D)), const((D, in_width)), const((1, ATT_WIDTH)), const((1, ATT_WIDTH)),
                  const((MXU_DIM, MXU_DIM)), rope, rope, rope, const((1, SGU_WIDTH)), const((1, SGU_WIDTH))],
        out_specs=qkv_specs + [row(SGU_WIDTH), row(SGU_WIDTH)],
        out_shape=qkv_shapes + [jax.ShapeDtypeStruct((T, SGU_WIDTH), F32), jax.ShapeDtypeStruct((T, SGU_WIDTH), BF16)],
        scratch_shapes=[pltpu.VMEM((3 * ATT_WIDTH // LANES, tm, LANES), F32)],
        compiler_params=pltpu.CompilerParams(dimension_semantics=("arbitrary",), vmem_limit_bytes=VMEM_LIMIT),
        name="in_proj",
    )(x2, p["attn_norm_w"], p["w_in"], p["q_norm_w"], p["k_norm_w"], p["head_ones"],
      *p["rope"][seq], p["sgu_ln_w"], p["sgu_ln_b"])
    q, k, v = (outs[i * N_GROUPS:(i + 1) * N_GROUPS] for i in range(3))
    return q, k, v, outs[-2], outs[-1]


def _attn_kernel(q_ref, kp_ref, km_ref, kn_ref, vp_ref, vm_ref, vn_ref, o_ref, lse_ref, *, sub_len, steps):
    lq = q_ref.shape[0]
    sq = min(lq, ATT_SUB_Q)
    lk = sq + 2 * steps
    nh = HEADS_PER_GROUP
    lb = pl.program_id(2)
    kk = jnp.concatenate([kp_ref[...], km_ref[...], kn_ref[...]], axis=0)
    vv = jnp.concatenate([vp_ref[...], vm_ref[...], vn_ref[...]], axis=0)
    head = lax.broadcasted_iota(I32, (1, GROUP_WIDTH), 1) // HEAD_DIM
    hm_f = [(head == h).astype(F32) for h in range(nh)]
    hm_b = [m.astype(BF16) for m in hm_f]
    qi = lax.broadcasted_iota(I32, (sq, 1), 0)
    kj = lax.broadcasted_iota(I32, (sq, lk), 1)
    for j in range(lq // sq):
        first = lb * lq + j * sq
        lo = jnp.maximum(qi, steps - first)
        hi = jnp.minimum(qi + 2 * steps, sub_len - 1 + steps - first)
        mask = ((kj - lo).astype(jnp.uint32) <= (hi - lo).astype(jnp.uint32))[None]
        qj = q_ref[j * sq:(j + 1) * sq, :]
        qs = jnp.concatenate([qj * hm_b[h] for h in range(nh)], axis=0)
        s = _dot_nt(qs, kk[j * sq:j * sq + lk]).reshape(nh, sq, lk)
        s = jnp.where(mask, s, NEG_INF)
        m = jnp.max(s, axis=-1, keepdims=True)
        pr = jnp.exp(s - m)
        den = jnp.sum(pr, axis=-1, keepdims=True)
        oh = _dot(pr.reshape(nh * sq, lk).astype(BF16), vv[j * sq:j * sq + lk]).reshape(nh, sq, GROUP_WIDTH) / den
        lh = m + jnp.log(den)
        o_ref[j * sq:(j + 1) * sq, :] = sum(oh[h] * hm_f[h] for h in range(nh))
        lse_ref[j * sq:(j + 1) * sq, :] = sum(lh[h] * hm_f[h] for h in range(nh))


def _attention(q, k, v, group):
    window, dil = ATT_WINDOWS[group]
    steps = window // (2 * dil)
    batch, _, sub_len, _ = q.shape
    assert sub_len % steps == 0
    lq = min(ATT_Q_BLOCK, sub_len)
    assert lq % steps == 0 and sub_len % lq == 0
    per_q = lq // steps
    n_halo = sub_len // steps
    main = pl.BlockSpec((None, None, lq, GROUP_WIDTH), lambda b, r, i: (b, r, i, 0))
    prev = pl.BlockSpec((None, None, steps, GROUP_WIDTH), lambda b, r, i: (b, r, jnp.maximum(i * per_q - 1, 0), 0))
    nxt = pl.BlockSpec((None, None, steps, GROUP_WIDTH),
                       lambda b, r, i: (b, r, jnp.minimum((i + 1) * per_q, n_halo - 1), 0))
    return pl.pallas_call(
        functools.partial(_attn_kernel, sub_len=sub_len, steps=steps),
        grid=(batch, dil, sub_len // lq),
        in_specs=[main, prev, main, nxt, prev, main, nxt],
        out_specs=[main, main],
        out_shape=[jax.ShapeDtypeStruct(q.shape, F32)] * 2,
        compiler_params=pltpu.CompilerParams(dimension_semantics=("arbitrary",) * 3, vmem_limit_bytes=VMEM_LIMIT),
        name=f"attention_g{group}",
    )(q, k, k, k, v, v, v)


def _post_mix_kernel(x_ref, *refs):
    o_refs = refs[:N_GROUPS]
    l_refs = refs[N_GROUPS:2 * N_GROUPS]
    (zu_ref, zv_ref, sguw_ref, sgub_ref, aonw_ref, sonw_ref, wout_ref, fnw_ref, rwt_ref, rb_ref, tri_ref,
     x1_ref, h2_hbm, gates_ref, eidx_ref, rank_ref, cnt_ref, stage, hstage, hsem) = refs[2 * N_GROUPS:]
    tm = x_ref.shape[0]

    def token_order(ref, slot):
        dil = ref.shape[0]
        if dil == 1:
            return ref[0]
        per_group = GROUP_WIDTH // LANES
        for r in range(dil):
            for c in range(per_group):
                stage[slot * per_group + c, pl.ds(r, tm // dil, stride=dil), :] = ref[r, :, c * LANES:(c + 1) * LANES]
        return jnp.concatenate([stage[slot * per_group + c] for c in range(per_group)], axis=1)

    o = [token_order(ref, g) for g, ref in enumerate(o_refs)]
    l = [token_order(ref, N_GROUPS + g) for g, ref in enumerate(l_refs)]
    lmax = jnp.maximum(jnp.maximum(l[0], l[1]), l[2])
    e = [jnp.exp(lg - lmax) for lg in l]
    esum = e[0] + e[1] + e[2]
    att = (e[0] / esum) * o[0] + (e[1] / esum) * o[1] + (e[2] / esum) * o[2]
    att_n = _rms(att, aonw_ref[...]).astype(BF16)
    cgrp = lax.broadcasted_iota(I32, (1, SGU_WIDTH), 1) // SGU_GROUP_DIM
    sguw = sguw_ref[...]
    gates = []
    for c in range(tm // SGU_CHUNK):
        r = _dot(sguw, zv_ref[c * SGU_CHUNK:(c + 1) * SGU_CHUNK, :])
        g = sgub_ref[...]
        for grp in range(SGU_GROUPS):
            g = g + r[grp * SGU_CHUNK:(grp + 1) * SGU_CHUNK, :] * (cgrp == grp).astype(F32)
        gates.append(g)
    sgu = zu_ref[...] * jnp.concatenate(gates, axis=0)
    sgu_n = _rms(sgu, sonw_ref[...]).astype(BF16)
    x1 = x_ref[...] + _dot(att_n, wout_ref[0:GROUP_WIDTH, :]) + _dot(sgu_n, wout_ref[GROUP_WIDTH:, :])
    x1_ref[...] = x1
    h2 = _rms(x1, fnw_ref[...])
    _to_tiled(hstage, (), h2)
    bt = tm // SUBLANES
    h2_copies = [pltpu.make_async_copy(hstage.at[:, :, j, :], h2_hbm.at[pl.ds(pl.program_id(0) * bt, bt), j], hsem)
                 for j in range(SUBLANES)]
    for c in h2_copies:
        c.start()
    logits = _dot_nt(rwt_ref[...], h2, precision=lax.Precision.HIGHEST) + rb_ref[...]
    eiota = lax.broadcasted_iota(I32, (N_EXPERTS, tm), 0)
    vals, idxs = [], []
    for _ in range(TOP_K):
        m = jnp.max(logits, axis=0, keepdims=True)
        idx = jnp.min(jnp.where(logits == m, eiota, N_EXPERTS), axis=0, keepdims=True)
        vals.append(m)
        idxs.append(idx)
        logits = jnp.where(eiota == idx, -jnp.inf, logits)
    exps = [jnp.exp(v - vals[0]) for v in vals]
    den = exps[0] + exps[1] + exps[2] + exps[3]
    gates_ref[...] = jnp.concatenate([ex / den for ex in exps], axis=0)
    eidx_ref[...] = jnp.concatenate(idxs, axis=0)
    onehots = [(eiota == idx).astype(F32) for idx in idxs]
    chosen = onehots[0] + onehots[1] + onehots[2] + onehots[3]
    before = _dot(chosen.astype(BF16), tri_ref[...])
    rank_ref[...] = jnp.concatenate(
        [jnp.sum(oh * before, axis=0, keepdims=True) for oh in onehots], axis=0).astype(I32)
    cnt_ref[...] = jnp.broadcast_to(jnp.sum(chosen, axis=1, keepdims=True), (N_EXPERTS, LANES)).astype(I32)
    for c in h2_copies:
        c.wait()


def _post_mix(x2, seq, o, lse, zu, zv, p):
    T, D = x2.shape
    tm = TOKEN_TILE
    n_tiles = T // tm
    n_seq_tiles = seq // tm
    chunks = D // LANES
    const = lambda shape: pl.BlockSpec(shape, lambda i: (0,) * len(shape))
    row = lambda w: pl.BlockSpec((tm, w), lambda i: (i, 0))
    colt = pl.BlockSpec((TOP_K, tm), lambda i: (0, i))
    phase = [_phase_spec(tm, dil, n_seq_tiles) for _, dil in ATT_WINDOWS]
    return pl.pallas_call(
        _post_mix_kernel,
        grid=(n_tiles,),
        in_specs=[row(D)] + phase + phase + [row(SGU_WIDTH), row(SGU_WIDTH),
                  const((SGU_GROUPS * SGU_CHUNK, SGU_CHUNK)), const((SGU_CHUNK, SGU_WIDTH)),
                  const((1, GROUP_WIDTH)), const((1, SGU_WIDTH)), const((GROUP_WIDTH + SGU_WIDTH, D)),
                  const((1, D)), const((N_EXPERTS, D)), const((N_EXPERTS, 1)), const((tm, tm))],
        out_specs=[row(D), pl.BlockSpec(memory_space=pl.ANY),
                   colt, colt, colt, pl.BlockSpec((None, N_EXPERTS, LANES), lambda i: (i, 0, 0))],
        out_shape=[jax.ShapeDtypeStruct((T, D), F32),
                   jax.ShapeDtypeStruct((T // SUBLANES, SUBLANES, chunks, LANES), F32),
                   jax.ShapeDtypeStruct((TOP_K, T), F32), jax.ShapeDtypeStruct((TOP_K, T), I32),
                   jax.ShapeDtypeStruct((TOP_K, T), I32), jax.ShapeDtypeStruct((n_tiles, N_EXPERTS, LANES), I32)],
        scratch_shapes=[pltpu.VMEM((2 * N_GROUPS * GROUP_WIDTH // LANES, tm, LANES), F32),
                        pltpu.VMEM((tm // SUBLANES, chunks, SUBLANES, LANES), F32), pltpu.SemaphoreType.DMA],
        compiler_params=pltpu.CompilerParams(dimension_semantics=("arbitrary",), vmem_limit_bytes=VMEM_LIMIT),
        name="post_mix",
    )(x2, *o, *lse, zu, zv, p["sgu_w"], p["sgu_b"], p["att_out_norm_w"],
      p["sgu_out_norm_w"], p["w_out"], p["ffn_norm_w"], p["router_wt"], p["router_b"], p["tri"])


def _sc_workers():
    info = plsc.get_sparse_core_info()
    return info.num_cores, info.num_cores * info.num_subcores


def _dispatch_sc(h2p, pos, pad_pos, n_rows):
    T, chunks, _ = h2p.shape
    n_cores, n_workers = _sc_workers()
    win = SC_WINDOW
    per_worker = T // n_workers
    pad_per_worker = pad_pos.shape[0] // n_workers
    assert per_worker % win == 0 and pad_per_worker % win == 0
    mesh = plsc.VectorSubcoreMesh(core_axis_name="c", subcore_axis_name="s")

    @functools.partial(
        pl.kernel, mesh=mesh, out_type=jax.ShapeDtypeStruct((n_rows, chunks, LANES), F32),
        scratch_types=[pltpu.VMEM((win,), I32), pltpu.VMEM((win, chunks, LANES), F32)],
        name="dispatch_sc")
    def scatter_rows(h_hbm, pos_hbm, pad_hbm, zeros_hbm, xs_hbm, idx_v, rows_v):
        worker = lax.axis_index("s") * n_cores + lax.axis_index("c")

        @pl.loop(0, per_worker // win)
        def _(w):
            t0 = worker * per_worker + w * win
            pltpu.sync_copy(h_hbm.at[pl.ds(t0, win)], rows_v)
            for k in range(TOP_K):
                pltpu.sync_copy(pos_hbm.at[pl.ds(k * T + t0, win)], idx_v)
                pltpu.sync_copy(rows_v, xs_hbm.at[idx_v])

        pltpu.sync_copy(zeros_hbm, rows_v)

        @pl.loop(0, pad_per_worker // win)
        def _(w):
            pltpu.sync_copy(pad_hbm.at[pl.ds(worker * pad_per_worker + w * win, win)], idx_v)
            pltpu.sync_copy(rows_v, xs_hbm.at[idx_v])

    return scatter_rows(h2p, pos, pad_pos, jnp.zeros((win, chunks, LANES), F32))


def _gather_sc(ys, pos):
    _, chunks, _ = ys.shape
    n_assign = pos.shape[0]
    n_cores, n_workers = _sc_workers()
    win = SC_WINDOW
    per_worker = n_assign // n_workers
    assert per_worker % win == 0
    mesh = plsc.VectorSubcoreMesh(core_axis_name="c", subcore_axis_name="s")

    @functools.partial(
        pl.kernel, mesh=mesh, out_type=jax.ShapeDtypeStruct((n_assign, chunks, LANES), F32),
        scratch_types=[pltpu.VMEM((win,), I32), pltpu.VMEM((win, chunks, LANES), F32)],
        name="gather_sc")
    def gather_rows(ys_hbm, pos_hbm, out_hbm, idx_v, rows_v):
        worker = lax.axis_index("s") * n_cores + lax.axis_index("c")

        @pl.loop(0, per_worker // win)
        def _(w):
            a0 = worker * per_worker + w * win
            pltpu.sync_copy(pos_hbm.at[pl.ds(a0, win)], idx_v)
            pltpu.sync_copy(ys_hbm.at[idx_v], rows_v)
            pltpu.sync_copy(rows_v, out_hbm.at[pl.ds(a0, win)])

    return gather_rows(ys, pos)


def _experts_kernel(blk_e_ref, n_used_ref, xs_hbm, wg_ref, bg_ref, wu_ref, bu_ref, wd_ref, bd_ref, ys_hbm,
                    xbuf, obuf, wg_b, wu_b, wd_b, in_sem, out_sem):
    bt = xbuf.shape[1]
    i = pl.program_id(0)
    n = pl.num_programs(0)
    n_used = n_used_ref[0]
    slot = i % 2

    def in_copies(blk, s):
        return [pltpu.make_async_copy(xs_hbm.at[pl.ds(blk * bt, bt), j], xbuf.at[s, :, :, j, :], in_sem.at[s])
                for j in range(SUBLANES)]

    def out_copies(blk, s):
        return [pltpu.make_async_copy(obuf.at[s, :, :, j, :], ys_hbm.at[pl.ds(blk * bt, bt), j], out_sem.at[s])
                for j in range(SUBLANES)]

    @pl.when(i == 0)
    def _():
        for c in in_copies(0, 0):
            c.start()

    @pl.when(i + 1 < n_used)
    def _():
        for c in in_copies(i + 1, 1 - slot):
            c.start()

    @pl.when(i >= 2)
    def _():
        for c in out_copies(i - 2, slot):
            c.wait()

    @pl.when(i >= n_used)
    def _():
        obuf[slot] = jnp.zeros(obuf.shape[1:], F32)

    @pl.when(i < n_used)
    def _():
        for c in in_copies(i, slot):
            c.wait()
        @pl.when((i == 0) | (blk_e_ref[i] != blk_e_ref[jnp.maximum(i - 1, 0)]))
        def _():
            wg_b[...] = wg_ref[...].astype(BF16)
            wu_b[...] = wu_ref[...].astype(BF16)
            wd_b[...] = wd_ref[...].astype(BF16)

        x = _from_tiled(xbuf, (slot,), slice(0, bt)).astype(BF16)
        g = _dot(x, wg_b[...]) + bg_ref[...]
        u = _dot(x, wu_b[...]) + bu_ref[...]
        g = jnp.minimum(g, SWIGLU_LIMIT)
        u = jnp.clip(u, -SWIGLU_LIMIT, SWIGLU_LIMIT)
        act = (u + 1.0) * (g * jax.nn.sigmoid(SWIGLU_ALPHA * g))
        _to_tiled(obuf, (slot,), _dot(act.astype(BF16), wd_b[...]) + bd_ref[...])

    for c in out_copies(i, slot):
        c.start()

    @pl.when(i == n - 1)
    def _():
        for c in out_copies(i - 1, 1 - slot) + out_copies(i, slot):
            c.wait()


def _experts(xs, blk_e, n_used, p):
    n_rows, chunks, _ = xs.shape
    D = chunks * LANES
    bm = MOE_BLOCK
    d_ff = p["w_gate"].shape[2]
    assert n_rows // bm >= 2
    wspec = lambda a, b: pl.BlockSpec((None, a, b), lambda i, be, nu: (be[i], 0, 0))
    by_tile = (n_rows // SUBLANES, SUBLANES, chunks, LANES)
    block_buf = pltpu.VMEM((2, bm // SUBLANES, chunks, SUBLANES, LANES), F32)
    ys = pl.pallas_call(
        _experts_kernel,
        grid_spec=pltpu.PrefetchScalarGridSpec(
            num_scalar_prefetch=2,
            grid=(n_rows // bm,),
            in_specs=[pl.BlockSpec(memory_space=pl.ANY), wspec(D, d_ff), wspec(1, d_ff), wspec(D, d_ff),
                      wspec(1, d_ff), wspec(d_ff, D), wspec(1, D)],
            out_specs=pl.BlockSpec(memory_space=pl.ANY),
            scratch_shapes=[block_buf, block_buf, pltpu.VMEM((D, d_ff), BF16), pltpu.VMEM((D, d_ff), BF16),
                            pltpu.VMEM((d_ff, D), BF16), pltpu.SemaphoreType.DMA((2,)), pltpu.SemaphoreType.DMA((2,))],
        ),
        out_shape=jax.ShapeDtypeStruct(by_tile, F32),
        compiler_params=pltpu.CompilerParams(dimension_semantics=("arbitrary",), vmem_limit_bytes=EXPERTS_VMEM_LIMIT),
        name="experts",
    )(blk_e, n_used, xs.reshape(by_tile), p["w_gate"], p["b_gate"], p["w_up"], p["b_up"], p["w_down"], p["b_down"])
    return ys.reshape(n_rows, chunks, LANES)


def _combine_kernel(gates_ref, x1_ref, yg_hbm, out_ref, buf, sem):
    tm = x1_ref.shape[0]
    bt = tm // SUBLANES
    j = pl.program_id(0)
    n = pl.num_programs(0) - 1
    tiles_per_k = yg_hbm.shape[0] // TOP_K

    def copies(tile, s):
        return [pltpu.make_async_copy(yg_hbm.at[pl.ds(k * tiles_per_k + tile * bt, bt), r],
                                      buf.at[s, pl.ds(k * bt, bt), :, r, :], sem.at[s])
                for k in range(TOP_K) for r in range(SUBLANES)]

    @pl.when(j < n)
    def _():
        for c in copies(j, j % 2):
            c.start()

    @pl.when(j >= 1)
    def _():
        done = 1 - j % 2
        for c in copies(j - 1, done):
            c.wait()
        y = gates_ref[:, 0:1] * _from_tiled(buf, (done,), slice(0, bt))
        for k in range(1, TOP_K):
            y = y + gates_ref[:, k:k + 1] * _from_tiled(buf, (done,), slice(k * bt, (k + 1) * bt))
        out_ref[...] = x1_ref[...] + y


def _combine(x1, gates_tok, yg):
    T, D = x1.shape
    tm = ROW_TILE
    n = T // tm
    chunks = D // LANES
    lag = lambda i: (jnp.maximum(i - 1, 0), 0)
    return pl.pallas_call(
        _combine_kernel,
        grid=(n + 1,),
        in_specs=[pl.BlockSpec((tm, TOP_K), lag), pl.BlockSpec((tm, D), lag), pl.BlockSpec(memory_space=pl.ANY)],
        out_specs=pl.BlockSpec((tm, D), lag),
        out_shape=jax.ShapeDtypeStruct((T, D), F32),
        scratch_shapes=[pltpu.VMEM((2, TOP_K * tm // SUBLANES, chunks, SUBLANES, LANES), F32),
                        pltpu.SemaphoreType.DMA((2,))],
        compiler_params=pltpu.CompilerParams(dimension_semantics=("arbitrary",), vmem_limit_bytes=VMEM_LIMIT),
        name="combine",
    )(gates_tok, x1, yg.reshape(TOP_K * T // SUBLANES, SUBLANES, chunks, LANES))


def _route(eidx, rank, tile_counts):
    T = eidx.shape[1]
    bm = MOE_BLOCK
    n_tiles = T // TOKEN_TILE
    tc = tile_counts[:, :, 0]
    counts = jnp.sum(tc, axis=0)
    tile_off = jnp.cumsum(tc, axis=0) - tc
    pcounts = (counts + bm - 1) // bm * bm
    pends = jnp.cumsum(pcounts)
    pstarts = pends - pcounts
    base = pstarts[None, :] + tile_off
    sel = eidx.reshape(TOP_K, n_tiles, 1, TOKEN_TILE) == jnp.arange(N_EXPERTS, dtype=I32)[None, None, :, None]
    pos = jnp.sum(jnp.where(sel, base[None, :, :, None], 0), axis=2).reshape(TOP_K, T) + rank
    n_blocks = (TOP_K * T) // bm + N_EXPERTS
    n_rows = n_blocks * bm
    blk_start = jnp.arange(n_blocks, dtype=I32) * bm
    blk_e = jnp.minimum(jnp.sum((pends[None, :] <= blk_start[:, None]).astype(I32), axis=1), N_EXPERTS - 1)
    n_used = (pends[-1] // bm).astype(I32).reshape(1)
    first_free = jnp.concatenate([pstarts + counts, pends[-1:]])
    free_off = jnp.cumsum(jnp.concatenate([jnp.zeros((1,), counts.dtype), pcounts - counts]))
    j = jnp.arange(n_rows - TOP_K * T, dtype=I32)[:, None]
    seg_id = jnp.sum((free_off[None, 1:] <= j).astype(I32), axis=1, keepdims=True)
    onehot = seg_id == jnp.arange(N_EXPERTS + 1, dtype=I32)[None, :]
    pad_pos = jnp.sum(jnp.where(onehot, (first_free - free_off)[None, :] + j, 0), axis=1).astype(I32)
    return pos.reshape(-1).astype(I32), pad_pos, blk_e.astype(I32), n_used, n_rows


def _layer(x, p):
    batch, seq, D = x.shape
    x2 = x.reshape(batch * seq, D)
    q, k, v, zu, zv = _in_proj(x2, batch, seq, p)
    o, lse = zip(*[_attention(q[g], k[g], v[g], g) for g in range(N_GROUPS)])
    x1, h2p, gates, eidx, rank, tile_counts = _post_mix(x2, seq, o, lse, zu, zv, p)
    pos, pad_pos, blk_e, n_used, n_rows = _route(eidx, rank, tile_counts)
    xs = _dispatch_sc(h2p.reshape(batch * seq, D // LANES, LANES), pos, pad_pos, n_rows)
    ys = _experts(xs, blk_e, n_used, p)
    out = _combine(x1, gates.T, _gather_sc(ys, pos))
    return out.reshape(batch, seq, D)


def _rope_tables(seq):
    half = ROT_DIM // 2
    inv_freq = jnp.power(ROPE_THETA, -2.0 * jnp.arange(half, dtype=F32) / ROT_DIM)
    ang = jnp.arange(seq, dtype=F32)[:, None] * inv_freq[None, :]
    cos, sin = jnp.cos(ang), jnp.sin(ang)
    zeros = jnp.zeros((seq, HEAD_DIM - ROT_DIM), F32)
    zh = jnp.zeros((seq, half), F32)
    cos_t = jnp.concatenate([cos, cos, zeros + 1.0], axis=1)
    sa = jnp.concatenate([-sin, zh, zeros], axis=1)
    sb = jnp.concatenate([zh, sin, zeros], axis=1)
    rep = LANES // HEAD_DIM
    return tuple(jnp.tile(t, (1, rep)) for t in (cos_t, sa, sb))


def kernel(x_prompt, x_sample, attn_norm_w, w_in, q_norm_w, k_norm_w, sgu_ln_w, sgu_ln_b, sgu_w, sgu_b,
           att_out_norm_w, sgu_out_norm_w, w_out, ffn_norm_w, router_w, router_b,
           w_gate, b_gate, w_up, b_up, w_down, b_down):
    depth = w_in.shape[0]
    n_heads = ATT_WIDTH // HEAD_DIM
    blk = np.arange(MXU_DIM) // HEAD_DIM
    tri = np.arange(TOKEN_TILE)
    xa, xb = x_prompt, x_sample
    rope = {x.shape[1]: _rope_tables(x.shape[1]) for x in (xa, xb)}
    for l in range(depth):
        shared = dict(
            attn_norm_w=attn_norm_w[l][None], w_in=w_in[l].astype(BF16),
            q_norm_w=jnp.tile(q_norm_w[l], n_heads)[None], k_norm_w=jnp.tile(k_norm_w[l], n_heads)[None],
            head_ones=jnp.asarray(blk[:, None] == blk[None, :], BF16),
            sgu_ln_w=sgu_ln_w[l][None], sgu_ln_b=sgu_ln_b[l][None],
            sgu_w=sgu_w[l].reshape(SGU_GROUPS * SGU_CHUNK, SGU_CHUNK).astype(BF16),
            sgu_b=jnp.repeat(sgu_b[l].T, SGU_GROUP_DIM, axis=1),
            att_out_norm_w=att_out_norm_w[l][None], sgu_out_norm_w=sgu_out_norm_w[l][None],
            w_out=w_out[l].astype(BF16), ffn_norm_w=ffn_norm_w[l][None],
            router_wt=router_w[l].T, router_b=router_b[l][:, None],
            tri=jnp.asarray(tri[:, None] < tri[None, :], BF16),
            w_gate=w_gate[l], b_gate=b_gate[l][:, None, :],
            w_up=w_up[l], b_up=b_up[l][:, None, :],
            w_down=w_down[l], b_down=b_down[l][:, None, :],
        )
        xa, xb = (_layer(x, dict(shared, rope=rope)) for x in (xa, xb))
    return xa, xb
```

```python
import functools

import numpy as np
import jax
import jax.numpy as jnp
from jax import lax
from jax.experimental import pallas as pl
from jax.experimental.pallas import tpu as pltpu
from jax.experimental.pallas import tpu_sc as plsc

F32 = jnp.float32
BF16 = jnp.bfloat16
I32 = jnp.int32

HEAD_DIM = 64
ATT_WINDOWS = ((128, 1), (512, 4), (2048, 16))
N_GROUPS = len(ATT_WINDOWS)
HEADS_PER_GROUP = 4
GROUP_WIDTH = HEADS_PER_GROUP * HEAD_DIM
ATT_WIDTH = N_GROUPS * GROUP_WIDTH
SGU_WIDTH = 256
SGU_GROUP_DIM = 64
SGU_GROUPS = SGU_WIDTH // SGU_GROUP_DIM
SGU_CHUNK = 128
ROT_DIM = HEAD_DIM // 4
ROPE_THETA = 500000.0
N_EXPERTS = 32
TOP_K = 4
SWIGLU_LIMIT = 7.0
SWIGLU_ALPHA = 1.702
NORM_EPS = 1e-6
LN_EPS = 1e-5
NEG_INF = -1e30

LANES = 128
SUBLANES = 8
MXU_DIM = 256
TOKEN_TILE = 512
ATT_Q_BLOCK = 256
ATT_SUB_Q = 128
MOE_BLOCK = 512
ROW_TILE = 256
SC_WINDOW = 64
VMEM_LIMIT = 48 * 1024 * 1024
EXPERTS_VMEM_LIMIT = 56 * 1024 * 1024


def _dot(a, b):
    return jnp.dot(a, b, preferred_element_type=F32)


def _dot_nt(a, b, precision=None):
    return lax.dot_general(a, b, (((1,), (1,)), ((), ())), precision=precision, preferred_element_type=F32)


def _rms(x, w):
    return x * lax.rsqrt(jnp.mean(x * x, axis=-1, keepdims=True) + NORM_EPS) * w


def _gelu(x):
    return 0.5 * x * (1.0 + lax.erf(x * np.float32(np.sqrt(0.5))))


def _pack_pairs(x):
    w = x.shape[1] // 2
    lo = lax.bitcast_convert_type(x[:, :w].astype(BF16).astype(F32), jnp.uint32) >> 16
    hi = lax.bitcast_convert_type(x[:, w:].astype(BF16).astype(F32), jnp.uint32) & jnp.uint32(0xFFFF0000)
    return lax.bitcast_convert_type(lo | hi, F32)


def _unpack_pairs(words):
    u = lax.bitcast_convert_type(words, jnp.uint32)
    lo = lax.bitcast_convert_type(u << 16, F32)
    hi = lax.bitcast_convert_type(u & jnp.uint32(0xFFFF0000), F32)
    return jnp.concatenate([lo, hi], axis=1)


def _to_tiled(ref, idx, value):
    rows = value.shape[0]
    for c in range(value.shape[1] // LANES):
        ref[idx + (slice(None), c)] = value[:, c * LANES:(c + 1) * LANES].reshape(rows // SUBLANES, SUBLANES, LANES)


def _from_tiled(ref, idx, row_tiles):
    chunks = ref.shape[-3]
    n = (row_tiles.stop - row_tiles.start) * SUBLANES
    return jnp.concatenate([ref[idx + (row_tiles, c)].reshape(n, LANES) for c in range(chunks)], axis=1)


def _in_proj_kernel(x_ref, anw_ref, w_ref, qnw_ref, knw_ref, ones_ref, cos_ref, sa_ref, sb_ref, lnw_ref, lnb_ref,
                    *refs):
    qkv_refs = refs[:3 * N_GROUPS]
    zu_ref, zv_ref, stage = refs[3 * N_GROUPS:]
    tm = x_ref.shape[0]
    h = _rms(x_ref[...], anw_ref[...]).astype(BF16)
    reps = ATT_WIDTH // LANES
    cos = jnp.concatenate([cos_ref[...]] * reps, axis=1)
    sa = jnp.concatenate([sa_ref[...]] * reps, axis=1)
    sb = jnp.concatenate([sb_ref[...]] * reps, axis=1)
    ones = ones_ref[...]

    def head_norm_rope(t, nw):
        sq = (t * t).astype(BF16)
        parts = []
        for j in range(ATT_WIDTH // MXU_DIM):
            sl = slice(j * MXU_DIM, (j + 1) * MXU_DIM)
            parts.append(_dot(sq[:, sl], ones))
        ssum = jnp.concatenate(parts, axis=1)
        t = t * lax.rsqrt(ssum * (1.0 / HEAD_DIM) + NORM_EPS) * nw
        half = ROT_DIM // 2
        return t * cos + pltpu.roll(t, ATT_WIDTH - half, 1) * sa + pltpu.roll(t, half, 1) * sb

    def emit(which, t):
        n_chunks = ATT_WIDTH // LANES
        per_group = GROUP_WIDTH // LANES
        for c in range(n_chunks):
            stage[which * n_chunks + c] = t[:, c * LANES:(c + 1) * LANES]
        for g, (_, dil) in enumerate(ATT_WINDOWS):
            out = qkv_refs[which * N_GROUPS + g]
            first = which * n_chunks + g * per_group
            for r in range(dil):
                rows = pl.ds(r, tm // dil, stride=dil)
                out[r] = jnp.concatenate([stage[first + c, rows, :] for c in range(per_group)], axis=1).astype(BF16)

    q = head_norm_rope(_dot(h, w_ref[:, 0:ATT_WIDTH]), qnw_ref[...])
    emit(0, q * (HEAD_DIM ** -0.5))
    emit(1, head_norm_rope(_dot(h, w_ref[:, ATT_WIDTH:2 * ATT_WIDTH]), knw_ref[...]))
    emit(2, _dot(h, w_ref[:, 2 * ATT_WIDTH:3 * ATT_WIDTH]))
    z = _dot(h, w_ref[:, 3 * ATT_WIDTH:3 * ATT_WIDTH + 2 * SGU_WIDTH])
    zu_ref[...] = _gelu(z[:, :SGU_WIDTH])
    gv = _gelu(z[:, SGU_WIDTH:])
    mu = jnp.mean(gv, axis=-1, keepdims=True)
    var = jnp.mean(jnp.square(gv - mu), axis=-1, keepdims=True)
    zv_ref[...] = ((gv - mu) * lax.rsqrt(var + LN_EPS) * lnw_ref[...] + lnb_ref[...]).astype(BF16)


def _phase_spec(tm, dil, n_seq_tiles):
    return pl.BlockSpec((None, dil, tm // dil, GROUP_WIDTH), lambda i: (i // n_seq_tiles, 0, i % n_seq_tiles, 0))


def _in_proj(x2, batch, seq, p):
    T, D = x2.shape
    tm = TOKEN_TILE
    n_seq_tiles = seq // tm
    in_width = p["w_in"].shape[1]
    const = lambda shape: pl.BlockSpec(shape, lambda i: (0,) * len(shape))
    rope = pl.BlockSpec((tm, LANES), lambda i: (i % n_seq_tiles, 0))
    row = lambda w: pl.BlockSpec((tm, w), lambda i: (i, 0))
    qkv_specs = [_phase_spec(tm, dil, n_seq_tiles) for _ in range(3) for _, dil in ATT_WINDOWS]
    qkv_shapes = [jax.ShapeDtypeStruct((batch, dil, seq // dil, GROUP_WIDTH), BF16)
                  for _ in range(3) for _, dil in ATT_WINDOWS]
    outs = pl.pallas_call(
        _in_proj_kernel,
        grid=(T // tm,),
        in_specs=[row(D), const((1, D)), const((D, in_width)), const((1, ATT_WIDTH)), const((1, ATT_WIDTH)),
                  const((MXU_DIM, MXU_DIM)), rope, rope, rope, const((1, SGU_WIDTH)), const((1, SGU_WIDTH))],
        out_specs=qkv_specs + [row(SGU_WIDTH), row(SGU_WIDTH)],
        out_shape=qkv_shapes + [jax.ShapeDtypeStruct((T, SGU_WIDTH), F32), jax.ShapeDtypeStruct((T, SGU_WIDTH), BF16)],
        scratch_shapes=[pltpu.VMEM((3 * ATT_WIDTH // LANES, tm, LANES), F32)],
        compiler_params=pltpu.CompilerParams(dimension_semantics=("arbitrary",), vmem_limit_bytes=VMEM_LIMIT),
        name="in_proj",
    )(x2, p["attn_norm_w"], p["w_in"], p["q_norm_w"], p["k_norm_w"], p["head_ones"],
      *p["rope"][seq], p["sgu_ln_w"], p["sgu_ln_b"])
    q, k, v = (outs[i * N_GROUPS:(i + 1) * N_GROUPS] for i in range(3))
    return q, k, v, outs[-2], outs[-1]


def _attn_kernel(q_ref, kp_ref, km_ref, kn_ref, vp_ref, vm_ref, vn_ref, o_ref, lse_ref, *, sub_len, steps):
    lq = q_ref.shape[0]
    sq = min(lq, ATT_SUB_Q)
    lk = sq + 2 * steps
    nh = HEADS_PER_GROUP
    lb = pl.program_id(2)
    kk = jnp.concatenate([kp_ref[...], km_ref[...], kn_ref[...]], axis=0)
    vv = jnp.concatenate([vp_ref[...], vm_ref[...], vn_ref[...]], axis=0)
    head = lax.broadcasted_iota(I32, (1, GROUP_WIDTH), 1) // HEAD_DIM
    hm_f = [(head == h).astype(F32) for h in range(nh)]
    hm_b = [m.astype(BF16) for m in hm_f]
    qi = lax.broadcasted_iota(I32, (sq, 1), 0)
    kj = lax.broadcasted_iota(I32, (sq, lk), 1)
    for j in range(lq // sq):
        first = lb * lq + j * sq
        lo = jnp.maximum(qi, steps - first)
        hi = jnp.minimum(qi + 2 * steps, sub_len - 1 + steps - first)
        mask = ((kj - lo).astype(jnp.uint32) <= (hi - lo).astype(jnp.uint32))[None]
        qj = q_ref[j * sq:(j + 1) * sq, :]
        qs = jnp.concatenate([qj * hm_b[h] for h in range(nh)], axis=0)
        s = _dot_nt(qs, kk[j * sq:j * sq + lk]).reshape(nh, sq, lk)
        s = jnp.where(mask, s, NEG_INF)
        m = jnp.max(s, axis=-1, keepdims=True)
        pr = jnp.exp(s - m)
        den = jnp.sum(pr, axis=-1, keepdims=True)
        oh = _dot(pr.reshape(nh * sq, lk).astype(BF16), vv[j * sq:j * sq + lk]).reshape(nh, sq, GROUP_WIDTH) / den
        lh = m + jnp.log(den)
        o_ref[j * sq:(j + 1) * sq, :] = sum(oh[h] * hm_f[h] for h in range(nh))
        lse_ref[j * sq:(j + 1) * sq, :] = sum(lh[h] * hm_f[h] for h in range(nh))


def _attention(q, k, v, group):
    window, dil = ATT_WINDOWS[group]
    steps = window // (2 * dil)
    batch, _, sub_len, _ = q.shape
    assert sub_len % steps == 0
    lq = min(ATT_Q_BLOCK, sub_len)
    assert lq % steps == 0 and sub_len % lq == 0
    per_q = lq // steps
    n_halo = sub_len // steps
    main = pl.BlockSpec((None, None, lq, GROUP_WIDTH), lambda b, r, i: (b, r, i, 0))
    prev = pl.BlockSpec((None, None, steps, GROUP_WIDTH), lambda b, r, i: (b, r, jnp.maximum(i * per_q - 1, 0), 0))
    nxt = pl.BlockSpec((None, None, steps, GROUP_WIDTH),
                       lambda b, r, i: (b, r, jnp.minimum((i + 1) * per_q, n_halo - 1), 0))
    return pl.pallas_call(
        functools.partial(_attn_kernel, sub_len=sub_len, steps=steps),
        grid=(batch, dil, sub_len // lq),
        in_specs=[main, prev, main, nxt, prev, main, nxt],
        out_specs=[main, main],
        out_shape=[jax.ShapeDtypeStruct(q.shape, F32)] * 2,
        compiler_params=pltpu.CompilerParams(dimension_semantics=("arbitrary",) * 3, vmem_limit_bytes=VMEM_LIMIT),
        name=f"attention_g{group}",
    )(q, k, k, k, v, v, v)


def _post_mix_kernel(x_ref, *refs):
    o_refs = refs[:N_GROUPS]
    l_refs = refs[N_GROUPS:2 * N_GROUPS]
    (zu_ref, zv_ref, sguw_ref, sgub_ref, aonw_ref, sonw_ref, wout_ref, fnw_ref, rwt_ref, rb_ref, tri_ref,
     x1_ref, h2_hbm, gates_ref, eidx_ref, rank_ref, cnt_ref, stage, hstage, hsem) = refs[2 * N_GROUPS:]
    tm = x_ref.shape[0]

    def token_order(ref, slot):
        dil = ref.shape[0]
        if dil == 1:
            return ref[0]
        per_group = GROUP_WIDTH // LANES
        for r in range(dil):
            for c in range(per_group):
                stage[slot * per_group + c, pl.ds(r, tm // dil, stride=dil), :] = ref[r, :, c * LANES:(c + 1) * LANES]
        return jnp.concatenate([stage[slot * per_group + c] for c in range(per_group)], axis=1)

    o = [token_order(ref, g) for g, ref in enumerate(o_refs)]
    l = [token_order(ref, N_GROUPS + g) for g, ref in enumerate(l_refs)]
    lmax = jnp.maximum(jnp.maximum(l[0], l[1]), l[2])
    e = [jnp.exp(lg - lmax) for lg in l]
    esum = e[0] + e[1] + e[2]
    att = (e[0] / esum) * o[0] + (e[1] / esum) * o[1] + (e[2] / esum) * o[2]
    att_n = _rms(att, aonw_ref[...]).astype(BF16)
    cgrp = lax.broadcasted_iota(I32, (1, SGU_WIDTH), 1) // SGU_GROUP_DIM
    sguw = sguw_ref[...]
    gates = []
    for c in range(tm // SGU_CHUNK):
        r = _dot(sguw, zv_ref[c * SGU_CHUNK:(c + 1) * SGU_CHUNK, :])
        g = sgub_ref[...]
        for grp in range(SGU_GROUPS):
            g = g + r[grp * SGU_CHUNK:(grp + 1) * SGU_CHUNK, :] * (cgrp == grp).astype(F32)
        gates.append(g)
    sgu = zu_ref[...] * jnp.concatenate(gates, axis=0)
    sgu_n = _rms(sgu, sonw_ref[...]).astype(BF16)
    x1 = x_ref[...] + _dot(att_n, wout_ref[0:GROUP_WIDTH, :]) + _dot(sgu_n, wout_ref[GROUP_WIDTH:, :])
    x1_ref[...] = x1
    h2 = _rms(x1, fnw_ref[...])
    _to_tiled(hstage, (), _pack_pairs(h2))
    bt = tm // SUBLANES
    h2_copies = [pltpu.make_async_copy(hstage.at[:, :, j, :], h2_hbm.at[pl.ds(pl.program_id(0) * bt, bt), j], hsem)
                 for j in range(SUBLANES)]
    for c in h2_copies:
        c.start()
    logits = _dot_nt(rwt_ref[...], h2, precision=lax.Precision.HIGHEST) + rb_ref[...]
    eiota = lax.broadcasted_iota(I32, (N_EXPERTS, tm), 0)
    vals, idxs = [], []
    for _ in range(TOP_K):
        m = jnp.max(logits, axis=0, keepdims=True)
        idx = jnp.min(jnp.where(logits == m, eiota, N_EXPERTS), axis=0, keepdims=True)
        vals.append(m)
        idxs.append(idx)
        logits = jnp.where(eiota == idx, -jnp.inf, logits)
    exps = [jnp.exp(v - vals[0]) for v in vals]
    den = exps[0] + exps[1] + exps[2] + exps[3]
    gates_ref[...] = jnp.concatenate([ex / den for ex in exps], axis=0)
    eidx_ref[...] = jnp.concatenate(idxs, axis=0)
    onehots = [(eiota == idx).astype(F32) for idx in idxs]
    chosen = onehots[0] + onehots[1] + onehots[2] + onehots[3]
    before = _dot(chosen.astype(BF16), tri_ref[...])
    rank_ref[...] = jnp.concatenate(
        [jnp.sum(oh * before, axis=0, keepdims=True) for oh in onehots], axis=0).astype(I32)
    cnt_ref[...] = jnp.broadcast_to(jnp.sum(chosen, axis=1, keepdims=True), (N_EXPERTS, LANES)).astype(I32)
    for c in h2_copies:
        c.wait()


def _post_mix(x2, seq, o, lse, zu, zv, p):
    T, D = x2.shape
    tm = TOKEN_TILE
    n_tiles = T // tm
    n_seq_tiles = seq // tm
    chunks = D // (2 * LANES)
    const = lambda shape: pl.BlockSpec(shape, lambda i: (0,) * len(shape))
    row = lambda w: pl.BlockSpec((tm, w), lambda i: (i, 0))
    colt = pl.BlockSpec((TOP_K, tm), lambda i: (0, i))
    phase = [_phase_spec(tm, dil, n_seq_tiles) for _, dil in ATT_WINDOWS]
    return pl.pallas_call(
        _post_mix_kernel,
        grid=(n_tiles,),
        in_specs=[row(D)] + phase + phase + [row(SGU_WIDTH), row(SGU_WIDTH),
                  const((SGU_GROUPS * SGU_CHUNK, SGU_CHUNK)), const((SGU_CHUNK, SGU_WIDTH)),
                  const((1, GROUP_WIDTH)), const((1, SGU_WIDTH)), const((GROUP_WIDTH + SGU_WIDTH, D)),
                  const((1, D)), const((N_EXPERTS, D)), const((N_EXPERTS, 1)), const((tm, tm))],
        out_specs=[row(D), pl.BlockSpec(memory_space=pl.ANY),
                   colt, colt, colt, pl.BlockSpec((None, N_EXPERTS, LANES), lambda i: (i, 0, 0))],
        out_shape=[jax.ShapeDtypeStruct((T, D), F32),
                   jax.ShapeDtypeStruct((T // SUBLANES, SUBLANES, chunks, LANES), F32),
                   jax.ShapeDtypeStruct((TOP_K, T), F32), jax.ShapeDtypeStruct((TOP_K, T), I32),
                   jax.ShapeDtypeStruct((TOP_K, T), I32), jax.ShapeDtypeStruct((n_tiles, N_EXPERTS, LANES), I32)],
        scratch_shapes=[pltpu.VMEM((2 * N_GROUPS * GROUP_WIDTH // LANES, tm, LANES), F32),
                        pltpu.VMEM((tm // SUBLANES, chunks, SUBLANES, LANES), F32), pltpu.SemaphoreType.DMA],
        compiler_params=pltpu.CompilerParams(dimension_semantics=("arbitrary",), vmem_limit_bytes=VMEM_LIMIT),
        name="post_mix",
    )(x2, *o, *lse, zu, zv, p["sgu_w"], p["sgu_b"], p["att_out_norm_w"],
      p["sgu_out_norm_w"], p["w_out"], p["ffn_norm_w"], p["router_wt"], p["router_b"], p["tri"])


def _sc_workers():
    info = plsc.get_sparse_core_info()
    return info.num_cores, info.num_cores * info.num_subcores


def _dispatch_sc(h2p, pos, pad_pos, n_rows):
    T, chunks, _ = h2p.shape
    n_cores, n_workers = _sc_workers()
    win = SC_WINDOW
    per_worker = T // n_workers
    pad_per_worker = pad_pos.shape[0] // n_workers
    assert per_worker % win == 0 and pad_per_worker % win == 0
    mesh = plsc.VectorSubcoreMesh(core_axis_name="c", subcore_axis_name="s")

    @functools.partial(
        pl.kernel, mesh=mesh, out_type=jax.ShapeDtypeStruct((n_rows, chunks, LANES), F32),
        scratch_types=[pltpu.VMEM((win,), I32), pltpu.VMEM((win, chunks, LANES), F32)],
        name="dispatch_sc")
    def scatter_rows(h_hbm, pos_hbm, pad_hbm, zeros_hbm, xs_hbm, idx_v, rows_v):
        worker = lax.axis_index("s") * n_cores + lax.axis_index("c")

        @pl.loop(0, per_worker // win)
        def _(w):
            t0 = worker * per_worker + w * win
            pltpu.sync_copy(h_hbm.at[pl.ds(t0, win)], rows_v)
            for k in range(TOP_K):
                pltpu.sync_copy(pos_hbm.at[pl.ds(k * T + t0, win)], idx_v)
                pltpu.sync_copy(rows_v, xs_hbm.at[idx_v])

        pltpu.sync_copy(zeros_hbm, rows_v)

        @pl.loop(0, pad_per_worker // win)
        def _(w):
            pltpu.sync_copy(pad_hbm.at[pl.ds(worker * pad_per_worker + w * win, win)], idx_v)
            pltpu.sync_copy(rows_v, xs_hbm.at[idx_v])

    return scatter_rows(h2p, pos, pad_pos, jnp.zeros((win, chunks, LANES), F32))


def _gather_sc(ys, pos):
    _, chunks, _ = ys.shape
    n_assign = pos.shape[0]
    n_cores, n_workers = _sc_workers()
    win = SC_WINDOW
    per_worker = n_assign // n_workers
    assert per_worker % win == 0
    mesh = plsc.VectorSubcoreMesh(core_axis_name="c", subcore_axis_name="s")

    @functools.partial(
        pl.kernel, mesh=mesh, out_type=jax.ShapeDtypeStruct((n_assign, chunks, LANES), F32),
        scratch_types=[pltpu.VMEM((win,), I32), pltpu.VMEM((win, chunks, LANES), F32)],
        name="gather_sc")
    def gather_rows(ys_hbm, pos_hbm, out_hbm, idx_v, rows_v):
        worker = lax.axis_index("s") * n_cores + lax.axis_index("c")

        @pl.loop(0, per_worker // win)
        def _(w):
            a0 = worker * per_worker + w * win
            pltpu.sync_copy(pos_hbm.at[pl.ds(a0, win)], idx_v)
            pltpu.sync_copy(ys_hbm.at[idx_v], rows_v)
            pltpu.sync_copy(rows_v, out_hbm.at[pl.ds(a0, win)])

    return gather_rows(ys, pos)


def _experts_kernel(blk_e_ref, n_used_ref, xs_hbm, wg_ref, bg_ref, wu_ref, bu_ref, wd_ref, bd_ref, ys_hbm,
                    xbuf, obuf, wg_b, wu_b, wd_b, in_sem, out_sem):
    bt = xbuf.shape[1]
    i = pl.program_id(0)
    n = pl.num_programs(0)
    n_used = n_used_ref[0]
    slot = i % 2

    def in_copies(blk, s):
        return [pltpu.make_async_copy(xs_hbm.at[pl.ds(blk * bt, bt), j], xbuf.at[s, :, :, j, :], in_sem.at[s])
                for j in range(SUBLANES)]

    def out_copies(blk, s):
        return [pltpu.make_async_copy(obuf.at[s, :, :, j, :], ys_hbm.at[pl.ds(blk * bt, bt), j], out_sem.at[s])
                for j in range(SUBLANES)]

    @pl.when(i == 0)
    def _():
        for c in in_copies(0, 0):
            c.start()

    @pl.when(i + 1 < n_used)
    def _():
        for c in in_copies(i + 1, 1 - slot):
            c.start()

    @pl.when(i >= 2)
    def _():
        for c in out_copies(i - 2, slot):
            c.wait()

    @pl.when(i >= n_used)
    def _():
        obuf[slot] = jnp.zeros(obuf.shape[1:], F32)

    @pl.when(i < n_used)
    def _():
        for c in in_copies(i, slot):
            c.wait()
        @pl.when((i == 0) | (blk_e_ref[i] != blk_e_ref[jnp.maximum(i - 1, 0)]))
        def _():
            wg_b[...] = wg_ref[...].astype(BF16)
            wu_b[...] = wu_ref[...].astype(BF16)
            wd_b[...] = wd_ref[...].astype(BF16)

        x = _unpack_pairs(_from_tiled(xbuf, (slot,), slice(0, bt))).astype(BF16)
        g = _dot(x, wg_b[...]) + bg_ref[...]
        u = _dot(x, wu_b[...]) + bu_ref[...]
        g = jnp.minimum(g, SWIGLU_LIMIT)
        u = jnp.clip(u, -SWIGLU_LIMIT, SWIGLU_LIMIT)
        act = (u + 1.0) * (g * jax.nn.sigmoid(SWIGLU_ALPHA * g))
        _to_tiled(obuf, (slot,), _pack_pairs(_dot(act.astype(BF16), wd_b[...]) + bd_ref[...]))

    for c in out_copies(i, slot):
        c.start()

    @pl.when(i == n - 1)
    def _():
        for c in out_copies(i - 1, 1 - slot) + out_copies(i, slot):
            c.wait()


def _experts(xs, blk_e, n_used, p):
    n_rows, chunks, _ = xs.shape
    D = 2 * chunks * LANES
    bm = MOE_BLOCK
    d_ff = p["w_gate"].shape[2]
    assert n_rows // bm >= 2
    wspec = lambda a, b: pl.BlockSpec((None, a, b), lambda i, be, nu: (be[i], 0, 0))
    by_tile = (n_rows // SUBLANES, SUBLANES, chunks, LANES)
    block_buf = pltpu.VMEM((2, bm // SUBLANES, chunks, SUBLANES, LANES), F32)
    ys = pl.pallas_call(
        _experts_kernel,
        grid_spec=pltpu.PrefetchScalarGridSpec(
            num_scalar_prefetch=2,
            grid=(n_rows // bm,),
            in_specs=[pl.BlockSpec(memory_space=pl.ANY), wspec(D, d_ff), wspec(1, d_ff), wspec(D, d_ff),
                      wspec(1, d_ff), wspec(d_ff, D), wspec(1, D)],
            out_specs=pl.BlockSpec(memory_space=pl.ANY),
            scratch_shapes=[block_buf, block_buf, pltpu.VMEM((D, d_ff), BF16), pltpu.VMEM((D, d_ff), BF16),
                            pltpu.VMEM((d_ff, D), BF16), pltpu.SemaphoreType.DMA((2,)), pltpu.SemaphoreType.DMA((2,))],
        ),
        out_shape=jax.ShapeDtypeStruct(by_tile, F32),
        compiler_params=pltpu.CompilerParams(dimension_semantics=("arbitrary",), vmem_limit_bytes=EXPERTS_VMEM_LIMIT),
        name="experts",
    )(blk_e, n_used, xs.reshape(by_tile), p["w_gate"], p["b_gate"], p["w_up"], p["b_up"], p["w_down"], p["b_down"])
    return ys.reshape(n_rows, chunks, LANES)


def _combine_kernel(gates_ref, x1_ref, yg_hbm, out_ref, buf, sem):
    tm = x1_ref.shape[0]
    bt = tm // SUBLANES
    j = pl.program_id(0)
    n = pl.num_programs(0) - 1
    tiles_per_k = yg_hbm.shape[0] // TOP_K

    def copies(tile, s):
        return [pltpu.make_async_copy(yg_hbm.at[pl.ds(k * tiles_per_k + tile * bt, bt), r],
                                      buf.at[s, pl.ds(k * bt, bt), :, r, :], sem.at[s])
                for k in range(TOP_K) for r in range(SUBLANES)]

    @pl.when(j < n)
    def _():
        for c in copies(j, j % 2):
            c.start()

    @pl.when(j >= 1)
    def _():
        done = 1 - j % 2
        for c in copies(j - 1, done):
            c.wait()
        y = gates_ref[:, 0:1] * _unpack_pairs(_from_tiled(buf, (done,), slice(0, bt)))
        for k in range(1, TOP_K):
            y = y + gates_ref[:, k:k + 1] * _unpack_pairs(_from_tiled(buf, (done,), slice(k * bt, (k + 1) * bt)))
        out_ref[...] = x1_ref[...] + y


def _combine(x1, gates_tok, yg):
    T, D = x1.shape
    tm = ROW_TILE
    n = T // tm
    chunks = D // (2 * LANES)
    lag = lambda i: (jnp.maximum(i - 1, 0), 0)
    return pl.pallas_call(
        _combine_kernel,
        grid=(n + 1,),
        in_specs=[pl.BlockSpec((tm, TOP_K), lag), pl.BlockSpec((tm, D), lag), pl.BlockSpec(memory_space=pl.ANY)],
        out_specs=pl.BlockSpec((tm, D), lag),
        out_shape=jax.ShapeDtypeStruct((T, D), F32),
        scratch_shapes=[pltpu.VMEM((2, TOP_K * tm // SUBLANES, chunks, SUBLANES, LANES), F32),
                        pltpu.SemaphoreType.DMA((2,))],
        compiler_params=pltpu.CompilerParams(dimension_semantics=("arbitrary",), vmem_limit_bytes=VMEM_LIMIT),
        name="combine",
    )(gates_tok, x1, yg.reshape(TOP_K * T // SUBLANES, SUBLANES, chunks, LANES))


def _route(eidx, rank, tile_counts):
    T = eidx.shape[1]
    bm = MOE_BLOCK
    n_tiles = T // TOKEN_TILE
    tc = tile_counts[:, :, 0]
    counts = jnp.sum(tc, axis=0)
    tile_off = jnp.cumsum(tc, axis=0) - tc
    pcounts = (counts + bm - 1) // bm * bm
    pends = jnp.cumsum(pcounts)
    pstarts = pends - pcounts
    base = pstarts[None, :] + tile_off
    sel = eidx.reshape(TOP_K, n_tiles, 1, TOKEN_TILE) == jnp.arange(N_EXPERTS, dtype=I32)[None, None, :, None]
    pos = jnp.sum(jnp.where(sel, base[None, :, :, None], 0), axis=2).reshape(TOP_K, T) + rank
    n_blocks = (TOP_K * T) // bm + N_EXPERTS
    n_rows = n_blocks * bm
    blk_start = jnp.arange(n_blocks, dtype=I32) * bm
    blk_e = jnp.minimum(jnp.sum((pends[None, :] <= blk_start[:, None]).astype(I32), axis=1), N_EXPERTS - 1)
    n_used = (pends[-1] // bm).astype(I32).reshape(1)
    first_free = jnp.concatenate([pstarts + counts, pends[-1:]])
    free_off = jnp.cumsum(jnp.concatenate([jnp.zeros((1,), counts.dtype), pcounts - counts]))
    j = jnp.arange(n_rows - TOP_K * T, dtype=I32)[:, None]
    seg_id = jnp.sum((free_off[None, 1:] <= j).astype(I32), axis=1, keepdims=True)
    onehot = seg_id == jnp.arange(N_EXPERTS + 1, dtype=I32)[None, :]
    pad_pos = jnp.sum(jnp.where(onehot, (first_free - free_off)[None, :] + j, 0), axis=1).astype(I32)
    return pos.reshape(-1).astype(I32), pad_pos, blk_e.astype(I32), n_used, n_rows


def _layer(x, p):
    batch, seq, D = x.shape
    x2 = x.reshape(batch * seq, D)
    q, k, v, zu, zv = _in_proj(x2, batch, seq, p)
    o, lse = zip(*[_attention(q[g], k[g], v[g], g) for g in range(N_GROUPS)])
    x1, h2p, gates, eidx, rank, tile_counts = _post_mix(x2, seq, o, lse, zu, zv, p)
    pos, pad_pos, blk_e, n_used, n_rows = _route(eidx, rank, tile_counts)
    xs = _dispatch_sc(h2p.reshape(batch * seq, D // (2 * LANES), LANES), pos, pad_pos, n_rows)
    ys = _experts(xs, blk_e, n_used, p)
    out = _combine(x1, gates.T, _gather_sc(ys, pos))
    return out.reshape(batch, seq, D)


def _rope_tables(seq):
    half = ROT_DIM // 2
    inv_freq = jnp.power(ROPE_THETA, -2.0 * jnp.arange(half, dtype=F32) / ROT_DIM)
    ang = jnp.arange(seq, dtype=F32)[:, None] * inv_freq[None, :]
    cos, sin = jnp.cos(ang), jnp.sin(ang)
    zeros = jnp.zeros((seq, HEAD_DIM - ROT_DIM), F32)
    zh = jnp.zeros((seq, half), F32)
    cos_t = jnp.concatenate([cos, cos, zeros + 1.0], axis=1)
    sa = jnp.concatenate([-sin, zh, zeros], axis=1)
    sb = jnp.concatenate([zh, sin, zeros], axis=1)
    rep = LANES // HEAD_DIM
    return tuple(jnp.tile(t, (1, rep)) for t in (cos_t, sa, sb))


def kernel(x_prompt, x_sample, attn_norm_w, w_in, q_norm_w, k_norm_w, sgu_ln_w, sgu_ln_b, sgu_w, sgu_b,
           att_out_norm_w, sgu_out_norm_w, w_out, ffn_norm_w, router_w, router_b,
           w_gate, b_gate, w_up, b_up, w_down, b_down):
    depth = w_in.shape[0]
    n_heads = ATT_WIDTH // HEAD_DIM
    blk = np.arange(MXU_DIM) // HEAD_DIM
    tri = np.arange(TOKEN_TILE)
    xa, xb = x_prompt, x_sample
    rope = {x.shape[1]: _rope_tables(x.shape[1]) for x in (xa, xb)}
    for l in range(depth):
        shared = dict(
            attn_norm_w=attn_norm_w[l][None], w_in=w_in[l].astype(BF16),
            q_norm_w=jnp.tile(q_norm_w[l], n_heads)[None], k_norm_w=jnp.tile(k_norm_w[l], n_heads)[None],
            head_ones=jnp.asarray(blk[:, None] == blk[None, :], BF16),
            sgu_ln_w=sgu_ln_w[l][None], sgu_ln_b=sgu_ln_b[l][None],
            sgu_w=sgu_w[l].reshape(SGU_GROUPS * SGU_CHUNK, SGU_CHUNK).astype(BF16),
            sgu_b=jnp.repeat(sgu_b[l].T, SGU_GROUP_DIM, axis=1),
            att_out_norm_w=att_out_norm_w[l][None], sgu_out_norm_w=sgu_out_norm_w[l][None],
            w_out=w_out[l].astype(BF16), ffn_norm_w=ffn_norm_w[l][None],
            router_wt=router_w[l].T, router_b=router_b[l][:, None],
            tri=jnp.asarray(tri[:, None] < tri[None, :], BF16),
            w_gate=w_gate[l], b_gate=b_gate[l][:, None, :],
            w_up=w_up[l], b_up=b_up[l][:, None, :],
            w_down=w_down[l], b_down=b_down[l][:, None, :],
        )
        xa, xb = (_layer(x, dict(shared, rope=rope)) for x in (xa, xb))
    return xa, xb
```

```python
import functools

import numpy as np
import jax
import jax.numpy as jnp
from jax import lax
from jax.experimental import pallas as pl
from jax.experimental.pallas import tpu as pltpu
from jax.experimental.pallas import tpu_sc as plsc

F32 = jnp.float32
BF16 = jnp.bfloat16
I32 = jnp.int32

HEAD_DIM = 64
ATT_WINDOWS = ((128, 1), (512, 4), (2048, 16))
N_GROUPS = len(ATT_WINDOWS)
HEADS_PER_GROUP = 4
GROUP_WIDTH = HEADS_PER_GROUP * HEAD_DIM
ATT_WIDTH = N_GROUPS * GROUP_WIDTH
SGU_WIDTH = 256
SGU_GROUP_DIM = 64
SGU_GROUPS = SGU_WIDTH // SGU_GROUP_DIM
SGU_CHUNK = 128
ROT_DIM = HEAD_DIM // 4
ROPE_THETA = 500000.0
N_EXPERTS = 32
TOP_K = 4
SWIGLU_LIMIT = 7.0
SWIGLU_ALPHA = 1.702
NORM_EPS = 1e-6
LN_EPS = 1e-5
NEG_INF = -1e30

LANES = 128
SUBLANES = 8
MXU_DIM = 256
TOKEN_TILE = 512
ATT_Q_BLOCK = 256
ATT_SUB_Q = 128
MOE_BLOCK = 512
ROW_TILE = 256
SC_WINDOW = 64
VMEM_LIMIT = 48 * 1024 * 1024
EXPERTS_VMEM_LIMIT = 56 * 1024 * 1024


def _dot(a, b):
    return jnp.dot(a, b, preferred_element_type=F32)


def _dot_nt(a, b, precision=None):
    return lax.dot_general(a, b, (((1,), (1,)), ((), ())), precision=precision, preferred_element_type=F32)


def _rms(x, w):
    return x * lax.rsqrt(jnp.mean(x * x, axis=-1, keepdims=True) + NORM_EPS) * w


def _gelu(x):
    return 0.5 * x * (1.0 + lax.erf(x * np.float32(np.sqrt(0.5))))


def _pack_pairs(x):
    w = x.shape[1] // 2
    lo = lax.bitcast_convert_type(x[:, :w].astype(BF16).astype(F32), jnp.uint32) >> 16
    hi = lax.bitcast_convert_type(x[:, w:].astype(BF16).astype(F32), jnp.uint32) & jnp.uint32(0xFFFF0000)
    return lax.bitcast_convert_type(lo | hi, F32)


def _unpack_pairs(words):
    u = lax.bitcast_convert_type(words, jnp.uint32)
    lo = lax.bitcast_convert_type(u << 16, F32)
    hi = lax.bitcast_convert_type(u & jnp.uint32(0xFFFF0000), F32)
    return jnp.concatenate([lo, hi], axis=1)


def _to_tiled(ref, idx, value):
    rows = value.shape[0]
    for c in range(value.shape[1] // LANES):
        ref[idx + (slice(None), c)] = value[:, c * LANES:(c + 1) * LANES].reshape(rows // SUBLANES, SUBLANES, LANES)


def _from_tiled(ref, idx, row_tiles):
    chunks = ref.shape[-3]
    n = (row_tiles.stop - row_tiles.start) * SUBLANES
    return jnp.concatenate([ref[idx + (row_tiles, c)].reshape(n, LANES) for c in range(chunks)], axis=1)


def _in_proj_kernel(x_ref, anw_ref, w_ref, qnw_ref, knw_ref, ones_ref, cos_ref, sa_ref, sb_ref, lnw_ref, lnb_ref,
                    *refs):
    qkv_refs = refs[:3 * N_GROUPS]
    zu_ref, zv_ref, stage = refs[3 * N_GROUPS:]
    tm = x_ref.shape[0]
    h = _rms(x_ref[...], anw_ref[...]).astype(BF16)
    reps = ATT_WIDTH // LANES
    cos = jnp.concatenate([cos_ref[...]] * reps, axis=1)
    sa = jnp.concatenate([sa_ref[...]] * reps, axis=1)
    sb = jnp.concatenate([sb_ref[...]] * reps, axis=1)
    ones = ones_ref[...]

    def head_norm_rope(t, nw):
        sq = (t * t).astype(BF16)
        parts = []
        for j in range(ATT_WIDTH // MXU_DIM):
            sl = slice(j * MXU_DIM, (j + 1) * MXU_DIM)
            parts.append(_dot(sq[:, sl], ones))
        ssum = jnp.concatenate(parts, axis=1)
        t = t * lax.rsqrt(ssum * (1.0 / HEAD_DIM) + NORM_EPS) * nw
        half = ROT_DIM // 2
        return t * cos + pltpu.roll(t, ATT_WIDTH - half, 1) * sa + pltpu.roll(t, half, 1) * sb

    def emit(which, t):
        n_chunks = ATT_WIDTH // LANES
        per_group = GROUP_WIDTH // LANES
        for c in range(n_chunks):
            stage[which * n_chunks + c] = t[:, c * LANES:(c + 1) * LANES]
        for g, (_, dil) in enumerate(ATT_WINDOWS):
            out = qkv_refs[which * N_GROUPS + g]
            first = which * n_chunks + g * per_group
            for r in range(dil):
                rows = pl.ds(r, tm // dil, stride=dil)
                out[r] = jnp.concatenate([stage[first + c, rows, :] for c in range(per_group)], axis=1).astype(BF16)

    q = head_norm_rope(_dot(h, w_ref[:, 0:ATT_WIDTH]), qnw_ref[...])
    emit(0, q * (HEAD_DIM ** -0.5))
    emit(1, head_norm_rope(_dot(h, w_ref[:, ATT_WIDTH:2 * ATT_WIDTH]), knw_ref[...]))
    emit(2, _dot(h, w_ref[:, 2 * ATT_WIDTH:3 * ATT_WIDTH]))
    z = _dot(h, w_ref[:, 3 * ATT_WIDTH:3 * ATT_WIDTH + 2 * SGU_WIDTH])
    zu_ref[...] = _gelu(z[:, :SGU_WIDTH])
    gv = _gelu(z[:, SGU_WIDTH:])
    mu = jnp.mean(gv, axis=-1, keepdims=True)
    var = jnp.mean(jnp.square(gv - mu), axis=-1, keepdims=True)
    zv_ref[...] = ((gv - mu) * lax.rsqrt(var + LN_EPS) * lnw_ref[...] + lnb_ref[...]).astype(BF16)


def _phase_spec(tm, dil, n_seq_tiles):
    return pl.BlockSpec((None, dil, tm // dil, GROUP_WIDTH), lambda i: (i // n_seq_tiles, 0, i % n_seq_tiles, 0))


def _in_proj(x2, batch, seq, p):
    T, D = x2.shape
    tm = TOKEN_TILE
    n_seq_tiles = seq // tm
    in_width = p["w_in"].shape[1]
    const = lambda shape: pl.BlockSpec(shape, lambda i: (0,) * len(shape))
    rope = pl.BlockSpec((tm, LANES), lambda i: (i % n_seq_tiles, 0))
    row = lambda w: pl.BlockSpec((tm, w), lambda i: (i, 0))
    qkv_specs = [_phase_spec(tm, dil, n_seq_tiles) for _ in range(3) for _, dil in ATT_WINDOWS]
    qkv_shapes = [jax.ShapeDtypeStruct((batch, dil, seq // dil, GROUP_WIDTH), BF16)
                  for _ in range(3) for _, dil in ATT_WINDOWS]
    outs = pl.pallas_call(
        _in_proj_kernel,
        grid=(T // tm,),
        in_specs=[row(D), const((1, D)), const((D, in_width)), const((1, ATT_WIDTH)), const((1, ATT_WIDTH)),
                  const((MXU_DIM, MXU_DIM)), rope, rope, rope, const((1, SGU_WIDTH)), const((1, SGU_WIDTH))],
        out_specs=qkv_specs + [row(SGU_WIDTH), row(SGU_WIDTH)],
        out_shape=qkv_shapes + [jax.ShapeDtypeStruct((T, SGU_WIDTH), F32), jax.ShapeDtypeStruct((T, SGU_WIDTH), BF16)],
        scratch_shapes=[pltpu.VMEM((3 * ATT_WIDTH // LANES, tm, LANES), F32)],
        compiler_params=pltpu.CompilerParams(dimension_semantics=("arbitrary",), vmem_limit_bytes=VMEM_LIMIT),
        name="in_proj",
    )(x2, p["attn_norm_w"], p["w_in"], p["q_norm_w"], p["k_norm_w"], p["head_ones"],
      *p["rope"][seq], p["sgu_ln_w"], p["sgu_ln_b"])
    q, k, v = (outs[i * N_GROUPS:(i + 1) * N_GROUPS] for i in range(3))
    return q, k, v, outs[-2], outs[-1]


def _attn_kernel(q_ref, kp_ref, km_ref, kn_ref, vp_ref, vm_ref, vn_ref, o_ref, lse_ref, *, sub_len, steps):
    lq = q_ref.shape[0]
    sq = min(lq, ATT_SUB_Q)
    lk = sq + 2 * steps
    nh = HEADS_PER_GROUP
    lb = pl.program_id(2)
    kk = jnp.concatenate([kp_ref[...], km_ref[...], kn_ref[...]], axis=0)
    vv = jnp.concatenate([vp_ref[...], vm_ref[...], vn_ref[...]], axis=0)
    head = lax.broadcasted_iota(I32, (1, GROUP_WIDTH), 1) // HEAD_DIM
    hm_f = [(head == h).astype(F32) for h in range(nh)]
    hm_b = [m.astype(BF16) for m in hm_f]
    qi = lax.broadcasted_iota(I32, (sq, 1), 0)
    kj = lax.broadcasted_iota(I32, (sq, lk), 1)
    for j in range(lq // sq):
        first = lb * lq + j * sq
        lo = jnp.maximum(qi, steps - first)
        hi = jnp.minimum(qi + 2 * steps, sub_len - 1 + steps - first)
        mask = ((kj - lo).astype(jnp.uint32) <= (hi - lo).astype(jnp.uint32))[None]
        qj = q_ref[j * sq:(j + 1) * sq, :]
        qs = jnp.concatenate([qj * hm_b[h] for h in range(nh)], axis=0)
        s = _dot_nt(qs, kk[j * sq:j * sq + lk]).reshape(nh, sq, lk)
        s = jnp.where(mask, s, NEG_INF)
        m = jnp.max(s, axis=-1, keepdims=True)
        pr = jnp.exp(s - m)
        den = jnp.sum(pr, axis=-1, keepdims=True)
        oh = _dot(pr.reshape(nh * sq, lk).astype(BF16), vv[j * sq:j * sq + lk]).reshape(nh, sq, GROUP_WIDTH) / den
        lh = m + jnp.log(den)
        o_ref[j * sq:(j + 1) * sq, :] = sum(oh[h] * hm_f[h] for h in range(nh))
        lse_ref[j * sq:(j + 1) * sq, :] = sum(lh[h] * hm_f[h] for h in range(nh))


def _attention(q, k, v, group):
    window, dil = ATT_WINDOWS[group]
    steps = window // (2 * dil)
    batch, _, sub_len, _ = q.shape
    assert sub_len % steps == 0
    lq = min(ATT_Q_BLOCK, sub_len)
    assert lq % steps == 0 and sub_len % lq == 0
    per_q = lq // steps
    n_halo = sub_len // steps
    main = pl.BlockSpec((None, None, lq, GROUP_WIDTH), lambda b, r, i: (b, r, i, 0))
    prev = pl.BlockSpec((None, None, steps, GROUP_WIDTH), lambda b, r, i: (b, r, jnp.maximum(i * per_q - 1, 0), 0))
    nxt = pl.BlockSpec((None, None, steps, GROUP_WIDTH),
                       lambda b, r, i: (b, r, jnp.minimum((i + 1) * per_q, n_halo - 1), 0))
    return pl.pallas_call(
        functools.partial(_attn_kernel, sub_len=sub_len, steps=steps),
        grid=(batch, dil, sub_len // lq),
        in_specs=[main, prev, main, nxt, prev, main, nxt],
        out_specs=[main, main],
        out_shape=[jax.ShapeDtypeStruct(q.shape, F32)] * 2,
        compiler_params=pltpu.CompilerParams(dimension_semantics=("arbitrary",) * 3, vmem_limit_bytes=VMEM_LIMIT),
        name=f"attention_g{group}",
    )(q, k, k, k, v, v, v)


def _post_mix_kernel(x_ref, *refs):
    o_refs = refs[:N_GROUPS]
    l_refs = refs[N_GROUPS:2 * N_GROUPS]
    (zu_ref, zv_ref, sguw_ref, sgub_ref, aonw_ref, sonw_ref, wout_ref, fnw_ref, rwt_ref, rb_ref, tri_ref,
     x1_ref, h2_hbm, gates_ref, eidx_ref, rank_ref, cnt_ref, stage, hstage, hsem) = refs[2 * N_GROUPS:]
    tm = x_ref.shape[0]

    def token_order(ref, slot):
        dil = ref.shape[0]
        if dil == 1:
            return ref[0]
        per_group = GROUP_WIDTH // LANES
        for r in range(dil):
            for c in range(per_group):
                stage[slot * per_group + c, pl.ds(r, tm // dil, stride=dil), :] = ref[r, :, c * LANES:(c + 1) * LANES]
        return jnp.concatenate([stage[slot * per_group + c] for c in range(per_group)], axis=1)

    o = [token_order(ref, g) for g, ref in enumerate(o_refs)]
    l = [token_order(ref, N_GROUPS + g) for g, ref in enumerate(l_refs)]
    lmax = jnp.maximum(jnp.maximum(l[0], l[1]), l[2])
    e = [jnp.exp(lg - lmax) for lg in l]
    esum = e[0] + e[1] + e[2]
    att = (e[0] / esum) * o[0] + (e[1] / esum) * o[1] + (e[2] / esum) * o[2]
    att_n = _rms(att, aonw_ref[...]).astype(BF16)
    cgrp = lax.broadcasted_iota(I32, (1, SGU_WIDTH), 1) // SGU_GROUP_DIM
    sguw = sguw_ref[...]
    gates = []
    for c in range(tm // SGU_CHUNK):
        r = _dot(sguw, zv_ref[c * SGU_CHUNK:(c + 1) * SGU_CHUNK, :])
        g = sgub_ref[...]
        for grp in range(SGU_GROUPS):
            g = g + r[grp * SGU_CHUNK:(grp + 1) * SGU_CHUNK, :] * (cgrp == grp).astype(F32)
        gates.append(g)
    sgu = zu_ref[...] * jnp.concatenate(gates, axis=0)
    sgu_n = _rms(sgu, sonw_ref[...]).astype(BF16)
    x1 = x_ref[...] + _dot(att_n, wout_ref[0:GROUP_WIDTH, :]) + _dot(sgu_n, wout_ref[GROUP_WIDTH:, :])
    x1_ref[...] = x1
    h2 = _rms(x1, fnw_ref[...])
    _to_tiled(hstage, (), _pack_pairs(h2))
    bt = tm // SUBLANES
    h2_copies = [pltpu.make_async_copy(hstage.at[:, :, j, :], h2_hbm.at[pl.ds(pl.program_id(0) * tm + j * bt, bt)], hsem)
                 for j in range(SUBLANES)]
    for c in h2_copies:
        c.start()
    logits = _dot_nt(rwt_ref[...], h2, precision=lax.Precision.HIGHEST) + rb_ref[...]
    eiota = lax.broadcasted_iota(I32, (N_EXPERTS, tm), 0)
    vals, idxs = [], []
    for _ in range(TOP_K):
        m = jnp.max(logits, axis=0, keepdims=True)
        idx = jnp.min(jnp.where(logits == m, eiota, N_EXPERTS), axis=0, keepdims=True)
        vals.append(m)
        idxs.append(idx)
        logits = jnp.where(eiota == idx, -jnp.inf, logits)
    exps = [jnp.exp(v - vals[0]) for v in vals]
    den = exps[0] + exps[1] + exps[2] + exps[3]
    gates_ref[...] = jnp.concatenate([ex / den for ex in exps], axis=0)
    eidx_ref[...] = jnp.concatenate(idxs, axis=0)
    onehots = [(eiota == idx).astype(F32) for idx in idxs]
    chosen = onehots[0] + onehots[1] + onehots[2] + onehots[3]
    before = _dot(chosen.astype(BF16), tri_ref[...])
    rank_ref[...] = jnp.concatenate(
        [jnp.sum(oh * before, axis=0, keepdims=True) for oh in onehots], axis=0).astype(I32)
    cnt_ref[...] = jnp.broadcast_to(jnp.sum(chosen, axis=1, keepdims=True), (N_EXPERTS, LANES)).astype(I32)
    for c in h2_copies:
        c.wait()


def _post_mix(x2, seq, o, lse, zu, zv, p):
    T, D = x2.shape
    tm = TOKEN_TILE
    n_tiles = T // tm
    n_seq_tiles = seq // tm
    chunks = D // (2 * LANES)
    const = lambda shape: pl.BlockSpec(shape, lambda i: (0,) * len(shape))
    row = lambda w: pl.BlockSpec((tm, w), lambda i: (i, 0))
    colt = pl.BlockSpec((TOP_K, tm), lambda i: (0, i))
    phase = [_phase_spec(tm, dil, n_seq_tiles) for _, dil in ATT_WINDOWS]
    return pl.pallas_call(
        _post_mix_kernel,
        grid=(n_tiles,),
        in_specs=[row(D)] + phase + phase + [row(SGU_WIDTH), row(SGU_WIDTH),
                  const((SGU_GROUPS * SGU_CHUNK, SGU_CHUNK)), const((SGU_CHUNK, SGU_WIDTH)),
                  const((1, GROUP_WIDTH)), const((1, SGU_WIDTH)), const((GROUP_WIDTH + SGU_WIDTH, D)),
                  const((1, D)), const((N_EXPERTS, D)), const((N_EXPERTS, 1)), const((tm, tm))],
        out_specs=[row(D), pl.BlockSpec(memory_space=pl.ANY),
                   colt, colt, colt, pl.BlockSpec((None, N_EXPERTS, LANES), lambda i: (i, 0, 0))],
        out_shape=[jax.ShapeDtypeStruct((T, D), F32),
                   jax.ShapeDtypeStruct((T, chunks, LANES), F32),
                   jax.ShapeDtypeStruct((TOP_K, T), F32), jax.ShapeDtypeStruct((TOP_K, T), I32),
                   jax.ShapeDtypeStruct((TOP_K, T), I32), jax.ShapeDtypeStruct((n_tiles, N_EXPERTS, LANES), I32)],
        scratch_shapes=[pltpu.VMEM((2 * N_GROUPS * GROUP_WIDTH // LANES, tm, LANES), F32),
                        pltpu.VMEM((tm // SUBLANES, chunks, SUBLANES, LANES), F32), pltpu.SemaphoreType.DMA],
        compiler_params=pltpu.CompilerParams(dimension_semantics=("arbitrary",), vmem_limit_bytes=VMEM_LIMIT),
        name="post_mix",
    )(x2, *o, *lse, zu, zv, p["sgu_w"], p["sgu_b"], p["att_out_norm_w"],
      p["sgu_out_norm_w"], p["w_out"], p["ffn_norm_w"], p["router_wt"], p["router_b"], p["tri"])


def _sc_workers():
    info = plsc.get_sparse_core_info()
    return info.num_cores, info.num_cores * info.num_subcores


def _dispatch_sc(h2p, pos, pad_pos, n_rows):
    T, chunks, _ = h2p.shape
    n_cores, n_workers = _sc_workers()
    win = SC_WINDOW
    per_worker = T // n_workers
    pad_per_worker = pad_pos.shape[0] // n_workers
    assert per_worker % win == 0 and pad_per_worker % win == 0
    mesh = plsc.VectorSubcoreMesh(core_axis_name="c", subcore_axis_name="s")

    @functools.partial(
        pl.kernel, mesh=mesh, out_type=jax.ShapeDtypeStruct((n_rows, chunks, LANES), F32),
        scratch_types=[pltpu.VMEM((win,), I32), pltpu.VMEM((win, chunks, LANES), F32)],
        name="dispatch_sc")
    def scatter_rows(h_hbm, pos_hbm, pad_hbm, zeros_hbm, xs_hbm, idx_v, rows_v):
        worker = lax.axis_index("s") * n_cores + lax.axis_index("c")

        @pl.loop(0, per_worker // win)
        def _(w):
            t0 = worker * per_worker + w * win
            pltpu.sync_copy(h_hbm.at[pl.ds(t0, win)], rows_v)
            for k in range(TOP_K):
                pltpu.sync_copy(pos_hbm.at[pl.ds(k * T + t0, win)], idx_v)
                pltpu.sync_copy(rows_v, xs_hbm.at[idx_v])

        pltpu.sync_copy(zeros_hbm, rows_v)

        @pl.loop(0, pad_per_worker // win)
        def _(w):
            pltpu.sync_copy(pad_hbm.at[pl.ds(worker * pad_per_worker + w * win, win)], idx_v)
            pltpu.sync_copy(rows_v, xs_hbm.at[idx_v])

    return scatter_rows(h2p, pos, pad_pos, jnp.zeros((win, chunks, LANES), F32))


def _gather_sc(ys, pos):
    _, chunks, _ = ys.shape
    n_assign = pos.shape[0]
    n_cores, n_workers = _sc_workers()
    win = SC_WINDOW
    per_worker = n_assign // n_workers
    assert per_worker % win == 0
    mesh = plsc.VectorSubcoreMesh(core_axis_name="c", subcore_axis_name="s")

    @functools.partial(
        pl.kernel, mesh=mesh, out_type=jax.ShapeDtypeStruct((n_assign, chunks, LANES), F32),
        scratch_types=[pltpu.VMEM((win,), I32), pltpu.VMEM((win, chunks, LANES), F32)],
        name="gather_sc")
    def gather_rows(ys_hbm, pos_hbm, out_hbm, idx_v, rows_v):
        worker = lax.axis_index("s") * n_cores + lax.axis_index("c")

        @pl.loop(0, per_worker // win)
        def _(w):
            a0 = worker * per_worker + w * win
            pltpu.sync_copy(pos_hbm.at[pl.ds(a0, win)], idx_v)
            pltpu.sync_copy(ys_hbm.at[idx_v], rows_v)
            pltpu.sync_copy(rows_v, out_hbm.at[pl.ds(a0, win)])

    return gather_rows(ys, pos)


def _experts_kernel(blk_e_ref, n_used_ref, xs_hbm, wg_ref, bg_ref, wu_ref, bu_ref, wd_ref, bd_ref, ys_hbm,
                    xbuf, obuf, wg_b, wu_b, wd_b, in_sem, out_sem):
    bt = xbuf.shape[1]
    i = pl.program_id(0)
    n = pl.num_programs(0)
    n_used = n_used_ref[0]
    slot = i % 2

    def in_copies(blk, s):
        return [pltpu.make_async_copy(xs_hbm.at[pl.ds((blk * SUBLANES + j) * bt, bt)], xbuf.at[s, :, :, j, :],
                                      in_sem.at[s]) for j in range(SUBLANES)]

    def out_copies(blk, s):
        return [pltpu.make_async_copy(obuf.at[s, :, :, j, :], ys_hbm.at[pl.ds((blk * SUBLANES + j) * bt, bt)],
                                      out_sem.at[s]) for j in range(SUBLANES)]

    @pl.when(i == 0)
    def _():
        for c in in_copies(0, 0):
            c.start()

    @pl.when(i + 1 < n_used)
    def _():
        for c in in_copies(i + 1, 1 - slot):
            c.start()

    @pl.when(i >= 2)
    def _():
        for c in out_copies(i - 2, slot):
            c.wait()

    @pl.when(i >= n_used)
    def _():
        obuf[slot] = jnp.zeros(obuf.shape[1:], F32)

    @pl.when(i < n_used)
    def _():
        for c in in_copies(i, slot):
            c.wait()
        @pl.when((i == 0) | (blk_e_ref[i] != blk_e_ref[jnp.maximum(i - 1, 0)]))
        def _():
            wg_b[...] = wg_ref[...].astype(BF16)
            wu_b[...] = wu_ref[...].astype(BF16)
            wd_b[...] = wd_ref[...].astype(BF16)

        x = _unpack_pairs(_from_tiled(xbuf, (slot,), slice(0, bt))).astype(BF16)
        g = _dot(x, wg_b[...]) + bg_ref[...]
        u = _dot(x, wu_b[...]) + bu_ref[...]
        g = jnp.minimum(g, SWIGLU_LIMIT)
        u = jnp.clip(u, -SWIGLU_LIMIT, SWIGLU_LIMIT)
        act = (u + 1.0) * (g * jax.nn.sigmoid(SWIGLU_ALPHA * g))
        _to_tiled(obuf, (slot,), _pack_pairs(_dot(act.astype(BF16), wd_b[...]) + bd_ref[...]))

    for c in out_copies(i, slot):
        c.start()

    @pl.when(i == n - 1)
    def _():
        for c in out_copies(i - 1, 1 - slot) + out_copies(i, slot):
            c.wait()


def _experts(xs, blk_e, n_used, p):
    n_rows, chunks, _ = xs.shape
    D = 2 * chunks * LANES
    bm = MOE_BLOCK
    d_ff = p["w_gate"].shape[2]
    assert n_rows // bm >= 2
    wspec = lambda a, b: pl.BlockSpec((None, a, b), lambda i, be, nu: (be[i], 0, 0))
    block_buf = pltpu.VMEM((2, bm // SUBLANES, chunks, SUBLANES, LANES), F32)
    return pl.pallas_call(
        _experts_kernel,
        grid_spec=pltpu.PrefetchScalarGridSpec(
            num_scalar_prefetch=2,
            grid=(n_rows // bm,),
            in_specs=[pl.BlockSpec(memory_space=pl.ANY), wspec(D, d_ff), wspec(1, d_ff), wspec(D, d_ff),
                      wspec(1, d_ff), wspec(d_ff, D), wspec(1, D)],
            out_specs=pl.BlockSpec(memory_space=pl.ANY),
            scratch_shapes=[block_buf, block_buf, pltpu.VMEM((D, d_ff), BF16), pltpu.VMEM((D, d_ff), BF16),
                            pltpu.VMEM((d_ff, D), BF16), pltpu.SemaphoreType.DMA((2,)), pltpu.SemaphoreType.DMA((2,))],
        ),
        out_shape=jax.ShapeDtypeStruct(xs.shape, F32),
        compiler_params=pltpu.CompilerParams(dimension_semantics=("arbitrary",), vmem_limit_bytes=EXPERTS_VMEM_LIMIT),
        name="experts",
    )(blk_e, n_used, xs, p["w_gate"], p["b_gate"], p["w_up"], p["b_up"], p["w_down"], p["b_down"])


def _combine_kernel(gates_ref, x1_ref, yg_hbm, out_ref, buf, sem):
    tm = x1_ref.shape[0]
    bt = tm // SUBLANES
    j = pl.program_id(0)
    n = pl.num_programs(0) - 1
    rows_per_k = yg_hbm.shape[0] // TOP_K

    def copies(tile, s):
        return [pltpu.make_async_copy(yg_hbm.at[pl.ds(k * rows_per_k + tile * tm + r * bt, bt)],
                                      buf.at[s, pl.ds(k * bt, bt), :, r, :], sem.at[s])
                for k in range(TOP_K) for r in range(SUBLANES)]

    @pl.when(j < n)
    def _():
        for c in copies(j, j % 2):
            c.start()

    @pl.when(j >= 1)
    def _():
        done = 1 - j % 2
        for c in copies(j - 1, done):
            c.wait()
        y = gates_ref[:, 0:1] * _unpack_pairs(_from_tiled(buf, (done,), slice(0, bt)))
        for k in range(1, TOP_K):
            y = y + gates_ref[:, k:k + 1] * _unpack_pairs(_from_tiled(buf, (done,), slice(k * bt, (k + 1) * bt)))
        out_ref[...] = x1_ref[...] + y


def _combine(x1, gates_tok, yg):
    T, D = x1.shape
    tm = ROW_TILE
    n = T // tm
    chunks = D // (2 * LANES)
    lag = lambda i: (jnp.maximum(i - 1, 0), 0)
    return pl.pallas_call(
        _combine_kernel,
        grid=(n + 1,),
        in_specs=[pl.BlockSpec((tm, TOP_K), lag), pl.BlockSpec((tm, D), lag), pl.BlockSpec(memory_space=pl.ANY)],
        out_specs=pl.BlockSpec((tm, D), lag),
        out_shape=jax.ShapeDtypeStruct((T, D), F32),
        scratch_shapes=[pltpu.VMEM((2, TOP_K * tm // SUBLANES, chunks, SUBLANES, LANES), F32),
                        pltpu.SemaphoreType.DMA((2,))],
        compiler_params=pltpu.CompilerParams(dimension_semantics=("arbitrary",), vmem_limit_bytes=VMEM_LIMIT),
        name="combine",
    )(gates_tok, x1, yg)


def _phase_major_index(i, block):
    per_phase = block // SUBLANES
    return i // block * block + i % SUBLANES * per_phase + i % block // SUBLANES


def _phase_major(a, block):
    lead = a.shape[:-1]
    a = a.reshape(lead + (a.shape[-1] // block, block // SUBLANES, SUBLANES))
    return jnp.swapaxes(a, -1, -2).reshape(lead + (-1,))


def _route(eidx, rank, tile_counts):
    T = eidx.shape[1]
    bm = MOE_BLOCK
    n_tiles = T // TOKEN_TILE
    tc = tile_counts[:, :, 0]
    counts = jnp.sum(tc, axis=0)
    tile_off = jnp.cumsum(tc, axis=0) - tc
    pcounts = (counts + bm - 1) // bm * bm
    pends = jnp.cumsum(pcounts)
    pstarts = pends - pcounts
    base = pstarts[None, :] + tile_off
    sel = eidx.reshape(TOP_K, n_tiles, 1, TOKEN_TILE) == jnp.arange(N_EXPERTS, dtype=I32)[None, None, :, None]
    pos = jnp.sum(jnp.where(sel, base[None, :, :, None], 0), axis=2).reshape(TOP_K, T) + rank
    n_blocks = (TOP_K * T) // bm + N_EXPERTS
    n_rows = n_blocks * bm
    blk_start = jnp.arange(n_blocks, dtype=I32) * bm
    blk_e = jnp.minimum(jnp.sum((pends[None, :] <= blk_start[:, None]).astype(I32), axis=1), N_EXPERTS - 1)
    n_used = (pends[-1] // bm).astype(I32).reshape(1)
    first_free = jnp.concatenate([pstarts + counts, pends[-1:]])
    free_off = jnp.cumsum(jnp.concatenate([jnp.zeros((1,), counts.dtype), pcounts - counts]))
    j = jnp.arange(n_rows - TOP_K * T, dtype=I32)[:, None]
    seg_id = jnp.sum((free_off[None, 1:] <= j).astype(I32), axis=1, keepdims=True)
    onehot = seg_id == jnp.arange(N_EXPERTS + 1, dtype=I32)[None, :]
    pad_pos = jnp.sum(jnp.where(onehot, (first_free - free_off)[None, :] + j, 0), axis=1).astype(I32)
    pos, pad_pos = _phase_major_index(pos, bm), _phase_major_index(pad_pos, bm)
    pos_dispatch = _phase_major(pos, TOKEN_TILE).reshape(-1).astype(I32)
    pos_gather = _phase_major(pos, ROW_TILE).reshape(-1).astype(I32)
    return pos_dispatch, pos_gather, pad_pos.astype(I32), blk_e.astype(I32), n_used, n_rows


def _layer(x, p):
    batch, seq, D = x.shape
    x2 = x.reshape(batch * seq, D)
    q, k, v, zu, zv = _in_proj(x2, batch, seq, p)
    o, lse = zip(*[_attention(q[g], k[g], v[g], g) for g in range(N_GROUPS)])
    x1, h2p, gates, eidx, rank, tile_counts = _post_mix(x2, seq, o, lse, zu, zv, p)
    pos_dispatch, pos_gather, pad_pos, blk_e, n_used, n_rows = _route(eidx, rank, tile_counts)
    xs = _dispatch_sc(h2p, pos_dispatch, pad_pos, n_rows)
    ys = _experts(xs, blk_e, n_used, p)
    out = _combine(x1, gates.T, _gather_sc(ys, pos_gather))
    return out.reshape(batch, seq, D)


def _rope_tables(seq):
    half = ROT_DIM // 2
    inv_freq = jnp.power(ROPE_THETA, -2.0 * jnp.arange(half, dtype=F32) / ROT_DIM)
    ang = jnp.arange(seq, dtype=F32)[:, None] * inv_freq[None, :]
    cos, sin = jnp.cos(ang), jnp.sin(ang)
    zeros = jnp.zeros((seq, HEAD_DIM - ROT_DIM), F32)
    zh = jnp.zeros((seq, half), F32)
    cos_t = jnp.concatenate([cos, cos, zeros + 1.0], axis=1)
    sa = jnp.concatenate([-sin, zh, zeros], axis=1)
    sb = jnp.concatenate([zh, sin, zeros], axis=1)
    rep = LANES // HEAD_DIM
    return tuple(jnp.tile(t, (1, rep)) for t in (cos_t, sa, sb))


def kernel(x_prompt, x_sample, attn_norm_w, w_in, q_norm_w, k_norm_w, sgu_ln_w, sgu_ln_b, sgu_w, sgu_b,
           att_out_norm_w, sgu_out_norm_w, w_out, ffn_norm_w, router_w, router_b,
           w_gate, b_gate, w_up, b_up, w_down, b_down):
    depth = w_in.shape[0]
    n_heads = ATT_WIDTH // HEAD_DIM
    blk = np.arange(MXU_DIM) // HEAD_DIM
    tri = np.arange(TOKEN_TILE)
    xa, xb = x_prompt, x_sample
    rope = {x.shape[1]: _rope_tables(x.shape[1]) for x in (xa, xb)}
    for l in range(depth):
        shared = dict(
            attn_norm_w=attn_norm_w[l][None], w_in=w_in[l].astype(BF16),
            q_norm_w=jnp.tile(q_norm_w[l], n_heads)[None], k_norm_w=jnp.tile(k_norm_w[l], n_heads)[None],
            head_ones=jnp.asarray(blk[:, None] == blk[None, :], BF16),
            sgu_ln_w=sgu_ln_w[l][None], sgu_ln_b=sgu_ln_b[l][None],
            sgu_w=sgu_w[l].reshape(SGU_GROUPS * SGU_CHUNK, SGU_CHUNK).astype(BF16),
            sgu_b=jnp.repeat(sgu_b[l].T, SGU_GROUP_DIM, axis=1),
            att_out_norm_w=att_out_norm_w[l][None], sgu_out_norm_w=sgu_out_norm_w[l][None],
            w_out=w_out[l].astype(BF16), ffn_norm_w=ffn_norm_w[l][None],
            router_wt=router_w[l].T, router_b=router_b[l][:, None],
            tri=jnp.asarray(tri[:, None] < tri[None, :], BF16),
            w_gate=w_gate[l], b_gate=b_gate[l][:, None, :],
            w_up=w_up[l], b_up=b_up[l][:, None, :],
            w_down=w_down[l], b_down=b_down[l][:, None, :],
        )
        xa, xb = (_layer(x, dict(shared, rope=rope)) for x in (xa, xb))
    return xa, xb
```

```python
import functools

import numpy as np
import jax
import jax.numpy as jnp
from jax import lax
from jax.experimental import pallas as pl
from jax.experimental.pallas import tpu as pltpu
from jax.experimental.pallas import tpu_sc as plsc

F32 = jnp.float32
BF16 = jnp.bfloat16
I32 = jnp.int32

HEAD_DIM = 64
ATT_WINDOWS = ((128, 1), (512, 4), (2048, 16))
N_GROUPS = len(ATT_WINDOWS)
HEADS_PER_GROUP = 4
GROUP_WIDTH = HEADS_PER_GROUP * HEAD_DIM
ATT_WIDTH = N_GROUPS * GROUP_WIDTH
SGU_WIDTH = 256
SGU_GROUP_DIM = 64
SGU_GROUPS = SGU_WIDTH // SGU_GROUP_DIM
SGU_CHUNK = 128
ROT_DIM = HEAD_DIM // 4
ROPE_THETA = 500000.0
N_EXPERTS = 32
TOP_K = 4
SWIGLU_LIMIT = 7.0
SWIGLU_ALPHA = 1.702
NORM_EPS = 1e-6
LN_EPS = 1e-5
NEG_INF = -1e30

LANES = 128
SUBLANES = 8
MXU_DIM = 256
TOKEN_TILE = 512
ATT_Q_BLOCK = 512
ATT_SUB_Q = 128
MOE_BLOCK = 512
ROW_TILE = 512
SC_WINDOW = 64
VMEM_LIMIT = 48 * 1024 * 1024
EXPERTS_VMEM_LIMIT = 56 * 1024 * 1024


def _dot(a, b):
    return jnp.dot(a, b, preferred_element_type=F32)


def _dot_nt(a, b, precision=None):
    return lax.dot_general(a, b, (((1,), (1,)), ((), ())), precision=precision, preferred_element_type=F32)


def _rms(x, w):
    return x * lax.rsqrt(jnp.mean(x * x, axis=-1, keepdims=True) + NORM_EPS) * w


def _gelu(x):
    return 0.5 * x * (1.0 + lax.erf(x * np.float32(np.sqrt(0.5))))


def _pack_pairs(x):
    w = x.shape[1] // 2
    lo = lax.bitcast_convert_type(x[:, :w].astype(BF16).astype(F32), jnp.uint32) >> 16
    hi = lax.bitcast_convert_type(x[:, w:].astype(BF16).astype(F32), jnp.uint32) & jnp.uint32(0xFFFF0000)
    return lax.bitcast_convert_type(lo | hi, F32)


def _unpack_pairs(words):
    u = lax.bitcast_convert_type(words, jnp.uint32)
    lo = lax.bitcast_convert_type(u << 16, F32)
    hi = lax.bitcast_convert_type(u & jnp.uint32(0xFFFF0000), F32)
    return jnp.concatenate([lo, hi], axis=1)


def _to_tiled(ref, idx, value):
    rows = value.shape[0]
    for c in range(value.shape[1] // LANES):
        ref[idx + (slice(None), c)] = value[:, c * LANES:(c + 1) * LANES].reshape(rows // SUBLANES, SUBLANES, LANES)


def _from_tiled(ref, idx, row_tiles):
    chunks = ref.shape[-3]
    n = (row_tiles.stop - row_tiles.start) * SUBLANES
    return jnp.concatenate([ref[idx + (row_tiles, c)].reshape(n, LANES) for c in range(chunks)], axis=1)


def _in_proj_kernel(x_ref, anw_ref, w_ref, qnw_ref, knw_ref, ones_ref, cos_ref, sa_ref, sb_ref, lnw_ref, lnb_ref,
                    *refs):
    qkv_refs = refs[:3 * N_GROUPS]
    zu_ref, zv_ref, stage = refs[3 * N_GROUPS:]
    tm = x_ref.shape[0]
    h = _rms(x_ref[...], anw_ref[...]).astype(BF16)
    reps = ATT_WIDTH // LANES
    cos = jnp.concatenate([cos_ref[...]] * reps, axis=1)
    sa = jnp.concatenate([sa_ref[...]] * reps, axis=1)
    sb = jnp.concatenate([sb_ref[...]] * reps, axis=1)
    ones = ones_ref[...]

    def head_norm_rope(t, nw):
        sq = (t * t).astype(BF16)
        parts = []
        for j in range(ATT_WIDTH // MXU_DIM):
            sl = slice(j * MXU_DIM, (j + 1) * MXU_DIM)
            parts.append(_dot(sq[:, sl], ones))
        ssum = jnp.concatenate(parts, axis=1)
        t = t * lax.rsqrt(ssum * (1.0 / HEAD_DIM) + NORM_EPS) * nw
        half = ROT_DIM // 2
        return t * cos + pltpu.roll(t, ATT_WIDTH - half, 1) * sa + pltpu.roll(t, half, 1) * sb

    def emit(which, t):
        n_chunks = ATT_WIDTH // LANES
        per_group = GROUP_WIDTH // LANES
        for c in range(n_chunks):
            stage[which * n_chunks + c] = t[:, c * LANES:(c + 1) * LANES]
        for g, (_, dil) in enumerate(ATT_WINDOWS):
            out = qkv_refs[which * N_GROUPS + g]
            first = which * n_chunks + g * per_group
            for r in range(dil):
                rows = pl.ds(r, tm // dil, stride=dil)
                out[r] = jnp.concatenate([stage[first + c, rows, :] for c in range(per_group)], axis=1).astype(BF16)

    q = head_norm_rope(_dot(h, w_ref[:, 0:ATT_WIDTH]), qnw_ref[...])
    emit(0, q * (HEAD_DIM ** -0.5))
    emit(1, head_norm_rope(_dot(h, w_ref[:, ATT_WIDTH:2 * ATT_WIDTH]), knw_ref[...]))
    emit(2, _dot(h, w_ref[:, 2 * ATT_WIDTH:3 * ATT_WIDTH]))
    z = _dot(h, w_ref[:, 3 * ATT_WIDTH:3 * ATT_WIDTH + 2 * SGU_WIDTH])
    zu_ref[...] = _gelu(z[:, :SGU_WIDTH])
    gv = _gelu(z[:, SGU_WIDTH:])
    mu = jnp.mean(gv, axis=-1, keepdims=True)
    var = jnp.mean(jnp.square(gv - mu), axis=-1, keepdims=True)
    zv_ref[...] = ((gv - mu) * lax.rsqrt(var + LN_EPS) * lnw_ref[...] + lnb_ref[...]).astype(BF16)


def _phase_spec(tm, dil, n_seq_tiles):
    return pl.BlockSpec((None, dil, tm // dil, GROUP_WIDTH), lambda i: (i // n_seq_tiles, 0, i % n_seq_tiles, 0))


def _in_proj(x2, batch, seq, p):
    T, D = x2.shape
    tm = TOKEN_TILE
    n_seq_tiles = seq // tm
    in_width = p["w_in"].shape[1]
    const = lambda shape: pl.BlockSpec(shape, lambda i: (0,) * len(shape))
    rope = pl.BlockSpec((tm, LANES), lambda i: (i % n_seq_tiles, 0))
    row = lambda w: pl.BlockSpec((tm, w), lambda i: (i, 0))
    qkv_specs = [_phase_spec(tm, dil, n_seq_tiles) for _ in range(3) for _, dil in ATT_WINDOWS]
    qkv_shapes = [jax.ShapeDtypeStruct((batch, dil, seq // dil, GROUP_WIDTH), BF16)
                  for _ in range(3) for _, dil in ATT_WINDOWS]
    outs = pl.pallas_call(
        _in_proj_kernel,
        grid=(T // tm,),
        in_specs=[row(D), const((1, D)), const((D, in_width)), const((1, ATT_WIDTH)), const((1, ATT_WIDTH)),
                  const((MXU_DIM, MXU_DIM)), rope, rope, rope, const((1, SGU_WIDTH)), const((1, SGU_WIDTH))],
        out_specs=qkv_specs + [row(SGU_WIDTH), row(SGU_WIDTH)],
        out_shape=qkv_shapes + [jax.ShapeDtypeStruct((T, SGU_WIDTH), F32), jax.ShapeDtypeStruct((T, SGU_WIDTH), BF16)],
        scratch_shapes=[pltpu.VMEM((3 * ATT_WIDTH // LANES, tm, LANES), F32)],
        compiler_params=pltpu.CompilerParams(dimension_semantics=("arbitrary",), vmem_limit_bytes=VMEM_LIMIT),
        name="in_proj",
    )(x2, p["attn_norm_w"], p["w_in"], p["q_norm_w"], p["k_norm_w"], p["head_ones"],
      *p["rope"][seq], p["sgu_ln_w"], p["sgu_ln_b"])
    q, k, v = (outs[i * N_GROUPS:(i + 1) * N_GROUPS] for i in range(3))
    return q, k, v, outs[-2], outs[-1]


def _attn_kernel(q_ref, kp_ref, km_ref, kn_ref, vp_ref, vm_ref, vn_ref, o_ref, lse_ref, *, sub_len, steps):
    lq = q_ref.shape[0]
    sq = min(lq, ATT_SUB_Q)
    lk = sq + 2 * steps
    nh = HEADS_PER_GROUP
    lb = pl.program_id(2)
    kk = jnp.concatenate([kp_ref[...], km_ref[...], kn_ref[...]], axis=0)
    vv = jnp.concatenate([vp_ref[...], vm_ref[...], vn_ref[...]], axis=0)
    head = lax.broadcasted_iota(I32, (1, GROUP_WIDTH), 1) // HEAD_DIM
    hm_f = [(head == h).astype(F32) for h in range(nh)]
    hm_b = [m.astype(BF16) for m in hm_f]
    qi = lax.broadcasted_iota(I32, (sq, 1), 0)
    kj = lax.broadcasted_iota(I32, (sq, lk), 1)
    for j in range(lq // sq):
        first = lb * lq + j * sq
        lo = jnp.maximum(qi, steps - first)
        hi = jnp.minimum(qi + 2 * steps, sub_len - 1 + steps - first)
        mask = ((kj - lo).astype(jnp.uint32) <= (hi - lo).astype(jnp.uint32))[None]
        qj = q_ref[j * sq:(j + 1) * sq, :]
        qs = jnp.concatenate([qj * hm_b[h] for h in range(nh)], axis=0)
        s = _dot_nt(qs, kk[j * sq:j * sq + lk]).reshape(nh, sq, lk)
        s = jnp.where(mask, s, NEG_INF)
        m = jnp.max(s, axis=-1, keepdims=True)
        pr = jnp.exp(s - m)
        den = jnp.sum(pr, axis=-1, keepdims=True)
        oh = _dot(pr.reshape(nh * sq, lk).astype(BF16), vv[j * sq:j * sq + lk]).reshape(nh, sq, GROUP_WIDTH) / den
        lh = m + jnp.log(den)
        o, lse = oh[0], jnp.broadcast_to(lh[0], (sq, GROUP_WIDTH))
        for h in range(1, nh):
            o = jnp.where(head == h, oh[h], o)
            lse = jnp.where(head == h, lh[h], lse)
        o_ref[j * sq:(j + 1) * sq, :] = o
        lse_ref[j * sq:(j + 1) * sq, :] = lse


def _attention(q, k, v, group):
    window, dil = ATT_WINDOWS[group]
    steps = window // (2 * dil)
    batch, _, sub_len, _ = q.shape
    assert sub_len % steps == 0
    lq = min(ATT_Q_BLOCK, sub_len)
    assert lq % steps == 0 and sub_len % lq == 0
    per_q = lq // steps
    n_halo = sub_len // steps
    main = pl.BlockSpec((None, None, lq, GROUP_WIDTH), lambda b, r, i: (b, r, i, 0))
    prev = pl.BlockSpec((None, None, steps, GROUP_WIDTH), lambda b, r, i: (b, r, jnp.maximum(i * per_q - 1, 0), 0))
    nxt = pl.BlockSpec((None, None, steps, GROUP_WIDTH),
                       lambda b, r, i: (b, r, jnp.minimum((i + 1) * per_q, n_halo - 1), 0))
    return pl.pallas_call(
        functools.partial(_attn_kernel, sub_len=sub_len, steps=steps),
        grid=(batch, dil, sub_len // lq),
        in_specs=[main, prev, main, nxt, prev, main, nxt],
        out_specs=[main, main],
        out_shape=[jax.ShapeDtypeStruct(q.shape, F32)] * 2,
        compiler_params=pltpu.CompilerParams(dimension_semantics=("arbitrary",) * 3, vmem_limit_bytes=VMEM_LIMIT),
        name=f"attention_g{group}",
    )(q, k, k, k, v, v, v)


def _post_mix_kernel(x_ref, *refs):
    o_refs = refs[:N_GROUPS]
    l_refs = refs[N_GROUPS:2 * N_GROUPS]
    (zu_ref, zv_ref, sguw_ref, sgub_ref, aonw_ref, sonw_ref, wout_ref, fnw_ref, rwt_ref, rb_ref, tri_ref,
     x1_ref, h2_hbm, gates_ref, eidx_ref, rank_ref, cnt_ref, stage, hstage, hsem) = refs[2 * N_GROUPS:]
    tm = x_ref.shape[0]

    def token_order(ref, slot):
        dil = ref.shape[0]
        if dil == 1:
            return ref[0]
        per_group = GROUP_WIDTH // LANES
        for r in range(dil):
            for c in range(per_group):
                stage[slot * per_group + c, pl.ds(r, tm // dil, stride=dil), :] = ref[r, :, c * LANES:(c + 1) * LANES]
        return jnp.concatenate([stage[slot * per_group + c] for c in range(per_group)], axis=1)

    o = [token_order(ref, g) for g, ref in enumerate(o_refs)]
    l = [token_order(ref, N_GROUPS + g) for g, ref in enumerate(l_refs)]
    lmax = jnp.maximum(jnp.maximum(l[0], l[1]), l[2])
    e = [jnp.exp(lg - lmax) for lg in l]
    esum = e[0] + e[1] + e[2]
    att = (e[0] / esum) * o[0] + (e[1] / esum) * o[1] + (e[2] / esum) * o[2]
    att_n = _rms(att, aonw_ref[...]).astype(BF16)
    cgrp = lax.broadcasted_iota(I32, (1, SGU_WIDTH), 1) // SGU_GROUP_DIM
    sguw = sguw_ref[...]
    gates = []
    for c in range(tm // SGU_CHUNK):
        r = _dot(sguw, zv_ref[c * SGU_CHUNK:(c + 1) * SGU_CHUNK, :])
        g = sgub_ref[...]
        for grp in range(SGU_GROUPS):
            g = g + r[grp * SGU_CHUNK:(grp + 1) * SGU_CHUNK, :] * (cgrp == grp).astype(F32)
        gates.append(g)
    sgu = zu_ref[...] * jnp.concatenate(gates, axis=0)
    sgu_n = _rms(sgu, sonw_ref[...]).astype(BF16)
    x1 = x_ref[...] + _dot(att_n, wout_ref[0:GROUP_WIDTH, :]) + _dot(sgu_n, wout_ref[GROUP_WIDTH:, :])
    x1_ref[...] = x1
    h2 = _rms(x1, fnw_ref[...])
    _to_tiled(hstage, (), _pack_pairs(h2))
    bt = tm // SUBLANES
    h2_copies = [pltpu.make_async_copy(hstage.at[:, :, j, :], h2_hbm.at[pl.ds(pl.program_id(0) * bt, bt), j], hsem)
                 for j in range(SUBLANES)]
    for c in h2_copies:
        c.start()
    logits = _dot_nt(rwt_ref[...], h2.astype(BF16)) + rb_ref[...]
    eiota = lax.broadcasted_iota(I32, (N_EXPERTS, tm), 0)
    vals, idxs = [], []
    for _ in range(TOP_K):
        m = jnp.max(logits, axis=0, keepdims=True)
        idx = jnp.min(jnp.where(logits == m, eiota, N_EXPERTS), axis=0, keepdims=True)
        vals.append(m)
        idxs.append(idx)
        logits = jnp.where(eiota == idx, -jnp.inf, logits)
    exps = [jnp.exp(v - vals[0]) for v in vals]
    den = exps[0] + exps[1] + exps[2] + exps[3]
    gates_ref[...] = jnp.concatenate([ex / den for ex in exps], axis=0)
    eidx_ref[...] = jnp.concatenate(idxs, axis=0)
    onehots = [(eiota == idx).astype(F32) for idx in idxs]
    chosen = onehots[0] + onehots[1] + onehots[2] + onehots[3]
    before = _dot(chosen.astype(BF16), tri_ref[...])
    rank_ref[...] = jnp.concatenate(
        [jnp.sum(oh * before, axis=0, keepdims=True) for oh in onehots], axis=0).astype(I32)
    cnt_ref[...] = jnp.broadcast_to(jnp.sum(chosen, axis=1, keepdims=True), (N_EXPERTS, LANES)).astype(I32)
    for c in h2_copies:
        c.wait()


def _post_mix(x2, seq, o, lse, zu, zv, p):
    T, D = x2.shape
    tm = TOKEN_TILE
    n_tiles = T // tm
    n_seq_tiles = seq // tm
    chunks = D // (2 * LANES)
    const = lambda shape: pl.BlockSpec(shape, lambda i: (0,) * len(shape))
    row = lambda w: pl.BlockSpec((tm, w), lambda i: (i, 0))
    colt = pl.BlockSpec((TOP_K, tm), lambda i: (0, i))
    phase = [_phase_spec(tm, dil, n_seq_tiles) for _, dil in ATT_WINDOWS]
    return pl.pallas_call(
        _post_mix_kernel,
        grid=(n_tiles,),
        in_specs=[row(D)] + phase + phase + [row(SGU_WIDTH), row(SGU_WIDTH),
                  const((SGU_GROUPS * SGU_CHUNK, SGU_CHUNK)), const((SGU_CHUNK, SGU_WIDTH)),
                  const((1, GROUP_WIDTH)), const((1, SGU_WIDTH)), const((GROUP_WIDTH + SGU_WIDTH, D)),
                  const((1, D)), const((N_EXPERTS, D)), const((N_EXPERTS, 1)), const((tm, tm))],
        out_specs=[row(D), pl.BlockSpec(memory_space=pl.ANY),
                   colt, colt, colt, pl.BlockSpec((None, N_EXPERTS, LANES), lambda i: (i, 0, 0))],
        out_shape=[jax.ShapeDtypeStruct((T, D), F32),
                   jax.ShapeDtypeStruct((T // SUBLANES, SUBLANES, chunks, LANES), F32),
                   jax.ShapeDtypeStruct((TOP_K, T), F32), jax.ShapeDtypeStruct((TOP_K, T), I32),
                   jax.ShapeDtypeStruct((TOP_K, T), I32), jax.ShapeDtypeStruct((n_tiles, N_EXPERTS, LANES), I32)],
        scratch_shapes=[pltpu.VMEM((2 * N_GROUPS * GROUP_WIDTH // LANES, tm, LANES), F32),
                        pltpu.VMEM((tm // SUBLANES, chunks, SUBLANES, LANES), F32), pltpu.SemaphoreType.DMA],
        compiler_params=pltpu.CompilerParams(dimension_semantics=("arbitrary",), vmem_limit_bytes=VMEM_LIMIT),
        name="post_mix",
    )(x2, *o, *lse, zu, zv, p["sgu_w"], p["sgu_b"], p["att_out_norm_w"],
      p["sgu_out_norm_w"], p["w_out"], p["ffn_norm_w"], p["router_wt"], p["router_b"], p["tri"])


def _sc_workers():
    info = plsc.get_sparse_core_info()
    return info.num_cores, info.num_cores * info.num_subcores


def _dispatch_sc(h2p, pos, pad_pos, n_rows):
    T, chunks, _ = h2p.shape
    n_cores, n_workers = _sc_workers()
    win = SC_WINDOW
    per_worker = T // n_workers
    pad_per_worker = pad_pos.shape[0] // n_workers
    assert per_worker % win == 0 and pad_per_worker % win == 0
    mesh = plsc.VectorSubcoreMesh(core_axis_name="c", subcore_axis_name="s")

    @functools.partial(
        pl.kernel, mesh=mesh, out_type=jax.ShapeDtypeStruct((n_rows, chunks, LANES), F32),
        scratch_types=[pltpu.VMEM((win,), I32), pltpu.VMEM((win, chunks, LANES), F32)],
        name="dispatch_sc")
    def scatter_rows(h_hbm, pos_hbm, pad_hbm, zeros_hbm, xs_hbm, idx_v, rows_v):
        worker = lax.axis_index("s") * n_cores + lax.axis_index("c")

        @pl.loop(0, per_worker // win)
        def _(w):
            t0 = worker * per_worker + w * win
            pltpu.sync_copy(h_hbm.at[pl.ds(t0, win)], rows_v)
            for k in range(TOP_K):
                pltpu.sync_copy(pos_hbm.at[pl.ds(k * T + t0, win)], idx_v)
                pltpu.sync_copy(rows_v, xs_hbm.at[idx_v])

        pltpu.sync_copy(zeros_hbm, rows_v)

        @pl.loop(0, pad_per_worker // win)
        def _(w):
            pltpu.sync_copy(pad_hbm.at[pl.ds(worker * pad_per_worker + w * win, win)], idx_v)
            pltpu.sync_copy(rows_v, xs_hbm.at[idx_v])

    return scatter_rows(h2p, pos, pad_pos, jnp.zeros((win, chunks, LANES), F32))


def _gather_sc(ys, pos):
    _, chunks, _ = ys.shape
    n_assign = pos.shape[0]
    n_cores, n_workers = _sc_workers()
    win = SC_WINDOW
    per_worker = n_assign // n_workers
    assert per_worker % win == 0
    mesh = plsc.VectorSubcoreMesh(core_axis_name="c", subcore_axis_name="s")

    @functools.partial(
        pl.kernel, mesh=mesh, out_type=jax.ShapeDtypeStruct((n_assign, chunks, LANES), F32),
        scratch_types=[pltpu.VMEM((win,), I32), pltpu.VMEM((win, chunks, LANES), F32)],
        name="gather_sc")
    def gather_rows(ys_hbm, pos_hbm, out_hbm, idx_v, rows_v):
        worker = lax.axis_index("s") * n_cores + lax.axis_index("c")

        @pl.loop(0, per_worker // win)
        def _(w):
            a0 = worker * per_worker + w * win
            pltpu.sync_copy(pos_hbm.at[pl.ds(a0, win)], idx_v)
            pltpu.sync_copy(ys_hbm.at[idx_v], rows_v)
            pltpu.sync_copy(rows_v, out_hbm.at[pl.ds(a0, win)])

    return gather_rows(ys, pos)


def _experts_kernel(blk_e_ref, n_used_ref, xs_hbm, wg_ref, bg_ref, wu_ref, bu_ref, wd_ref, bd_ref, ys_hbm,
                    xbuf, obuf, wg_b, wu_b, wd_b, in_sem, out_sem):
    bt = xbuf.shape[1]
    i = pl.program_id(0)
    n = pl.num_programs(0)
    n_used = n_used_ref[0]
    slot = i % 2

    def in_copies(blk, s):
        return [pltpu.make_async_copy(xs_hbm.at[pl.ds(blk * bt, bt), j], xbuf.at[s, :, :, j, :], in_sem.at[s])
                for j in range(SUBLANES)]

    def out_copies(blk, s):
        return [pltpu.make_async_copy(obuf.at[s, :, :, j, :], ys_hbm.at[pl.ds(blk * bt, bt), j], out_sem.at[s])
                for j in range(SUBLANES)]

    @pl.when(i == 0)
    def _():
        for c in in_copies(0, 0):
            c.start()

    @pl.when(i + 1 < n_used)
    def _():
        for c in in_copies(i + 1, 1 - slot):
            c.start()

    @pl.when(i >= 2)
    def _():
        for c in out_copies(i - 2, slot):
            c.wait()

    @pl.when(i >= n_used)
    def _():
        obuf[slot] = jnp.zeros(obuf.shape[1:], F32)

    @pl.when(i < n_used)
    def _():
        for c in in_copies(i, slot):
            c.wait()
        @pl.when((i == 0) | (blk_e_ref[i] != blk_e_ref[jnp.maximum(i - 1, 0)]))
        def _():
            wg_b[...] = wg_ref[...].astype(BF16)
            wu_b[...] = wu_ref[...].astype(BF16)
            wd_b[...] = wd_ref[...].astype(BF16)

        x = _unpack_pairs(_from_tiled(xbuf, (slot,), slice(0, bt))).astype(BF16)
        g = _dot(x, wg_b[...]) + bg_ref[...]
        u = _dot(x, wu_b[...]) + bu_ref[...]
        g = jnp.minimum(g, SWIGLU_LIMIT)
        u = jnp.clip(u, -SWIGLU_LIMIT, SWIGLU_LIMIT)
        act = (u + 1.0) * (g * jax.nn.sigmoid(SWIGLU_ALPHA * g))
        _to_tiled(obuf, (slot,), _pack_pairs(_dot(act.astype(BF16), wd_b[...]) + bd_ref[...]))

    for c in out_copies(i, slot):
        c.start()

    @pl.when(i == n - 1)
    def _():
        for c in out_copies(i - 1, 1 - slot) + out_copies(i, slot):
            c.wait()


def _experts(xs, blk_e, n_used, p):
    n_rows, chunks, _ = xs.shape
    D = 2 * chunks * LANES
    bm = MOE_BLOCK
    d_ff = p["w_gate"].shape[2]
    assert n_rows // bm >= 2
    wspec = lambda a, b: pl.BlockSpec((None, a, b), lambda i, be, nu: (be[i], 0, 0))
    by_tile = (n_rows // SUBLANES, SUBLANES, chunks, LANES)
    block_buf = pltpu.VMEM((2, bm // SUBLANES, chunks, SUBLANES, LANES), F32)
    ys = pl.pallas_call(
        _experts_kernel,
        grid_spec=pltpu.PrefetchScalarGridSpec(
            num_scalar_prefetch=2,
            grid=(n_rows // bm,),
            in_specs=[pl.BlockSpec(memory_space=pl.ANY), wspec(D, d_ff), wspec(1, d_ff), wspec(D, d_ff),
                      wspec(1, d_ff), wspec(d_ff, D), wspec(1, D)],
            out_specs=pl.BlockSpec(memory_space=pl.ANY),
            scratch_shapes=[block_buf, block_buf, pltpu.VMEM((D, d_ff), BF16), pltpu.VMEM((D, d_ff), BF16),
                            pltpu.VMEM((d_ff, D), BF16), pltpu.SemaphoreType.DMA((2,)), pltpu.SemaphoreType.DMA((2,))],
        ),
        out_shape=jax.ShapeDtypeStruct(by_tile, F32),
        compiler_params=pltpu.CompilerParams(dimension_semantics=("arbitrary",), vmem_limit_bytes=EXPERTS_VMEM_LIMIT),
        name="experts",
    )(blk_e, n_used, xs.reshape(by_tile), p["w_gate"], p["b_gate"], p["w_up"], p["b_up"], p["w_down"], p["b_down"])
    return ys.reshape(n_rows, chunks, LANES)


def _combine_kernel(gates_ref, x1_ref, yg_hbm, out_ref, buf, sem):
    tm = x1_ref.shape[0]
    bt = tm // SUBLANES
    j = pl.program_id(0)
    n = pl.num_programs(0) - 1
    tiles_per_k = yg_hbm.shape[0] // TOP_K

    def copies(tile, s):
        return [pltpu.make_async_copy(yg_hbm.at[pl.ds(k * tiles_per_k + tile * bt, bt), r],
                                      buf.at[s, pl.ds(k * bt, bt), :, r, :], sem.at[s])
                for k in range(TOP_K) for r in range(SUBLANES)]

    @pl.when(j < n)
    def _():
        for c in copies(j, j % 2):
            c.start()

    @pl.when(j >= 1)
    def _():
        done = 1 - j % 2
        for c in copies(j - 1, done):
            c.wait()
        y = gates_ref[:, 0:1] * _unpack_pairs(_from_tiled(buf, (done,), slice(0, bt)))
        for k in range(1, TOP_K):
            y = y + gates_ref[:, k:k + 1] * _unpack_pairs(_from_tiled(buf, (done,), slice(k * bt, (k + 1) * bt)))
        out_ref[...] = x1_ref[...] + y


def _combine(x1, gates_tok, yg):
    T, D = x1.shape
    tm = ROW_TILE
    n = T // tm
    chunks = D // (2 * LANES)
    lag = lambda i: (jnp.maximum(i - 1, 0), 0)
    return pl.pallas_call(
        _combine_kernel,
        grid=(n + 1,),
        in_specs=[pl.BlockSpec((tm, TOP_K), lag), pl.BlockSpec((tm, D), lag), pl.BlockSpec(memory_space=pl.ANY)],
        out_specs=pl.BlockSpec((tm, D), lag),
        out_shape=jax.ShapeDtypeStruct((T, D), F32),
        scratch_shapes=[pltpu.VMEM((2, TOP_K * tm // SUBLANES, chunks, SUBLANES, LANES), F32),
                        pltpu.SemaphoreType.DMA((2,))],
        compiler_params=pltpu.CompilerParams(dimension_semantics=("arbitrary",), vmem_limit_bytes=VMEM_LIMIT),
        name="combine",
    )(gates_tok, x1, yg.reshape(TOP_K * T // SUBLANES, SUBLANES, chunks, LANES))


def _route(eidx, rank, tile_counts):
    T = eidx.shape[1]
    bm = MOE_BLOCK
    n_tiles = T // TOKEN_TILE
    tc = tile_counts[:, :, 0]
    counts = jnp.sum(tc, axis=0)
    tile_off = jnp.cumsum(tc, axis=0) - tc
    pcounts = (counts + bm - 1) // bm * bm
    pends = jnp.cumsum(pcounts)
    pstarts = pends - pcounts
    base = pstarts[None, :] + tile_off
    sel = eidx.reshape(TOP_K, n_tiles, 1, TOKEN_TILE) == jnp.arange(N_EXPERTS, dtype=I32)[None, None, :, None]
    pos = jnp.sum(jnp.where(sel, base[None, :, :, None], 0), axis=2).reshape(TOP_K, T) + rank
    n_blocks = (TOP_K * T) // bm + N_EXPERTS
    n_rows = n_blocks * bm
    blk_start = jnp.arange(n_blocks, dtype=I32) * bm
    blk_e = jnp.minimum(jnp.sum((pends[None, :] <= blk_start[:, None]).astype(I32), axis=1), N_EXPERTS - 1)
    n_used = (pends[-1] // bm).astype(I32).reshape(1)
    first_free = jnp.concatenate([pstarts + counts, pends[-1:]])
    free_off = jnp.cumsum(jnp.concatenate([jnp.zeros((1,), counts.dtype), pcounts - counts]))
    j = jnp.arange(n_rows - TOP_K * T, dtype=I32)[:, None]
    seg_id = jnp.sum((free_off[None, 1:] <= j).astype(I32), axis=1, keepdims=True)
    onehot = seg_id == jnp.arange(N_EXPERTS + 1, dtype=I32)[None, :]
    pad_pos = jnp.sum(jnp.where(onehot, (first_free - free_off)[None, :] + j, 0), axis=1).astype(I32)
    return pos.reshape(-1).astype(I32), pad_pos, blk_e.astype(I32), n_used, n_rows


def _layer(x, p):
    batch, seq, D = x.shape
    x2 = x.reshape(batch * seq, D)
    q, k, v, zu, zv = _in_proj(x2, batch, seq, p)
    o, lse = zip(*[_attention(q[g], k[g], v[g], g) for g in range(N_GROUPS)])
    x1, h2p, gates, eidx, rank, tile_counts = _post_mix(x2, seq, o, lse, zu, zv, p)
    pos, pad_pos, blk_e, n_used, n_rows = _route(eidx, rank, tile_counts)
    xs = _dispatch_sc(h2p.reshape(batch * seq, D // (2 * LANES), LANES), pos, pad_pos, n_rows)
    ys = _experts(xs, blk_e, n_used, p)
    out = _combine(x1, gates.T, _gather_sc(ys, pos))
    return out.reshape(batch, seq, D)


def _rope_tables(seq):
    half = ROT_DIM // 2
    inv_freq = jnp.power(ROPE_THETA, -2.0 * jnp.arange(half, dtype=F32) / ROT_DIM)
    ang = jnp.arange(seq, dtype=F32)[:, None] * inv_freq[None, :]
    cos, sin = jnp.cos(ang), jnp.sin(ang)
    zeros = jnp.zeros((seq, HEAD_DIM - ROT_DIM), F32)
    zh = jnp.zeros((seq, half), F32)
    cos_t = jnp.concatenate([cos, cos, zeros + 1.0], axis=1)
    sa = jnp.concatenate([-sin, zh, zeros], axis=1)
    sb = jnp.concatenate([zh, sin, zeros], axis=1)
    rep = LANES // HEAD_DIM
    return tuple(jnp.tile(t, (1, rep)) for t in (cos_t, sa, sb))


def kernel(x_prompt, x_sample, attn_norm_w, w_in, q_norm_w, k_norm_w, sgu_ln_w, sgu_ln_b, sgu_w, sgu_b,
           att_out_norm_w, sgu_out_norm_w, w_out, ffn_norm_w, router_w, router_b,
           w_gate, b_gate, w_up, b_up, w_down, b_down):
    depth = w_in.shape[0]
    n_heads = ATT_WIDTH // HEAD_DIM
    blk = np.arange(MXU_DIM) // HEAD_DIM
    tri = np.arange(TOKEN_TILE)
    xa, xb = x_prompt, x_sample
    rope = {x.shape[1]: _rope_tables(x.shape[1]) for x in (xa, xb)}
    for l in range(depth):
        shared = dict(
            attn_norm_w=attn_norm_w[l][None], w_in=w_in[l].astype(BF16),
            q_norm_w=jnp.tile(q_norm_w[l], n_heads)[None], k_norm_w=jnp.tile(k_norm_w[l], n_heads)[None],
            head_ones=jnp.asarray(blk[:, None] == blk[None, :], BF16),
            sgu_ln_w=sgu_ln_w[l][None], sgu_ln_b=sgu_ln_b[l][None],
            sgu_w=sgu_w[l].reshape(SGU_GROUPS * SGU_CHUNK, SGU_CHUNK).astype(BF16),
            sgu_b=jnp.repeat(sgu_b[l].T, SGU_GROUP_DIM, axis=1),
            att_out_norm_w=att_out_norm_w[l][None], sgu_out_norm_w=sgu_out_norm_w[l][None],
            w_out=w_out[l].astype(BF16), ffn_norm_w=ffn_norm_w[l][None],
            router_wt=router_w[l].T.astype(BF16), router_b=router_b[l][:, None],
            tri=jnp.asarray(tri[:, None] < tri[None, :], BF16),
            w_gate=w_gate[l], b_gate=b_gate[l][:, None, :],
            w_up=w_up[l], b_up=b_up[l][:, None, :],
            w_down=w_down[l], b_down=b_down[l][:, None, :],
        )
        xa, xb = (_layer(x, dict(shared, rope=rope)) for x in (xa, xb))
    return xa, xb
```

```python
import functools

import numpy as np
import jax
import jax.numpy as jnp
from jax import lax
from jax.experimental import pallas as pl
from jax.experimental.pallas import tpu as pltpu
from jax.experimental.pallas import tpu_sc as plsc

F32 = jnp.float32
BF16 = jnp.bfloat16
I32 = jnp.int32

HEAD_DIM = 64
ATT_WINDOWS = ((128, 1), (512, 4), (2048, 16))
N_GROUPS = len(ATT_WINDOWS)
HEADS_PER_GROUP = 4
GROUP_WIDTH = HEADS_PER_GROUP * HEAD_DIM
ATT_WIDTH = N_GROUPS * GROUP_WIDTH
SGU_WIDTH = 256
SGU_GROUP_DIM = 64
SGU_GROUPS = SGU_WIDTH // SGU_GROUP_DIM
SGU_CHUNK = 128
ROT_DIM = HEAD_DIM // 4
ROPE_THETA = 500000.0
N_EXPERTS = 32
TOP_K = 4
SWIGLU_LIMIT = 7.0
SWIGLU_ALPHA = 1.702
NORM_EPS = 1e-6
LN_EPS = 1e-5
NEG_INF = -1e30

LANES = 128
SUBLANES = 8
MXU_DIM = 256
TOKEN_TILE = 512
ATT_Q_BLOCK = 512
ATT_SUB_Q = 128
MOE_BLOCK = 512
ROW_TILE = 512
SC_WINDOW = 64
VMEM_LIMIT = 48 * 1024 * 1024
EXPERTS_VMEM_LIMIT = 56 * 1024 * 1024


def _dot(a, b):
    return jnp.dot(a, b, preferred_element_type=F32)


def _dot_nt(a, b, precision=None):
    return lax.dot_general(a, b, (((1,), (1,)), ((), ())), precision=precision, preferred_element_type=F32)


def _rms(x, w):
    return x * lax.rsqrt(jnp.mean(x * x, axis=-1, keepdims=True) + NORM_EPS) * w


def _gelu(x):
    return 0.5 * x * (1.0 + lax.erf(x * np.float32(np.sqrt(0.5))))


def _pack_pairs(x):
    w = x.shape[1] // 2
    lo = lax.bitcast_convert_type(x[:, :w].astype(BF16).astype(F32), jnp.uint32) >> 16
    hi = lax.bitcast_convert_type(x[:, w:].astype(BF16).astype(F32), jnp.uint32) & jnp.uint32(0xFFFF0000)
    return lax.bitcast_convert_type(lo | hi, F32)


def _unpack_pairs(words):
    u = lax.bitcast_convert_type(words, jnp.uint32)
    lo = lax.bitcast_convert_type(u << 16, F32)
    hi = lax.bitcast_convert_type(u & jnp.uint32(0xFFFF0000), F32)
    return jnp.concatenate([lo, hi], axis=1)


def _to_tiled(ref, idx, value):
    rows = value.shape[0]
    for c in range(value.shape[1] // LANES):
        ref[idx + (slice(None), c)] = value[:, c * LANES:(c + 1) * LANES].reshape(rows // SUBLANES, SUBLANES, LANES)


def _from_tiled(ref, idx, row_tiles):
    chunks = ref.shape[-3]
    n = (row_tiles.stop - row_tiles.start) * SUBLANES
    return jnp.concatenate([ref[idx + (row_tiles, c)].reshape(n, LANES) for c in range(chunks)], axis=1)


def _in_proj_kernel(x_ref, anw_ref, w_ref, qnw_ref, knw_ref, ones_ref, cos_ref, sa_ref, sb_ref, lnw_ref, lnb_ref,
                    *refs):
    qkv_refs = refs[:3 * N_GROUPS]
    zu_ref, zv_ref, stage = refs[3 * N_GROUPS:]
    tm = x_ref.shape[0]
    h = _rms(x_ref[...], anw_ref[...]).astype(BF16)
    reps = ATT_WIDTH // LANES
    cos = jnp.concatenate([cos_ref[...]] * reps, axis=1)
    sa = jnp.concatenate([sa_ref[...]] * reps, axis=1)
    sb = jnp.concatenate([sb_ref[...]] * reps, axis=1)
    ones = ones_ref[...]

    def head_norm_rope(t, nw):
        sq = (t * t).astype(BF16)
        parts = []
        for j in range(ATT_WIDTH // MXU_DIM):
            sl = slice(j * MXU_DIM, (j + 1) * MXU_DIM)
            parts.append(_dot(sq[:, sl], ones))
        ssum = jnp.concatenate(parts, axis=1)
        t = t * lax.rsqrt(ssum * (1.0 / HEAD_DIM) + NORM_EPS) * nw
        half = ROT_DIM // 2
        return t * cos + pltpu.roll(t, ATT_WIDTH - half, 1) * sa + pltpu.roll(t, half, 1) * sb

    def emit(which, t):
        n_chunks = ATT_WIDTH // LANES
        per_group = GROUP_WIDTH // LANES
        for c in range(n_chunks):
            stage[which * n_chunks + c] = t[:, c * LANES:(c + 1) * LANES]
        for g, (_, dil) in enumerate(ATT_WINDOWS):
            out = qkv_refs[which * N_GROUPS + g]
            first = which * n_chunks + g * per_group
            for r in range(dil):
                rows = pl.ds(r, tm // dil, stride=dil)
                out[r] = jnp.concatenate([stage[first + c, rows, :] for c in range(per_group)], axis=1).astype(BF16)

    q = head_norm_rope(_dot(h, w_ref[:, 0:ATT_WIDTH]), qnw_ref[...])
    emit(0, q * (HEAD_DIM ** -0.5))
    emit(1, head_norm_rope(_dot(h, w_ref[:, ATT_WIDTH:2 * ATT_WIDTH]), knw_ref[...]))
    emit(2, _dot(h, w_ref[:, 2 * ATT_WIDTH:3 * ATT_WIDTH]))
    z = _dot(h, w_ref[:, 3 * ATT_WIDTH:3 * ATT_WIDTH + 2 * SGU_WIDTH])
    zu_ref[...] = _gelu(z[:, :SGU_WIDTH])
    gv = _gelu(z[:, SGU_WIDTH:])
    mu = jnp.mean(gv, axis=-1, keepdims=True)
    var = jnp.mean(jnp.square(gv - mu), axis=-1, keepdims=True)
    zv_ref[...] = ((gv - mu) * lax.rsqrt(var + LN_EPS) * lnw_ref[...] + lnb_ref[...]).astype(BF16)


def _phase_spec(tm, dil, n_seq_tiles):
    return pl.BlockSpec((None, dil, tm // dil, GROUP_WIDTH), lambda i: (i // n_seq_tiles, 0, i % n_seq_tiles, 0))


def _in_proj(x2, batch, seq, p):
    T, D = x2.shape
    tm = TOKEN_TILE
    n_seq_tiles = seq // tm
    in_width = p["w_in"].shape[1]
    const = lambda shape: pl.BlockSpec(shape, lambda i: (0,) * len(shape))
    rope = pl.BlockSpec((tm, LANES), lambda i: (i % n_seq_tiles, 0))
    row = lambda w: pl.BlockSpec((tm, w), lambda i: (i, 0))
    qkv_specs = [_phase_spec(tm, dil, n_seq_tiles) for _ in range(3) for _, dil in ATT_WINDOWS]
    qkv_shapes = [jax.ShapeDtypeStruct((batch, dil, seq // dil, GROUP_WIDTH), BF16)
                  for _ in range(3) for _, dil in ATT_WINDOWS]
    outs = pl.pallas_call(
        _in_proj_kernel,
        grid=(T // tm,),
        in_specs=[row(D), const((1, D)), const((D, in_width)), const((1, ATT_WIDTH)), const((1, ATT_WIDTH)),
                  const((MXU_DIM, MXU_DIM)), rope, rope, rope, const((1, SGU_WIDTH)), const((1, SGU_WIDTH))],
        out_specs=qkv_specs + [row(SGU_WIDTH), row(SGU_WIDTH)],
        out_shape=qkv_shapes + [jax.ShapeDtypeStruct((T, SGU_WIDTH), F32), jax.ShapeDtypeStruct((T, SGU_WIDTH), BF16)],
        scratch_shapes=[pltpu.VMEM((3 * ATT_WIDTH // LANES, tm, LANES), F32)],
        compiler_params=pltpu.CompilerParams(dimension_semantics=("arbitrary",), vmem_limit_bytes=VMEM_LIMIT),
        name="in_proj",
    )(x2, p["attn_norm_w"], p["w_in"], p["q_norm_w"], p["k_norm_w"], p["head_ones"],
      *p["rope"][seq], p["sgu_ln_w"], p["sgu_ln_b"])
    q, k, v = (outs[i * N_GROUPS:(i + 1) * N_GROUPS] for i in range(3))
    return q, k, v, outs[-2], outs[-1]


def _attn_kernel(q_ref, kp_ref, km_ref, kn_ref, vp_ref, vm_ref, vn_ref, o_ref, lse_ref, *, sub_len, steps):
    lq = q_ref.shape[0]
    sq = min(lq, ATT_SUB_Q)
    lk = sq + 2 * steps
    nh = HEADS_PER_GROUP
    lb = pl.program_id(2)
    kk = jnp.concatenate([kp_ref[...], km_ref[...], kn_ref[...]], axis=0)
    vv = jnp.concatenate([vp_ref[...], vm_ref[...], vn_ref[...]], axis=0)
    head = lax.broadcasted_iota(I32, (1, GROUP_WIDTH), 1) // HEAD_DIM
    hm_f = [(head == h).astype(F32) for h in range(nh)]
    hm_b = [m.astype(BF16) for m in hm_f]
    qi = lax.broadcasted_iota(I32, (sq, 1), 0)
    kj = lax.broadcasted_iota(I32, (sq, lk), 1)
    for j in range(lq // sq):
        first = lb * lq + j * sq
        lo = jnp.maximum(qi, steps - first)
        hi = jnp.minimum(qi + 2 * steps, sub_len - 1 + steps - first)
        mask = ((kj - lo).astype(jnp.uint32) <= (hi - lo).astype(jnp.uint32))[None]
        qj = q_ref[j * sq:(j + 1) * sq, :]
        qs = jnp.concatenate([qj * hm_b[h] for h in range(nh)], axis=0)
        s = _dot_nt(qs, kk[j * sq:j * sq + lk]).reshape(nh, sq, lk)
        s = jnp.where(mask, s, NEG_INF)
        m = jnp.max(s, axis=-1, keepdims=True)
        pr = jnp.exp(s - m)
        den = jnp.sum(pr, axis=-1, keepdims=True)
        oh = _dot(pr.reshape(nh * sq, lk).astype(BF16), vv[j * sq:j * sq + lk]).reshape(nh, sq, GROUP_WIDTH) / den
        lh = m + jnp.log(den)
        o, lse = oh[0], jnp.broadcast_to(lh[0], (sq, GROUP_WIDTH))
        for h in range(1, nh):
            o = jnp.where(head == h, oh[h], o)
            lse = jnp.where(head == h, lh[h], lse)
        o_ref[j * sq:(j + 1) * sq, :] = o
        lse_ref[j * sq:(j + 1) * sq, :] = lse


def _attention(q, k, v, group):
    window, dil = ATT_WINDOWS[group]
    steps = window // (2 * dil)
    batch, _, sub_len, _ = q.shape
    assert sub_len % steps == 0
    lq = min(ATT_Q_BLOCK, sub_len)
    assert lq % steps == 0 and sub_len % lq == 0
    per_q = lq // steps
    n_halo = sub_len // steps
    main = pl.BlockSpec((None, None, lq, GROUP_WIDTH), lambda b, r, i: (b, r, i, 0))
    prev = pl.BlockSpec((None, None, steps, GROUP_WIDTH), lambda b, r, i: (b, r, jnp.maximum(i * per_q - 1, 0), 0))
    nxt = pl.BlockSpec((None, None, steps, GROUP_WIDTH),
                       lambda b, r, i: (b, r, jnp.minimum((i + 1) * per_q, n_halo - 1), 0))
    return pl.pallas_call(
        functools.partial(_attn_kernel, sub_len=sub_len, steps=steps),
        grid=(batch, dil, sub_len // lq),
        in_specs=[main, prev, main, nxt, prev, main, nxt],
        out_specs=[main, main],
        out_shape=[jax.ShapeDtypeStruct(q.shape, F32)] * 2,
        compiler_params=pltpu.CompilerParams(dimension_semantics=("arbitrary",) * 3, vmem_limit_bytes=VMEM_LIMIT),
        name=f"attention_g{group}",
    )(q, k, k, k, v, v, v)


def _post_mix_kernel(x_ref, *refs):
    o_refs = refs[:N_GROUPS]
    l_refs = refs[N_GROUPS:2 * N_GROUPS]
    (zu_ref, zv_ref, sguw_ref, sgub_ref, aonw_ref, sonw_ref, wout_ref, fnw_ref, rwt_ref, rb_ref, tri_ref,
     x1_ref, h2_hbm, gates_ref, eidx_ref, rank_ref, cnt_ref, stage, hstage, hsem) = refs[2 * N_GROUPS:]
    tm = x_ref.shape[0]

    def token_order(ref, slot):
        dil = ref.shape[0]
        if dil == 1:
            return ref[0]
        per_group = GROUP_WIDTH // LANES
        for r in range(dil):
            for c in range(per_group):
                stage[slot * per_group + c, pl.ds(r, tm // dil, stride=dil), :] = ref[r, :, c * LANES:(c + 1) * LANES]
        return jnp.concatenate([stage[slot * per_group + c] for c in range(per_group)], axis=1)

    o = [token_order(ref, g) for g, ref in enumerate(o_refs)]
    l = [token_order(ref, N_GROUPS + g) for g, ref in enumerate(l_refs)]
    lmax = jnp.maximum(jnp.maximum(l[0], l[1]), l[2])
    e = [jnp.exp(lg - lmax) for lg in l]
    esum = e[0] + e[1] + e[2]
    att = (e[0] / esum) * o[0] + (e[1] / esum) * o[1] + (e[2] / esum) * o[2]
    att_n = _rms(att, aonw_ref[...]).astype(BF16)
    cgrp = lax.broadcasted_iota(I32, (1, SGU_WIDTH), 1) // SGU_GROUP_DIM
    sguw = sguw_ref[...]
    gates = []
    for c in range(tm // SGU_CHUNK):
        r = _dot(sguw, zv_ref[c * SGU_CHUNK:(c + 1) * SGU_CHUNK, :])
        g = sgub_ref[...]
        for grp in range(SGU_GROUPS):
            g = g + r[grp * SGU_CHUNK:(grp + 1) * SGU_CHUNK, :] * (cgrp == grp).astype(F32)
        gates.append(g)
    sgu = zu_ref[...] * jnp.concatenate(gates, axis=0)
    sgu_n = _rms(sgu, sonw_ref[...]).astype(BF16)
    x1 = x_ref[...] + _dot(att_n, wout_ref[0:GROUP_WIDTH, :]) + _dot(sgu_n, wout_ref[GROUP_WIDTH:, :])
    x1_ref[...] = x1
    h2 = _rms(x1, fnw_ref[...])
    step = pl.program_id(0)
    slot = step % 2
    bt = tm // SUBLANES

    def h2_copies(tile, s):
        return [pltpu.make_async_copy(hstage.at[s, :, :, j, :], h2_hbm.at[pl.ds(tile * bt, bt), j], hsem.at[s])
                for j in range(SUBLANES)]

    _to_tiled(hstage, (slot,), _pack_pairs(h2))
    for c in h2_copies(step, slot):
        c.start()

    @pl.when(step > 0)
    def _():
        for c in h2_copies(step - 1, 1 - slot):
            c.wait()
    logits = _dot_nt(rwt_ref[...], h2.astype(BF16)) + rb_ref[...]
    eiota = lax.broadcasted_iota(I32, (N_EXPERTS, tm), 0)
    vals, idxs = [], []
    for _ in range(TOP_K):
        m = jnp.max(logits, axis=0, keepdims=True)
        idx = jnp.min(jnp.where(logits == m, eiota, N_EXPERTS), axis=0, keepdims=True)
        vals.append(m)
        idxs.append(idx)
        logits = jnp.where(eiota == idx, -jnp.inf, logits)
    exps = [jnp.exp(v - vals[0]) for v in vals]
    den = exps[0] + exps[1] + exps[2] + exps[3]
    gates_ref[...] = jnp.concatenate([ex / den for ex in exps], axis=0)
    eidx_ref[...] = jnp.concatenate(idxs, axis=0)
    onehots = [(eiota == idx).astype(F32) for idx in idxs]
    chosen = onehots[0] + onehots[1] + onehots[2] + onehots[3]
    before = _dot(chosen.astype(BF16), tri_ref[...])
    rank_ref[...] = jnp.concatenate(
        [jnp.sum(oh * before, axis=0, keepdims=True) for oh in onehots], axis=0).astype(I32)
    cnt_ref[...] = jnp.broadcast_to(jnp.sum(chosen, axis=1, keepdims=True), (N_EXPERTS, LANES)).astype(I32)

    @pl.when(step == pl.num_programs(0) - 1)
    def _():
        for c in h2_copies(step, slot):
            c.wait()


def _post_mix(x2, seq, o, lse, zu, zv, p):
    T, D = x2.shape
    tm = TOKEN_TILE
    n_tiles = T // tm
    n_seq_tiles = seq // tm
    chunks = D // (2 * LANES)
    const = lambda shape: pl.BlockSpec(shape, lambda i: (0,) * len(shape))
    row = lambda w: pl.BlockSpec((tm, w), lambda i: (i, 0))
    colt = pl.BlockSpec((TOP_K, tm), lambda i: (0, i))
    phase = [_phase_spec(tm, dil, n_seq_tiles) for _, dil in ATT_WINDOWS]
    return pl.pallas_call(
        _post_mix_kernel,
        grid=(n_tiles,),
        in_specs=[row(D)] + phase + phase + [row(SGU_WIDTH), row(SGU_WIDTH),
                  const((SGU_GROUPS * SGU_CHUNK, SGU_CHUNK)), const((SGU_CHUNK, SGU_WIDTH)),
                  const((1, GROUP_WIDTH)), const((1, SGU_WIDTH)), const((GROUP_WIDTH + SGU_WIDTH, D)),
                  const((1, D)), const((N_EXPERTS, D)), const((N_EXPERTS, 1)), const((tm, tm))],
        out_specs=[row(D), pl.BlockSpec(memory_space=pl.ANY),
                   colt, colt, colt, pl.BlockSpec((None, N_EXPERTS, LANES), lambda i: (i, 0, 0))],
        out_shape=[jax.ShapeDtypeStruct((T, D), F32),
                   jax.ShapeDtypeStruct((T // SUBLANES, SUBLANES, chunks, LANES), F32),
                   jax.ShapeDtypeStruct((TOP_K, T), F32), jax.ShapeDtypeStruct((TOP_K, T), I32),
                   jax.ShapeDtypeStruct((TOP_K, T), I32), jax.ShapeDtypeStruct((n_tiles, N_EXPERTS, LANES), I32)],
        scratch_shapes=[pltpu.VMEM((2 * N_GROUPS * GROUP_WIDTH // LANES, tm, LANES), F32),
                        pltpu.VMEM((2, tm // SUBLANES, chunks, SUBLANES, LANES), F32), pltpu.SemaphoreType.DMA((2,))],
        compiler_params=pltpu.CompilerParams(dimension_semantics=("arbitrary",), vmem_limit_bytes=VMEM_LIMIT),
        name="post_mix",
    )(x2, *o, *lse, zu, zv, p["sgu_w"], p["sgu_b"], p["att_out_norm_w"],
      p["sgu_out_norm_w"], p["w_out"], p["ffn_norm_w"], p["router_wt"], p["router_b"], p["tri"])


def _sc_workers():
    info = plsc.get_sparse_core_info()
    return info.num_cores, info.num_cores * info.num_subcores


def _dispatch_sc(h2p, pos, pad_pos, n_rows):
    T, chunks, _ = h2p.shape
    n_cores, n_workers = _sc_workers()
    win = SC_WINDOW
    per_worker = T // n_workers
    pad_per_worker = pad_pos.shape[0] // n_workers
    assert per_worker % win == 0 and pad_per_worker % win == 0
    mesh = plsc.VectorSubcoreMesh(core_axis_name="c", subcore_axis_name="s")

    @functools.partial(
        pl.kernel, mesh=mesh, out_type=jax.ShapeDtypeStruct((n_rows, chunks, LANES), F32),
        scratch_types=[pltpu.VMEM((win,), I32), pltpu.VMEM((win, chunks, LANES), F32)],
        name="dispatch_sc")
    def scatter_rows(h_hbm, pos_hbm, pad_hbm, zeros_hbm, xs_hbm, idx_v, rows_v):
        worker = lax.axis_index("s") * n_cores + lax.axis_index("c")

        @pl.loop(0, per_worker // win)
        def _(w):
            t0 = worker * per_worker + w * win
            pltpu.sync_copy(h_hbm.at[pl.ds(t0, win)], rows_v)
            for k in range(TOP_K):
                pltpu.sync_copy(pos_hbm.at[pl.ds(k * T + t0, win)], idx_v)
                pltpu.sync_copy(rows_v, xs_hbm.at[idx_v])

        pltpu.sync_copy(zeros_hbm, rows_v)

        @pl.loop(0, pad_per_worker // win)
        def _(w):
            pltpu.sync_copy(pad_hbm.at[pl.ds(worker * pad_per_worker + w * win, win)], idx_v)
            pltpu.sync_copy(rows_v, xs_hbm.at[idx_v])

    return scatter_rows(h2p, pos, pad_pos, jnp.zeros((win, chunks, LANES), F32))


def _gather_sc(ys, pos):
    _, chunks, _ = ys.shape
    n_assign = pos.shape[0]
    n_cores, n_workers = _sc_workers()
    win = SC_WINDOW
    per_worker = n_assign // n_workers
    assert per_worker % win == 0
    mesh = plsc.VectorSubcoreMesh(core_axis_name="c", subcore_axis_name="s")

    @functools.partial(
        pl.kernel, mesh=mesh, out_type=jax.ShapeDtypeStruct((n_assign, chunks, LANES), F32),
        scratch_types=[pltpu.VMEM((win,), I32), pltpu.VMEM((win, chunks, LANES), F32)],
        name="gather_sc")
    def gather_rows(ys_hbm, pos_hbm, out_hbm, idx_v, rows_v):
        worker = lax.axis_index("s") * n_cores + lax.axis_index("c")

        @pl.loop(0, per_worker // win)
        def _(w):
            a0 = worker * per_worker + w * win
            pltpu.sync_copy(pos_hbm.at[pl.ds(a0, win)], idx_v)
            pltpu.sync_copy(ys_hbm.at[idx_v], rows_v)
            pltpu.sync_copy(rows_v, out_hbm.at[pl.ds(a0, win)])

    return gather_rows(ys, pos)


def _experts_kernel(blk_e_ref, n_used_ref, xs_hbm, wg_ref, bg_ref, wu_ref, bu_ref, wd_ref, bd_ref, ys_hbm,
                    xbuf, obuf, wg_b, wu_b, wd_b, in_sem, out_sem):
    bt = xbuf.shape[1]
    i = pl.program_id(0)
    n = pl.num_programs(0)
    n_used = n_used_ref[0]
    slot = i % 2

    def in_copies(blk, s):
        return [pltpu.make_async_copy(xs_hbm.at[pl.ds(blk * bt, bt), j], xbuf.at[s, :, :, j, :], in_sem.at[s])
                for j in range(SUBLANES)]

    def out_copies(blk, s):
        return [pltpu.make_async_copy(obuf.at[s, :, :, j, :], ys_hbm.at[pl.ds(blk * bt, bt), j], out_sem.at[s])
                for j in range(SUBLANES)]

    @pl.when(i == 0)
    def _():
        for c in in_copies(0, 0):
            c.start()

    @pl.when(i + 1 < n_used)
    def _():
        for c in in_copies(i + 1, 1 - slot):
            c.start()

    @pl.when(i >= 2)
    def _():
        for c in out_copies(i - 2, slot):
            c.wait()

    @pl.when(i >= n_used)
    def _():
        obuf[slot] = jnp.zeros(obuf.shape[1:], F32)

    @pl.when(i < n_used)
    def _():
        for c in in_copies(i, slot):
            c.wait()
        @pl.when((i == 0) | (blk_e_ref[i] != blk_e_ref[jnp.maximum(i - 1, 0)]))
        def _():
            wg_b[...] = wg_ref[...].astype(BF16)
            wu_b[...] = wu_ref[...].astype(BF16)
            wd_b[...] = wd_ref[...].astype(BF16)

        x = _unpack_pairs(_from_tiled(xbuf, (slot,), slice(0, bt))).astype(BF16)
        g = _dot(x, wg_b[...]) + bg_ref[...]
        u = _dot(x, wu_b[...]) + bu_ref[...]
        g = jnp.minimum(g, SWIGLU_LIMIT)
        u = jnp.clip(u, -SWIGLU_LIMIT, SWIGLU_LIMIT)
        act = (u + 1.0) * (g * jax.nn.sigmoid(SWIGLU_ALPHA * g))
        _to_tiled(obuf, (slot,), _pack_pairs(_dot(act.astype(BF16), wd_b[...]) + bd_ref[...]))

    for c in out_copies(i, slot):
        c.start()

    @pl.when(i == n - 1)
    def _():
        for c in out_copies(i - 1, 1 - slot) + out_copies(i, slot):
            c.wait()


def _experts(xs, blk_e, n_used, p):
    n_rows, chunks, _ = xs.shape
    D = 2 * chunks * LANES
    bm = MOE_BLOCK
    d_ff = p["w_gate"].shape[2]
    assert n_rows // bm >= 2
    wspec = lambda a, b: pl.BlockSpec((None, a, b), lambda i, be, nu: (be[i], 0, 0))
    by_tile = (n_rows // SUBLANES, SUBLANES, chunks, LANES)
    block_buf = pltpu.VMEM((2, bm // SUBLANES, chunks, SUBLANES, LANES), F32)
    ys = pl.pallas_call(
        _experts_kernel,
        grid_spec=pltpu.PrefetchScalarGridSpec(
            num_scalar_prefetch=2,
            grid=(n_rows // bm,),
            in_specs=[pl.BlockSpec(memory_space=pl.ANY), wspec(D, d_ff), wspec(1, d_ff), wspec(D, d_ff),
                      wspec(1, d_ff), wspec(d_ff, D), wspec(1, D)],
            out_specs=pl.BlockSpec(memory_space=pl.ANY),
            scratch_shapes=[block_buf, block_buf, pltpu.VMEM((D, d_ff), BF16), pltpu.VMEM((D, d_ff), BF16),
                            pltpu.VMEM((d_ff, D), BF16), pltpu.SemaphoreType.DMA((2,)), pltpu.SemaphoreType.DMA((2,))],
        ),
        out_shape=jax.ShapeDtypeStruct(by_tile, F32),
        compiler_params=pltpu.CompilerParams(dimension_semantics=("arbitrary",), vmem_limit_bytes=EXPERTS_VMEM_LIMIT),
        name="experts",
    )(blk_e, n_used, xs.reshape(by_tile), p["w_gate"], p["b_gate"], p["w_up"], p["b_up"], p["w_down"], p["b_down"])
    return ys.reshape(n_rows, chunks, LANES)


def _combine_kernel(gates_ref, x1_ref, yg_hbm, out_ref, buf, sem):
    tm = x1_ref.shape[0]
    bt = tm // SUBLANES
    j = pl.program_id(0)
    n = pl.num_programs(0) - 1
    tiles_per_k = yg_hbm.shape[0] // TOP_K

    def copies(tile, s):
        return [pltpu.make_async_copy(yg_hbm.at[pl.ds(k * tiles_per_k + tile * bt, bt), r],
                                      buf.at[s, pl.ds(k * bt, bt), :, r, :], sem.at[s])
                for k in range(TOP_K) for r in range(SUBLANES)]

    @pl.when(j < n)
    def _():
        for c in copies(j, j % 2):
            c.start()

    @pl.when(j >= 1)
    def _():
        done = 1 - j % 2
        for c in copies(j - 1, done):
            c.wait()
        y = gates_ref[:, 0:1] * _unpack_pairs(_from_tiled(buf, (done,), slice(0, bt)))
        for k in range(1, TOP_K):
            y = y + gates_ref[:, k:k + 1] * _unpack_pairs(_from_tiled(buf, (done,), slice(k * bt, (k + 1) * bt)))
        out_ref[...] = x1_ref[...] + y


def _combine(x1, gates_tok, yg):
    T, D = x1.shape
    tm = ROW_TILE
    n = T // tm
    chunks = D // (2 * LANES)
    lag = lambda i: (jnp.maximum(i - 1, 0), 0)
    return pl.pallas_call(
        _combine_kernel,
        grid=(n + 1,),
        in_specs=[pl.BlockSpec((tm, TOP_K), lag), pl.BlockSpec((tm, D), lag), pl.BlockSpec(memory_space=pl.ANY)],
        out_specs=pl.BlockSpec((tm, D), lag),
        out_shape=jax.ShapeDtypeStruct((T, D), F32),
        scratch_shapes=[pltpu.VMEM((2, TOP_K * tm // SUBLANES, chunks, SUBLANES, LANES), F32),
                        pltpu.SemaphoreType.DMA((2,))],
        compiler_params=pltpu.CompilerParams(dimension_semantics=("arbitrary",), vmem_limit_bytes=VMEM_LIMIT),
        name="combine",
    )(gates_tok, x1, yg.reshape(TOP_K * T // SUBLANES, SUBLANES, chunks, LANES))


def _route(eidx, rank, tile_counts):
    T = eidx.shape[1]
    bm = MOE_BLOCK
    n_tiles = T // TOKEN_TILE
    tc = tile_counts[:, :, 0]
    counts = jnp.sum(tc, axis=0)
    tile_off = jnp.cumsum(tc, axis=0) - tc
    pcounts = (counts + bm - 1) // bm * bm
    pends = jnp.cumsum(pcounts)
    pstarts = pends - pcounts
    base = pstarts[None, :] + tile_off
    sel = eidx.reshape(TOP_K, n_tiles, 1, TOKEN_TILE) == jnp.arange(N_EXPERTS, dtype=I32)[None, None, :, None]
    pos = jnp.sum(jnp.where(sel, base[None, :, :, None], 0), axis=2).reshape(TOP_K, T) + rank
    n_blocks = (TOP_K * T) // bm + N_EXPERTS
    n_rows = n_blocks * bm
    blk_start = jnp.arange(n_blocks, dtype=I32) * bm
    blk_e = jnp.minimum(jnp.sum((pends[None, :] <= blk_start[:, None]).astype(I32), axis=1), N_EXPERTS - 1)
    n_used = (pends[-1] // bm).astype(I32).reshape(1)
    first_free = jnp.concatenate([pstarts + counts, pends[-1:]])
    free_off = jnp.cumsum(jnp.concatenate([jnp.zeros((1,), counts.dtype), pcounts - counts]))
    j = jnp.arange(n_rows - TOP_K * T, dtype=I32)[:, None]
    seg_id = jnp.sum((free_off[None, 1:] <= j).astype(I32), axis=1, keepdims=True)
    onehot = seg_id == jnp.arange(N_EXPERTS + 1, dtype=I32)[None, :]
    pad_pos = jnp.sum(jnp.where(onehot, (first_free - free_off)[None, :] + j, 0), axis=1).astype(I32)
    return pos.reshape(-1).astype(I32), pad_pos, blk_e.astype(I32), n_used, n_rows


def _layer(x, p):
    batch, seq, D = x.shape
    x2 = x.reshape(batch * seq, D)
    q, k, v, zu, zv = _in_proj(x2, batch, seq, p)
    o, lse = zip(*[_attention(q[g], k[g], v[g], g) for g in range(N_GROUPS)])
    x1, h2p, gates, eidx, rank, tile_counts = _post_mix(x2, seq, o, lse, zu, zv, p)
    pos, pad_pos, blk_e, n_used, n_rows = _route(eidx, rank, tile_counts)
    xs = _dispatch_sc(h2p.reshape(batch * seq, D // (2 * LANES), LANES), pos, pad_pos, n_rows)
    ys = _experts(xs, blk_e, n_used, p)
    out = _combine(x1, gates.T, _gather_sc(ys, pos))
    return out.reshape(batch, seq, D)


def _rope_tables(seq):
    half = ROT_DIM // 2
    inv_freq = jnp.power(ROPE_THETA, -2.0 * jnp.arange(half, dtype=F32) / ROT_DIM)
    ang = jnp.arange(seq, dtype=F32)[:, None] * inv_freq[None, :]
    cos, sin = jnp.cos(ang), jnp.sin(ang)
    zeros = jnp.zeros((seq, HEAD_DIM - ROT_DIM), F32)
    zh = jnp.zeros((seq, half), F32)
    cos_t = jnp.concatenate([cos, cos, zeros + 1.0], axis=1)
    sa = jnp.concatenate([-sin, zh, zeros], axis=1)
    sb = jnp.concatenate([zh, sin, zeros], axis=1)
    rep = LANES // HEAD_DIM
    return tuple(jnp.tile(t, (1, rep)) for t in (cos_t, sa, sb))


def kernel(x_prompt, x_sample, attn_norm_w, w_in, q_norm_w, k_norm_w, sgu_ln_w, sgu_ln_b, sgu_w, sgu_b,
           att_out_norm_w, sgu_out_norm_w, w_out, ffn_norm_w, router_w, router_b,
           w_gate, b_gate, w_up, b_up, w_down, b_down):
    depth = w_in.shape[0]
    n_heads = ATT_WIDTH // HEAD_DIM
    blk = np.arange(MXU_DIM) // HEAD_DIM
    tri = np.arange(TOKEN_TILE)
    xa, xb = x_prompt, x_sample
    rope = {x.shape[1]: _rope_tables(x.shape[1]) for x in (xa, xb)}
    for l in range(depth):
        shared = dict(
            attn_norm_w=attn_norm_w[l][None], w_in=w_in[l].astype(BF16),
            q_norm_w=jnp.tile(q_norm_w[l], n_heads)[None], k_norm_w=jnp.tile(k_norm_w[l], n_heads)[None],
            head_ones=jnp.asarray(blk[:, None] == blk[None, :], BF16),
            sgu_ln_w=sgu_ln_w[l][None], sgu_ln_b=sgu_ln_b[l][None],
            sgu_w=sgu_w[l].reshape(SGU_GROUPS * SGU_CHUNK, SGU_CHUNK).astype(BF16),
            sgu_b=jnp.repeat(sgu_b[l].T, SGU_GROUP_DIM, axis=1),
            att_out_norm_w=att_out_norm_w[l][None], sgu_out_norm_w=sgu_out_norm_w[l][None],
            w_out=w_out[l].astype(BF16), ffn_norm_w=ffn_norm_w[l][None],
            router_wt=router_w[l].T.astype(BF16), router_b=router_b[l][:, None],
            tri=jnp.asarray(tri[:, None] < tri[None, :], BF16),
            w_gate=w_gate[l], b_gate=b_gate[l][:, None, :],
            w_up=w_up[l], b_up=b_up[l][:, None, :],
            w_down=w_down[l], b_down=b_down[l][:, None, :],
        )
        xa, xb = (_layer(x, dict(shared, rope=rope)) for x in (xa, xb))
    return xa, xb
```

```python
import functools

import numpy as np
import jax
import jax.numpy as jnp
from jax import lax
from jax.experimental import pallas as pl
from jax.experimental.pallas import tpu as pltpu
from jax.experimental.pallas import tpu_sc as plsc

F32 = jnp.float32
BF16 = jnp.bfloat16
I32 = jnp.int32

HEAD_DIM = 64
ATT_WINDOWS = ((128, 1), (512, 4), (2048, 16))
N_GROUPS = len(ATT_WINDOWS)
HEADS_PER_GROUP = 4
GROUP_WIDTH = HEADS_PER_GROUP * HEAD_DIM
ATT_WIDTH = N_GROUPS * GROUP_WIDTH
SGU_WIDTH = 256
SGU_GROUP_DIM = 64
SGU_GROUPS = SGU_WIDTH // SGU_GROUP_DIM
SGU_CHUNK = 128
ROT_DIM = HEAD_DIM // 4
ROPE_THETA = 500000.0
N_EXPERTS = 32
TOP_K = 4
SWIGLU_LIMIT = 7.0
SWIGLU_ALPHA = 1.702
NORM_EPS = 1e-6
LN_EPS = 1e-5
NEG_INF = -1e30

LANES = 128
SUBLANES = 8
MXU_DIM = 256
TOKEN_TILE = 512
ATT_Q_BLOCK = 512
ATT_SUB_Q = 128
LSE_LANES = LANES // HEADS_PER_GROUP
MOE_BLOCK = 512
ROW_TILE = 512
SC_WINDOW = 64
VMEM_LIMIT = 48 * 1024 * 1024
EXPERTS_VMEM_LIMIT = 56 * 1024 * 1024


def _dot(a, b):
    return jnp.dot(a, b, preferred_element_type=F32)


def _dot_nt(a, b, precision=None):
    return lax.dot_general(a, b, (((1,), (1,)), ((), ())), precision=precision, preferred_element_type=F32)


def _rms(x, w):
    return x * lax.rsqrt(jnp.mean(x * x, axis=-1, keepdims=True) + NORM_EPS) * w


def _gelu(x):
    return 0.5 * x * (1.0 + lax.erf(x * np.float32(np.sqrt(0.5))))


def _pack_pairs(x):
    w = x.shape[1] // 2
    lo = lax.bitcast_convert_type(x[:, :w].astype(BF16).astype(F32), jnp.uint32) >> 16
    hi = lax.bitcast_convert_type(x[:, w:].astype(BF16).astype(F32), jnp.uint32) & jnp.uint32(0xFFFF0000)
    return lax.bitcast_convert_type(lo | hi, F32)


def _unpack_pairs(words):
    u = lax.bitcast_convert_type(words, jnp.uint32)
    lo = lax.bitcast_convert_type(u << 16, F32)
    hi = lax.bitcast_convert_type(u & jnp.uint32(0xFFFF0000), F32)
    return jnp.concatenate([lo, hi], axis=1)


def _to_tiled(ref, idx, value):
    rows = value.shape[0]
    for c in range(value.shape[1] // LANES):
        ref[idx + (slice(None), c)] = value[:, c * LANES:(c + 1) * LANES].reshape(rows // SUBLANES, SUBLANES, LANES)


def _from_tiled(ref, idx, row_tiles):
    chunks = ref.shape[-3]
    n = (row_tiles.stop - row_tiles.start) * SUBLANES
    return jnp.concatenate([ref[idx + (row_tiles, c)].reshape(n, LANES) for c in range(chunks)], axis=1)


def _in_proj_kernel(x_ref, anw_ref, w_ref, qnw_ref, knw_ref, ones_ref, cos_ref, sa_ref, sb_ref, lnw_ref, lnb_ref,
                    *refs):
    qkv_refs = refs[:3 * N_GROUPS]
    zu_ref, zv_ref, stage = refs[3 * N_GROUPS:]
    tm = x_ref.shape[0]
    h = _rms(x_ref[...], anw_ref[...]).astype(BF16)
    reps = ATT_WIDTH // LANES
    cos = jnp.concatenate([cos_ref[...]] * reps, axis=1)
    sa = jnp.concatenate([sa_ref[...]] * reps, axis=1)
    sb = jnp.concatenate([sb_ref[...]] * reps, axis=1)
    ones = ones_ref[...]

    def head_norm_rope(t, nw):
        sq = (t * t).astype(BF16)
        parts = []
        for j in range(ATT_WIDTH // MXU_DIM):
            sl = slice(j * MXU_DIM, (j + 1) * MXU_DIM)
            parts.append(_dot(sq[:, sl], ones))
        ssum = jnp.concatenate(parts, axis=1)
        t = t * lax.rsqrt(ssum * (1.0 / HEAD_DIM) + NORM_EPS) * nw
        half = ROT_DIM // 2
        return t * cos + pltpu.roll(t, ATT_WIDTH - half, 1) * sa + pltpu.roll(t, half, 1) * sb

    def emit(which, t):
        n_chunks = ATT_WIDTH // LANES
        per_group = GROUP_WIDTH // LANES
        for c in range(n_chunks):
            stage[which * n_chunks + c] = t[:, c * LANES:(c + 1) * LANES]
        for g, (_, dil) in enumerate(ATT_WINDOWS):
            out = qkv_refs[which * N_GROUPS + g]
            first = which * n_chunks + g * per_group
            for r in range(dil):
                rows = pl.ds(r, tm // dil, stride=dil)
                out[r] = jnp.concatenate([stage[first + c, rows, :] for c in range(per_group)], axis=1).astype(BF16)

    q = head_norm_rope(_dot(h, w_ref[:, 0:ATT_WIDTH]), qnw_ref[...])
    emit(0, q * (HEAD_DIM ** -0.5))
    emit(1, head_norm_rope(_dot(h, w_ref[:, ATT_WIDTH:2 * ATT_WIDTH]), knw_ref[...]))
    emit(2, _dot(h, w_ref[:, 2 * ATT_WIDTH:3 * ATT_WIDTH]))
    z = _dot(h, w_ref[:, 3 * ATT_WIDTH:3 * ATT_WIDTH + 2 * SGU_WIDTH])
    zu_ref[...] = _gelu(z[:, :SGU_WIDTH])
    gv = _gelu(z[:, SGU_WIDTH:])
    mu = jnp.mean(gv, axis=-1, keepdims=True)
    var = jnp.mean(jnp.square(gv - mu), axis=-1, keepdims=True)
    zv_ref[...] = ((gv - mu) * lax.rsqrt(var + LN_EPS) * lnw_ref[...] + lnb_ref[...]).astype(BF16)


def _phase_spec(tm, dil, n_seq_tiles, width=GROUP_WIDTH):
    return pl.BlockSpec((None, dil, tm // dil, width), lambda i: (i // n_seq_tiles, 0, i % n_seq_tiles, 0))


def _in_proj(x2, batch, seq, p):
    T, D = x2.shape
    tm = TOKEN_TILE
    n_seq_tiles = seq // tm
    in_width = p["w_in"].shape[1]
    const = lambda shape: pl.BlockSpec(shape, lambda i: (0,) * len(shape))
    rope = pl.BlockSpec((tm, LANES), lambda i: (i % n_seq_tiles, 0))
    row = lambda w: pl.BlockSpec((tm, w), lambda i: (i, 0))
    qkv_specs = [_phase_spec(tm, dil, n_seq_tiles) for _ in range(3) for _, dil in ATT_WINDOWS]
    qkv_shapes = [jax.ShapeDtypeStruct((batch, dil, seq // dil, GROUP_WIDTH), BF16)
                  for _ in range(3) for _, dil in ATT_WINDOWS]
    outs = pl.pallas_call(
        _in_proj_kernel,
        grid=(T // tm,),
        in_specs=[row(D), const((1, D)), const((D, in_width)), const((1, ATT_WIDTH)), const((1, ATT_WIDTH)),
                  const((MXU_DIM, MXU_DIM)), rope, rope, rope, const((1, SGU_WIDTH)), const((1, SGU_WIDTH))],
        out_specs=qkv_specs + [row(SGU_WIDTH), row(SGU_WIDTH)],
        out_shape=qkv_shapes + [jax.ShapeDtypeStruct((T, SGU_WIDTH), F32), jax.ShapeDtypeStruct((T, SGU_WIDTH), BF16)],
        scratch_shapes=[pltpu.VMEM((3 * ATT_WIDTH // LANES, tm, LANES), F32)],
        compiler_params=pltpu.CompilerParams(dimension_semantics=("arbitrary",), vmem_limit_bytes=VMEM_LIMIT),
        name="in_proj",
    )(x2, p["attn_norm_w"], p["w_in"], p["q_norm_w"], p["k_norm_w"], p["head_ones"],
      *p["rope"][seq], p["sgu_ln_w"], p["sgu_ln_b"])
    q, k, v = (outs[i * N_GROUPS:(i + 1) * N_GROUPS] for i in range(3))
    return q, k, v, outs[-2], outs[-1]


def _attn_kernel(q_ref, kp_ref, km_ref, kn_ref, vp_ref, vm_ref, vn_ref, o_ref, lse_ref, *, sub_len, steps):
    lq = q_ref.shape[0]
    sq = min(lq, ATT_SUB_Q)
    lk = sq + 2 * steps
    nh = HEADS_PER_GROUP
    lb = pl.program_id(2)
    kk = jnp.concatenate([kp_ref[...], km_ref[...], kn_ref[...]], axis=0)
    vv = jnp.concatenate([vp_ref[...], vm_ref[...], vn_ref[...]], axis=0)
    head = lax.broadcasted_iota(I32, (1, GROUP_WIDTH), 1) // HEAD_DIM
    lse_head = lax.broadcasted_iota(I32, (1, LANES), 1) // LSE_LANES
    hm_f = [(head == h).astype(F32) for h in range(nh)]
    hm_b = [m.astype(BF16) for m in hm_f]
    qi = lax.broadcasted_iota(I32, (sq, 1), 0)
    kj = lax.broadcasted_iota(I32, (sq, lk), 1)
    for j in range(lq // sq):
        first = lb * lq + j * sq
        lo = jnp.maximum(qi, steps - first)
        hi = jnp.minimum(qi + 2 * steps, sub_len - 1 + steps - first)
        mask = ((kj - lo).astype(jnp.uint32) <= (hi - lo).astype(jnp.uint32))[None]
        qj = q_ref[j * sq:(j + 1) * sq, :]
        qs = jnp.concatenate([qj * hm_b[h] for h in range(nh)], axis=0)
        s = _dot_nt(qs, kk[j * sq:j * sq + lk]).reshape(nh, sq, lk)
        s = jnp.where(mask, s, NEG_INF)
        m = jnp.max(s, axis=-1, keepdims=True)
        pr = jnp.exp(s - m)
        den = jnp.sum(pr, axis=-1, keepdims=True)
        oh = _dot(pr.reshape(nh * sq, lk).astype(BF16), vv[j * sq:j * sq + lk]).reshape(nh, sq, GROUP_WIDTH) / den
        lh = m + jnp.log(den)
        o, lse = oh[0], jnp.broadcast_to(lh[0], (sq, LANES))
        for h in range(1, nh):
            o = jnp.where(head == h, oh[h], o)
            lse = jnp.where(lse_head == h, lh[h], lse)
        o_ref[j * sq:(j + 1) * sq, :] = o.astype(BF16)
        lse_ref[j * sq:(j + 1) * sq, :] = lse


def _attention(q, k, v, group):
    window, dil = ATT_WINDOWS[group]
    steps = window // (2 * dil)
    batch, _, sub_len, _ = q.shape
    assert sub_len % steps == 0
    lq = min(ATT_Q_BLOCK, sub_len)
    assert lq % steps == 0 and sub_len % lq == 0
    per_q = lq // steps
    n_halo = sub_len // steps
    main = pl.BlockSpec((None, None, lq, GROUP_WIDTH), lambda b, r, i: (b, r, i, 0))
    prev = pl.BlockSpec((None, None, steps, GROUP_WIDTH), lambda b, r, i: (b, r, jnp.maximum(i * per_q - 1, 0), 0))
    nxt = pl.BlockSpec((None, None, steps, GROUP_WIDTH),
                       lambda b, r, i: (b, r, jnp.minimum((i + 1) * per_q, n_halo - 1), 0))
    return pl.pallas_call(
        functools.partial(_attn_kernel, sub_len=sub_len, steps=steps),
        grid=(batch, dil, sub_len // lq),
        in_specs=[main, prev, main, nxt, prev, main, nxt],
        out_specs=[main, pl.BlockSpec((None, None, lq, LANES), lambda b, r, i: (b, r, i, 0))],
        out_shape=[jax.ShapeDtypeStruct(q.shape, BF16), jax.ShapeDtypeStruct(q.shape[:3] + (LANES,), F32)],
        compiler_params=pltpu.CompilerParams(dimension_semantics=("arbitrary",) * 3, vmem_limit_bytes=VMEM_LIMIT),
        name=f"attention_g{group}",
    )(q, k, k, k, v, v, v)


def _post_mix_kernel(x_ref, *refs):
    o_refs = refs[:N_GROUPS]
    l_refs = refs[N_GROUPS:2 * N_GROUPS]
    (zu_ref, zv_ref, sguw_ref, sgub_ref, aonw_ref, sonw_ref, wout_ref, fnw_ref, rwt_ref, rb_ref, tri_ref,
     x1_ref, h2_hbm, gates_ref, eidx_ref, rank_ref, cnt_ref, stage, hstage, hsem) = refs[2 * N_GROUPS:]
    tm = x_ref.shape[0]

    def token_order(ref, slot):
        dil = ref.shape[0]
        if dil == 1:
            return ref[0].astype(F32)
        n_chunks = ref.shape[2] // LANES
        per_group = GROUP_WIDTH // LANES
        for r in range(dil):
            for c in range(n_chunks):
                stage[slot * per_group + c, pl.ds(r, tm // dil, stride=dil), :] = (
                    ref[r, :, c * LANES:(c + 1) * LANES].astype(F32))
        return jnp.concatenate([stage[slot * per_group + c] for c in range(n_chunks)], axis=1)

    def per_column(w):
        lane = lax.broadcasted_iota(I32, (1, LANES), 1)
        r1 = pltpu.roll(w, LSE_LANES, 1)
        r2 = pltpu.roll(w, 2 * LSE_LANES, 1)
        r3 = pltpu.roll(w, 3 * LSE_LANES, 1)
        low = jnp.where(lane < LSE_LANES, w, jnp.where(lane < 3 * LSE_LANES, r1, r2))
        high = jnp.where(lane < LSE_LANES, r2, jnp.where(lane < 3 * LSE_LANES, r3, w))
        return jnp.concatenate([low, high], axis=1)

    o = [token_order(ref, g) for g, ref in enumerate(o_refs)]
    l = [token_order(ref, N_GROUPS + g) for g, ref in enumerate(l_refs)]
    lmax = jnp.maximum(jnp.maximum(l[0], l[1]), l[2])
    e = [jnp.exp(lg - lmax) for lg in l]
    esum = e[0] + e[1] + e[2]
    att = (per_column(e[0] / esum) * o[0] + per_column(e[1] / esum) * o[1] + per_column(e[2] / esum) * o[2])
    att_n = _rms(att, aonw_ref[...]).astype(BF16)
    cgrp = lax.broadcasted_iota(I32, (1, SGU_WIDTH), 1) // SGU_GROUP_DIM
    sguw = sguw_ref[...]
    gates = []
    for c in range(tm // SGU_CHUNK):
        r = _dot(sguw, zv_ref[c * SGU_CHUNK:(c + 1) * SGU_CHUNK, :])
        g = sgub_ref[...]
        for grp in range(SGU_GROUPS):
            g = g + r[grp * SGU_CHUNK:(grp + 1) * SGU_CHUNK, :] * (cgrp == grp).astype(F32)
        gates.append(g)
    sgu = zu_ref[...] * jnp.concatenate(gates, axis=0)
    sgu_n = _rms(sgu, sonw_ref[...]).astype(BF16)
    x1 = x_ref[...] + _dot(att_n, wout_ref[0:GROUP_WIDTH, :]) + _dot(sgu_n, wout_ref[GROUP_WIDTH:, :])
    x1_ref[...] = x1
    h2 = _rms(x1, fnw_ref[...])
    step = pl.program_id(0)
    slot = step % 2
    bt = tm // SUBLANES

    def h2_copies(tile, s):
        return [pltpu.make_async_copy(hstage.at[s, :, :, j, :], h2_hbm.at[pl.ds(tile * bt, bt), j], hsem.at[s])
                for j in range(SUBLANES)]

    _to_tiled(hstage, (slot,), _pack_pairs(h2))
    for c in h2_copies(step, slot):
        c.start()

    @pl.when(step > 0)
    def _():
        for c in h2_copies(step - 1, 1 - slot):
            c.wait()
    logits = _dot_nt(rwt_ref[...], h2.astype(BF16)) + rb_ref[...]
    eiota = lax.broadcasted_iota(I32, (N_EXPERTS, tm), 0)
    vals, idxs = [], []
    for _ in range(TOP_K):
        m = jnp.max(logits, axis=0, keepdims=True)
        idx = jnp.min(jnp.where(logits == m, eiota, N_EXPERTS), axis=0, keepdims=True)
        vals.append(m)
        idxs.append(idx)
        logits = jnp.where(eiota == idx, -jnp.inf, logits)
    exps = [jnp.exp(v - vals[0]) for v in vals]
    den = exps[0] + exps[1] + exps[2] + exps[3]
    gates_ref[...] = jnp.concatenate([ex / den for ex in exps], axis=0)
    eidx_ref[...] = jnp.concatenate(idxs, axis=0)
    onehots = [(eiota == idx).astype(F32) for idx in idxs]
    chosen = onehots[0] + onehots[1] + onehots[2] + onehots[3]
    before = _dot(chosen.astype(BF16), tri_ref[...])
    rank_ref[...] = jnp.concatenate(
        [jnp.sum(oh * before, axis=0, keepdims=True) for oh in onehots], axis=0).astype(I32)
    cnt_ref[...] = jnp.broadcast_to(jnp.sum(chosen, axis=1, keepdims=True), (N_EXPERTS, LANES)).astype(I32)

    @pl.when(step == pl.num_programs(0) - 1)
    def _():
        for c in h2_copies(step, slot):
            c.wait()


def _post_mix(x2, seq, o, lse, zu, zv, p):
    T, D = x2.shape
    tm = TOKEN_TILE
    n_tiles = T // tm
    n_seq_tiles = seq // tm
    chunks = D // (2 * LANES)
    const = lambda shape: pl.BlockSpec(shape, lambda i: (0,) * len(shape))
    row = lambda w: pl.BlockSpec((tm, w), lambda i: (i, 0))
    colt = pl.BlockSpec((TOP_K, tm), lambda i: (0, i))
    phase = [_phase_spec(tm, dil, n_seq_tiles) for _, dil in ATT_WINDOWS]
    lse_phase = [_phase_spec(tm, dil, n_seq_tiles, LANES) for _, dil in ATT_WINDOWS]
    return pl.pallas_call(
        _post_mix_kernel,
        grid=(n_tiles,),
        in_specs=[row(D)] + phase + lse_phase + [row(SGU_WIDTH), row(SGU_WIDTH),
                  const((SGU_GROUPS * SGU_CHUNK, SGU_CHUNK)), const((SGU_CHUNK, SGU_WIDTH)),
                  const((1, GROUP_WIDTH)), const((1, SGU_WIDTH)), const((GROUP_WIDTH + SGU_WIDTH, D)),
                  const((1, D)), const((N_EXPERTS, D)), const((N_EXPERTS, 1)), const((tm, tm))],
        out_specs=[row(D), pl.BlockSpec(memory_space=pl.ANY),
                   colt, colt, colt, pl.BlockSpec((None, N_EXPERTS, LANES), lambda i: (i, 0, 0))],
        out_shape=[jax.ShapeDtypeStruct((T, D), F32),
                   jax.ShapeDtypeStruct((T // SUBLANES, SUBLANES, chunks, LANES), F32),
                   jax.ShapeDtypeStruct((TOP_K, T), F32), jax.ShapeDtypeStruct((TOP_K, T), I32),
                   jax.ShapeDtypeStruct((TOP_K, T), I32), jax.ShapeDtypeStruct((n_tiles, N_EXPERTS, LANES), I32)],
        scratch_shapes=[pltpu.VMEM((2 * N_GROUPS * GROUP_WIDTH // LANES, tm, LANES), F32),
                        pltpu.VMEM((2, tm // SUBLANES, chunks, SUBLANES, LANES), F32), pltpu.SemaphoreType.DMA((2,))],
        compiler_params=pltpu.CompilerParams(dimension_semantics=("arbitrary",), vmem_limit_bytes=VMEM_LIMIT),
        name="post_mix",
    )(x2, *o, *lse, zu, zv, p["sgu_w"], p["sgu_b"], p["att_out_norm_w"],
      p["sgu_out_norm_w"], p["w_out"], p["ffn_norm_w"], p["router_wt"], p["router_b"], p["tri"])


def _sc_workers():
    info = plsc.get_sparse_core_info()
    return info.num_cores, info.num_cores * info.num_subcores


def _dispatch_sc(h2p, pos, pad_pos, n_rows):
    T, chunks, _ = h2p.shape
    n_cores, n_workers = _sc_workers()
    win = SC_WINDOW
    per_worker = T // n_workers
    pad_per_worker = pad_pos.shape[0] // n_workers
    assert per_worker % win == 0 and pad_per_worker % win == 0
    mesh = plsc.VectorSubcoreMesh(core_axis_name="c", subcore_axis_name="s")

    @functools.partial(
        pl.kernel, mesh=mesh, out_type=jax.ShapeDtypeStruct((n_rows, chunks, LANES), F32),
        scratch_types=[pltpu.VMEM((win,), I32), pltpu.VMEM((win, chunks, LANES), F32)],
        name="dispatch_sc")
    def scatter_rows(h_hbm, pos_hbm, pad_hbm, zeros_hbm, xs_hbm, idx_v, rows_v):
        worker = lax.axis_index("s") * n_cores + lax.axis_index("c")

        @pl.loop(0, per_worker // win)
        def _(w):
            t0 = worker * per_worker + w * win
            pltpu.sync_copy(h_hbm.at[pl.ds(t0, win)], rows_v)
            for k in range(TOP_K):
                pltpu.sync_copy(pos_hbm.at[pl.ds(k * T + t0, win)], idx_v)
                pltpu.sync_copy(rows_v, xs_hbm.at[idx_v])

        pltpu.sync_copy(zeros_hbm, rows_v)

        @pl.loop(0, pad_per_worker // win)
        def _(w):
            pltpu.sync_copy(pad_hbm.at[pl.ds(worker * pad_per_worker + w * win, win)], idx_v)
            pltpu.sync_copy(rows_v, xs_hbm.at[idx_v])

    return scatter_rows(h2p, pos, pad_pos, jnp.zeros((win, chunks, LANES), F32))


def _gather_sc(ys, pos):
    _, chunks, _ = ys.shape
    n_assign = pos.shape[0]
    n_cores, n_workers = _sc_workers()
    win = SC_WINDOW
    per_worker = n_assign // n_workers
    assert per_worker % win == 0
    mesh = plsc.VectorSubcoreMesh(core_axis_name="c", subcore_axis_name="s")

    @functools.partial(
        pl.kernel, mesh=mesh, out_type=jax.ShapeDtypeStruct((n_assign, chunks, LANES), F32),
        scratch_types=[pltpu.VMEM((win,), I32), pltpu.VMEM((win, chunks, LANES), F32)],
        name="gather_sc")
    def gather_rows(ys_hbm, pos_hbm, out_hbm, idx_v, rows_v):
        worker = lax.axis_index("s") * n_cores + lax.axis_index("c")

        @pl.loop(0, per_worker // win)
        def _(w):
            a0 = worker * per_worker + w * win
            pltpu.sync_copy(pos_hbm.at[pl.ds(a0, win)], idx_v)
            pltpu.sync_copy(ys_hbm.at[idx_v], rows_v)
            pltpu.sync_copy(rows_v, out_hbm.at[pl.ds(a0, win)])

    return gather_rows(ys, pos)


def _experts_kernel(blk_e_ref, n_used_ref, xs_hbm, wg_ref, bg_ref, wu_ref, bu_ref, wd_ref, bd_ref, ys_hbm,
                    xbuf, obuf, wg_b, wu_b, wd_b, in_sem, out_sem):
    bt = xbuf.shape[1]
    i = pl.program_id(0)
    n = pl.num_programs(0)
    n_used = n_used_ref[0]
    slot = i % 2

    def in_copies(blk, s):
        return [pltpu.make_async_copy(xs_hbm.at[pl.ds(blk * bt, bt), j], xbuf.at[s, :, :, j, :], in_sem.at[s])
                for j in range(SUBLANES)]

    def out_copies(blk, s):
        return [pltpu.make_async_copy(obuf.at[s, :, :, j, :], ys_hbm.at[pl.ds(blk * bt, bt), j], out_sem.at[s])
                for j in range(SUBLANES)]

    @pl.when(i == 0)
    def _():
        for c in in_copies(0, 0):
            c.start()

    @pl.when(i + 1 < n_used)
    def _():
        for c in in_copies(i + 1, 1 - slot):
            c.start()

    @pl.when(i >= 2)
    def _():
        for c in out_copies(i - 2, slot):
            c.wait()

    @pl.when(i >= n_used)
    def _():
        obuf[slot] = jnp.zeros(obuf.shape[1:], F32)

    @pl.when(i < n_used)
    def _():
        for c in in_copies(i, slot):
            c.wait()
        @pl.when((i == 0) | (blk_e_ref[i] != blk_e_ref[jnp.maximum(i - 1, 0)]))
        def _():
            wg_b[...] = wg_ref[...].astype(BF16)
            wu_b[...] = wu_ref[...].astype(BF16)
            wd_b[...] = wd_ref[...].astype(BF16)

        x = _unpack_pairs(_from_tiled(xbuf, (slot,), slice(0, bt))).astype(BF16)
        g = _dot(x, wg_b[...]) + bg_ref[...]
        u = _dot(x, wu_b[...]) + bu_ref[...]
        g = jnp.minimum(g, SWIGLU_LIMIT)
        u = jnp.clip(u, -SWIGLU_LIMIT, SWIGLU_LIMIT)
        act = (u + 1.0) * (g * jax.nn.sigmoid(SWIGLU_ALPHA * g))
        _to_tiled(obuf, (slot,), _pack_pairs(_dot(act.astype(BF16), wd_b[...]) + bd_ref[...]))

    for c in out_copies(i, slot):
        c.start()

    @pl.when(i == n - 1)
    def _():
        for c in out_copies(i - 1, 1 - slot) + out_copies(i, slot):
            c.wait()


def _experts(xs, blk_e, n_used, p):
    n_rows, chunks, _ = xs.shape
    D = 2 * chunks * LANES
    bm = MOE_BLOCK
    d_ff = p["w_gate"].shape[2]
    assert n_rows // bm >= 2
    wspec = lambda a, b: pl.BlockSpec((None, a, b), lambda i, be, nu: (be[i], 0, 0))
    by_tile = (n_rows // SUBLANES, SUBLANES, chunks, LANES)
    block_buf = pltpu.VMEM((2, bm // SUBLANES, chunks, SUBLANES, LANES), F32)
    ys = pl.pallas_call(
        _experts_kernel,
        grid_spec=pltpu.PrefetchScalarGridSpec(
            num_scalar_prefetch=2,
            grid=(n_rows // bm,),
            in_specs=[pl.BlockSpec(memory_space=pl.ANY), wspec(D, d_ff), wspec(1, d_ff), wspec(D, d_ff),
                      wspec(1, d_ff), wspec(d_ff, D), wspec(1, D)],
            out_specs=pl.BlockSpec(memory_space=pl.ANY),
            scratch_shapes=[block_buf, block_buf, pltpu.VMEM((D, d_ff), BF16), pltpu.VMEM((D, d_ff), BF16),
                            pltpu.VMEM((d_ff, D), BF16), pltpu.SemaphoreType.DMA((2,)), pltpu.SemaphoreType.DMA((2,))],
        ),
        out_shape=jax.ShapeDtypeStruct(by_tile, F32),
        compiler_params=pltpu.CompilerParams(dimension_semantics=("arbitrary",), vmem_limit_bytes=EXPERTS_VMEM_LIMIT),
        name="experts",
    )(blk_e, n_used, xs.reshape(by_tile), p["w_gate"], p["b_gate"], p["w_up"], p["b_up"], p["w_down"], p["b_down"])
    return ys.reshape(n_rows, chunks, LANES)


def _combine_kernel(gates_ref, x1_ref, yg_hbm, out_ref, buf, sem):
    tm = x1_ref.shape[0]
    bt = tm // SUBLANES
    j = pl.program_id(0)
    n = pl.num_programs(0) - 1
    tiles_per_k = yg_hbm.shape[0] // TOP_K

    def copies(tile, s):
        return [pltpu.make_async_copy(yg_hbm.at[pl.ds(k * tiles_per_k + tile * bt, bt), r],
                                      buf.at[s, pl.ds(k * bt, bt), :, r, :], sem.at[s])
                for k in range(TOP_K) for r in range(SUBLANES)]

    @pl.when(j < n)
    def _():
        for c in copies(j, j % 2):
            c.start()

    @pl.when(j >= 1)
    def _():
        done = 1 - j % 2
        for c in copies(j - 1, done):
            c.wait()
        y = gates_ref[:, 0:1] * _unpack_pairs(_from_tiled(buf, (done,), slice(0, bt)))
        for k in range(1, TOP_K):
            y = y + gates_ref[:, k:k + 1] * _unpack_pairs(_from_tiled(buf, (done,), slice(k * bt, (k + 1) * bt)))
        out_ref[...] = x1_ref[...] + y


def _combine(x1, gates_tok, yg):
    T, D = x1.shape
    tm = ROW_TILE
    n = T // tm
    chunks = D // (2 * LANES)
    lag = lambda i: (jnp.maximum(i - 1, 0), 0)
    return pl.pallas_call(
        _combine_kernel,
        grid=(n + 1,),
        in_specs=[pl.BlockSpec((tm, TOP_K), lag), pl.BlockSpec((tm, D), lag), pl.BlockSpec(memory_space=pl.ANY)],
        out_specs=pl.BlockSpec((tm, D), lag),
        out_shape=jax.ShapeDtypeStruct((T, D), F32),
        scratch_shapes=[pltpu.VMEM((2, TOP_K * tm // SUBLANES, chunks, SUBLANES, LANES), F32),
                        pltpu.SemaphoreType.DMA((2,))],
        compiler_params=pltpu.CompilerParams(dimension_semantics=("arbitrary",), vmem_limit_bytes=VMEM_LIMIT),
        name="combine",
    )(gates_tok, x1, yg.reshape(TOP_K * T // SUBLANES, SUBLANES, chunks, LANES))


def _route(eidx, rank, tile_counts):
    T = eidx.shape[1]
    bm = MOE_BLOCK
    n_tiles = T // TOKEN_TILE
    tc = tile_counts[:, :, 0]
    counts = jnp.sum(tc, axis=0)
    tile_off = jnp.cumsum(tc, axis=0) - tc
    pcounts = (counts + bm - 1) // bm * bm
    pends = jnp.cumsum(pcounts)
    pstarts = pends - pcounts
    base = pstarts[None, :] + tile_off
    sel = eidx.reshape(TOP_K, n_tiles, 1, TOKEN_TILE) == jnp.arange(N_EXPERTS, dtype=I32)[None, None, :, None]
    pos = jnp.sum(jnp.where(sel, base[None, :, :, None], 0), axis=2).reshape(TOP_K, T) + rank
    n_blocks = (TOP_K * T) // bm + N_EXPERTS
    n_rows = n_blocks * bm
    blk_start = jnp.arange(n_blocks, dtype=I32) * bm
    blk_e = jnp.minimum(jnp.sum((pends[None, :] <= blk_start[:, None]).astype(I32), axis=1), N_EXPERTS - 1)
    n_used = (pends[-1] // bm).astype(I32).reshape(1)
    first_free = jnp.concatenate([pstarts + counts, pends[-1:]])
    free_off = jnp.cumsum(jnp.concatenate([jnp.zeros((1,), counts.dtype), pcounts - counts]))
    j = jnp.arange(n_rows - TOP_K * T, dtype=I32)[:, None]
    seg_id = jnp.sum((free_off[None, 1:] <= j).astype(I32), axis=1, keepdims=True)
    onehot = seg_id == jnp.arange(N_EXPERTS + 1, dtype=I32)[None, :]
    pad_pos = jnp.sum(jnp.where(onehot, (first_free - free_off)[None, :] + j, 0), axis=1).astype(I32)
    return pos.reshape(-1).astype(I32), pad_pos, blk_e.astype(I32), n_used, n_rows


def _layer(x, p):
    batch, seq, D = x.shape
    x2 = x.reshape(batch * seq, D)
    q, k, v, zu, zv = _in_proj(x2, batch, seq, p)
    o, lse = zip(*[_attention(q[g], k[g], v[g], g) for g in range(N_GROUPS)])
    x1, h2p, gates, eidx, rank, tile_counts = _post_mix(x2, seq, o, lse, zu, zv, p)
    pos, pad_pos, blk_e, n_used, n_rows = _route(eidx, rank, tile_counts)
    xs = _dispatch_sc(h2p.reshape(batch * seq, D // (2 * LANES), LANES), pos, pad_pos, n_rows)
    ys = _experts(xs, blk_e, n_used, p)
    out = _combine(x1, gates.T, _gather_sc(ys, pos))
    return out.reshape(batch, seq, D)


def _rope_tables(seq):
    half = ROT_DIM // 2
    inv_freq = jnp.power(ROPE_THETA, -2.0 * jnp.arange(half, dtype=F32) / ROT_DIM)
    ang = jnp.arange(seq, dtype=F32)[:, None] * inv_freq[None, :]
    cos, sin = jnp.cos(ang), jnp.sin(ang)
    zeros = jnp.zeros((seq, HEAD_DIM - ROT_DIM), F32)
    zh = jnp.zeros((seq, half), F32)
    cos_t = jnp.concatenate([cos, cos, zeros + 1.0], axis=1)
    sa = jnp.concatenate([-sin, zh, zeros], axis=1)
    sb = jnp.concatenate([zh, sin, zeros], axis=1)
    rep = LANES // HEAD_DIM
    return tuple(jnp.tile(t, (1, rep)) for t in (cos_t, sa, sb))


def kernel(x_prompt, x_sample, attn_norm_w, w_in, q_norm_w, k_norm_w, sgu_ln_w, sgu_ln_b, sgu_w, sgu_b,
           att_out_norm_w, sgu_out_norm_w, w_out, ffn_norm_w, router_w, router_b,
           w_gate, b_gate, w_up, b_up, w_down, b_down):
    depth = w_in.shape[0]
    n_heads = ATT_WIDTH // HEAD_DIM
    blk = np.arange(MXU_DIM) // HEAD_DIM
    tri = np.arange(TOKEN_TILE)
    xa, xb = x_prompt, x_sample
    rope = {x.shape[1]: _rope_tables(x.shape[1]) for x in (xa, xb)}
    for l in range(depth):
        shared = dict(
            attn_norm_w=attn_norm_w[l][None], w_in=w_in[l].astype(BF16),
            q_norm_w=jnp.tile(q_norm_w[l], n_heads)[None], k_norm_w=jnp.tile(k_norm_w[l], n_heads)[None],
            head_ones=jnp.asarray(blk[:, None] == blk[None, :], BF16),
            sgu_ln_w=sgu_ln_w[l][None], sgu_ln_b=sgu_ln_b[l][None],
            sgu_w=sgu_w[l].reshape(SGU_GROUPS * SGU_CHUNK, SGU_CHUNK).astype(BF16),
            sgu_b=jnp.repeat(sgu_b[l].T, SGU_GROUP_DIM, axis=1),
            att_out_norm_w=att_out_norm_w[l][None], sgu_out_norm_w=sgu_out_norm_w[l][None],
            w_out=w_out[l].astype(BF16), ffn_norm_w=ffn_norm_w[l][None],
            router_wt=router_w[l].T.astype(BF16), router_b=router_b[l][:, None],
            tri=jnp.asarray(tri[:, None] < tri[None, :], BF16),
            w_gate=w_gate[l], b_gate=b_gate[l][:, None, :],
            w_up=w_up[l], b_up=b_up[l][:, None, :],
            w_down=w_down[l], b_down=b_down[l][:, None, :],
        )
        xa, xb = (_layer(x, dict(shared, rope=rope)) for x in (xa, xb))
    return xa, xb
```

```python
import functools

import numpy as np
import jax
import jax.numpy as jnp
from jax import lax
from jax.experimental import pallas as pl
from jax.experimental.pallas import tpu as pltpu
from jax.experimental.pallas import tpu_sc as plsc

F32 = jnp.float32
BF16 = jnp.bfloat16
I32 = jnp.int32

HEAD_DIM = 64
ATT_WINDOWS = ((128, 1), (512, 4), (2048, 16))
N_GROUPS = len(ATT_WINDOWS)
HEADS_PER_GROUP = 4
GROUP_WIDTH = HEADS_PER_GROUP * HEAD_DIM
ATT_WIDTH = N_GROUPS * GROUP_WIDTH
SGU_WIDTH = 256
SGU_GROUP_DIM = 64
SGU_GROUPS = SGU_WIDTH // SGU_GROUP_DIM
SGU_CHUNK = 128
ROT_DIM = HEAD_DIM // 4
ROPE_THETA = 500000.0
N_EXPERTS = 32
TOP_K = 4
SWIGLU_LIMIT = 7.0
SWIGLU_ALPHA = 1.702
NORM_EPS = 1e-6
LN_EPS = 1e-5
NEG_INF = -1e30

LANES = 128
SUBLANES = 8
MXU_DIM = 256
TOKEN_TILE = 512
ATT_Q_BLOCK = 512
ATT_SUB_Q = 128
LSE_LANES = LANES // HEADS_PER_GROUP
MOE_BLOCK = 512
ROW_TILE = 512
SC_WINDOW = 64
VMEM_LIMIT = 48 * 1024 * 1024
EXPERTS_VMEM_LIMIT = 56 * 1024 * 1024


def _dot(a, b):
    return jnp.dot(a, b, preferred_element_type=F32)


def _dot_nt(a, b, precision=None):
    return lax.dot_general(a, b, (((1,), (1,)), ((), ())), precision=precision, preferred_element_type=F32)


def _rms(x, w):
    return x * lax.rsqrt(jnp.mean(x * x, axis=-1, keepdims=True) + NORM_EPS) * w


def _gelu(x):
    return 0.5 * x * (1.0 + lax.erf(x * np.float32(np.sqrt(0.5))))


def _pack_pairs(x):
    w = x.shape[1] // 2
    lo = lax.bitcast_convert_type(x[:, :w].astype(BF16).astype(F32), jnp.uint32) >> 16
    hi = lax.bitcast_convert_type(x[:, w:].astype(BF16).astype(F32), jnp.uint32) & jnp.uint32(0xFFFF0000)
    return lax.bitcast_convert_type(lo | hi, F32)


def _unpack_pairs(words):
    u = lax.bitcast_convert_type(words, jnp.uint32)
    lo = lax.bitcast_convert_type(u << 16, F32)
    hi = lax.bitcast_convert_type(u & jnp.uint32(0xFFFF0000), F32)
    return jnp.concatenate([lo, hi], axis=1)


def _to_tiled(ref, idx, value):
    rows = value.shape[0]
    for c in range(value.shape[1] // LANES):
        ref[idx + (slice(None), c)] = value[:, c * LANES:(c + 1) * LANES].reshape(rows // SUBLANES, SUBLANES, LANES)


def _from_tiled(ref, idx, row_tiles):
    chunks = ref.shape[-3]
    n = (row_tiles.stop - row_tiles.start) * SUBLANES
    return jnp.concatenate([ref[idx + (row_tiles, c)].reshape(n, LANES) for c in range(chunks)], axis=1)


def _in_proj_kernel(x_ref, anw_ref, w_ref, qnw_ref, knw_ref, ones_ref, cos_ref, sa_ref, sb_ref, lnw_ref, lnb_ref,
                    *refs):
    qkv_refs = refs[:3 * N_GROUPS]
    zu_ref, zv_ref, stage = refs[3 * N_GROUPS:]
    tm = x_ref.shape[0]
    h = _rms(x_ref[...], anw_ref[...]).astype(BF16)
    reps = ATT_WIDTH // LANES
    cos = jnp.concatenate([cos_ref[...]] * reps, axis=1)
    sa = jnp.concatenate([sa_ref[...]] * reps, axis=1)
    sb = jnp.concatenate([sb_ref[...]] * reps, axis=1)
    ones = ones_ref[...]

    def head_norm_rope(t, nw):
        sq = (t * t).astype(BF16)
        parts = []
        for j in range(ATT_WIDTH // MXU_DIM):
            sl = slice(j * MXU_DIM, (j + 1) * MXU_DIM)
            parts.append(_dot(sq[:, sl], ones))
        ssum = jnp.concatenate(parts, axis=1)
        t = t * lax.rsqrt(ssum * (1.0 / HEAD_DIM) + NORM_EPS) * nw
        half = ROT_DIM // 2
        return t * cos + pltpu.roll(t, ATT_WIDTH - half, 1) * sa + pltpu.roll(t, half, 1) * sb

    def emit(which, t):
        n_chunks = ATT_WIDTH // LANES
        per_group = GROUP_WIDTH // LANES
        for c in range(n_chunks):
            stage[which * n_chunks + c] = t[:, c * LANES:(c + 1) * LANES]
        for g, (_, dil) in enumerate(ATT_WINDOWS):
            out = qkv_refs[which * N_GROUPS + g]
            first = which * n_chunks + g * per_group
            for r in range(dil):
                rows = pl.ds(r, tm // dil, stride=dil)
                out[r] = jnp.concatenate([stage[first + c, rows, :] for c in range(per_group)], axis=1).astype(BF16)

    q = head_norm_rope(_dot(h, w_ref[:, 0:ATT_WIDTH]), qnw_ref[...])
    emit(0, q * (HEAD_DIM ** -0.5))
    emit(1, head_norm_rope(_dot(h, w_ref[:, ATT_WIDTH:2 * ATT_WIDTH]), knw_ref[...]))
    emit(2, _dot(h, w_ref[:, 2 * ATT_WIDTH:3 * ATT_WIDTH]))
    z = _dot(h, w_ref[:, 3 * ATT_WIDTH:3 * ATT_WIDTH + 2 * SGU_WIDTH])
    zu_ref[...] = _gelu(z[:, :SGU_WIDTH])
    gv = _gelu(z[:, SGU_WIDTH:])
    mu = jnp.mean(gv, axis=-1, keepdims=True)
    var = jnp.mean(jnp.square(gv - mu), axis=-1, keepdims=True)
    zv_ref[...] = ((gv - mu) * lax.rsqrt(var + LN_EPS) * lnw_ref[...] + lnb_ref[...]).astype(BF16)


def _phase_spec(tm, dil, n_seq_tiles, width=GROUP_WIDTH):
    return pl.BlockSpec((None, dil, tm // dil, width), lambda i: (i // n_seq_tiles, 0, i % n_seq_tiles, 0))


def _in_proj(x2, batch, seq, p):
    T, D = x2.shape
    tm = TOKEN_TILE
    n_seq_tiles = seq // tm
    in_width = p["w_in"].shape[1]
    const = lambda shape: pl.BlockSpec(shape, lambda i: (0,) * len(shape))
    rope = pl.BlockSpec((tm, LANES), lambda i: (i % n_seq_tiles, 0))
    row = lambda w: pl.BlockSpec((tm, w), lambda i: (i, 0))
    qkv_specs = [_phase_spec(tm, dil, n_seq_tiles) for _ in range(3) for _, dil in ATT_WINDOWS]
    qkv_shapes = [jax.ShapeDtypeStruct((batch, dil, seq // dil, GROUP_WIDTH), BF16)
                  for _ in range(3) for _, dil in ATT_WINDOWS]
    outs = pl.pallas_call(
        _in_proj_kernel,
        grid=(T // tm,),
        in_specs=[row(D), const((1, D)), const((D, in_width)), const((1, ATT_WIDTH)), const((1, ATT_WIDTH)),
                  const((MXU_DIM, MXU_DIM)), rope, rope, rope, const((1, SGU_WIDTH)), const((1, SGU_WIDTH))],
        out_specs=qkv_specs + [row(SGU_WIDTH), row(SGU_WIDTH)],
        out_shape=qkv_shapes + [jax.ShapeDtypeStruct((T, SGU_WIDTH), F32), jax.ShapeDtypeStruct((T, SGU_WIDTH), BF16)],
        scratch_shapes=[pltpu.VMEM((3 * ATT_WIDTH // LANES, tm, LANES), F32)],
        compiler_params=pltpu.CompilerParams(dimension_semantics=("arbitrary",), vmem_limit_bytes=VMEM_LIMIT),
        name="in_proj",
    )(x2, p["attn_norm_w"], p["w_in"], p["q_norm_w"], p["k_norm_w"], p["head_ones"],
      *p["rope"][seq], p["sgu_ln_w"], p["sgu_ln_b"])
    q, k, v = (outs[i * N_GROUPS:(i + 1) * N_GROUPS] for i in range(3))
    return q, k, v, outs[-2], outs[-1]


def _attn_kernel(q_ref, kp_ref, km_ref, kn_ref, vp_ref, vm_ref, vn_ref, o_ref, lse_ref, *, sub_len, steps):
    lq = q_ref.shape[0]
    sq = min(lq, ATT_SUB_Q)
    lk = sq + 2 * steps
    nh = HEADS_PER_GROUP
    lb = pl.program_id(2)
    kk = jnp.concatenate([kp_ref[...], km_ref[...], kn_ref[...]], axis=0)
    vv = jnp.concatenate([vp_ref[...], vm_ref[...], vn_ref[...]], axis=0)
    head = lax.broadcasted_iota(I32, (1, GROUP_WIDTH), 1) // HEAD_DIM
    lse_head = lax.broadcasted_iota(I32, (1, LANES), 1) // LSE_LANES
    hm_f = [(head == h).astype(F32) for h in range(nh)]
    hm_b = [m.astype(BF16) for m in hm_f]
    qi = lax.broadcasted_iota(I32, (sq, 1), 0)
    kj = lax.broadcasted_iota(I32, (sq, lk), 1)
    for j in range(lq // sq):
        first = lb * lq + j * sq
        lo = jnp.maximum(qi, steps - first)
        hi = jnp.minimum(qi + 2 * steps, sub_len - 1 + steps - first)
        mask = ((kj - lo).astype(jnp.uint32) <= (hi - lo).astype(jnp.uint32))[None]
        qj = q_ref[j * sq:(j + 1) * sq, :]
        qs = jnp.concatenate([qj * hm_b[h] for h in range(nh)], axis=0)
        s = _dot_nt(qs, kk[j * sq:j * sq + lk]).reshape(nh, sq, lk)
        s = jnp.where(mask, s, NEG_INF)
        m = jnp.max(s, axis=-1, keepdims=True)
        pr = jnp.exp(s - m)
        den = jnp.sum(pr, axis=-1, keepdims=True)
        oh = _dot(pr.reshape(nh * sq, lk).astype(BF16), vv[j * sq:j * sq + lk]).reshape(nh, sq, GROUP_WIDTH) / den
        lh = m + jnp.log(den)
        o, lse = oh[0], jnp.broadcast_to(lh[0], (sq, LANES))
        for h in range(1, nh):
            o = jnp.where(head == h, oh[h], o)
            lse = jnp.where(lse_head == h, lh[h], lse)
        o_ref[j * sq:(j + 1) * sq, :] = o.astype(BF16)
        lse_ref[j * sq:(j + 1) * sq, :] = lse


def _attention(q, k, v, group):
    window, dil = ATT_WINDOWS[group]
    steps = window // (2 * dil)
    batch, _, sub_len, _ = q.shape
    assert sub_len % steps == 0
    lq = min(ATT_Q_BLOCK, sub_len)
    assert lq % steps == 0 and sub_len % lq == 0
    per_q = lq // steps
    n_halo = sub_len // steps
    main = pl.BlockSpec((None, None, lq, GROUP_WIDTH), lambda b, r, i: (b, r, i, 0))
    prev = pl.BlockSpec((None, None, steps, GROUP_WIDTH), lambda b, r, i: (b, r, jnp.maximum(i * per_q - 1, 0), 0))
    nxt = pl.BlockSpec((None, None, steps, GROUP_WIDTH),
                       lambda b, r, i: (b, r, jnp.minimum((i + 1) * per_q, n_halo - 1), 0))
    return pl.pallas_call(
        functools.partial(_attn_kernel, sub_len=sub_len, steps=steps),
        grid=(batch, dil, sub_len // lq),
        in_specs=[main, prev, main, nxt, prev, main, nxt],
        out_specs=[main, pl.BlockSpec((None, None, lq, LANES), lambda b, r, i: (b, r, i, 0))],
        out_shape=[jax.ShapeDtypeStruct(q.shape, BF16), jax.ShapeDtypeStruct(q.shape[:3] + (LANES,), F32)],
        compiler_params=pltpu.CompilerParams(dimension_semantics=("arbitrary",) * 3, vmem_limit_bytes=VMEM_LIMIT),
        name=f"attention_g{group}",
    )(q, k, k, k, v, v, v)


def _post_mix_kernel(x_ref, *refs):
    o_refs = refs[:N_GROUPS]
    l_refs = refs[N_GROUPS:2 * N_GROUPS]
    (zu_ref, zv_ref, sguw_ref, sgub_ref, aonw_ref, sonw_ref, wout_ref, fnw_ref, rwt_ref, rb_ref, tri_ref,
     sel_ref, x1_ref, h2_hbm, gates_ref, eidx_ref, rank_ref, cnt_ref, stage, hstage, hsem) = refs[2 * N_GROUPS:]
    tm = x_ref.shape[0]

    def token_order(ref, slot):
        dil = ref.shape[0]
        if dil == 1:
            return ref[0].astype(F32)
        n_chunks = ref.shape[2] // LANES
        per_group = GROUP_WIDTH // LANES
        for r in range(dil):
            for c in range(n_chunks):
                stage[slot * per_group + c, pl.ds(r, tm // dil, stride=dil), :] = (
                    ref[r, :, c * LANES:(c + 1) * LANES].astype(F32))
        return jnp.concatenate([stage[slot * per_group + c] for c in range(n_chunks)], axis=1)

    def per_column(w):
        lane = lax.broadcasted_iota(I32, (1, LANES), 1)
        r1 = pltpu.roll(w, LSE_LANES, 1)
        r2 = pltpu.roll(w, 2 * LSE_LANES, 1)
        r3 = pltpu.roll(w, 3 * LSE_LANES, 1)
        low = jnp.where(lane < LSE_LANES, w, jnp.where(lane < 3 * LSE_LANES, r1, r2))
        high = jnp.where(lane < LSE_LANES, r2, jnp.where(lane < 3 * LSE_LANES, r3, w))
        return jnp.concatenate([low, high], axis=1)

    o = [token_order(ref, g) for g, ref in enumerate(o_refs)]
    l = [token_order(ref, N_GROUPS + g) for g, ref in enumerate(l_refs)]
    lmax = jnp.maximum(jnp.maximum(l[0], l[1]), l[2])
    e = [jnp.exp(lg - lmax) for lg in l]
    esum = e[0] + e[1] + e[2]
    att = (per_column(e[0] / esum) * o[0] + per_column(e[1] / esum) * o[1] + per_column(e[2] / esum) * o[2])
    att_n = _rms(att, aonw_ref[...]).astype(BF16)
    cgrp = lax.broadcasted_iota(I32, (1, SGU_WIDTH), 1) // SGU_GROUP_DIM
    sguw = sguw_ref[...]
    gates = []
    for c in range(tm // SGU_CHUNK):
        r = _dot(sguw, zv_ref[c * SGU_CHUNK:(c + 1) * SGU_CHUNK, :])
        g = sgub_ref[...]
        for grp in range(SGU_GROUPS):
            g = g + r[grp * SGU_CHUNK:(grp + 1) * SGU_CHUNK, :] * (cgrp == grp).astype(F32)
        gates.append(g)
    sgu = zu_ref[...] * jnp.concatenate(gates, axis=0)
    sgu_n = _rms(sgu, sonw_ref[...]).astype(BF16)
    x1 = x_ref[...] + _dot(att_n, wout_ref[0:GROUP_WIDTH, :]) + _dot(sgu_n, wout_ref[GROUP_WIDTH:, :])
    x1_ref[...] = x1
    h2 = _rms(x1, fnw_ref[...])
    step = pl.program_id(0)
    slot = step % 2
    bt = tm // SUBLANES

    def h2_copies(tile, s):
        return [pltpu.make_async_copy(hstage.at[s, :, :, j, :], h2_hbm.at[pl.ds(tile * bt, bt), j], hsem.at[s])
                for j in range(SUBLANES)]

    _to_tiled(hstage, (slot,), _pack_pairs(h2))
    for c in h2_copies(step, slot):
        c.start()

    @pl.when(step > 0)
    def _():
        for c in h2_copies(step - 1, 1 - slot):
            c.wait()
    logits = _dot_nt(rwt_ref[...], h2.astype(BF16)) + rb_ref[...]
    eiota = lax.broadcasted_iota(I32, (N_EXPERTS, tm), 0)
    vals, idxs = [], []
    for _ in range(TOP_K):
        m = jnp.max(logits, axis=0, keepdims=True)
        idx = jnp.min(jnp.where(logits == m, eiota, N_EXPERTS), axis=0, keepdims=True)
        vals.append(m)
        idxs.append(idx)
        logits = jnp.where(eiota == idx, -jnp.inf, logits)
    exps = [jnp.exp(v - vals[0]) for v in vals]
    den = exps[0] + exps[1] + exps[2] + exps[3]
    g = jnp.concatenate([ex / den for ex in exps], axis=0)
    g1 = g.astype(BF16).astype(F32)
    g2 = (g - g1).astype(BF16).astype(F32)
    g3 = g - g1 - g2
    pieces = jnp.concatenate([g1, g2, g3, jnp.zeros_like(g)], axis=0).astype(BF16)
    gates_ref[...] = lax.dot_general(pieces, sel_ref[...], (((0,), (0,)), ((), ())), preferred_element_type=F32)
    eidx_ref[...] = jnp.concatenate(idxs, axis=0)
    onehots = [(eiota == idx).astype(F32) for idx in idxs]
    chosen = onehots[0] + onehots[1] + onehots[2] + onehots[3]
    before = _dot(chosen.astype(BF16), tri_ref[...])
    rank_ref[...] = jnp.concatenate(
        [jnp.sum(oh * before, axis=0, keepdims=True) for oh in onehots], axis=0).astype(I32)
    cnt_ref[...] = jnp.broadcast_to(jnp.sum(chosen, axis=1, keepdims=True), (N_EXPERTS, LANES)).astype(I32)

    @pl.when(step == pl.num_programs(0) - 1)
    def _():
        for c in h2_copies(step, slot):
            c.wait()


def _post_mix(x2, seq, o, lse, zu, zv, p):
    T, D = x2.shape
    tm = TOKEN_TILE
    n_tiles = T // tm
    n_seq_tiles = seq // tm
    chunks = D // (2 * LANES)
    const = lambda shape: pl.BlockSpec(shape, lambda i: (0,) * len(shape))
    row = lambda w: pl.BlockSpec((tm, w), lambda i: (i, 0))
    colt = pl.BlockSpec((TOP_K, tm), lambda i: (0, i))
    phase = [_phase_spec(tm, dil, n_seq_tiles) for _, dil in ATT_WINDOWS]
    lse_phase = [_phase_spec(tm, dil, n_seq_tiles, LANES) for _, dil in ATT_WINDOWS]
    return pl.pallas_call(
        _post_mix_kernel,
        grid=(n_tiles,),
        in_specs=[row(D)] + phase + lse_phase + [row(SGU_WIDTH), row(SGU_WIDTH),
                  const((SGU_GROUPS * SGU_CHUNK, SGU_CHUNK)), const((SGU_CHUNK, SGU_WIDTH)),
                  const((1, GROUP_WIDTH)), const((1, SGU_WIDTH)), const((GROUP_WIDTH + SGU_WIDTH, D)),
                  const((1, D)), const((N_EXPERTS, D)), const((N_EXPERTS, 1)), const((tm, tm)),
                  const((4 * TOP_K, LANES))],
        out_specs=[row(D), pl.BlockSpec(memory_space=pl.ANY),
                   row(LANES), colt, colt, pl.BlockSpec((None, N_EXPERTS, LANES), lambda i: (i, 0, 0))],
        out_shape=[jax.ShapeDtypeStruct((T, D), F32),
                   jax.ShapeDtypeStruct((T // SUBLANES, SUBLANES, chunks, LANES), F32),
                   jax.ShapeDtypeStruct((T, LANES), F32), jax.ShapeDtypeStruct((TOP_K, T), I32),
                   jax.ShapeDtypeStruct((TOP_K, T), I32), jax.ShapeDtypeStruct((n_tiles, N_EXPERTS, LANES), I32)],
        scratch_shapes=[pltpu.VMEM((2 * N_GROUPS * GROUP_WIDTH // LANES, tm, LANES), F32),
                        pltpu.VMEM((2, tm // SUBLANES, chunks, SUBLANES, LANES), F32), pltpu.SemaphoreType.DMA((2,))],
        compiler_params=pltpu.CompilerParams(dimension_semantics=("arbitrary",), vmem_limit_bytes=VMEM_LIMIT),
        name="post_mix",
    )(x2, *o, *lse, zu, zv, p["sgu_w"], p["sgu_b"], p["att_out_norm_w"],
      p["sgu_out_norm_w"], p["w_out"], p["ffn_norm_w"], p["router_wt"], p["router_b"], p["tri"], p["gate_sel"])


def _sc_workers():
    info = plsc.get_sparse_core_info()
    return info.num_cores, info.num_cores * info.num_subcores


def _dispatch_sc(h2p, pos, pad_pos, n_rows):
    T, chunks, _ = h2p.shape
    n_cores, n_workers = _sc_workers()
    win = SC_WINDOW
    per_worker = T // n_workers
    pad_per_worker = pad_pos.shape[0] // n_workers
    assert per_worker % win == 0 and pad_per_worker % win == 0
    mesh = plsc.VectorSubcoreMesh(core_axis_name="c", subcore_axis_name="s")

    @functools.partial(
        pl.kernel, mesh=mesh, out_type=jax.ShapeDtypeStruct((n_rows, chunks, LANES), F32),
        scratch_types=[pltpu.VMEM((win,), I32), pltpu.VMEM((win, chunks, LANES), F32)],
        name="dispatch_sc")
    def scatter_rows(h_hbm, pos_hbm, pad_hbm, zeros_hbm, xs_hbm, idx_v, rows_v):
        worker = lax.axis_index("s") * n_cores + lax.axis_index("c")

        @pl.loop(0, per_worker // win)
        def _(w):
            t0 = worker * per_worker + w * win
            pltpu.sync_copy(h_hbm.at[pl.ds(t0, win)], rows_v)
            for k in range(TOP_K):
                pltpu.sync_copy(pos_hbm.at[pl.ds(k * T + t0, win)], idx_v)
                pltpu.sync_copy(rows_v, xs_hbm.at[idx_v])

        pltpu.sync_copy(zeros_hbm, rows_v)

        @pl.loop(0, pad_per_worker // win)
        def _(w):
            pltpu.sync_copy(pad_hbm.at[pl.ds(worker * pad_per_worker + w * win, win)], idx_v)
            pltpu.sync_copy(rows_v, xs_hbm.at[idx_v])

    return scatter_rows(h2p, pos, pad_pos, jnp.zeros((win, chunks, LANES), F32))


def _gather_sc(ys, pos):
    _, chunks, _ = ys.shape
    n_assign = pos.shape[0]
    n_cores, n_workers = _sc_workers()
    win = SC_WINDOW
    per_worker = n_assign // n_workers
    assert per_worker % win == 0
    mesh = plsc.VectorSubcoreMesh(core_axis_name="c", subcore_axis_name="s")

    @functools.partial(
        pl.kernel, mesh=mesh, out_type=jax.ShapeDtypeStruct((n_assign, chunks, LANES), F32),
        scratch_types=[pltpu.VMEM((win,), I32), pltpu.VMEM((win, chunks, LANES), F32)],
        name="gather_sc")
    def gather_rows(ys_hbm, pos_hbm, out_hbm, idx_v, rows_v):
        worker = lax.axis_index("s") * n_cores + lax.axis_index("c")

        @pl.loop(0, per_worker // win)
        def _(w):
            a0 = worker * per_worker + w * win
            pltpu.sync_copy(pos_hbm.at[pl.ds(a0, win)], idx_v)
            pltpu.sync_copy(ys_hbm.at[idx_v], rows_v)
            pltpu.sync_copy(rows_v, out_hbm.at[pl.ds(a0, win)])

    return gather_rows(ys, pos)


def _experts_kernel(blk_e_ref, n_used_ref, xs_hbm, wg_ref, bg_ref, wu_ref, bu_ref, wd_ref, bd_ref, ys_hbm,
                    xbuf, obuf, wg_b, wu_b, wd_b, in_sem, out_sem):
    bt = xbuf.shape[1]
    i = pl.program_id(0)
    n = pl.num_programs(0)
    n_used = n_used_ref[0]
    slot = i % 2

    def in_copies(blk, s):
        return [pltpu.make_async_copy(xs_hbm.at[pl.ds(blk * bt, bt), j], xbuf.at[s, :, :, j, :], in_sem.at[s])
                for j in range(SUBLANES)]

    def out_copies(blk, s):
        return [pltpu.make_async_copy(obuf.at[s, :, :, j, :], ys_hbm.at[pl.ds(blk * bt, bt), j], out_sem.at[s])
                for j in range(SUBLANES)]

    @pl.when(i == 0)
    def _():
        for c in in_copies(0, 0):
            c.start()

    @pl.when(i + 1 < n_used)
    def _():
        for c in in_copies(i + 1, 1 - slot):
            c.start()

    @pl.when(i >= 2)
    def _():
        for c in out_copies(i - 2, slot):
            c.wait()

    @pl.when(i >= n_used)
    def _():
        obuf[slot] = jnp.zeros(obuf.shape[1:], F32)

    @pl.when(i < n_used)
    def _():
        for c in in_copies(i, slot):
            c.wait()
        @pl.when((i == 0) | (blk_e_ref[i] != blk_e_ref[jnp.maximum(i - 1, 0)]))
        def _():
            wg_b[...] = wg_ref[...].astype(BF16)
            wu_b[...] = wu_ref[...].astype(BF16)
            wd_b[...] = wd_ref[...].astype(BF16)

        x = _unpack_pairs(_from_tiled(xbuf, (slot,), slice(0, bt))).astype(BF16)
        g = _dot(x, wg_b[...]) + bg_ref[...]
        u = _dot(x, wu_b[...]) + bu_ref[...]
        g = jnp.minimum(g, SWIGLU_LIMIT)
        u = jnp.clip(u, -SWIGLU_LIMIT, SWIGLU_LIMIT)
        act = (u + 1.0) * (g * jax.nn.sigmoid(SWIGLU_ALPHA * g))
        _to_tiled(obuf, (slot,), _pack_pairs(_dot(act.astype(BF16), wd_b[...]) + bd_ref[...]))

    for c in out_copies(i, slot):
        c.start()

    @pl.when(i == n - 1)
    def _():
        for c in out_copies(i - 1, 1 - slot) + out_copies(i, slot):
            c.wait()


def _experts(xs, blk_e, n_used, p):
    n_rows, chunks, _ = xs.shape
    D = 2 * chunks * LANES
    bm = MOE_BLOCK
    d_ff = p["w_gate"].shape[2]
    assert n_rows // bm >= 2
    wspec = lambda a, b: pl.BlockSpec((None, a, b), lambda i, be, nu: (be[i], 0, 0))
    by_tile = (n_rows // SUBLANES, SUBLANES, chunks, LANES)
    block_buf = pltpu.VMEM((2, bm // SUBLANES, chunks, SUBLANES, LANES), F32)
    ys = pl.pallas_call(
        _experts_kernel,
        grid_spec=pltpu.PrefetchScalarGridSpec(
            num_scalar_prefetch=2,
            grid=(n_rows // bm,),
            in_specs=[pl.BlockSpec(memory_space=pl.ANY), wspec(D, d_ff), wspec(1, d_ff), wspec(D, d_ff),
                      wspec(1, d_ff), wspec(d_ff, D), wspec(1, D)],
            out_specs=pl.BlockSpec(memory_space=pl.ANY),
            scratch_shapes=[block_buf, block_buf, pltpu.VMEM((D, d_ff), BF16), pltpu.VMEM((D, d_ff), BF16),
                            pltpu.VMEM((d_ff, D), BF16), pltpu.SemaphoreType.DMA((2,)), pltpu.SemaphoreType.DMA((2,))],
        ),
        out_shape=jax.ShapeDtypeStruct(by_tile, F32),
        compiler_params=pltpu.CompilerParams(dimension_semantics=("arbitrary",), vmem_limit_bytes=EXPERTS_VMEM_LIMIT),
        name="experts",
    )(blk_e, n_used, xs.reshape(by_tile), p["w_gate"], p["b_gate"], p["w_up"], p["b_up"], p["w_down"], p["b_down"])
    return ys.reshape(n_rows, chunks, LANES)


def _combine_kernel(gates_ref, x1_ref, yg_hbm, out_ref, buf, sem):
    tm = x1_ref.shape[0]
    bt = tm // SUBLANES
    j = pl.program_id(0)
    n = pl.num_programs(0) - 1
    tiles_per_k = yg_hbm.shape[0] // TOP_K

    def copies(tile, s):
        return [pltpu.make_async_copy(yg_hbm.at[pl.ds(k * tiles_per_k + tile * bt, bt), r],
                                      buf.at[s, pl.ds(k * bt, bt), :, r, :], sem.at[s])
                for k in range(TOP_K) for r in range(SUBLANES)]

    @pl.when(j < n)
    def _():
        for c in copies(j, j % 2):
            c.start()

    @pl.when(j >= 1)
    def _():
        done = 1 - j % 2
        for c in copies(j - 1, done):
            c.wait()
        y = gates_ref[:, 0:1] * _unpack_pairs(_from_tiled(buf, (done,), slice(0, bt)))
        for k in range(1, TOP_K):
            y = y + gates_ref[:, k:k + 1] * _unpack_pairs(_from_tiled(buf, (done,), slice(k * bt, (k + 1) * bt)))
        out_ref[...] = x1_ref[...] + y


def _combine(x1, gates_tok, yg):
    T, D = x1.shape
    tm = ROW_TILE
    n = T // tm
    chunks = D // (2 * LANES)
    lag = lambda i: (jnp.maximum(i - 1, 0), 0)
    return pl.pallas_call(
        _combine_kernel,
        grid=(n + 1,),
        in_specs=[pl.BlockSpec((tm, LANES), lag), pl.BlockSpec((tm, D), lag), pl.BlockSpec(memory_space=pl.ANY)],
        out_specs=pl.BlockSpec((tm, D), lag),
        out_shape=jax.ShapeDtypeStruct((T, D), F32),
        scratch_shapes=[pltpu.VMEM((2, TOP_K * tm // SUBLANES, chunks, SUBLANES, LANES), F32),
                        pltpu.SemaphoreType.DMA((2,))],
        compiler_params=pltpu.CompilerParams(dimension_semantics=("arbitrary",), vmem_limit_bytes=VMEM_LIMIT),
        name="combine",
    )(gates_tok, x1, yg.reshape(TOP_K * T // SUBLANES, SUBLANES, chunks, LANES))


def _route(eidx, rank, tile_counts):
    T = eidx.shape[1]
    bm = MOE_BLOCK
    n_tiles = T // TOKEN_TILE
    tc = tile_counts[:, :, 0]
    counts = jnp.sum(tc, axis=0)
    tile_off = jnp.cumsum(tc, axis=0) - tc
    pcounts = (counts + bm - 1) // bm * bm
    pends = jnp.cumsum(pcounts)
    pstarts = pends - pcounts
    base = pstarts[None, :] + tile_off
    sel = eidx.reshape(TOP_K, n_tiles, 1, TOKEN_TILE) == jnp.arange(N_EXPERTS, dtype=I32)[None, None, :, None]
    pos = jnp.sum(jnp.where(sel, base[None, :, :, None], 0), axis=2).reshape(TOP_K, T) + rank
    n_blocks = (TOP_K * T) // bm + N_EXPERTS
    n_rows = n_blocks * bm
    blk_start = jnp.arange(n_blocks, dtype=I32) * bm
    blk_e = jnp.minimum(jnp.sum((pends[None, :] <= blk_start[:, None]).astype(I32), axis=1), N_EXPERTS - 1)
    n_used = (pends[-1] // bm).astype(I32).reshape(1)
    first_free = jnp.concatenate([pstarts + counts, pends[-1:]])
    free_off = jnp.cumsum(jnp.concatenate([jnp.zeros((1,), counts.dtype), pcounts - counts]))
    j = jnp.arange(n_rows - TOP_K * T, dtype=I32)[:, None]
    seg_id = jnp.sum((free_off[None, 1:] <= j).astype(I32), axis=1, keepdims=True)
    onehot = seg_id == jnp.arange(N_EXPERTS + 1, dtype=I32)[None, :]
    pad_pos = jnp.sum(jnp.where(onehot, (first_free - free_off)[None, :] + j, 0), axis=1).astype(I32)
    return pos.reshape(-1).astype(I32), pad_pos, blk_e.astype(I32), n_used, n_rows


def _layer(x, p):
    batch, seq, D = x.shape
    x2 = x.reshape(batch * seq, D)
    q, k, v, zu, zv = _in_proj(x2, batch, seq, p)
    o, lse = zip(*[_attention(q[g], k[g], v[g], g) for g in range(N_GROUPS)])
    x1, h2p, gates, eidx, rank, tile_counts = _post_mix(x2, seq, o, lse, zu, zv, p)
    pos, pad_pos, blk_e, n_used, n_rows = _route(eidx, rank, tile_counts)
    xs = _dispatch_sc(h2p.reshape(batch * seq, D // (2 * LANES), LANES), pos, pad_pos, n_rows)
    ys = _experts(xs, blk_e, n_used, p)
    out = _combine(x1, gates, _gather_sc(ys, pos))
    return out.reshape(batch, seq, D)


def _rope_tables(seq):
    half = ROT_DIM // 2
    inv_freq = jnp.power(ROPE_THETA, -2.0 * jnp.arange(half, dtype=F32) / ROT_DIM)
    ang = jnp.arange(seq, dtype=F32)[:, None] * inv_freq[None, :]
    cos, sin = jnp.cos(ang), jnp.sin(ang)
    zeros = jnp.zeros((seq, HEAD_DIM - ROT_DIM), F32)
    zh = jnp.zeros((seq, half), F32)
    cos_t = jnp.concatenate([cos, cos, zeros + 1.0], axis=1)
    sa = jnp.concatenate([-sin, zh, zeros], axis=1)
    sb = jnp.concatenate([zh, sin, zeros], axis=1)
    rep = LANES // HEAD_DIM
    return tuple(jnp.tile(t, (1, rep)) for t in (cos_t, sa, sb))


def kernel(x_prompt, x_sample, attn_norm_w, w_in, q_norm_w, k_norm_w, sgu_ln_w, sgu_ln_b, sgu_w, sgu_b,
           att_out_norm_w, sgu_out_norm_w, w_out, ffn_norm_w, router_w, router_b,
           w_gate, b_gate, w_up, b_up, w_down, b_down):
    depth = w_in.shape[0]
    n_heads = ATT_WIDTH // HEAD_DIM
    blk = np.arange(MXU_DIM) // HEAD_DIM
    tri = np.arange(TOKEN_TILE)
    sel = np.arange(4 * TOP_K)
    xa, xb = x_prompt, x_sample
    rope = {x.shape[1]: _rope_tables(x.shape[1]) for x in (xa, xb)}
    for l in range(depth):
        shared = dict(
            attn_norm_w=attn_norm_w[l][None], w_in=w_in[l].astype(BF16),
            q_norm_w=jnp.tile(q_norm_w[l], n_heads)[None], k_norm_w=jnp.tile(k_norm_w[l], n_heads)[None],
            head_ones=jnp.asarray(blk[:, None] == blk[None, :], BF16),
            sgu_ln_w=sgu_ln_w[l][None], sgu_ln_b=sgu_ln_b[l][None],
            sgu_w=sgu_w[l].reshape(SGU_GROUPS * SGU_CHUNK, SGU_CHUNK).astype(BF16),
            sgu_b=jnp.repeat(sgu_b[l].T, SGU_GROUP_DIM, axis=1),
            att_out_norm_w=att_out_norm_w[l][None], sgu_out_norm_w=sgu_out_norm_w[l][None],
            w_out=w_out[l].astype(BF16), ffn_norm_w=ffn_norm_w[l][None],
            router_wt=router_w[l].T.astype(BF16), router_b=router_b[l][:, None],
            tri=jnp.asarray(tri[:, None] < tri[None, :], BF16),
            gate_sel=jnp.asarray((sel[:, None] % TOP_K == np.arange(LANES)[None, :]) & (sel[:, None] < 3 * TOP_K), BF16),
            w_gate=w_gate[l], b_gate=b_gate[l][:, None, :],
            w_up=w_up[l], b_up=b_up[l][:, None, :],
            w_down=w_down[l], b_down=b_down[l][:, None, :],
        )
        xa, xb = (_layer(x, dict(shared, rope=rope)) for x in (xa, xb))
    return xa, xb
```

```python
import functools

import numpy as np
import jax
import jax.numpy as jnp
from jax import lax
from jax.experimental import pallas as pl
from jax.experimental.pallas import tpu as pltpu
from jax.experimental.pallas import tpu_sc as plsc

F32 = jnp.float32
BF16 = jnp.bfloat16
I32 = jnp.int32

HEAD_DIM = 64
ATT_WINDOWS = ((128, 1), (512, 4), (2048, 16))
N_GROUPS = len(ATT_WINDOWS)
HEADS_PER_GROUP = 4
GROUP_WIDTH = HEADS_PER_GROUP * HEAD_DIM
ATT_WIDTH = N_GROUPS * GROUP_WIDTH
SGU_WIDTH = 256
SGU_GROUP_DIM = 64
SGU_GROUPS = SGU_WIDTH // SGU_GROUP_DIM
SGU_CHUNK = 128
ROT_DIM = HEAD_DIM // 4
ROPE_THETA = 500000.0
N_EXPERTS = 32
TOP_K = 4
SWIGLU_LIMIT = 7.0
SWIGLU_ALPHA = 1.702
NORM_EPS = 1e-6
LN_EPS = 1e-5
NEG_INF = -1e30

LANES = 128
SUBLANES = 8
MXU_DIM = 256
TOKEN_TILE = 512
ATT_Q_BLOCK = 512
ATT_SUB_Q = 128
LSE_LANES = LANES // HEADS_PER_GROUP
MOE_BLOCK = 512
ROW_TILE = 512
SC_WINDOW = 128
VMEM_LIMIT = 48 * 1024 * 1024
EXPERTS_VMEM_LIMIT = 56 * 1024 * 1024


def _dot(a, b):
    return jnp.dot(a, b, preferred_element_type=F32)


def _dot_nt(a, b, precision=None):
    return lax.dot_general(a, b, (((1,), (1,)), ((), ())), precision=precision, preferred_element_type=F32)


def _rms(x, w):
    return x * lax.rsqrt(jnp.mean(x * x, axis=-1, keepdims=True) + NORM_EPS) * w


def _gelu(x):
    return 0.5 * x * (1.0 + lax.erf(x * np.float32(np.sqrt(0.5))))


def _pack_pairs(x):
    w = x.shape[1] // 2
    lo = lax.bitcast_convert_type(x[:, :w].astype(BF16).astype(F32), jnp.uint32) >> 16
    hi = lax.bitcast_convert_type(x[:, w:].astype(BF16).astype(F32), jnp.uint32) & jnp.uint32(0xFFFF0000)
    return lax.bitcast_convert_type(lo | hi, F32)


def _unpack_pairs(words):
    u = lax.bitcast_convert_type(words, jnp.uint32)
    lo = lax.bitcast_convert_type(u << 16, F32)
    hi = lax.bitcast_convert_type(u & jnp.uint32(0xFFFF0000), F32)
    return jnp.concatenate([lo, hi], axis=1)


def _to_tiled(ref, idx, value):
    rows = value.shape[0]
    for c in range(value.shape[1] // LANES):
        ref[idx + (slice(None), c)] = value[:, c * LANES:(c + 1) * LANES].reshape(rows // SUBLANES, SUBLANES, LANES)


def _from_tiled(ref, idx, row_tiles):
    chunks = ref.shape[-3]
    n = (row_tiles.stop - row_tiles.start) * SUBLANES
    return jnp.concatenate([ref[idx + (row_tiles, c)].reshape(n, LANES) for c in range(chunks)], axis=1)


def _in_proj_kernel(x_ref, anw_ref, w_ref, qnw_ref, knw_ref, ones_ref, cos_ref, sa_ref, sb_ref, lnw_ref, lnb_ref,
                    *refs):
    qkv_refs = refs[:3 * N_GROUPS]
    zu_ref, zv_ref, stage = refs[3 * N_GROUPS:]
    tm = x_ref.shape[0]
    h = _rms(x_ref[...], anw_ref[...]).astype(BF16)
    reps = ATT_WIDTH // LANES
    cos = jnp.concatenate([cos_ref[...]] * reps, axis=1)
    sa = jnp.concatenate([sa_ref[...]] * reps, axis=1)
    sb = jnp.concatenate([sb_ref[...]] * reps, axis=1)
    ones = ones_ref[...]

    def head_norm_rope(t, nw):
        sq = (t * t).astype(BF16)
        parts = []
        for j in range(ATT_WIDTH // MXU_DIM):
            sl = slice(j * MXU_DIM, (j + 1) * MXU_DIM)
            parts.append(_dot(sq[:, sl], ones))
        ssum = jnp.concatenate(parts, axis=1)
        t = t * lax.rsqrt(ssum * (1.0 / HEAD_DIM) + NORM_EPS) * nw
        half = ROT_DIM // 2
        return t * cos + pltpu.roll(t, ATT_WIDTH - half, 1) * sa + pltpu.roll(t, half, 1) * sb

    def emit(which, t):
        n_chunks = ATT_WIDTH // LANES
        per_group = GROUP_WIDTH // LANES
        for c in range(n_chunks):
            stage[which * n_chunks + c] = t[:, c * LANES:(c + 1) * LANES]
        for g, (_, dil) in enumerate(ATT_WINDOWS):
            out = qkv_refs[which * N_GROUPS + g]
            first = which * n_chunks + g * per_group
            for r in range(dil):
                rows = pl.ds(r, tm // dil, stride=dil)
                out[r] = jnp.concatenate([stage[first + c, rows, :] for c in range(per_group)], axis=1).astype(BF16)

    q = head_norm_rope(_dot(h, w_ref[:, 0:ATT_WIDTH]), qnw_ref[...])
    emit(0, q * (HEAD_DIM ** -0.5))
    emit(1, head_norm_rope(_dot(h, w_ref[:, ATT_WIDTH:2 * ATT_WIDTH]), knw_ref[...]))
    emit(2, _dot(h, w_ref[:, 2 * ATT_WIDTH:3 * ATT_WIDTH]))
    z = _dot(h, w_ref[:, 3 * ATT_WIDTH:3 * ATT_WIDTH + 2 * SGU_WIDTH])
    zu_ref[...] = _gelu(z[:, :SGU_WIDTH])
    gv = _gelu(z[:, SGU_WIDTH:])
    mu = jnp.mean(gv, axis=-1, keepdims=True)
    var = jnp.mean(jnp.square(gv - mu), axis=-1, keepdims=True)
    zv_ref[...] = ((gv - mu) * lax.rsqrt(var + LN_EPS) * lnw_ref[...] + lnb_ref[...]).astype(BF16)


def _phase_spec(tm, dil, n_seq_tiles, width=GROUP_WIDTH):
    return pl.BlockSpec((None, dil, tm // dil, width), lambda i: (i // n_seq_tiles, 0, i % n_seq_tiles, 0))


def _in_proj(x2, batch, seq, p):
    T, D = x2.shape
    tm = TOKEN_TILE
    n_seq_tiles = seq // tm
    in_width = p["w_in"].shape[1]
    const = lambda shape: pl.BlockSpec(shape, lambda i: (0,) * len(shape))
    rope = pl.BlockSpec((tm, LANES), lambda i: (i % n_seq_tiles, 0))
    row = lambda w: pl.BlockSpec((tm, w), lambda i: (i, 0))
    qkv_specs = [_phase_spec(tm, dil, n_seq_tiles) for _ in range(3) for _, dil in ATT_WINDOWS]
    qkv_shapes = [jax.ShapeDtypeStruct((batch, dil, seq // dil, GROUP_WIDTH), BF16)
                  for _ in range(3) for _, dil in ATT_WINDOWS]
    outs = pl.pallas_call(
        _in_proj_kernel,
        grid=(T // tm,),
        in_specs=[row(D), const((1, D)), const((D, in_width)), const((1, ATT_WIDTH)), const((1, ATT_WIDTH)),
                  const((MXU_DIM, MXU_DIM)), rope, rope, rope, const((1, SGU_WIDTH)), const((1, SGU_WIDTH))],
        out_specs=qkv_specs + [row(SGU_WIDTH), row(SGU_WIDTH)],
        out_shape=qkv_shapes + [jax.ShapeDtypeStruct((T, SGU_WIDTH), F32), jax.ShapeDtypeStruct((T, SGU_WIDTH), BF16)],
        scratch_shapes=[pltpu.VMEM((3 * ATT_WIDTH // LANES, tm, LANES), F32)],
        compiler_params=pltpu.CompilerParams(dimension_semantics=("arbitrary",), vmem_limit_bytes=VMEM_LIMIT),
        name="in_proj",
    )(x2, p["attn_norm_w"], p["w_in"], p["q_norm_w"], p["k_norm_w"], p["head_ones"],
      *p["rope"][seq], p["sgu_ln_w"], p["sgu_ln_b"])
    q, k, v = (outs[i * N_GROUPS:(i + 1) * N_GROUPS] for i in range(3))
    return q, k, v, outs[-2], outs[-1]


def _attn_kernel(q_ref, kp_ref, km_ref, kn_ref, vp_ref, vm_ref, vn_ref, o_ref, lse_ref, *, sub_len, steps):
    lq = q_ref.shape[0]
    sq = min(lq, ATT_SUB_Q)
    lk = sq + 2 * steps
    nh = HEADS_PER_GROUP
    lb = pl.program_id(2)
    kk = jnp.concatenate([kp_ref[...], km_ref[...], kn_ref[...]], axis=0)
    vv = jnp.concatenate([vp_ref[...], vm_ref[...], vn_ref[...]], axis=0)
    head = lax.broadcasted_iota(I32, (1, GROUP_WIDTH), 1) // HEAD_DIM
    lse_head = lax.broadcasted_iota(I32, (1, LANES), 1) // LSE_LANES
    hm_f = [(head == h).astype(F32) for h in range(nh)]
    hm_b = [m.astype(BF16) for m in hm_f]
    qi = lax.broadcasted_iota(I32, (sq, 1), 0)
    kj = lax.broadcasted_iota(I32, (sq, lk), 1)
    for j in range(lq // sq):
        first = lb * lq + j * sq
        lo = jnp.maximum(qi, steps - first)
        hi = jnp.minimum(qi + 2 * steps, sub_len - 1 + steps - first)
        mask = ((kj - lo).astype(jnp.uint32) <= (hi - lo).astype(jnp.uint32))[None]
        qj = q_ref[j * sq:(j + 1) * sq, :]
        qs = jnp.concatenate([qj * hm_b[h] for h in range(nh)], axis=0)
        s = _dot_nt(qs, kk[j * sq:j * sq + lk]).reshape(nh, sq, lk)
        s = jnp.where(mask, s, NEG_INF)
        m = jnp.max(s, axis=-1, keepdims=True)
        pr = jnp.exp(s - m)
        den = jnp.sum(pr, axis=-1, keepdims=True)
        oh = _dot(pr.reshape(nh * sq, lk).astype(BF16), vv[j * sq:j * sq + lk]).reshape(nh, sq, GROUP_WIDTH) / den
        lh = m + jnp.log(den)
        o, lse = oh[0], jnp.broadcast_to(lh[0], (sq, LANES))
        for h in range(1, nh):
            o = jnp.where(head == h, oh[h], o)
            lse = jnp.where(lse_head == h, lh[h], lse)
        o_ref[j * sq:(j + 1) * sq, :] = o.astype(BF16)
        lse_ref[j * sq:(j + 1) * sq, :] = lse


def _attention(q, k, v, group):
    window, dil = ATT_WINDOWS[group]
    steps = window // (2 * dil)
    batch, _, sub_len, _ = q.shape
    assert sub_len % steps == 0
    lq = min(ATT_Q_BLOCK, sub_len)
    assert lq % steps == 0 and sub_len % lq == 0
    per_q = lq // steps
    n_halo = sub_len // steps
    main = pl.BlockSpec((None, None, lq, GROUP_WIDTH), lambda b, r, i: (b, r, i, 0))
    prev = pl.BlockSpec((None, None, steps, GROUP_WIDTH), lambda b, r, i: (b, r, jnp.maximum(i * per_q - 1, 0), 0))
    nxt = pl.BlockSpec((None, None, steps, GROUP_WIDTH),
                       lambda b, r, i: (b, r, jnp.minimum((i + 1) * per_q, n_halo - 1), 0))
    return pl.pallas_call(
        functools.partial(_attn_kernel, sub_len=sub_len, steps=steps),
        grid=(batch, dil, sub_len // lq),
        in_specs=[main, prev, main, nxt, prev, main, nxt],
        out_specs=[main, pl.BlockSpec((None, None, lq, LANES), lambda b, r, i: (b, r, i, 0))],
        out_shape=[jax.ShapeDtypeStruct(q.shape, BF16), jax.ShapeDtypeStruct(q.shape[:3] + (LANES,), F32)],
        compiler_params=pltpu.CompilerParams(dimension_semantics=("arbitrary",) * 3, vmem_limit_bytes=VMEM_LIMIT),
        name=f"attention_g{group}",
    )(q, k, k, k, v, v, v)


def _post_mix_kernel(x_ref, *refs):
    o_refs = refs[:N_GROUPS]
    l_refs = refs[N_GROUPS:2 * N_GROUPS]
    (zu_ref, zv_ref, sguw_ref, sgub_ref, aonw_ref, sonw_ref, wout_ref, fnw_ref, rwt_ref, rb_ref, tri_ref,
     x1_ref, h2_hbm, gates_ref, eidx_ref, rank_ref, cnt_ref, stage, hstage, hsem) = refs[2 * N_GROUPS:]
    tm = x_ref.shape[0]

    def token_order(ref, slot):
        dil = ref.shape[0]
        if dil == 1:
            return ref[0].astype(F32)
        n_chunks = ref.shape[2] // LANES
        per_group = GROUP_WIDTH // LANES
        for r in range(dil):
            for c in range(n_chunks):
                stage[slot * per_group + c, pl.ds(r, tm // dil, stride=dil), :] = (
                    ref[r, :, c * LANES:(c + 1) * LANES].astype(F32))
        return jnp.concatenate([stage[slot * per_group + c] for c in range(n_chunks)], axis=1)

    def per_column(w):
        lane = lax.broadcasted_iota(I32, (1, LANES), 1)
        r1 = pltpu.roll(w, LSE_LANES, 1)
        r2 = pltpu.roll(w, 2 * LSE_LANES, 1)
        r3 = pltpu.roll(w, 3 * LSE_LANES, 1)
        low = jnp.where(lane < LSE_LANES, w, jnp.where(lane < 3 * LSE_LANES, r1, r2))
        high = jnp.where(lane < LSE_LANES, r2, jnp.where(lane < 3 * LSE_LANES, r3, w))
        return jnp.concatenate([low, high], axis=1)

    o = [token_order(ref, g) for g, ref in enumerate(o_refs)]
    l = [token_order(ref, N_GROUPS + g) for g, ref in enumerate(l_refs)]
    lmax = jnp.maximum(jnp.maximum(l[0], l[1]), l[2])
    e = [jnp.exp(lg - lmax) for lg in l]
    esum = e[0] + e[1] + e[2]
    att = (per_column(e[0] / esum) * o[0] + per_column(e[1] / esum) * o[1] + per_column(e[2] / esum) * o[2])
    att_n = _rms(att, aonw_ref[...]).astype(BF16)
    cgrp = lax.broadcasted_iota(I32, (1, SGU_WIDTH), 1) // SGU_GROUP_DIM
    sguw = sguw_ref[...]
    gates = []
    for c in range(tm // SGU_CHUNK):
        r = _dot(sguw, zv_ref[c * SGU_CHUNK:(c + 1) * SGU_CHUNK, :])
        g = sgub_ref[...]
        for grp in range(SGU_GROUPS):
            g = g + r[grp * SGU_CHUNK:(grp + 1) * SGU_CHUNK, :] * (cgrp == grp).astype(F32)
        gates.append(g)
    sgu = zu_ref[...] * jnp.concatenate(gates, axis=0)
    sgu_n = _rms(sgu, sonw_ref[...]).astype(BF16)
    x1 = x_ref[...] + _dot(att_n, wout_ref[0:GROUP_WIDTH, :]) + _dot(sgu_n, wout_ref[GROUP_WIDTH:, :])
    x1_ref[...] = x1
    h2 = _rms(x1, fnw_ref[...])
    step = pl.program_id(0)
    slot = step % 2
    bt = tm // SUBLANES

    def h2_copies(tile, s):
        return [pltpu.make_async_copy(hstage.at[s, :, :, j, :], h2_hbm.at[pl.ds(tile * bt, bt), j], hsem.at[s])
                for j in range(SUBLANES)]

    _to_tiled(hstage, (slot,), _pack_pairs(h2))
    for c in h2_copies(step, slot):
        c.start()

    @pl.when(step > 0)
    def _():
        for c in h2_copies(step - 1, 1 - slot):
            c.wait()
    logits = _dot_nt(rwt_ref[...], h2.astype(BF16)) + rb_ref[...]
    eiota = lax.broadcasted_iota(I32, (N_EXPERTS, tm), 0)
    vals, idxs = [], []
    for _ in range(TOP_K):
        m = jnp.max(logits, axis=0, keepdims=True)
        idx = jnp.min(jnp.where(logits == m, eiota, N_EXPERTS), axis=0, keepdims=True)
        vals.append(m)
        idxs.append(idx)
        logits = jnp.where(eiota == idx, -jnp.inf, logits)
    exps = [jnp.exp(v - vals[0]) for v in vals]
    den = exps[0] + exps[1] + exps[2] + exps[3]
    gates_ref[...] = jnp.concatenate([ex / den for ex in exps], axis=0)
    eidx_ref[...] = jnp.concatenate(idxs, axis=0)
    onehots = [(eiota == idx).astype(F32) for idx in idxs]
    chosen = onehots[0] + onehots[1] + onehots[2] + onehots[3]
    before = _dot(chosen.astype(BF16), tri_ref[...])
    rank_ref[...] = jnp.concatenate(
        [jnp.sum(oh * before, axis=0, keepdims=True) for oh in onehots], axis=0).astype(I32)
    cnt_ref[...] = jnp.broadcast_to(jnp.sum(chosen, axis=1, keepdims=True), (N_EXPERTS, LANES)).astype(I32)

    @pl.when(step == pl.num_programs(0) - 1)
    def _():
        for c in h2_copies(step, slot):
            c.wait()


def _post_mix(x2, seq, o, lse, zu, zv, p):
    T, D = x2.shape
    tm = TOKEN_TILE
    n_tiles = T // tm
    n_seq_tiles = seq // tm
    chunks = D // (2 * LANES)
    const = lambda shape: pl.BlockSpec(shape, lambda i: (0,) * len(shape))
    row = lambda w: pl.BlockSpec((tm, w), lambda i: (i, 0))
    colt = pl.BlockSpec((TOP_K, tm), lambda i: (0, i))
    phase = [_phase_spec(tm, dil, n_seq_tiles) for _, dil in ATT_WINDOWS]
    lse_phase = [_phase_spec(tm, dil, n_seq_tiles, LANES) for _, dil in ATT_WINDOWS]
    return pl.pallas_call(
        _post_mix_kernel,
        grid=(n_tiles,),
        in_specs=[row(D)] + phase + lse_phase + [row(SGU_WIDTH), row(SGU_WIDTH),
                  const((SGU_GROUPS * SGU_CHUNK, SGU_CHUNK)), const((SGU_CHUNK, SGU_WIDTH)),
                  const((1, GROUP_WIDTH)), const((1, SGU_WIDTH)), const((GROUP_WIDTH + SGU_WIDTH, D)),
                  const((1, D)), const((N_EXPERTS, D)), const((N_EXPERTS, 1)), const((tm, tm))],
        out_specs=[row(D), pl.BlockSpec(memory_space=pl.ANY),
                   colt, colt, colt, pl.BlockSpec((None, N_EXPERTS, LANES), lambda i: (i, 0, 0))],
        out_shape=[jax.ShapeDtypeStruct((T, D), F32),
                   jax.ShapeDtypeStruct((T // SUBLANES, SUBLANES, chunks, LANES), F32),
                   jax.ShapeDtypeStruct((TOP_K, T), F32), jax.ShapeDtypeStruct((TOP_K, T), I32),
                   jax.ShapeDtypeStruct((TOP_K, T), I32), jax.ShapeDtypeStruct((n_tiles, N_EXPERTS, LANES), I32)],
        scratch_shapes=[pltpu.VMEM((2 * N_GROUPS * GROUP_WIDTH // LANES, tm, LANES), F32),
                        pltpu.VMEM((2, tm // SUBLANES, chunks, SUBLANES, LANES), F32), pltpu.SemaphoreType.DMA((2,))],
        compiler_params=pltpu.CompilerParams(dimension_semantics=("arbitrary",), vmem_limit_bytes=VMEM_LIMIT),
        name="post_mix",
    )(x2, *o, *lse, zu, zv, p["sgu_w"], p["sgu_b"], p["att_out_norm_w"],
      p["sgu_out_norm_w"], p["w_out"], p["ffn_norm_w"], p["router_wt"], p["router_b"], p["tri"])


def _sc_workers():
    info = plsc.get_sparse_core_info()
    return info.num_cores, info.num_cores * info.num_subcores


def _dispatch_sc(h2p, pos, pad_pos, n_rows):
    T, chunks, _ = h2p.shape
    n_cores, n_workers = _sc_workers()
    win = SC_WINDOW
    per_worker = T // n_workers
    pad_per_worker = pad_pos.shape[0] // n_workers
    assert per_worker % win == 0 and pad_per_worker % win == 0
    mesh = plsc.VectorSubcoreMesh(core_axis_name="c", subcore_axis_name="s")

    @functools.partial(
        pl.kernel, mesh=mesh, out_type=jax.ShapeDtypeStruct((n_rows, chunks, LANES), F32),
        scratch_types=[pltpu.VMEM((win,), I32), pltpu.VMEM((win, chunks, LANES), F32)],
        name="dispatch_sc")
    def scatter_rows(h_hbm, pos_hbm, pad_hbm, zeros_hbm, xs_hbm, idx_v, rows_v):
        worker = lax.axis_index("s") * n_cores + lax.axis_index("c")

        @pl.loop(0, per_worker // win)
        def _(w):
            t0 = worker * per_worker + w * win
            pltpu.sync_copy(h_hbm.at[pl.ds(t0, win)], rows_v)
            for k in range(TOP_K):
                pltpu.sync_copy(pos_hbm.at[pl.ds(k * T + t0, win)], idx_v)
                pltpu.sync_copy(rows_v, xs_hbm.at[idx_v])

        pltpu.sync_copy(zeros_hbm, rows_v)

        @pl.loop(0, pad_per_worker // win)
        def _(w):
            pltpu.sync_copy(pad_hbm.at[pl.ds(worker * pad_per_worker + w * win, win)], idx_v)
            pltpu.sync_copy(rows_v, xs_hbm.at[idx_v])

    return scatter_rows(h2p, pos, pad_pos, jnp.zeros((win, chunks, LANES), F32))


def _gather_sc(ys, pos):
    _, chunks, _ = ys.shape
    n_assign = pos.shape[0]
    n_cores, n_workers = _sc_workers()
    win = SC_WINDOW
    per_worker = n_assign // n_workers
    assert per_worker % win == 0
    mesh = plsc.VectorSubcoreMesh(core_axis_name="c", subcore_axis_name="s")

    @functools.partial(
        pl.kernel, mesh=mesh, out_type=jax.ShapeDtypeStruct((n_assign, chunks, LANES), F32),
        scratch_types=[pltpu.VMEM((win,), I32), pltpu.VMEM((win, chunks, LANES), F32)],
        name="gather_sc")
    def gather_rows(ys_hbm, pos_hbm, out_hbm, idx_v, rows_v):
        worker = lax.axis_index("s") * n_cores + lax.axis_index("c")

        @pl.loop(0, per_worker // win)
        def _(w):
            a0 = worker * per_worker + w * win
            pltpu.sync_copy(pos_hbm.at[pl.ds(a0, win)], idx_v)
            pltpu.sync_copy(ys_hbm.at[idx_v], rows_v)
            pltpu.sync_copy(rows_v, out_hbm.at[pl.ds(a0, win)])

    return gather_rows(ys, pos)


def _experts_kernel(blk_e_ref, n_used_ref, xs_hbm, wg_ref, bg_ref, wu_ref, bu_ref, wd_ref, bd_ref, ys_hbm,
                    xbuf, obuf, wg_b, wu_b, wd_b, in_sem, out_sem):
    bt = xbuf.shape[1]
    i = pl.program_id(0)
    n = pl.num_programs(0)
    n_used = n_used_ref[0]
    slot = i % 2

    def in_copies(blk, s):
        return [pltpu.make_async_copy(xs_hbm.at[pl.ds(blk * bt, bt), j], xbuf.at[s, :, :, j, :], in_sem.at[s])
                for j in range(SUBLANES)]

    def out_copies(blk, s):
        return [pltpu.make_async_copy(obuf.at[s, :, :, j, :], ys_hbm.at[pl.ds(blk * bt, bt), j], out_sem.at[s])
                for j in range(SUBLANES)]

    @pl.when(i == 0)
    def _():
        for c in in_copies(0, 0):
            c.start()

    @pl.when(i + 1 < n_used)
    def _():
        for c in in_copies(i + 1, 1 - slot):
            c.start()

    @pl.when(i >= 2)
    def _():
        for c in out_copies(i - 2, slot):
            c.wait()

    @pl.when(i >= n_used)
    def _():
        obuf[slot] = jnp.zeros(obuf.shape[1:], F32)

    @pl.when(i < n_used)
    def _():
        for c in in_copies(i, slot):
            c.wait()
        @pl.when((i == 0) | (blk_e_ref[i] != blk_e_ref[jnp.maximum(i - 1, 0)]))
        def _():
            wg_b[...] = wg_ref[...].astype(BF16)
            wu_b[...] = wu_ref[...].astype(BF16)
            wd_b[...] = wd_ref[...].astype(BF16)

        x = _unpack_pairs(_from_tiled(xbuf, (slot,), slice(0, bt))).astype(BF16)
        g = _dot(x, wg_b[...]) + bg_ref[...]
        u = _dot(x, wu_b[...]) + bu_ref[...]
        g = jnp.minimum(g, SWIGLU_LIMIT)
        u = jnp.clip(u, -SWIGLU_LIMIT, SWIGLU_LIMIT)
        act = (u + 1.0) * (g * jax.nn.sigmoid(SWIGLU_ALPHA * g))
        _to_tiled(obuf, (slot,), _pack_pairs(_dot(act.astype(BF16), wd_b[...]) + bd_ref[...]))

    for c in out_copies(i, slot):
        c.start()

    @pl.when(i == n - 1)
    def _():
        for c in out_copies(i - 1, 1 - slot) + out_copies(i, slot):
            c.wait()


def _experts(xs, blk_e, n_used, p):
    n_rows, chunks, _ = xs.shape
    D = 2 * chunks * LANES
    bm = MOE_BLOCK
    d_ff = p["w_gate"].shape[2]
    assert n_rows // bm >= 2
    wspec = lambda a, b: pl.BlockSpec((None, a, b), lambda i, be, nu: (be[i], 0, 0))
    by_tile = (n_rows // SUBLANES, SUBLANES, chunks, LANES)
    block_buf = pltpu.VMEM((2, bm // SUBLANES, chunks, SUBLANES, LANES), F32)
    ys = pl.pallas_call(
        _experts_kernel,
        grid_spec=pltpu.PrefetchScalarGridSpec(
            num_scalar_prefetch=2,
            grid=(n_rows // bm,),
            in_specs=[pl.BlockSpec(memory_space=pl.ANY), wspec(D, d_ff), wspec(1, d_ff), wspec(D, d_ff),
                      wspec(1, d_ff), wspec(d_ff, D), wspec(1, D)],
            out_specs=pl.BlockSpec(memory_space=pl.ANY),
            scratch_shapes=[block_buf, block_buf, pltpu.VMEM((D, d_ff), BF16), pltpu.VMEM((D, d_ff), BF16),
                            pltpu.VMEM((d_ff, D), BF16), pltpu.SemaphoreType.DMA((2,)), pltpu.SemaphoreType.DMA((2,))],
        ),
        out_shape=jax.ShapeDtypeStruct(by_tile, F32),
        compiler_params=pltpu.CompilerParams(dimension_semantics=("arbitrary",), vmem_limit_bytes=EXPERTS_VMEM_LIMIT),
        name="experts",
    )(blk_e, n_used, xs.reshape(by_tile), p["w_gate"], p["b_gate"], p["w_up"], p["b_up"], p["w_down"], p["b_down"])
    return ys.reshape(n_rows, chunks, LANES)


def _combine_kernel(gates_ref, x1_ref, yg_hbm, out_ref, buf, sem):
    tm = x1_ref.shape[0]
    bt = tm // SUBLANES
    j = pl.program_id(0)
    n = pl.num_programs(0) - 1
    tiles_per_k = yg_hbm.shape[0] // TOP_K

    def copies(tile, s):
        return [pltpu.make_async_copy(yg_hbm.at[pl.ds(k * tiles_per_k + tile * bt, bt), r],
                                      buf.at[s, pl.ds(k * bt, bt), :, r, :], sem.at[s])
                for k in range(TOP_K) for r in range(SUBLANES)]

    @pl.when(j < n)
    def _():
        for c in copies(j, j % 2):
            c.start()

    @pl.when(j >= 1)
    def _():
        done = 1 - j % 2
        for c in copies(j - 1, done):
            c.wait()
        y = gates_ref[:, 0:1] * _unpack_pairs(_from_tiled(buf, (done,), slice(0, bt)))
        for k in range(1, TOP_K):
            y = y + gates_ref[:, k:k + 1] * _unpack_pairs(_from_tiled(buf, (done,), slice(k * bt, (k + 1) * bt)))
        out_ref[...] = x1_ref[...] + y


def _combine(x1, gates_tok, yg):
    T, D = x1.shape
    tm = ROW_TILE
    n = T // tm
    chunks = D // (2 * LANES)
    lag = lambda i: (jnp.maximum(i - 1, 0), 0)
    return pl.pallas_call(
        _combine_kernel,
        grid=(n + 1,),
        in_specs=[pl.BlockSpec((tm, TOP_K), lag), pl.BlockSpec((tm, D), lag), pl.BlockSpec(memory_space=pl.ANY)],
        out_specs=pl.BlockSpec((tm, D), lag),
        out_shape=jax.ShapeDtypeStruct((T, D), F32),
        scratch_shapes=[pltpu.VMEM((2, TOP_K * tm // SUBLANES, chunks, SUBLANES, LANES), F32),
                        pltpu.SemaphoreType.DMA((2,))],
        compiler_params=pltpu.CompilerParams(dimension_semantics=("arbitrary",), vmem_limit_bytes=VMEM_LIMIT),
        name="combine",
    )(gates_tok, x1, yg.reshape(TOP_K * T // SUBLANES, SUBLANES, chunks, LANES))


def _route(eidx, rank, tile_counts):
    T = eidx.shape[1]
    bm = MOE_BLOCK
    n_tiles = T // TOKEN_TILE
    tc = tile_counts[:, :, 0]
    counts = jnp.sum(tc, axis=0)
    tile_off = jnp.cumsum(tc, axis=0) - tc
    pcounts = (counts + bm - 1) // bm * bm
    pends = jnp.cumsum(pcounts)
    pstarts = pends - pcounts
    base = pstarts[None, :] + tile_off
    sel = eidx.reshape(TOP_K, n_tiles, 1, TOKEN_TILE) == jnp.arange(N_EXPERTS, dtype=I32)[None, None, :, None]
    pos = jnp.sum(jnp.where(sel, base[None, :, :, None], 0), axis=2).reshape(TOP_K, T) + rank
    n_blocks = (TOP_K * T) // bm + N_EXPERTS
    n_rows = n_blocks * bm
    blk_start = jnp.arange(n_blocks, dtype=I32) * bm
    blk_e = jnp.minimum(jnp.sum((pends[None, :] <= blk_start[:, None]).astype(I32), axis=1), N_EXPERTS - 1)
    n_used = (pends[-1] // bm).astype(I32).reshape(1)
    first_free = jnp.concatenate([pstarts + counts, pends[-1:]])
    free_off = jnp.cumsum(jnp.concatenate([jnp.zeros((1,), counts.dtype), pcounts - counts]))
    j = jnp.arange(n_rows - TOP_K * T, dtype=I32)[:, None]
    seg_id = jnp.sum((free_off[None, 1:] <= j).astype(I32), axis=1, keepdims=True)
    onehot = seg_id == jnp.arange(N_EXPERTS + 1, dtype=I32)[None, :]
    pad_pos = jnp.sum(jnp.where(onehot, (first_free - free_off)[None, :] + j, 0), axis=1).astype(I32)
    return pos.reshape(-1).astype(I32), pad_pos, blk_e.astype(I32), n_used, n_rows


def _layer(x, p):
    batch, seq, D = x.shape
    x2 = x.reshape(batch * seq, D)
    q, k, v, zu, zv = _in_proj(x2, batch, seq, p)
    o, lse = zip(*[_attention(q[g], k[g], v[g], g) for g in range(N_GROUPS)])
    x1, h2p, gates, eidx, rank, tile_counts = _post_mix(x2, seq, o, lse, zu, zv, p)
    pos, pad_pos, blk_e, n_used, n_rows = _route(eidx, rank, tile_counts)
    xs = _dispatch_sc(h2p.reshape(batch * seq, D // (2 * LANES), LANES), pos, pad_pos, n_rows)
    ys = _experts(xs, blk_e, n_used, p)
    out = _combine(x1, gates.T, _gather_sc(ys, pos))
    return out.reshape(batch, seq, D)


def _rope_tables(seq):
    half = ROT_DIM // 2
    inv_freq = jnp.power(ROPE_THETA, -2.0 * jnp.arange(half, dtype=F32) / ROT_DIM)
    ang = jnp.arange(seq, dtype=F32)[:, None] * inv_freq[None, :]
    cos, sin = jnp.cos(ang), jnp.sin(ang)
    zeros = jnp.zeros((seq, HEAD_DIM - ROT_DIM), F32)
    zh = jnp.zeros((seq, half), F32)
    cos_t = jnp.concatenate([cos, cos, zeros + 1.0], axis=1)
    sa = jnp.concatenate([-sin, zh, zeros], axis=1)
    sb = jnp.concatenate([zh, sin, zeros], axis=1)
    rep = LANES // HEAD_DIM
    return tuple(jnp.tile(t, (1, rep)) for t in (cos_t, sa, sb))


def kernel(x_prompt, x_sample, attn_norm_w, w_in, q_norm_w, k_norm_w, sgu_ln_w, sgu_ln_b, sgu_w, sgu_b,
           att_out_norm_w, sgu_out_norm_w, w_out, ffn_norm_w, router_w, router_b,
           w_gate, b_gate, w_up, b_up, w_down, b_down):
    depth = w_in.shape[0]
    n_heads = ATT_WIDTH // HEAD_DIM
    blk = np.arange(MXU_DIM) // HEAD_DIM
    tri = np.arange(TOKEN_TILE)
    xa, xb = x_prompt, x_sample
    rope = {x.shape[1]: _rope_tables(x.shape[1]) for x in (xa, xb)}
    for l in range(depth):
        shared = dict(
            attn_norm_w=attn_norm_w[l][None], w_in=w_in[l].astype(BF16),
            q_norm_w=jnp.tile(q_norm_w[l], n_heads)[None], k_norm_w=jnp.tile(k_norm_w[l], n_heads)[None],
            head_ones=jnp.asarray(blk[:, None] == blk[None, :], BF16),
            sgu_ln_w=sgu_ln_w[l][None], sgu_ln_b=sgu_ln_b[l][None],
            sgu_w=sgu_w[l].reshape(SGU_GROUPS * SGU_CHUNK, SGU_CHUNK).astype(BF16),
            sgu_b=jnp.repeat(sgu_b[l].T, SGU_GROUP_DIM, axis=1),
            att_out_norm_w=att_out_norm_w[l][None], sgu_out_norm_w=sgu_out_norm_w[l][None],
            w_out=w_out[l].astype(BF16), ffn_norm_w=ffn_norm_w[l][None],
            router_wt=router_w[l].T.astype(BF16), router_b=router_b[l][:, None],
            tri=jnp.asarray(tri[:, None] < tri[None, :], BF16),
            w_gate=w_gate[l], b_gate=b_gate[l][:, None, :],
            w_up=w_up[l], b_up=b_up[l][:, None, :],
            w_down=w_down[l], b_down=b_down[l][:, None, :],
        )
        xa, xb = (_layer(x, dict(shared, rope=rope)) for x in (xa, xb))
    return xa, xb
```

```python
import functools

import numpy as np
import jax
import jax.numpy as jnp
from jax import lax
from jax.experimental import pallas as pl
from jax.experimental.pallas import tpu as pltpu
from jax.experimental.pallas import tpu_sc as plsc

F32 = jnp.float32
BF16 = jnp.bfloat16
I32 = jnp.int32

HEAD_DIM = 64
ATT_WINDOWS = ((128, 1), (512, 4), (2048, 16))
N_GROUPS = len(ATT_WINDOWS)
HEADS_PER_GROUP = 4
GROUP_WIDTH = HEADS_PER_GROUP * HEAD_DIM
ATT_WIDTH = N_GROUPS * GROUP_WIDTH
SGU_WIDTH = 256
SGU_GROUP_DIM = 64
SGU_GROUPS = SGU_WIDTH // SGU_GROUP_DIM
SGU_CHUNK = 128
ROT_DIM = HEAD_DIM // 4
ROPE_THETA = 500000.0
N_EXPERTS = 32
TOP_K = 4
SWIGLU_LIMIT = 7.0
SWIGLU_ALPHA = 1.702
NORM_EPS = 1e-6
LN_EPS = 1e-5
NEG_INF = -1e30

LANES = 128
SUBLANES = 8
MXU_DIM = 256
TOKEN_TILE = 512
ATT_Q_BLOCK = 1024
ATT_SUB_Q = 128
LSE_LANES = LANES // HEADS_PER_GROUP
MOE_BLOCK = 512
ROW_TILE = 512
SC_WINDOW = 64
VMEM_LIMIT = 48 * 1024 * 1024
EXPERTS_VMEM_LIMIT = 56 * 1024 * 1024


def _dot(a, b):
    return jnp.dot(a, b, preferred_element_type=F32)


def _dot_nt(a, b, precision=None):
    return lax.dot_general(a, b, (((1,), (1,)), ((), ())), precision=precision, preferred_element_type=F32)


def _rms(x, w):
    return x * lax.rsqrt(jnp.mean(x * x, axis=-1, keepdims=True) + NORM_EPS) * w


def _gelu(x):
    return 0.5 * x * (1.0 + lax.erf(x * np.float32(np.sqrt(0.5))))


def _pack_pairs(x):
    w = x.shape[1] // 2
    lo = lax.bitcast_convert_type(x[:, :w].astype(BF16).astype(F32), jnp.uint32) >> 16
    hi = lax.bitcast_convert_type(x[:, w:].astype(BF16).astype(F32), jnp.uint32) & jnp.uint32(0xFFFF0000)
    return lax.bitcast_convert_type(lo | hi, F32)


def _unpack_pairs(words):
    u = lax.bitcast_convert_type(words, jnp.uint32)
    lo = lax.bitcast_convert_type(u << 16, F32)
    hi = lax.bitcast_convert_type(u & jnp.uint32(0xFFFF0000), F32)
    return jnp.concatenate([lo, hi], axis=1)


def _to_tiled(ref, idx, value):
    rows = value.shape[0]
    for c in range(value.shape[1] // LANES):
        ref[idx + (slice(None), c)] = value[:, c * LANES:(c + 1) * LANES].reshape(rows // SUBLANES, SUBLANES, LANES)


def _from_tiled(ref, idx, row_tiles):
    chunks = ref.shape[-3]
    n = (row_tiles.stop - row_tiles.start) * SUBLANES
    return jnp.concatenate([ref[idx + (row_tiles, c)].reshape(n, LANES) for c in range(chunks)], axis=1)


def _in_proj_kernel(x_ref, anw_ref, w_ref, qnw_ref, knw_ref, ones_ref, cos_ref, sa_ref, sb_ref, lnw_ref, lnb_ref,
                    *refs):
    qkv_refs = refs[:3 * N_GROUPS]
    zu_ref, zv_ref, stage = refs[3 * N_GROUPS:]
    tm = x_ref.shape[0]
    h = _rms(x_ref[...], anw_ref[...]).astype(BF16)
    reps = ATT_WIDTH // LANES
    cos = jnp.concatenate([cos_ref[...]] * reps, axis=1)
    sa = jnp.concatenate([sa_ref[...]] * reps, axis=1)
    sb = jnp.concatenate([sb_ref[...]] * reps, axis=1)
    ones = ones_ref[...]

    def head_norm_rope(t, nw):
        sq = (t * t).astype(BF16)
        parts = []
        for j in range(ATT_WIDTH // MXU_DIM):
            sl = slice(j * MXU_DIM, (j + 1) * MXU_DIM)
            parts.append(_dot(sq[:, sl], ones))
        ssum = jnp.concatenate(parts, axis=1)
        t = t * lax.rsqrt(ssum * (1.0 / HEAD_DIM) + NORM_EPS) * nw
        half = ROT_DIM // 2
        return t * cos + pltpu.roll(t, ATT_WIDTH - half, 1) * sa + pltpu.roll(t, half, 1) * sb

    def emit(which, t):
        n_chunks = ATT_WIDTH // LANES
        per_group = GROUP_WIDTH // LANES
        for c in range(n_chunks):
            stage[which * n_chunks + c] = t[:, c * LANES:(c + 1) * LANES]
        for g, (_, dil) in enumerate(ATT_WINDOWS):
            out = qkv_refs[which * N_GROUPS + g]
            first = which * n_chunks + g * per_group
            for r in range(dil):
                rows = pl.ds(r, tm // dil, stride=dil)
                out[r] = jnp.concatenate([stage[first + c, rows, :] for c in range(per_group)], axis=1).astype(BF16)

    q = head_norm_rope(_dot(h, w_ref[:, 0:ATT_WIDTH]), qnw_ref[...])
    emit(0, q * (HEAD_DIM ** -0.5))
    emit(1, head_norm_rope(_dot(h, w_ref[:, ATT_WIDTH:2 * ATT_WIDTH]), knw_ref[...]))
    emit(2, _dot(h, w_ref[:, 2 * ATT_WIDTH:3 * ATT_WIDTH]))
    z = _dot(h, w_ref[:, 3 * ATT_WIDTH:3 * ATT_WIDTH + 2 * SGU_WIDTH])
    zu_ref[...] = _gelu(z[:, :SGU_WIDTH])
    gv = _gelu(z[:, SGU_WIDTH:])
    mu = jnp.mean(gv, axis=-1, keepdims=True)
    var = jnp.mean(jnp.square(gv - mu), axis=-1, keepdims=True)
    zv_ref[...] = ((gv - mu) * lax.rsqrt(var + LN_EPS) * lnw_ref[...] + lnb_ref[...]).astype(BF16)


def _phase_spec(tm, dil, n_seq_tiles, width=GROUP_WIDTH):
    return pl.BlockSpec((None, dil, tm // dil, width), lambda i: (i // n_seq_tiles, 0, i % n_seq_tiles, 0))


def _in_proj(x2, batch, seq, p):
    T, D = x2.shape
    tm = TOKEN_TILE
    n_seq_tiles = seq // tm
    in_width = p["w_in"].shape[1]
    const = lambda shape: pl.BlockSpec(shape, lambda i: (0,) * len(shape))
    rope = pl.BlockSpec((tm, LANES), lambda i: (i % n_seq_tiles, 0))
    row = lambda w: pl.BlockSpec((tm, w), lambda i: (i, 0))
    qkv_specs = [_phase_spec(tm, dil, n_seq_tiles) for _ in range(3) for _, dil in ATT_WINDOWS]
    qkv_shapes = [jax.ShapeDtypeStruct((batch, dil, seq // dil, GROUP_WIDTH), BF16)
                  for _ in range(3) for _, dil in ATT_WINDOWS]
    outs = pl.pallas_call(
        _in_proj_kernel,
        grid=(T // tm,),
        in_specs=[row(D), const((1, D)), const((D, in_width)), const((1, ATT_WIDTH)), const((1, ATT_WIDTH)),
                  const((MXU_DIM, MXU_DIM)), rope, rope, rope, const((1, SGU_WIDTH)), const((1, SGU_WIDTH))],
        out_specs=qkv_specs + [row(SGU_WIDTH), row(SGU_WIDTH)],
        out_shape=qkv_shapes + [jax.ShapeDtypeStruct((T, SGU_WIDTH), F32), jax.ShapeDtypeStruct((T, SGU_WIDTH), BF16)],
        scratch_shapes=[pltpu.VMEM((3 * ATT_WIDTH // LANES, tm, LANES), F32)],
        compiler_params=pltpu.CompilerParams(dimension_semantics=("arbitrary",), vmem_limit_bytes=VMEM_LIMIT),
        name="in_proj",
    )(x2, p["attn_norm_w"], p["w_in"], p["q_norm_w"], p["k_norm_w"], p["head_ones"],
      *p["rope"][seq], p["sgu_ln_w"], p["sgu_ln_b"])
    q, k, v = (outs[i * N_GROUPS:(i + 1) * N_GROUPS] for i in range(3))
    return q, k, v, outs[-2], outs[-1]


def _attn_kernel(q_ref, kp_ref, km_ref, kn_ref, vp_ref, vm_ref, vn_ref, o_ref, lse_ref, *, sub_len, steps):
    lq = q_ref.shape[0]
    sq = min(lq, ATT_SUB_Q)
    lk = sq + 2 * steps
    nh = HEADS_PER_GROUP
    lb = pl.program_id(2)
    kk = jnp.concatenate([kp_ref[...], km_ref[...], kn_ref[...]], axis=0)
    vv = jnp.concatenate([vp_ref[...], vm_ref[...], vn_ref[...]], axis=0)
    head = lax.broadcasted_iota(I32, (1, GROUP_WIDTH), 1) // HEAD_DIM
    lse_head = lax.broadcasted_iota(I32, (1, LANES), 1) // LSE_LANES
    hm_f = [(head == h).astype(F32) for h in range(nh)]
    hm_b = [m.astype(BF16) for m in hm_f]
    qi = lax.broadcasted_iota(I32, (sq, 1), 0)
    kj = lax.broadcasted_iota(I32, (sq, lk), 1)
    for j in range(lq // sq):
        first = lb * lq + j * sq
        lo = jnp.maximum(qi, steps - first)
        hi = jnp.minimum(qi + 2 * steps, sub_len - 1 + steps - first)
        mask = ((kj - lo).astype(jnp.uint32) <= (hi - lo).astype(jnp.uint32))[None]
        qj = q_ref[j * sq:(j + 1) * sq, :]
        qs = jnp.concatenate([qj * hm_b[h] for h in range(nh)], axis=0)
        s = _dot_nt(qs, kk[j * sq:j * sq + lk]).reshape(nh, sq, lk)
        s = jnp.where(mask, s, NEG_INF)
        m = jnp.max(s, axis=-1, keepdims=True)
        pr = jnp.exp(s - m)
        den = jnp.sum(pr, axis=-1, keepdims=True)
        oh = _dot(pr.reshape(nh * sq, lk).astype(BF16), vv[j * sq:j * sq + lk]).reshape(nh, sq, GROUP_WIDTH) / den
        lh = m + jnp.log(den)
        o, lse = oh[0], jnp.broadcast_to(lh[0], (sq, LANES))
        for h in range(1, nh):
            o = jnp.where(head == h, oh[h], o)
            lse = jnp.where(lse_head == h, lh[h], lse)
        o_ref[j * sq:(j + 1) * sq, :] = o.astype(BF16)
        lse_ref[j * sq:(j + 1) * sq, :] = lse


def _attention(q, k, v, group):
    window, dil = ATT_WINDOWS[group]
    steps = window // (2 * dil)
    batch, _, sub_len, _ = q.shape
    assert sub_len % steps == 0
    lq = min(ATT_Q_BLOCK, sub_len)
    assert lq % steps == 0 and sub_len % lq == 0
    per_q = lq // steps
    n_halo = sub_len // steps
    main = pl.BlockSpec((None, None, lq, GROUP_WIDTH), lambda b, r, i: (b, r, i, 0))
    prev = pl.BlockSpec((None, None, steps, GROUP_WIDTH), lambda b, r, i: (b, r, jnp.maximum(i * per_q - 1, 0), 0))
    nxt = pl.BlockSpec((None, None, steps, GROUP_WIDTH),
                       lambda b, r, i: (b, r, jnp.minimum((i + 1) * per_q, n_halo - 1), 0))
    return pl.pallas_call(
        functools.partial(_attn_kernel, sub_len=sub_len, steps=steps),
        grid=(batch, dil, sub_len // lq),
        in_specs=[main, prev, main, nxt, prev, main, nxt],
        out_specs=[main, pl.BlockSpec((None, None, lq, LANES), lambda b, r, i: (b, r, i, 0))],
        out_shape=[jax.ShapeDtypeStruct(q.shape, BF16), jax.ShapeDtypeStruct(q.shape[:3] + (LANES,), F32)],
        compiler_params=pltpu.CompilerParams(dimension_semantics=("arbitrary",) * 3, vmem_limit_bytes=VMEM_LIMIT),
        name=f"attention_g{group}",
    )(q, k, k, k, v, v, v)


def _post_mix_kernel(x_ref, *refs):
    o_refs = refs[:N_GROUPS]
    l_refs = refs[N_GROUPS:2 * N_GROUPS]
    (zu_ref, zv_ref, sguw_ref, sgub_ref, aonw_ref, sonw_ref, wout_ref, fnw_ref, rwt_ref, rb_ref, tri_ref,
     x1_ref, h2_hbm, gates_ref, eidx_ref, rank_ref, cnt_ref, stage, hstage, hsem) = refs[2 * N_GROUPS:]
    tm = x_ref.shape[0]

    def token_order(ref, slot):
        dil = ref.shape[0]
        if dil == 1:
            return ref[0].astype(F32)
        n_chunks = ref.shape[2] // LANES
        per_group = GROUP_WIDTH // LANES
        for r in range(dil):
            for c in range(n_chunks):
                stage[slot * per_group + c, pl.ds(r, tm // dil, stride=dil), :] = (
                    ref[r, :, c * LANES:(c + 1) * LANES].astype(F32))
        return jnp.concatenate([stage[slot * per_group + c] for c in range(n_chunks)], axis=1)

    def per_column(w):
        lane = lax.broadcasted_iota(I32, (1, LANES), 1)
        r1 = pltpu.roll(w, LSE_LANES, 1)
        r2 = pltpu.roll(w, 2 * LSE_LANES, 1)
        r3 = pltpu.roll(w, 3 * LSE_LANES, 1)
        low = jnp.where(lane < LSE_LANES, w, jnp.where(lane < 3 * LSE_LANES, r1, r2))
        high = jnp.where(lane < LSE_LANES, r2, jnp.where(lane < 3 * LSE_LANES, r3, w))
        return jnp.concatenate([low, high], axis=1)

    o = [token_order(ref, g) for g, ref in enumerate(o_refs)]
    l = [token_order(ref, N_GROUPS + g) for g, ref in enumerate(l_refs)]
    lmax = jnp.maximum(jnp.maximum(l[0], l[1]), l[2])
    e = [jnp.exp(lg - lmax) for lg in l]
    esum = e[0] + e[1] + e[2]
    att = (per_column(e[0] / esum) * o[0] + per_column(e[1] / esum) * o[1] + per_column(e[2] / esum) * o[2])
    att_n = _rms(att, aonw_ref[...]).astype(BF16)
    cgrp = lax.broadcasted_iota(I32, (1, SGU_WIDTH), 1) // SGU_GROUP_DIM
    sguw = sguw_ref[...]
    gates = []
    for c in range(tm // SGU_CHUNK):
        r = _dot(sguw, zv_ref[c * SGU_CHUNK:(c + 1) * SGU_CHUNK, :])
        g = sgub_ref[...]
        for grp in range(SGU_GROUPS):
            g = g + r[grp * SGU_CHUNK:(grp + 1) * SGU_CHUNK, :] * (cgrp == grp).astype(F32)
        gates.append(g)
    sgu = zu_ref[...] * jnp.concatenate(gates, axis=0)
    sgu_n = _rms(sgu, sonw_ref[...]).astype(BF16)
    x1 = x_ref[...] + _dot(att_n, wout_ref[0:GROUP_WIDTH, :]) + _dot(sgu_n, wout_ref[GROUP_WIDTH:, :])
    x1_ref[...] = x1
    h2 = _rms(x1, fnw_ref[...])
    step = pl.program_id(0)
    slot = step % 2
    bt = tm // SUBLANES

    def h2_copies(tile, s):
        return [pltpu.make_async_copy(hstage.at[s, :, :, j, :], h2_hbm.at[pl.ds(tile * bt, bt), j], hsem.at[s])
                for j in range(SUBLANES)]

    _to_tiled(hstage, (slot,), _pack_pairs(h2))
    for c in h2_copies(step, slot):
        c.start()

    @pl.when(step > 0)
    def _():
        for c in h2_copies(step - 1, 1 - slot):
            c.wait()
    logits = _dot_nt(rwt_ref[...], h2.astype(BF16)) + rb_ref[...]
    eiota = lax.broadcasted_iota(I32, (N_EXPERTS, tm), 0)
    vals, idxs = [], []
    for _ in range(TOP_K):
        m = jnp.max(logits, axis=0, keepdims=True)
        idx = jnp.min(jnp.where(logits == m, eiota, N_EXPERTS), axis=0, keepdims=True)
        vals.append(m)
        idxs.append(idx)
        logits = jnp.where(eiota == idx, -jnp.inf, logits)
    exps = [jnp.exp(v - vals[0]) for v in vals]
    den = exps[0] + exps[1] + exps[2] + exps[3]
    gates_ref[...] = jnp.concatenate([ex / den for ex in exps], axis=0)
    eidx_ref[...] = jnp.concatenate(idxs, axis=0)
    onehots = [(eiota == idx).astype(F32) for idx in idxs]
    chosen = onehots[0] + onehots[1] + onehots[2] + onehots[3]
    before = _dot(chosen.astype(BF16), tri_ref[...])
    rank_ref[...] = jnp.concatenate(
        [jnp.sum(oh * before, axis=0, keepdims=True) for oh in onehots], axis=0).astype(I32)
    cnt_ref[...] = jnp.broadcast_to(jnp.sum(chosen, axis=1, keepdims=True), (N_EXPERTS, LANES)).astype(I32)

    @pl.when(step == pl.num_programs(0) - 1)
    def _():
        for c in h2_copies(step, slot):
            c.wait()


def _post_mix(x2, seq, o, lse, zu, zv, p):
    T, D = x2.shape
    tm = TOKEN_TILE
    n_tiles = T // tm
    n_seq_tiles = seq // tm
    chunks = D // (2 * LANES)
    const = lambda shape: pl.BlockSpec(shape, lambda i: (0,) * len(shape))
    row = lambda w: pl.BlockSpec((tm, w), lambda i: (i, 0))
    colt = pl.BlockSpec((TOP_K, tm), lambda i: (0, i))
    phase = [_phase_spec(tm, dil, n_seq_tiles) for _, dil in ATT_WINDOWS]
    lse_phase = [_phase_spec(tm, dil, n_seq_tiles, LANES) for _, dil in ATT_WINDOWS]
    return pl.pallas_call(
        _post_mix_kernel,
        grid=(n_tiles,),
        in_specs=[row(D)] + phase + lse_phase + [row(SGU_WIDTH), row(SGU_WIDTH),
                  const((SGU_GROUPS * SGU_CHUNK, SGU_CHUNK)), const((SGU_CHUNK, SGU_WIDTH)),
                  const((1, GROUP_WIDTH)), const((1, SGU_WIDTH)), const((GROUP_WIDTH + SGU_WIDTH, D)),
                  const((1, D)), const((N_EXPERTS, D)), const((N_EXPERTS, 1)), const((tm, tm))],
        out_specs=[row(D), pl.BlockSpec(memory_space=pl.ANY),
                   colt, colt, colt, pl.BlockSpec((None, N_EXPERTS, LANES), lambda i: (i, 0, 0))],
        out_shape=[jax.ShapeDtypeStruct((T, D), F32),
                   jax.ShapeDtypeStruct((T // SUBLANES, SUBLANES, chunks, LANES), F32),
                   jax.ShapeDtypeStruct((TOP_K, T), F32), jax.ShapeDtypeStruct((TOP_K, T), I32),
                   jax.ShapeDtypeStruct((TOP_K, T), I32), jax.ShapeDtypeStruct((n_tiles, N_EXPERTS, LANES), I32)],
        scratch_shapes=[pltpu.VMEM((2 * N_GROUPS * GROUP_WIDTH // LANES, tm, LANES), F32),
                        pltpu.VMEM((2, tm // SUBLANES, chunks, SUBLANES, LANES), F32), pltpu.SemaphoreType.DMA((2,))],
        compiler_params=pltpu.CompilerParams(dimension_semantics=("arbitrary",), vmem_limit_bytes=VMEM_LIMIT),
        name="post_mix",
    )(x2, *o, *lse, zu, zv, p["sgu_w"], p["sgu_b"], p["att_out_norm_w"],
      p["sgu_out_norm_w"], p["w_out"], p["ffn_norm_w"], p["router_wt"], p["router_b"], p["tri"])


def _sc_workers():
    info = plsc.get_sparse_core_info()
    return info.num_cores, info.num_cores * info.num_subcores


def _dispatch_sc(h2p, pos, pad_pos, n_rows):
    T, chunks, _ = h2p.shape
    n_cores, n_workers = _sc_workers()
    win = SC_WINDOW
    per_worker = T // n_workers
    pad_per_worker = pad_pos.shape[0] // n_workers
    assert per_worker % win == 0 and pad_per_worker % win == 0
    mesh = plsc.VectorSubcoreMesh(core_axis_name="c", subcore_axis_name="s")

    @functools.partial(
        pl.kernel, mesh=mesh, out_type=jax.ShapeDtypeStruct((n_rows, chunks, LANES), F32),
        scratch_types=[pltpu.VMEM((win,), I32), pltpu.VMEM((win, chunks, LANES), F32)],
        name="dispatch_sc")
    def scatter_rows(h_hbm, pos_hbm, pad_hbm, zeros_hbm, xs_hbm, idx_v, rows_v):
        worker = lax.axis_index("s") * n_cores + lax.axis_index("c")

        @pl.loop(0, per_worker // win)
        def _(w):
            t0 = worker * per_worker + w * win
            pltpu.sync_copy(h_hbm.at[pl.ds(t0, win)], rows_v)
            for k in range(TOP_K):
                pltpu.sync_copy(pos_hbm.at[pl.ds(k * T + t0, win)], idx_v)
                pltpu.sync_copy(rows_v, xs_hbm.at[idx_v])

        pltpu.sync_copy(zeros_hbm, rows_v)

        @pl.loop(0, pad_per_worker // win)
        def _(w):
            pltpu.sync_copy(pad_hbm.at[pl.ds(worker * pad_per_worker + w * win, win)], idx_v)
            pltpu.sync_copy(rows_v, xs_hbm.at[idx_v])

    return scatter_rows(h2p, pos, pad_pos, jnp.zeros((win, chunks, LANES), F32))


def _gather_sc(ys, pos):
    _, chunks, _ = ys.shape
    n_assign = pos.shape[0]
    n_cores, n_workers = _sc_workers()
    win = SC_WINDOW
    per_worker = n_assign // n_workers
    assert per_worker % win == 0
    mesh = plsc.VectorSubcoreMesh(core_axis_name="c", subcore_axis_name="s")

    @functools.partial(
        pl.kernel, mesh=mesh, out_type=jax.ShapeDtypeStruct((n_assign, chunks, LANES), F32),
        scratch_types=[pltpu.VMEM((win,), I32), pltpu.VMEM((win, chunks, LANES), F32)],
        name="gather_sc")
    def gather_rows(ys_hbm, pos_hbm, out_hbm, idx_v, rows_v):
        worker = lax.axis_index("s") * n_cores + lax.axis_index("c")

        @pl.loop(0, per_worker // win)
        def _(w):
            a0 = worker * per_worker + w * win
            pltpu.sync_copy(pos_hbm.at[pl.ds(a0, win)], idx_v)
            pltpu.sync_copy(ys_hbm.at[idx_v], rows_v)
            pltpu.sync_copy(rows_v, out_hbm.at[pl.ds(a0, win)])

    return gather_rows(ys, pos)


def _experts_kernel(blk_e_ref, n_used_ref, xs_hbm, wg_ref, bg_ref, wu_ref, bu_ref, wd_ref, bd_ref, ys_hbm,
                    xbuf, obuf, wg_b, wu_b, wd_b, in_sem, out_sem):
    bt = xbuf.shape[1]
    i = pl.program_id(0)
    n = pl.num_programs(0)
    n_used = n_used_ref[0]
    slot = i % 2

    def in_copies(blk, s):
        return [pltpu.make_async_copy(xs_hbm.at[pl.ds(blk * bt, bt), j], xbuf.at[s, :, :, j, :], in_sem.at[s])
                for j in range(SUBLANES)]

    def out_copies(blk, s):
        return [pltpu.make_async_copy(obuf.at[s, :, :, j, :], ys_hbm.at[pl.ds(blk * bt, bt), j], out_sem.at[s])
                for j in range(SUBLANES)]

    @pl.when(i == 0)
    def _():
        for c in in_copies(0, 0):
            c.start()

    @pl.when(i + 1 < n_used)
    def _():
        for c in in_copies(i + 1, 1 - slot):
            c.start()

    @pl.when(i >= 2)
    def _():
        for c in out_copies(i - 2, slot):
            c.wait()

    @pl.when(i >= n_used)
    def _():
        obuf[slot] = jnp.zeros(obuf.shape[1:], F32)

    @pl.when(i < n_used)
    def _():
        for c in in_copies(i, slot):
            c.wait()
        @pl.when((i == 0) | (blk_e_ref[i] != blk_e_ref[jnp.maximum(i - 1, 0)]))
        def _():
            wg_b[...] = wg_ref[...].astype(BF16)
            wu_b[...] = wu_ref[...].astype(BF16)
            wd_b[...] = wd_ref[...].astype(BF16)

        x = _unpack_pairs(_from_tiled(xbuf, (slot,), slice(0, bt))).astype(BF16)
        g = _dot(x, wg_b[...]) + bg_ref[...]
        u = _dot(x, wu_b[...]) + bu_ref[...]
        g = jnp.minimum(g, SWIGLU_LIMIT)
        u = jnp.clip(u, -SWIGLU_LIMIT, SWIGLU_LIMIT)
        act = (u + 1.0) * (g * jax.nn.sigmoid(SWIGLU_ALPHA * g))
        _to_tiled(obuf, (slot,), _pack_pairs(_dot(act.astype(BF16), wd_b[...]) + bd_ref[...]))

    for c in out_copies(i, slot):
        c.start()

    @pl.when(i == n - 1)
    def _():
        for c in out_copies(i - 1, 1 - slot) + out_copies(i, slot):
            c.wait()


def _experts(xs, blk_e, n_used, p):
    n_rows, chunks, _ = xs.shape
    D = 2 * chunks * LANES
    bm = MOE_BLOCK
    d_ff = p["w_gate"].shape[2]
    assert n_rows // bm >= 2
    wspec = lambda a, b: pl.BlockSpec((None, a, b), lambda i, be, nu: (be[i], 0, 0))
    by_tile = (n_rows // SUBLANES, SUBLANES, chunks, LANES)
    block_buf = pltpu.VMEM((2, bm // SUBLANES, chunks, SUBLANES, LANES), F32)
    ys = pl.pallas_call(
        _experts_kernel,
        grid_spec=pltpu.PrefetchScalarGridSpec(
            num_scalar_prefetch=2,
            grid=(n_rows // bm,),
            in_specs=[pl.BlockSpec(memory_space=pl.ANY), wspec(D, d_ff), wspec(1, d_ff), wspec(D, d_ff),
                      wspec(1, d_ff), wspec(d_ff, D), wspec(1, D)],
            out_specs=pl.BlockSpec(memory_space=pl.ANY),
            scratch_shapes=[block_buf, block_buf, pltpu.VMEM((D, d_ff), BF16), pltpu.VMEM((D, d_ff), BF16),
                            pltpu.VMEM((d_ff, D), BF16), pltpu.SemaphoreType.DMA((2,)), pltpu.SemaphoreType.DMA((2,))],
        ),
        out_shape=jax.ShapeDtypeStruct(by_tile, F32),
        compiler_params=pltpu.CompilerParams(dimension_semantics=("arbitrary",), vmem_limit_bytes=EXPERTS_VMEM_LIMIT),
        name="experts",
    )(blk_e, n_used, xs.reshape(by_tile), p["w_gate"], p["b_gate"], p["w_up"], p["b_up"], p["w_down"], p["b_down"])
    return ys.reshape(n_rows, chunks, LANES)


def _combine_kernel(gates_ref, x1_ref, yg_hbm, out_ref, buf, sem):
    tm = x1_ref.shape[0]
    bt = tm // SUBLANES
    j = pl.program_id(0)
    n = pl.num_programs(0) - 1
    tiles_per_k = yg_hbm.shape[0] // TOP_K

    def copies(tile, s):
        return [pltpu.make_async_copy(yg_hbm.at[pl.ds(k * tiles_per_k + tile * bt, bt), r],
                                      buf.at[s, pl.ds(k * bt, bt), :, r, :], sem.at[s])
                for k in range(TOP_K) for r in range(SUBLANES)]

    @pl.when(j < n)
    def _():
        for c in copies(j, j % 2):
            c.start()

    @pl.when(j >= 1)
    def _():
        done = 1 - j % 2
        for c in copies(j - 1, done):
            c.wait()
        y = gates_ref[:, 0:1] * _unpack_pairs(_from_tiled(buf, (done,), slice(0, bt)))
        for k in range(1, TOP_K):
            y = y + gates_ref[:, k:k + 1] * _unpack_pairs(_from_tiled(buf, (done,), slice(k * bt, (k + 1) * bt)))
        out_ref[...] = x1_ref[...] + y


def _combine(x1, gates_tok, yg):
    T, D = x1.shape
    tm = ROW_TILE
    n = T // tm
    chunks = D // (2 * LANES)
    lag = lambda i: (jnp.maximum(i - 1, 0), 0)
    return pl.pallas_call(
        _combine_kernel,
        grid=(n + 1,),
        in_specs=[pl.BlockSpec((tm, TOP_K), lag), pl.BlockSpec((tm, D), lag), pl.BlockSpec(memory_space=pl.ANY)],
        out_specs=pl.BlockSpec((tm, D), lag),
        out_shape=jax.ShapeDtypeStruct((T, D), F32),
        scratch_shapes=[pltpu.VMEM((2, TOP_K * tm // SUBLANES, chunks, SUBLANES, LANES), F32),
                        pltpu.SemaphoreType.DMA((2,))],
        compiler_params=pltpu.CompilerParams(dimension_semantics=("arbitrary",), vmem_limit_bytes=VMEM_LIMIT),
        name="combine",
    )(gates_tok, x1, yg.reshape(TOP_K * T // SUBLANES, SUBLANES, chunks, LANES))


def _route(eidx, rank, tile_counts):
    T = eidx.shape[1]
    bm = MOE_BLOCK
    n_tiles = T // TOKEN_TILE
    tc = tile_counts[:, :, 0]
    counts = jnp.sum(tc, axis=0)
    tile_off = jnp.cumsum(tc, axis=0) - tc
    pcounts = (counts + bm - 1) // bm * bm
    pends = jnp.cumsum(pcounts)
    pstarts = pends - pcounts
    base = pstarts[None, :] + tile_off
    sel = eidx.reshape(TOP_K, n_tiles, 1, TOKEN_TILE) == jnp.arange(N_EXPERTS, dtype=I32)[None, None, :, None]
    pos = jnp.sum(jnp.where(sel, base[None, :, :, None], 0), axis=2).reshape(TOP_K, T) + rank
    n_blocks = (TOP_K * T) // bm + N_EXPERTS
    n_rows = n_blocks * bm
    blk_start = jnp.arange(n_blocks, dtype=I32) * bm
    blk_e = jnp.minimum(jnp.sum((pends[None, :] <= blk_start[:, None]).astype(I32), axis=1), N_EXPERTS - 1)
    n_used = (pends[-1] // bm).astype(I32).reshape(1)
    first_free = jnp.concatenate([pstarts + counts, pends[-1:]])
    free_off = jnp.cumsum(jnp.concatenate([jnp.zeros((1,), counts.dtype), pcounts - counts]))
    j = jnp.arange(n_rows - TOP_K * T, dtype=I32)[:, None]
    seg_id = jnp.sum((free_off[None, 1:] <= j).astype(I32), axis=1, keepdims=True)
    onehot = seg_id == jnp.arange(N_EXPERTS + 1, dtype=I32)[None, :]
    pad_pos = jnp.sum(jnp.where(onehot, (first_free - free_off)[None, :] + j, 0), axis=1).astype(I32)
    return pos.reshape(-1).astype(I32), pad_pos, blk_e.astype(I32), n_used, n_rows


def _layer(x, p):
    batch, seq, D = x.shape
    x2 = x.reshape(batch * seq, D)
    q, k, v, zu, zv = _in_proj(x2, batch, seq, p)
    o, lse = zip(*[_attention(q[g], k[g], v[g], g) for g in range(N_GROUPS)])
    x1, h2p, gates, eidx, rank, tile_counts = _post_mix(x2, seq, o, lse, zu, zv, p)
    pos, pad_pos, blk_e, n_used, n_rows = _route(eidx, rank, tile_counts)
    xs = _dispatch_sc(h2p.reshape(batch * seq, D // (2 * LANES), LANES), pos, pad_pos, n_rows)
    ys = _experts(xs, blk_e, n_used, p)
    out = _combine(x1, gates.T, _gather_sc(ys, pos))
    return out.reshape(batch, seq, D)


def _rope_tables(seq):
    half = ROT_DIM // 2
    inv_freq = jnp.power(ROPE_THETA, -2.0 * jnp.arange(half, dtype=F32) / ROT_DIM)
    ang = jnp.arange(seq, dtype=F32)[:, None] * inv_freq[None, :]
    cos, sin = jnp.cos(ang), jnp.sin(ang)
    zeros = jnp.zeros((seq, HEAD_DIM - ROT_DIM), F32)
    zh = jnp.zeros((seq, half), F32)
    cos_t = jnp.concatenate([cos, cos, zeros + 1.0], axis=1)
    sa = jnp.concatenate([-sin, zh, zeros], axis=1)
    sb = jnp.concatenate([zh, sin, zeros], axis=1)
    rep = LANES // HEAD_DIM
    return tuple(jnp.tile(t, (1, rep)) for t in (cos_t, sa, sb))


def kernel(x_prompt, x_sample, attn_norm_w, w_in, q_norm_w, k_norm_w, sgu_ln_w, sgu_ln_b, sgu_w, sgu_b,
           att_out_norm_w, sgu_out_norm_w, w_out, ffn_norm_w, router_w, router_b,
           w_gate, b_gate, w_up, b_up, w_down, b_down):
    depth = w_in.shape[0]
    n_heads = ATT_WIDTH // HEAD_DIM
    blk = np.arange(MXU_DIM) // HEAD_DIM
    tri = np.arange(TOKEN_TILE)
    xa, xb = x_prompt, x_sample
    rope = {x.shape[1]: _rope_tables(x.shape[1]) for x in (xa, xb)}
    for l in range(depth):
        shared = dict(
            attn_norm_w=attn_norm_w[l][None], w_in=w_in[l].astype(BF16),
            q_norm_w=jnp.tile(q_norm_w[l], n_heads)[None], k_norm_w=jnp.tile(k_norm_w[l], n_heads)[None],
            head_ones=jnp.asarray(blk[:, None] == blk[None, :], BF16),
            sgu_ln_w=sgu_ln_w[l][None], sgu_ln_b=sgu_ln_b[l][None],
            sgu_w=sgu_w[l].reshape(SGU_GROUPS * SGU_CHUNK, SGU_CHUNK).astype(BF16),
            sgu_b=jnp.repeat(sgu_b[l].T, SGU_GROUP_DIM, axis=1),
            att_out_norm_w=att_out_norm_w[l][None], sgu_out_norm_w=sgu_out_norm_w[l][None],
            w_out=w_out[l].astype(BF16), ffn_norm_w=ffn_norm_w[l][None],
            router_wt=router_w[l].T.astype(BF16), router_b=router_b[l][:, None],
            tri=jnp.asarray(tri[:, None] < tri[None, :], BF16),
            w_gate=w_gate[l], b_gate=b_gate[l][:, None, :],
            w_up=w_up[l], b_up=b_up[l][:, None, :],
            w_down=w_down[l], b_down=b_down[l][:, None, :],
        )
        xa, xb = (_layer(x, dict(shared, rope=rope)) for x in (xa, xb))
    return xa, xb
```

```python
import functools

import numpy as np
import jax
import jax.numpy as jnp
from jax import lax
from jax.experimental import pallas as pl
from jax.experimental.pallas import tpu as pltpu
from jax.experimental.pallas import tpu_sc as plsc

F32 = jnp.float32
BF16 = jnp.bfloat16
I32 = jnp.int32

HEAD_DIM = 64
ATT_WINDOWS = ((128, 1), (512, 4), (2048, 16))
N_GROUPS = len(ATT_WINDOWS)
HEADS_PER_GROUP = 4
GROUP_WIDTH = HEADS_PER_GROUP * HEAD_DIM
ATT_WIDTH = N_GROUPS * GROUP_WIDTH
SGU_WIDTH = 256
SGU_GROUP_DIM = 64
SGU_GROUPS = SGU_WIDTH // SGU_GROUP_DIM
SGU_CHUNK = 128
ROT_DIM = HEAD_DIM // 4
ROPE_THETA = 500000.0
N_EXPERTS = 32
TOP_K = 4
SWIGLU_LIMIT = 7.0
SWIGLU_ALPHA = 1.702
NORM_EPS = 1e-6
LN_EPS = 1e-5
NEG_INF = -1e30

LANES = 128
SUBLANES = 8
MXU_DIM = 256
TOKEN_TILE = 512
ATT_Q_BLOCK = 1024
ATT_SUB_Q = 128
LSE_LANES = LANES // HEADS_PER_GROUP
MOE_BLOCK = 512
ROW_TILE = 512
SC_WINDOW = 64
VMEM_LIMIT = 48 * 1024 * 1024
EXPERTS_VMEM_LIMIT = 56 * 1024 * 1024


def _dot(a, b):
    return jnp.dot(a, b, preferred_element_type=F32)


def _dot_nt(a, b, precision=None):
    return lax.dot_general(a, b, (((1,), (1,)), ((), ())), precision=precision, preferred_element_type=F32)


def _rms(x, w):
    return x * lax.rsqrt(jnp.mean(x * x, axis=-1, keepdims=True) + NORM_EPS) * w


def _gelu(x):
    return 0.5 * x * (1.0 + lax.erf(x * np.float32(np.sqrt(0.5))))


def _pack_pairs(x):
    w = x.shape[1] // 2
    lo = lax.bitcast_convert_type(x[:, :w].astype(BF16).astype(F32), jnp.uint32) >> 16
    hi = lax.bitcast_convert_type(x[:, w:].astype(BF16).astype(F32), jnp.uint32) & jnp.uint32(0xFFFF0000)
    return lax.bitcast_convert_type(lo | hi, F32)


def _unpack_pairs(words):
    u = lax.bitcast_convert_type(words, jnp.uint32)
    lo = lax.bitcast_convert_type(u << 16, F32)
    hi = lax.bitcast_convert_type(u & jnp.uint32(0xFFFF0000), F32)
    return jnp.concatenate([lo, hi], axis=1)


def _to_tiled(ref, idx, value):
    rows = value.shape[0]
    for c in range(value.shape[1] // LANES):
        ref[idx + (slice(None), c)] = value[:, c * LANES:(c + 1) * LANES].reshape(rows // SUBLANES, SUBLANES, LANES)


def _from_tiled(ref, idx, row_tiles):
    chunks = ref.shape[-3]
    n = (row_tiles.stop - row_tiles.start) * SUBLANES
    return jnp.concatenate([ref[idx + (row_tiles, c)].reshape(n, LANES) for c in range(chunks)], axis=1)


def _in_proj_kernel(x_ref, anw_ref, w_ref, qnw_ref, knw_ref, ones_ref, cos_ref, sa_ref, sb_ref, lnw_ref, lnb_ref,
                    *refs):
    qkv_refs = refs[:3 * N_GROUPS]
    zu_ref, zv_ref, stage = refs[3 * N_GROUPS:]
    tm = x_ref.shape[0]
    h = _rms(x_ref[...], anw_ref[...]).astype(BF16)
    reps = ATT_WIDTH // LANES
    cos = jnp.concatenate([cos_ref[...]] * reps, axis=1)
    sa = jnp.concatenate([sa_ref[...]] * reps, axis=1)
    sb = jnp.concatenate([sb_ref[...]] * reps, axis=1)
    ones = ones_ref[...]

    def head_norm_rope(t, nw):
        sq = (t * t).astype(BF16)
        parts = []
        for j in range(ATT_WIDTH // MXU_DIM):
            sl = slice(j * MXU_DIM, (j + 1) * MXU_DIM)
            parts.append(_dot(sq[:, sl], ones))
        ssum = jnp.concatenate(parts, axis=1)
        t = t * lax.rsqrt(ssum * (1.0 / HEAD_DIM) + NORM_EPS) * nw
        half = ROT_DIM // 2
        return t * cos + pltpu.roll(t, ATT_WIDTH - half, 1) * sa + pltpu.roll(t, half, 1) * sb

    def emit(which, t):
        n_chunks = ATT_WIDTH // LANES
        per_group = GROUP_WIDTH // LANES
        for c in range(n_chunks):
            stage[which * n_chunks + c] = t[:, c * LANES:(c + 1) * LANES]
        for g, (_, dil) in enumerate(ATT_WINDOWS):
            out = qkv_refs[which * N_GROUPS + g]
            first = which * n_chunks + g * per_group
            for r in range(dil):
                rows = pl.ds(r, tm // dil, stride=dil)
                out[r] = jnp.concatenate([stage[first + c, rows, :] for c in range(per_group)], axis=1).astype(BF16)

    q = head_norm_rope(_dot(h, w_ref[:, 0:ATT_WIDTH]), qnw_ref[...])
    emit(0, q * (HEAD_DIM ** -0.5))
    emit(1, head_norm_rope(_dot(h, w_ref[:, ATT_WIDTH:2 * ATT_WIDTH]), knw_ref[...]))
    emit(2, _dot(h, w_ref[:, 2 * ATT_WIDTH:3 * ATT_WIDTH]))
    z = _dot(h, w_ref[:, 3 * ATT_WIDTH:3 * ATT_WIDTH + 2 * SGU_WIDTH])
    zu_ref[...] = _gelu(z[:, :SGU_WIDTH])
    gv = _gelu(z[:, SGU_WIDTH:])
    mu = jnp.mean(gv, axis=-1, keepdims=True)
    var = jnp.mean(jnp.square(gv - mu), axis=-1, keepdims=True)
    zv_ref[...] = ((gv - mu) * lax.rsqrt(var + LN_EPS) * lnw_ref[...] + lnb_ref[...]).astype(BF16)


def _phase_spec(tm, dil, n_seq_tiles, width=GROUP_WIDTH):
    return pl.BlockSpec((None, dil, tm // dil, width), lambda i: (i // n_seq_tiles, 0, i % n_seq_tiles, 0))


def _in_proj(x2, batch, seq, p):
    T, D = x2.shape
    tm = TOKEN_TILE
    n_seq_tiles = seq // tm
    in_width = p["w_in"].shape[1]
    const = lambda shape: pl.BlockSpec(shape, lambda i: (0,) * len(shape))
    rope = pl.BlockSpec((tm, LANES), lambda i: (i % n_seq_tiles, 0))
    row = lambda w: pl.BlockSpec((tm, w), lambda i: (i, 0))
    qkv_specs = [_phase_spec(tm, dil, n_seq_tiles) for _ in range(3) for _, dil in ATT_WINDOWS]
    qkv_shapes = [jax.ShapeDtypeStruct((batch, dil, seq // dil, GROUP_WIDTH), BF16)
                  for _ in range(3) for _, dil in ATT_WINDOWS]
    outs = pl.pallas_call(
        _in_proj_kernel,
        grid=(T // tm,),
        in_specs=[row(D), const((1, D)), const((D, in_width)), const((1, ATT_WIDTH)), const((1, ATT_WIDTH)),
                  const((MXU_DIM, MXU_DIM)), rope, rope, rope, const((1, SGU_WIDTH)), const((1, SGU_WIDTH))],
        out_specs=qkv_specs + [row(SGU_WIDTH), row(SGU_WIDTH)],
        out_shape=qkv_shapes + [jax.ShapeDtypeStruct((T, SGU_WIDTH), F32), jax.ShapeDtypeStruct((T, SGU_WIDTH), BF16)],
        scratch_shapes=[pltpu.VMEM((3 * ATT_WIDTH // LANES, tm, LANES), F32)],
        compiler_params=pltpu.CompilerParams(dimension_semantics=("arbitrary",), vmem_limit_bytes=VMEM_LIMIT),
        name="in_proj",
    )(x2, p["attn_norm_w"], p["w_in"], p["q_norm_w"], p["k_norm_w"], p["head_ones"],
      *p["rope"][seq], p["sgu_ln_w"], p["sgu_ln_b"])
    q, k, v = (outs[i * N_GROUPS:(i + 1) * N_GROUPS] for i in range(3))
    return q, k, v, outs[-2], outs[-1]


def _attn_kernel(q_ref, kp_ref, km_ref, kn_ref, vp_ref, vm_ref, vn_ref, o_ref, lse_ref, *, sub_len, steps):
    n_phases, lq, _ = q_ref.shape
    sq = min(lq, ATT_SUB_Q)
    lk = sq + 2 * steps
    nh = HEADS_PER_GROUP
    lb = pl.program_id(2)
    head = lax.broadcasted_iota(I32, (1, GROUP_WIDTH), 1) // HEAD_DIM
    lse_head = lax.broadcasted_iota(I32, (1, LANES), 1) // LSE_LANES
    hm_b = [(head == h).astype(BF16) for h in range(nh)]
    qi = lax.broadcasted_iota(I32, (sq, 1), 0)
    kj = lax.broadcasted_iota(I32, (sq, lk), 1)
    for ph in range(n_phases):
        kk = jnp.concatenate([kp_ref[ph], km_ref[ph], kn_ref[ph]], axis=0)
        vv = jnp.concatenate([vp_ref[ph], vm_ref[ph], vn_ref[ph]], axis=0)
        for j in range(lq // sq):
            first = lb * lq + j * sq
            lo = jnp.maximum(qi, steps - first)
            hi = jnp.minimum(qi + 2 * steps, sub_len - 1 + steps - first)
            mask = ((kj - lo).astype(jnp.uint32) <= (hi - lo).astype(jnp.uint32))[None]
            qj = q_ref[ph, j * sq:(j + 1) * sq, :]
            qs = jnp.concatenate([qj * hm_b[h] for h in range(nh)], axis=0)
            s = _dot_nt(qs, kk[j * sq:j * sq + lk]).reshape(nh, sq, lk)
            s = jnp.where(mask, s, NEG_INF)
            m = jnp.max(s, axis=-1, keepdims=True)
            pr = jnp.exp(s - m)
            den = jnp.sum(pr, axis=-1, keepdims=True)
            oh = _dot(pr.reshape(nh * sq, lk).astype(BF16), vv[j * sq:j * sq + lk]).reshape(nh, sq, GROUP_WIDTH) / den
            lh = m + jnp.log(den)
            o, lse = oh[0], jnp.broadcast_to(lh[0], (sq, LANES))
            for h in range(1, nh):
                o = jnp.where(head == h, oh[h], o)
                lse = jnp.where(lse_head == h, lh[h], lse)
            o_ref[ph, j * sq:(j + 1) * sq, :] = o.astype(BF16)
            lse_ref[ph, j * sq:(j + 1) * sq, :] = lse


def _attention(q, k, v, group):
    window, dil = ATT_WINDOWS[group]
    steps = window // (2 * dil)
    batch, _, sub_len, _ = q.shape
    assert sub_len % steps == 0
    lq = min(ATT_Q_BLOCK, sub_len)
    assert lq % steps == 0 and sub_len % lq == 0
    per_q = lq // steps
    n_halo = sub_len // steps
    n_ph = min(dil, ATT_Q_BLOCK // lq)
    assert dil % n_ph == 0
    main = pl.BlockSpec((None, n_ph, lq, GROUP_WIDTH), lambda b, r, i: (b, r, i, 0))
    prev = pl.BlockSpec((None, n_ph, steps, GROUP_WIDTH), lambda b, r, i: (b, r, jnp.maximum(i * per_q - 1, 0), 0))
    nxt = pl.BlockSpec((None, n_ph, steps, GROUP_WIDTH),
                       lambda b, r, i: (b, r, jnp.minimum((i + 1) * per_q, n_halo - 1), 0))
    return pl.pallas_call(
        functools.partial(_attn_kernel, sub_len=sub_len, steps=steps),
        grid=(batch, dil // n_ph, sub_len // lq),
        in_specs=[main, prev, main, nxt, prev, main, nxt],
        out_specs=[main, pl.BlockSpec((None, n_ph, lq, LANES), lambda b, r, i: (b, r, i, 0))],
        out_shape=[jax.ShapeDtypeStruct(q.shape, BF16), jax.ShapeDtypeStruct(q.shape[:3] + (LANES,), F32)],
        compiler_params=pltpu.CompilerParams(dimension_semantics=("arbitrary",) * 3, vmem_limit_bytes=VMEM_LIMIT),
        name=f"attention_g{group}",
    )(q, k, k, k, v, v, v)


def _post_mix_kernel(x_ref, *refs):
    o_refs = refs[:N_GROUPS]
    l_refs = refs[N_GROUPS:2 * N_GROUPS]
    (zu_ref, zv_ref, sguw_ref, sgub_ref, aonw_ref, sonw_ref, wout_ref, fnw_ref, rwt_ref, rb_ref, tri_ref,
     x1_ref, h2_hbm, gates_ref, eidx_ref, rank_ref, cnt_ref, stage, hstage, hsem) = refs[2 * N_GROUPS:]
    tm = x_ref.shape[0]

    def token_order(ref, slot):
        dil = ref.shape[0]
        if dil == 1:
            return ref[0].astype(F32)
        n_chunks = ref.shape[2] // LANES
        per_group = GROUP_WIDTH // LANES
        for r in range(dil):
            for c in range(n_chunks):
                stage[slot * per_group + c, pl.ds(r, tm // dil, stride=dil), :] = (
                    ref[r, :, c * LANES:(c + 1) * LANES].astype(F32))
        return jnp.concatenate([stage[slot * per_group + c] for c in range(n_chunks)], axis=1)

    def per_column(w):
        lane = lax.broadcasted_iota(I32, (1, LANES), 1)
        r1 = pltpu.roll(w, LSE_LANES, 1)
        r2 = pltpu.roll(w, 2 * LSE_LANES, 1)
        r3 = pltpu.roll(w, 3 * LSE_LANES, 1)
        low = jnp.where(lane < LSE_LANES, w, jnp.where(lane < 3 * LSE_LANES, r1, r2))
        high = jnp.where(lane < LSE_LANES, r2, jnp.where(lane < 3 * LSE_LANES, r3, w))
        return jnp.concatenate([low, high], axis=1)

    o = [token_order(ref, g) for g, ref in enumerate(o_refs)]
    l = [token_order(ref, N_GROUPS + g) for g, ref in enumerate(l_refs)]
    lmax = jnp.maximum(jnp.maximum(l[0], l[1]), l[2])
    e = [jnp.exp(lg - lmax) for lg in l]
    esum = e[0] + e[1] + e[2]
    att = (per_column(e[0] / esum) * o[0] + per_column(e[1] / esum) * o[1] + per_column(e[2] / esum) * o[2])
    att_n = _rms(att, aonw_ref[...]).astype(BF16)
    cgrp = lax.broadcasted_iota(I32, (1, SGU_WIDTH), 1) // SGU_GROUP_DIM
    sguw = sguw_ref[...]
    gates = []
    for c in range(tm // SGU_CHUNK):
        r = _dot(sguw, zv_ref[c * SGU_CHUNK:(c + 1) * SGU_CHUNK, :])
        g = sgub_ref[...]
        for grp in range(SGU_GROUPS):
            g = g + r[grp * SGU_CHUNK:(grp + 1) * SGU_CHUNK, :] * (cgrp == grp).astype(F32)
        gates.append(g)
    sgu = zu_ref[...] * jnp.concatenate(gates, axis=0)
    sgu_n = _rms(sgu, sonw_ref[...]).astype(BF16)
    x1 = x_ref[...] + _dot(att_n, wout_ref[0:GROUP_WIDTH, :]) + _dot(sgu_n, wout_ref[GROUP_WIDTH:, :])
    x1_ref[...] = x1
    h2 = _rms(x1, fnw_ref[...])
    step = pl.program_id(0)
    slot = step % 2
    bt = tm // SUBLANES

    def h2_copies(tile, s):
        return [pltpu.make_async_copy(hstage.at[s, :, :, j, :], h2_hbm.at[pl.ds(tile * bt, bt), j], hsem.at[s])
                for j in range(SUBLANES)]

    _to_tiled(hstage, (slot,), _pack_pairs(h2))
    for c in h2_copies(step, slot):
        c.start()

    @pl.when(step > 0)
    def _():
        for c in h2_copies(step - 1, 1 - slot):
            c.wait()
    logits = _dot_nt(rwt_ref[...], h2.astype(BF16)) + rb_ref[...]
    eiota = lax.broadcasted_iota(I32, (N_EXPERTS, tm), 0)
    vals, idxs = [], []
    for _ in range(TOP_K):
        m = jnp.max(logits, axis=0, keepdims=True)
        idx = jnp.min(jnp.where(logits == m, eiota, N_EXPERTS), axis=0, keepdims=True)
        vals.append(m)
        idxs.append(idx)
        logits = jnp.where(eiota == idx, -jnp.inf, logits)
    exps = [jnp.exp(v - vals[0]) for v in vals]
    den = exps[0] + exps[1] + exps[2] + exps[3]
    gates_ref[...] = jnp.concatenate([ex / den for ex in exps], axis=0)
    eidx_ref[...] = jnp.concatenate(idxs, axis=0)
    onehots = [(eiota == idx).astype(F32) for idx in idxs]
    chosen = onehots[0] + onehots[1] + onehots[2] + onehots[3]
    before = _dot(chosen.astype(BF16), tri_ref[...])
    rank_ref[...] = jnp.concatenate(
        [jnp.sum(oh * before, axis=0, keepdims=True) for oh in onehots], axis=0).astype(I32)
    cnt_ref[...] = jnp.broadcast_to(jnp.sum(chosen, axis=1, keepdims=True), (N_EXPERTS, LANES)).astype(I32)

    @pl.when(step == pl.num_programs(0) - 1)
    def _():
        for c in h2_copies(step, slot):
            c.wait()


def _post_mix(x2, seq, o, lse, zu, zv, p):
    T, D = x2.shape
    tm = TOKEN_TILE
    n_tiles = T // tm
    n_seq_tiles = seq // tm
    chunks = D // (2 * LANES)
    const = lambda shape: pl.BlockSpec(shape, lambda i: (0,) * len(shape))
    row = lambda w: pl.BlockSpec((tm, w), lambda i: (i, 0))
    colt = pl.BlockSpec((TOP_K, tm), lambda i: (0, i))
    phase = [_phase_spec(tm, dil, n_seq_tiles) for _, dil in ATT_WINDOWS]
    lse_phase = [_phase_spec(tm, dil, n_seq_tiles, LANES) for _, dil in ATT_WINDOWS]
    return pl.pallas_call(
        _post_mix_kernel,
        grid=(n_tiles,),
        in_specs=[row(D)] + phase + lse_phase + [row(SGU_WIDTH), row(SGU_WIDTH),
                  const((SGU_GROUPS * SGU_CHUNK, SGU_CHUNK)), const((SGU_CHUNK, SGU_WIDTH)),
                  const((1, GROUP_WIDTH)), const((1, SGU_WIDTH)), const((GROUP_WIDTH + SGU_WIDTH, D)),
                  const((1, D)), const((N_EXPERTS, D)), const((N_EXPERTS, 1)), const((tm, tm))],
        out_specs=[row(D), pl.BlockSpec(memory_space=pl.ANY),
                   colt, colt, colt, pl.BlockSpec((None, N_EXPERTS, LANES), lambda i: (i, 0, 0))],
        out_shape=[jax.ShapeDtypeStruct((T, D), F32),
                   jax.ShapeDtypeStruct((T // SUBLANES, SUBLANES, chunks, LANES), F32),
                   jax.ShapeDtypeStruct((TOP_K, T), F32), jax.ShapeDtypeStruct((TOP_K, T), I32),
                   jax.ShapeDtypeStruct((TOP_K, T), I32), jax.ShapeDtypeStruct((n_tiles, N_EXPERTS, LANES), I32)],
        scratch_shapes=[pltpu.VMEM((2 * N_GROUPS * GROUP_WIDTH // LANES, tm, LANES), F32),
                        pltpu.VMEM((2, tm // SUBLANES, chunks, SUBLANES, LANES), F32), pltpu.SemaphoreType.DMA((2,))],
        compiler_params=pltpu.CompilerParams(dimension_semantics=("arbitrary",), vmem_limit_bytes=VMEM_LIMIT),
        name="post_mix",
    )(x2, *o, *lse, zu, zv, p["sgu_w"], p["sgu_b"], p["att_out_norm_w"],
      p["sgu_out_norm_w"], p["w_out"], p["ffn_norm_w"], p["router_wt"], p["router_b"], p["tri"])


def _sc_workers():
    info = plsc.get_sparse_core_info()
    return info.num_cores, info.num_cores * info.num_subcores


def _dispatch_sc(h2p, pos, pad_pos, n_rows):
    T, chunks, _ = h2p.shape
    n_cores, n_workers = _sc_workers()
    win = SC_WINDOW
    per_worker = T // n_workers
    pad_per_worker = pad_pos.shape[0] // n_workers
    assert per_worker % win == 0 and pad_per_worker % win == 0
    mesh = plsc.VectorSubcoreMesh(core_axis_name="c", subcore_axis_name="s")

    @functools.partial(
        pl.kernel, mesh=mesh, out_type=jax.ShapeDtypeStruct((n_rows, chunks, LANES), F32),
        scratch_types=[pltpu.VMEM((win,), I32), pltpu.VMEM((win, chunks, LANES), F32)],
        name="dispatch_sc")
    def scatter_rows(h_hbm, pos_hbm, pad_hbm, zeros_hbm, xs_hbm, idx_v, rows_v):
        worker = lax.axis_index("s") * n_cores + lax.axis_index("c")

        @pl.loop(0, per_worker // win)
        def _(w):
            t0 = worker * per_worker + w * win
            pltpu.sync_copy(h_hbm.at[pl.ds(t0, win)], rows_v)
            for k in range(TOP_K):
                pltpu.sync_copy(pos_hbm.at[pl.ds(k * T + t0, win)], idx_v)
                pltpu.sync_copy(rows_v, xs_hbm.at[idx_v])

        pltpu.sync_copy(zeros_hbm, rows_v)

        @pl.loop(0, pad_per_worker // win)
        def _(w):
            pltpu.sync_copy(pad_hbm.at[pl.ds(worker * pad_per_worker + w * win, win)], idx_v)
            pltpu.sync_copy(rows_v, xs_hbm.at[idx_v])

    return scatter_rows(h2p, pos, pad_pos, jnp.zeros((win, chunks, LANES), F32))


def _gather_sc(ys, pos):
    _, chunks, _ = ys.shape
    n_assign = pos.shape[0]
    n_cores, n_workers = _sc_workers()
    win = SC_WINDOW
    per_worker = n_assign // n_workers
    assert per_worker % win == 0
    mesh = plsc.VectorSubcoreMesh(core_axis_name="c", subcore_axis_name="s")

    @functools.partial(
        pl.kernel, mesh=mesh, out_type=jax.ShapeDtypeStruct((n_assign, chunks, LANES), F32),
        scratch_types=[pltpu.VMEM((win,), I32), pltpu.VMEM((win, chunks, LANES), F32)],
        name="gather_sc")
    def gather_rows(ys_hbm, pos_hbm, out_hbm, idx_v, rows_v):
        worker = lax.axis_index("s") * n_cores + lax.axis_index("c")

        @pl.loop(0, per_worker // win)
        def _(w):
            a0 = worker * per_worker + w * win
            pltpu.sync_copy(pos_hbm.at[pl.ds(a0, win)], idx_v)
            pltpu.sync_copy(ys_hbm.at[idx_v], rows_v)
            pltpu.sync_copy(rows_v, out_hbm.at[pl.ds(a0, win)])

    return gather_rows(ys, pos)


def _experts_kernel(blk_e_ref, n_used_ref, xs_hbm, wg_ref, bg_ref, wu_ref, bu_ref, wd_ref, bd_ref, ys_hbm,
                    xbuf, obuf, wg_b, wu_b, wd_b, in_sem, out_sem):
    bt = xbuf.shape[1]
    i = pl.program_id(0)
    n = pl.num_programs(0)
    n_used = n_used_ref[0]
    slot = i % 2

    def in_copies(blk, s):
        return [pltpu.make_async_copy(xs_hbm.at[pl.ds(blk * bt, bt), j], xbuf.at[s, :, :, j, :], in_sem.at[s])
                for j in range(SUBLANES)]

    def out_copies(blk, s):
        return [pltpu.make_async_copy(obuf.at[s, :, :, j, :], ys_hbm.at[pl.ds(blk * bt, bt), j], out_sem.at[s])
                for j in range(SUBLANES)]

    @pl.when(i == 0)
    def _():
        for c in in_copies(0, 0):
            c.start()

    @pl.when(i + 1 < n_used)
    def _():
        for c in in_copies(i + 1, 1 - slot):
            c.start()

    @pl.when(i >= 2)
    def _():
        for c in out_copies(i - 2, slot):
            c.wait()

    @pl.when(i >= n_used)
    def _():
        obuf[slot] = jnp.zeros(obuf.shape[1:], F32)

    @pl.when(i < n_used)
    def _():
        for c in in_copies(i, slot):
            c.wait()
        @pl.when((i == 0) | (blk_e_ref[i] != blk_e_ref[jnp.maximum(i - 1, 0)]))
        def _():
            wg_b[...] = wg_ref[...].astype(BF16)
            wu_b[...] = wu_ref[...].astype(BF16)
            wd_b[...] = wd_ref[...].astype(BF16)

        x = _unpack_pairs(_from_tiled(xbuf, (slot,), slice(0, bt))).astype(BF16)
        g = _dot(x, wg_b[...]) + bg_ref[...]
        u = _dot(x, wu_b[...]) + bu_ref[...]
        g = jnp.minimum(g, SWIGLU_LIMIT)
        u = jnp.clip(u, -SWIGLU_LIMIT, SWIGLU_LIMIT)
        act = (u + 1.0) * (g * jax.nn.sigmoid(SWIGLU_ALPHA * g))
        _to_tiled(obuf, (slot,), _pack_pairs(_dot(act.astype(BF16), wd_b[...]) + bd_ref[...]))

    for c in out_copies(i, slot):
        c.start()

    @pl.when(i == n - 1)
    def _():
        for c in out_copies(i - 1, 1 - slot) + out_copies(i, slot):
            c.wait()


def _experts(xs, blk_e, n_used, p):
    n_rows, chunks, _ = xs.shape
    D = 2 * chunks * LANES
    bm = MOE_BLOCK
    d_ff = p["w_gate"].shape[2]
    assert n_rows // bm >= 2
    wspec = lambda a, b: pl.BlockSpec((None, a, b), lambda i, be, nu: (be[i], 0, 0))
    by_tile = (n_rows // SUBLANES, SUBLANES, chunks, LANES)
    block_buf = pltpu.VMEM((2, bm // SUBLANES, chunks, SUBLANES, LANES), F32)
    ys = pl.pallas_call(
        _experts_kernel,
        grid_spec=pltpu.PrefetchScalarGridSpec(
            num_scalar_prefetch=2,
            grid=(n_rows // bm,),
            in_specs=[pl.BlockSpec(memory_space=pl.ANY), wspec(D, d_ff), wspec(1, d_ff), wspec(D, d_ff),
                      wspec(1, d_ff), wspec(d_ff, D), wspec(1, D)],
            out_specs=pl.BlockSpec(memory_space=pl.ANY),
            scratch_shapes=[block_buf, block_buf, pltpu.VMEM((D, d_ff), BF16), pltpu.VMEM((D, d_ff), BF16),
                            pltpu.VMEM((d_ff, D), BF16), pltpu.SemaphoreType.DMA((2,)), pltpu.SemaphoreType.DMA((2,))],
        ),
        out_shape=jax.ShapeDtypeStruct(by_tile, F32),
        compiler_params=pltpu.CompilerParams(dimension_semantics=("arbitrary",), vmem_limit_bytes=EXPERTS_VMEM_LIMIT),
        name="experts",
    )(blk_e, n_used, xs.reshape(by_tile), p["w_gate"], p["b_gate"], p["w_up"], p["b_up"], p["w_down"], p["b_down"])
    return ys.reshape(n_rows, chunks, LANES)


def _combine_kernel(gates_ref, x1_ref, yg_hbm, out_ref, buf, sem):
    tm = x1_ref.shape[0]
    bt = tm // SUBLANES
    j = pl.program_id(0)
    n = pl.num_programs(0) - 1
    tiles_per_k = yg_hbm.shape[0] // TOP_K

    def copies(tile, s):
        return [pltpu.make_async_copy(yg_hbm.at[pl.ds(k * tiles_per_k + tile * bt, bt), r],
                                      buf.at[s, pl.ds(k * bt, bt), :, r, :], sem.at[s])
                for k in range(TOP_K) for r in range(SUBLANES)]

    @pl.when(j < n)
    def _():
        for c in copies(j, j % 2):
            c.start()

    @pl.when(j >= 1)
    def _():
        done = 1 - j % 2
        for c in copies(j - 1, done):
            c.wait()
        y = gates_ref[:, 0:1] * _unpack_pairs(_from_tiled(buf, (done,), slice(0, bt)))
        for k in range(1, TOP_K):
            y = y + gates_ref[:, k:k + 1] * _unpack_pairs(_from_tiled(buf, (done,), slice(k * bt, (k + 1) * bt)))
        out_ref[...] = x1_ref[...] + y


def _combine(x1, gates_tok, yg):
    T, D = x1.shape
    tm = ROW_TILE
    n = T // tm
    chunks = D // (2 * LANES)
    lag = lambda i: (jnp.maximum(i - 1, 0), 0)
    return pl.pallas_call(
        _combine_kernel,
        grid=(n + 1,),
        in_specs=[pl.BlockSpec((tm, TOP_K), lag), pl.BlockSpec((tm, D), lag), pl.BlockSpec(memory_space=pl.ANY)],
        out_specs=pl.BlockSpec((tm, D), lag),
        out_shape=jax.ShapeDtypeStruct((T, D), F32),
        scratch_shapes=[pltpu.VMEM((2, TOP_K * tm // SUBLANES, chunks, SUBLANES, LANES), F32),
                        pltpu.SemaphoreType.DMA((2,))],
        compiler_params=pltpu.CompilerParams(dimension_semantics=("arbitrary",), vmem_limit_bytes=VMEM_LIMIT),
        name="combine",
    )(gates_tok, x1, yg.reshape(TOP_K * T // SUBLANES, SUBLANES, chunks, LANES))


def _route(eidx, rank, tile_counts):
    T = eidx.shape[1]
    bm = MOE_BLOCK
    n_tiles = T // TOKEN_TILE
    tc = tile_counts[:, :, 0]
    counts = jnp.sum(tc, axis=0)
    tile_off = jnp.cumsum(tc, axis=0) - tc
    pcounts = (counts + bm - 1) // bm * bm
    pends = jnp.cumsum(pcounts)
    pstarts = pends - pcounts
    base = pstarts[None, :] + tile_off
    sel = eidx.reshape(TOP_K, n_tiles, 1, TOKEN_TILE) == jnp.arange(N_EXPERTS, dtype=I32)[None, None, :, None]
    pos = jnp.sum(jnp.where(sel, base[None, :, :, None], 0), axis=2).reshape(TOP_K, T) + rank
    n_blocks = (TOP_K * T) // bm + N_EXPERTS
    n_rows = n_blocks * bm
    blk_start = jnp.arange(n_blocks, dtype=I32) * bm
    blk_e = jnp.minimum(jnp.sum((pends[None, :] <= blk_start[:, None]).astype(I32), axis=1), N_EXPERTS - 1)
    n_used = (pends[-1] // bm).astype(I32).reshape(1)
    first_free = jnp.concatenate([pstarts + counts, pends[-1:]])
    free_off = jnp.cumsum(jnp.concatenate([jnp.zeros((1,), counts.dtype), pcounts - counts]))
    j = jnp.arange(n_rows - TOP_K * T, dtype=I32)[:, None]
    seg_id = jnp.sum((free_off[None, 1:] <= j).astype(I32), axis=1, keepdims=True)
    onehot = seg_id == jnp.arange(N_EXPERTS + 1, dtype=I32)[None, :]
    pad_pos = jnp.sum(jnp.where(onehot, (first_free - free_off)[None, :] + j, 0), axis=1).astype(I32)
    return pos.reshape(-1).astype(I32), pad_pos, blk_e.astype(I32), n_used, n_rows


def _layer(x, p):
    batch, seq, D = x.shape
    x2 = x.reshape(batch * seq, D)
    q, k, v, zu, zv = _in_proj(x2, batch, seq, p)
    o, lse = zip(*[_attention(q[g], k[g], v[g], g) for g in range(N_GROUPS)])
    x1, h2p, gates, eidx, rank, tile_counts = _post_mix(x2, seq, o, lse, zu, zv, p)
    pos, pad_pos, blk_e, n_used, n_rows = _route(eidx, rank, tile_counts)
    xs = _dispatch_sc(h2p.reshape(batch * seq, D // (2 * LANES), LANES), pos, pad_pos, n_rows)
    ys = _experts(xs, blk_e, n_used, p)
    out = _combine(x1, gates.T, _gather_sc(ys, pos))
    return out.reshape(batch, seq, D)


def _rope_tables(seq):
    half = ROT_DIM // 2
    inv_freq = jnp.power(ROPE_THETA, -2.0 * jnp.arange(half, dtype=F32) / ROT_DIM)
    ang = jnp.arange(seq, dtype=F32)[:, None] * inv_freq[None, :]
    cos, sin = jnp.cos(ang), jnp.sin(ang)
    zeros = jnp.zeros((seq, HEAD_DIM - ROT_DIM), F32)
    zh = jnp.zeros((seq, half), F32)
    cos_t = jnp.concatenate([cos, cos, zeros + 1.0], axis=1)
    sa = jnp.concatenate([-sin, zh, zeros], axis=1)
    sb = jnp.concatenate([zh, sin, zeros], axis=1)
    rep = LANES // HEAD_DIM
    return tuple(jnp.tile(t, (1, rep)) for t in (cos_t, sa, sb))


def kernel(x_prompt, x_sample, attn_norm_w, w_in, q_norm_w, k_norm_w, sgu_ln_w, sgu_ln_b, sgu_w, sgu_b,
           att_out_norm_w, sgu_out_norm_w, w_out, ffn_norm_w, router_w, router_b,
           w_gate, b_gate, w_up, b_up, w_down, b_down):
    depth = w_in.shape[0]
    n_heads = ATT_WIDTH // HEAD_DIM
    blk = np.arange(MXU_DIM) // HEAD_DIM
    tri = np.arange(TOKEN_TILE)
    xa, xb = x_prompt, x_sample
    rope = {x.shape[1]: _rope_tables(x.shape[1]) for x in (xa, xb)}
    for l in range(depth):
        shared = dict(
            attn_norm_w=attn_norm_w[l][None], w_in=w_in[l].astype(BF16),
            q_norm_w=jnp.tile(q_norm_w[l], n_heads)[None], k_norm_w=jnp.tile(k_norm_w[l], n_heads)[None],
            head_ones=jnp.asarray(blk[:, None] == blk[None, :], BF16),
            sgu_ln_w=sgu_ln_w[l][None], sgu_ln_b=sgu_ln_b[l][None],
            sgu_w=sgu_w[l].reshape(SGU_GROUPS * SGU_CHUNK, SGU_CHUNK).astype(BF16),
            sgu_b=jnp.repeat(sgu_b[l].T, SGU_GROUP_DIM, axis=1),
            att_out_norm_w=att_out_norm_w[l][None], sgu_out_norm_w=sgu_out_norm_w[l][None],
            w_out=w_out[l].astype(BF16), ffn_norm_w=ffn_norm_w[l][None],
            router_wt=router_w[l].T.astype(BF16), router_b=router_b[l][:, None],
            tri=jnp.asarray(tri[:, None] < tri[None, :], BF16),
            w_gate=w_gate[l], b_gate=b_gate[l][:, None, :],
            w_up=w_up[l], b_up=b_up[l][:, None, :],
            w_down=w_down[l], b_down=b_down[l][:, None, :],
        )
        xa, xb = (_layer(x, dict(shared, rope=rope)) for x in (xa, xb))
    return xa, xb
```

```python
import functools

import numpy as np
import jax
import jax.numpy as jnp
from jax import lax
from jax.experimental import pallas as pl
from jax.experimental.pallas import tpu as pltpu
from jax.experimental.pallas import tpu_sc as plsc

F32 = jnp.float32
BF16 = jnp.bfloat16
I32 = jnp.int32

HEAD_DIM = 64
ATT_WINDOWS = ((128, 1), (512, 4), (2048, 16))
N_GROUPS = len(ATT_WINDOWS)
HEADS_PER_GROUP = 4
GROUP_WIDTH = HEADS_PER_GROUP * HEAD_DIM
ATT_WIDTH = N_GROUPS * GROUP_WIDTH
SGU_WIDTH = 256
SGU_GROUP_DIM = 64
SGU_GROUPS = SGU_WIDTH // SGU_GROUP_DIM
SGU_CHUNK = 128
ROT_DIM = HEAD_DIM // 4
ROPE_THETA = 500000.0
N_EXPERTS = 32
TOP_K = 4
SWIGLU_LIMIT = 7.0
SWIGLU_ALPHA = 1.702
NORM_EPS = 1e-6
LN_EPS = 1e-5
NEG_INF = -1e30

LANES = 128
SUBLANES = 8
MXU_DIM = 256
TOKEN_TILE = 512
ATT_Q_BLOCK = 2048
ATT_SUB_Q = 128
LSE_LANES = LANES // HEADS_PER_GROUP
MOE_BLOCK = 512
ROW_TILE = 512
SC_WINDOW = 64
VMEM_LIMIT = 48 * 1024 * 1024
EXPERTS_VMEM_LIMIT = 56 * 1024 * 1024


def _dot(a, b):
    return jnp.dot(a, b, preferred_element_type=F32)


def _dot_nt(a, b, precision=None):
    return lax.dot_general(a, b, (((1,), (1,)), ((), ())), precision=precision, preferred_element_type=F32)


def _rms(x, w):
    return x * lax.rsqrt(jnp.mean(x * x, axis=-1, keepdims=True) + NORM_EPS) * w


def _gelu(x):
    return 0.5 * x * (1.0 + lax.erf(x * np.float32(np.sqrt(0.5))))


def _pack_pairs(x):
    w = x.shape[1] // 2
    lo = lax.bitcast_convert_type(x[:, :w].astype(BF16).astype(F32), jnp.uint32) >> 16
    hi = lax.bitcast_convert_type(x[:, w:].astype(BF16).astype(F32), jnp.uint32) & jnp.uint32(0xFFFF0000)
    return lax.bitcast_convert_type(lo | hi, F32)


def _unpack_pairs(words):
    u = lax.bitcast_convert_type(words, jnp.uint32)
    lo = lax.bitcast_convert_type(u << 16, F32)
    hi = lax.bitcast_convert_type(u & jnp.uint32(0xFFFF0000), F32)
    return jnp.concatenate([lo, hi], axis=1)


def _to_tiled(ref, idx, value):
    rows = value.shape[0]
    for c in range(value.shape[1] // LANES):
        ref[idx + (slice(None), c)] = value[:, c * LANES:(c + 1) * LANES].reshape(rows // SUBLANES, SUBLANES, LANES)


def _from_tiled(ref, idx, row_tiles):
    chunks = ref.shape[-3]
    n = (row_tiles.stop - row_tiles.start) * SUBLANES
    return jnp.concatenate([ref[idx + (row_tiles, c)].reshape(n, LANES) for c in range(chunks)], axis=1)


def _in_proj_kernel(x_ref, anw_ref, w_ref, qnw_ref, knw_ref, ones_ref, cos_ref, sa_ref, sb_ref, lnw_ref, lnb_ref,
                    *refs):
    qkv_refs = refs[:3 * N_GROUPS]
    zu_ref, zv_ref, stage = refs[3 * N_GROUPS:]
    tm = x_ref.shape[0]
    h = _rms(x_ref[...], anw_ref[...]).astype(BF16)
    reps = ATT_WIDTH // LANES
    cos = jnp.concatenate([cos_ref[...]] * reps, axis=1)
    sa = jnp.concatenate([sa_ref[...]] * reps, axis=1)
    sb = jnp.concatenate([sb_ref[...]] * reps, axis=1)
    ones = ones_ref[...]

    def head_norm_rope(t, nw):
        sq = (t * t).astype(BF16)
        parts = []
        for j in range(ATT_WIDTH // MXU_DIM):
            sl = slice(j * MXU_DIM, (j + 1) * MXU_DIM)
            parts.append(_dot(sq[:, sl], ones))
        ssum = jnp.concatenate(parts, axis=1)
        t = t * lax.rsqrt(ssum * (1.0 / HEAD_DIM) + NORM_EPS) * nw
        half = ROT_DIM // 2
        return t * cos + pltpu.roll(t, ATT_WIDTH - half, 1) * sa + pltpu.roll(t, half, 1) * sb

    def emit(which, t):
        n_chunks = ATT_WIDTH // LANES
        per_group = GROUP_WIDTH // LANES
        for c in range(n_chunks):
            stage[which * n_chunks + c] = t[:, c * LANES:(c + 1) * LANES]
        for g, (_, dil) in enumerate(ATT_WINDOWS):
            out = qkv_refs[which * N_GROUPS + g]
            first = which * n_chunks + g * per_group
            for r in range(dil):
                rows = pl.ds(r, tm // dil, stride=dil)
                out[r] = jnp.concatenate([stage[first + c, rows, :] for c in range(per_group)], axis=1).astype(BF16)

    q = head_norm_rope(_dot(h, w_ref[:, 0:ATT_WIDTH]), qnw_ref[...])
    emit(0, q * (HEAD_DIM ** -0.5))
    emit(1, head_norm_rope(_dot(h, w_ref[:, ATT_WIDTH:2 * ATT_WIDTH]), knw_ref[...]))
    emit(2, _dot(h, w_ref[:, 2 * ATT_WIDTH:3 * ATT_WIDTH]))
    z = _dot(h, w_ref[:, 3 * ATT_WIDTH:3 * ATT_WIDTH + 2 * SGU_WIDTH])
    zu_ref[...] = _gelu(z[:, :SGU_WIDTH])
    gv = _gelu(z[:, SGU_WIDTH:])
    mu = jnp.mean(gv, axis=-1, keepdims=True)
    var = jnp.mean(jnp.square(gv - mu), axis=-1, keepdims=True)
    zv_ref[...] = ((gv - mu) * lax.rsqrt(var + LN_EPS) * lnw_ref[...] + lnb_ref[...]).astype(BF16)


def _phase_spec(tm, dil, n_seq_tiles, width=GROUP_WIDTH):
    return pl.BlockSpec((None, dil, tm // dil, width), lambda i: (i // n_seq_tiles, 0, i % n_seq_tiles, 0))


def _in_proj(x2, batch, seq, p):
    T, D = x2.shape
    tm = TOKEN_TILE
    n_seq_tiles = seq // tm
    in_width = p["w_in"].shape[1]
    const = lambda shape: pl.BlockSpec(shape, lambda i: (0,) * len(shape))
    rope = pl.BlockSpec((tm, LANES), lambda i: (i % n_seq_tiles, 0))
    row = lambda w: pl.BlockSpec((tm, w), lambda i: (i, 0))
    qkv_specs = [_phase_spec(tm, dil, n_seq_tiles) for _ in range(3) for _, dil in ATT_WINDOWS]
    qkv_shapes = [jax.ShapeDtypeStruct((batch, dil, seq // dil, GROUP_WIDTH), BF16)
                  for _ in range(3) for _, dil in ATT_WINDOWS]
    outs = pl.pallas_call(
        _in_proj_kernel,
        grid=(T // tm,),
        in_specs=[row(D), const((1, D)), const((D, in_width)), const((1, ATT_WIDTH)), const((1, ATT_WIDTH)),
                  const((MXU_DIM, MXU_DIM)), rope, rope, rope, const((1, SGU_WIDTH)), const((1, SGU_WIDTH))],
        out_specs=qkv_specs + [row(SGU_WIDTH), row(SGU_WIDTH)],
        out_shape=qkv_shapes + [jax.ShapeDtypeStruct((T, SGU_WIDTH), F32), jax.ShapeDtypeStruct((T, SGU_WIDTH), BF16)],
        scratch_shapes=[pltpu.VMEM((3 * ATT_WIDTH // LANES, tm, LANES), F32)],
        compiler_params=pltpu.CompilerParams(dimension_semantics=("arbitrary",), vmem_limit_bytes=VMEM_LIMIT),
        name="in_proj",
    )(x2, p["attn_norm_w"], p["w_in"], p["q_norm_w"], p["k_norm_w"], p["head_ones"],
      *p["rope"][seq], p["sgu_ln_w"], p["sgu_ln_b"])
    q, k, v = (outs[i * N_GROUPS:(i + 1) * N_GROUPS] for i in range(3))
    return q, k, v, outs[-2], outs[-1]


def _attn_kernel(q_ref, kp_ref, km_ref, kn_ref, vp_ref, vm_ref, vn_ref, o_ref, lse_ref, *, sub_len, steps):
    n_phases, lq, _ = q_ref.shape
    sq = min(lq, ATT_SUB_Q)
    lk = sq + 2 * steps
    nh = HEADS_PER_GROUP
    lb = pl.program_id(2)
    head = lax.broadcasted_iota(I32, (1, GROUP_WIDTH), 1) // HEAD_DIM
    lse_head = lax.broadcasted_iota(I32, (1, LANES), 1) // LSE_LANES
    hm_b = [(head == h).astype(BF16) for h in range(nh)]
    qi = lax.broadcasted_iota(I32, (sq, 1), 0)
    kj = lax.broadcasted_iota(I32, (sq, lk), 1)
    for ph in range(n_phases):
        kk = jnp.concatenate([kp_ref[ph], km_ref[ph], kn_ref[ph]], axis=0)
        vv = jnp.concatenate([vp_ref[ph], vm_ref[ph], vn_ref[ph]], axis=0)
        for j in range(lq // sq):
            first = lb * lq + j * sq
            lo = jnp.maximum(qi, steps - first)
            hi = jnp.minimum(qi + 2 * steps, sub_len - 1 + steps - first)
            mask = ((kj - lo).astype(jnp.uint32) <= (hi - lo).astype(jnp.uint32))[None]
            qj = q_ref[ph, j * sq:(j + 1) * sq, :]
            qs = jnp.concatenate([qj * hm_b[h] for h in range(nh)], axis=0)
            s = _dot_nt(qs, kk[j * sq:j * sq + lk]).reshape(nh, sq, lk)
            s = jnp.where(mask, s, NEG_INF)
            m = jnp.max(s, axis=-1, keepdims=True)
            pr = jnp.exp(s - m)
            den = jnp.sum(pr, axis=-1, keepdims=True)
            oh = _dot(pr.reshape(nh * sq, lk).astype(BF16), vv[j * sq:j * sq + lk]).reshape(nh, sq, GROUP_WIDTH) / den
            lh = m + jnp.log(den)
            o, lse = oh[0], jnp.broadcast_to(lh[0], (sq, LANES))
            for h in range(1, nh):
                o = jnp.where(head == h, oh[h], o)
                lse = jnp.where(lse_head == h, lh[h], lse)
            o_ref[ph, j * sq:(j + 1) * sq, :] = o.astype(BF16)
            lse_ref[ph, j * sq:(j + 1) * sq, :] = lse


def _attention(q, k, v, group):
    window, dil = ATT_WINDOWS[group]
    steps = window // (2 * dil)
    batch, _, sub_len, _ = q.shape
    assert sub_len % steps == 0
    lq = min(ATT_Q_BLOCK, sub_len)
    assert lq % steps == 0 and sub_len % lq == 0
    per_q = lq // steps
    n_halo = sub_len // steps
    n_ph = min(dil, ATT_Q_BLOCK // lq)
    assert dil % n_ph == 0
    main = pl.BlockSpec((None, n_ph, lq, GROUP_WIDTH), lambda b, r, i: (b, r, i, 0))
    prev = pl.BlockSpec((None, n_ph, steps, GROUP_WIDTH), lambda b, r, i: (b, r, jnp.maximum(i * per_q - 1, 0), 0))
    nxt = pl.BlockSpec((None, n_ph, steps, GROUP_WIDTH),
                       lambda b, r, i: (b, r, jnp.minimum((i + 1) * per_q, n_halo - 1), 0))
    return pl.pallas_call(
        functools.partial(_attn_kernel, sub_len=sub_len, steps=steps),
        grid=(batch, dil // n_ph, sub_len // lq),
        in_specs=[main, prev, main, nxt, prev, main, nxt],
        out_specs=[main, pl.BlockSpec((None, n_ph, lq, LANES), lambda b, r, i: (b, r, i, 0))],
        out_shape=[jax.ShapeDtypeStruct(q.shape, BF16), jax.ShapeDtypeStruct(q.shape[:3] + (LANES,), F32)],
        compiler_params=pltpu.CompilerParams(dimension_semantics=("arbitrary",) * 3, vmem_limit_bytes=VMEM_LIMIT),
        name=f"attention_g{group}",
    )(q, k, k, k, v, v, v)


def _post_mix_kernel(x_ref, *refs):
    o_refs = refs[:N_GROUPS]
    l_refs = refs[N_GROUPS:2 * N_GROUPS]
    (zu_ref, zv_ref, sguw_ref, sgub_ref, aonw_ref, sonw_ref, wout_ref, fnw_ref, rwt_ref, rb_ref, tri_ref,
     x1_ref, h2_hbm, gates_ref, eidx_ref, rank_ref, cnt_ref, stage, hstage, hsem) = refs[2 * N_GROUPS:]
    tm = x_ref.shape[0]

    def token_order(ref, slot):
        dil = ref.shape[0]
        if dil == 1:
            return ref[0].astype(F32)
        n_chunks = ref.shape[2] // LANES
        per_group = GROUP_WIDTH // LANES
        for r in range(dil):
            for c in range(n_chunks):
                stage[slot * per_group + c, pl.ds(r, tm // dil, stride=dil), :] = (
                    ref[r, :, c * LANES:(c + 1) * LANES].astype(F32))
        return jnp.concatenate([stage[slot * per_group + c] for c in range(n_chunks)], axis=1)

    def per_column(w):
        lane = lax.broadcasted_iota(I32, (1, LANES), 1)
        r1 = pltpu.roll(w, LSE_LANES, 1)
        r2 = pltpu.roll(w, 2 * LSE_LANES, 1)
        r3 = pltpu.roll(w, 3 * LSE_LANES, 1)
        low = jnp.where(lane < LSE_LANES, w, jnp.where(lane < 3 * LSE_LANES, r1, r2))
        high = jnp.where(lane < LSE_LANES, r2, jnp.where(lane < 3 * LSE_LANES, r3, w))
        return jnp.concatenate([low, high], axis=1)

    o = [token_order(ref, g) for g, ref in enumerate(o_refs)]
    l = [token_order(ref, N_GROUPS + g) for g, ref in enumerate(l_refs)]
    lmax = jnp.maximum(jnp.maximum(l[0], l[1]), l[2])
    e = [jnp.exp(lg - lmax) for lg in l]
    esum = e[0] + e[1] + e[2]
    att = (per_column(e[0] / esum) * o[0] + per_column(e[1] / esum) * o[1] + per_column(e[2] / esum) * o[2])
    att_n = _rms(att, aonw_ref[...]).astype(BF16)
    cgrp = lax.broadcasted_iota(I32, (1, SGU_WIDTH), 1) // SGU_GROUP_DIM
    sguw = sguw_ref[...]
    gates = []
    for c in range(tm // SGU_CHUNK):
        r = _dot(sguw, zv_ref[c * SGU_CHUNK:(c + 1) * SGU_CHUNK, :])
        g = sgub_ref[...]
        for grp in range(SGU_GROUPS):
            g = g + r[grp * SGU_CHUNK:(grp + 1) * SGU_CHUNK, :] * (cgrp == grp).astype(F32)
        gates.append(g)
    sgu = zu_ref[...] * jnp.concatenate(gates, axis=0)
    sgu_n = _rms(sgu, sonw_ref[...]).astype(BF16)
    x1 = x_ref[...] + _dot(att_n, wout_ref[0:GROUP_WIDTH, :]) + _dot(sgu_n, wout_ref[GROUP_WIDTH:, :])
    x1_ref[...] = x1
    h2 = _rms(x1, fnw_ref[...])
    step = pl.program_id(0)
    slot = step % 2
    bt = tm // SUBLANES

    def h2_copies(tile, s):
        return [pltpu.make_async_copy(hstage.at[s, :, :, j, :], h2_hbm.at[pl.ds(tile * bt, bt), j], hsem.at[s])
                for j in range(SUBLANES)]

    _to_tiled(hstage, (slot,), _pack_pairs(h2))
    for c in h2_copies(step, slot):
        c.start()

    @pl.when(step > 0)
    def _():
        for c in h2_copies(step - 1, 1 - slot):
            c.wait()
    logits = _dot_nt(rwt_ref[...], h2.astype(BF16)) + rb_ref[...]
    eiota = lax.broadcasted_iota(I32, (N_EXPERTS, tm), 0)
    vals, idxs = [], []
    for _ in range(TOP_K):
        m = jnp.max(logits, axis=0, keepdims=True)
        idx = jnp.min(jnp.where(logits == m, eiota, N_EXPERTS), axis=0, keepdims=True)
        vals.append(m)
        idxs.append(idx)
        logits = jnp.where(eiota == idx, -jnp.inf, logits)
    exps = [jnp.exp(v - vals[0]) for v in vals]
    den = exps[0] + exps[1] + exps[2] + exps[3]
    gates_ref[...] = jnp.concatenate([ex / den for ex in exps], axis=0)
    eidx_ref[...] = jnp.concatenate(idxs, axis=0)
    onehots = [(eiota == idx).astype(F32) for idx in idxs]
    chosen = onehots[0] + onehots[1] + onehots[2] + onehots[3]
    before = _dot(chosen.astype(BF16), tri_ref[...])
    rank_ref[...] = jnp.concatenate(
        [jnp.sum(oh * before, axis=0, keepdims=True) for oh in onehots], axis=0).astype(I32)
    cnt_ref[...] = jnp.broadcast_to(jnp.sum(chosen, axis=1, keepdims=True), (N_EXPERTS, LANES)).astype(I32)

    @pl.when(step == pl.num_programs(0) - 1)
    def _():
        for c in h2_copies(step, slot):
            c.wait()


def _post_mix(x2, seq, o, lse, zu, zv, p):
    T, D = x2.shape
    tm = TOKEN_TILE
    n_tiles = T // tm
    n_seq_tiles = seq // tm
    chunks = D // (2 * LANES)
    const = lambda shape: pl.BlockSpec(shape, lambda i: (0,) * len(shape))
    row = lambda w: pl.BlockSpec((tm, w), lambda i: (i, 0))
    colt = pl.BlockSpec((TOP_K, tm), lambda i: (0, i))
    phase = [_phase_spec(tm, dil, n_seq_tiles) for _, dil in ATT_WINDOWS]
    lse_phase = [_phase_spec(tm, dil, n_seq_tiles, LANES) for _, dil in ATT_WINDOWS]
    return pl.pallas_call(
        _post_mix_kernel,
        grid=(n_tiles,),
        in_specs=[row(D)] + phase + lse_phase + [row(SGU_WIDTH), row(SGU_WIDTH),
                  const((SGU_GROUPS * SGU_CHUNK, SGU_CHUNK)), const((SGU_CHUNK, SGU_WIDTH)),
                  const((1, GROUP_WIDTH)), const((1, SGU_WIDTH)), const((GROUP_WIDTH + SGU_WIDTH, D)),
                  const((1, D)), const((N_EXPERTS, D)), const((N_EXPERTS, 1)), const((tm, tm))],
        out_specs=[row(D), pl.BlockSpec(memory_space=pl.ANY),
                   colt, colt, colt, pl.BlockSpec((None, N_EXPERTS, LANES), lambda i: (i, 0, 0))],
        out_shape=[jax.ShapeDtypeStruct((T, D), F32),
                   jax.ShapeDtypeStruct((T // SUBLANES, SUBLANES, chunks, LANES), F32),
                   jax.ShapeDtypeStruct((TOP_K, T), F32), jax.ShapeDtypeStruct((TOP_K, T), I32),
                   jax.ShapeDtypeStruct((TOP_K, T), I32), jax.ShapeDtypeStruct((n_tiles, N_EXPERTS, LANES), I32)],
        scratch_shapes=[pltpu.VMEM((2 * N_GROUPS * GROUP_WIDTH // LANES, tm, LANES), F32),
                        pltpu.VMEM((2, tm // SUBLANES, chunks, SUBLANES, LANES), F32), pltpu.SemaphoreType.DMA((2,))],
        compiler_params=pltpu.CompilerParams(dimension_semantics=("arbitrary",), vmem_limit_bytes=VMEM_LIMIT),
        name="post_mix",
    )(x2, *o, *lse, zu, zv, p["sgu_w"], p["sgu_b"], p["att_out_norm_w"],
      p["sgu_out_norm_w"], p["w_out"], p["ffn_norm_w"], p["router_wt"], p["router_b"], p["tri"])


def _sc_workers():
    info = plsc.get_sparse_core_info()
    return info.num_cores, info.num_cores * info.num_subcores


def _dispatch_sc(h2p, pos, pad_pos, n_rows):
    T, chunks, _ = h2p.shape
    n_cores, n_workers = _sc_workers()
    win = SC_WINDOW
    per_worker = T // n_workers
    pad_per_worker = pad_pos.shape[0] // n_workers
    assert per_worker % win == 0 and pad_per_worker % win == 0
    mesh = plsc.VectorSubcoreMesh(core_axis_name="c", subcore_axis_name="s")

    @functools.partial(
        pl.kernel, mesh=mesh, out_type=jax.ShapeDtypeStruct((n_rows, chunks, LANES), F32),
        scratch_types=[pltpu.VMEM((win,), I32), pltpu.VMEM((win, chunks, LANES), F32)],
        name="dispatch_sc")
    def scatter_rows(h_hbm, pos_hbm, pad_hbm, zeros_hbm, xs_hbm, idx_v, rows_v):
        worker = lax.axis_index("s") * n_cores + lax.axis_index("c")

        @pl.loop(0, per_worker // win)
        def _(w):
            t0 = worker * per_worker + w * win
            pltpu.sync_copy(h_hbm.at[pl.ds(t0, win)], rows_v)
            for k in range(TOP_K):
                pltpu.sync_copy(pos_hbm.at[pl.ds(k * T + t0, win)], idx_v)
                pltpu.sync_copy(rows_v, xs_hbm.at[idx_v])

        pltpu.sync_copy(zeros_hbm, rows_v)

        @pl.loop(0, pad_per_worker // win)
        def _(w):
            pltpu.sync_copy(pad_hbm.at[pl.ds(worker * pad_per_worker + w * win, win)], idx_v)
            pltpu.sync_copy(rows_v, xs_hbm.at[idx_v])

    return scatter_rows(h2p, pos, pad_pos, jnp.zeros((win, chunks, LANES), F32))


def _gather_sc(ys, pos):
    _, chunks, _ = ys.shape
    n_assign = pos.shape[0]
    n_cores, n_workers = _sc_workers()
    win = SC_WINDOW
    per_worker = n_assign // n_workers
    assert per_worker % win == 0
    mesh = plsc.VectorSubcoreMesh(core_axis_name="c", subcore_axis_name="s")

    @functools.partial(
        pl.kernel, mesh=mesh, out_type=jax.ShapeDtypeStruct((n_assign, chunks, LANES), F32),
        scratch_types=[pltpu.VMEM((win,), I32), pltpu.VMEM((win, chunks, LANES), F32)],
        name="gather_sc")
    def gather_rows(ys_hbm, pos_hbm, out_hbm, idx_v, rows_v):
        worker = lax.axis_index("s") * n_cores + lax.axis_index("c")

        @pl.loop(0, per_worker // win)
        def _(w):
            a0 = worker * per_worker + w * win
            pltpu.sync_copy(pos_hbm.at[pl.ds(a0, win)], idx_v)
            pltpu.sync_copy(ys_hbm.at[idx_v], rows_v)
            pltpu.sync_copy(rows_v, out_hbm.at[pl.ds(a0, win)])

    return gather_rows(ys, pos)


def _experts_kernel(blk_e_ref, n_used_ref, xs_hbm, wg_ref, bg_ref, wu_ref, bu_ref, wd_ref, bd_ref, ys_hbm,
                    xbuf, obuf, wg_b, wu_b, wd_b, in_sem, out_sem):
    bt = xbuf.shape[1]
    i = pl.program_id(0)
    n = pl.num_programs(0)
    n_used = n_used_ref[0]
    slot = i % 2

    def in_copies(blk, s):
        return [pltpu.make_async_copy(xs_hbm.at[pl.ds(blk * bt, bt), j], xbuf.at[s, :, :, j, :], in_sem.at[s])
                for j in range(SUBLANES)]

    def out_copies(blk, s):
        return [pltpu.make_async_copy(obuf.at[s, :, :, j, :], ys_hbm.at[pl.ds(blk * bt, bt), j], out_sem.at[s])
                for j in range(SUBLANES)]

    @pl.when(i == 0)
    def _():
        for c in in_copies(0, 0):
            c.start()

    @pl.when(i + 1 < n_used)
    def _():
        for c in in_copies(i + 1, 1 - slot):
            c.start()

    @pl.when(i >= 2)
    def _():
        for c in out_copies(i - 2, slot):
            c.wait()

    @pl.when(i >= n_used)
    def _():
        obuf[slot] = jnp.zeros(obuf.shape[1:], F32)

    @pl.when(i < n_used)
    def _():
        for c in in_copies(i, slot):
            c.wait()
        @pl.when((i == 0) | (blk_e_ref[i] != blk_e_ref[jnp.maximum(i - 1, 0)]))
        def _():
            wg_b[...] = wg_ref[...].astype(BF16)
            wu_b[...] = wu_ref[...].astype(BF16)
            wd_b[...] = wd_ref[...].astype(BF16)

        x = _unpack_pairs(_from_tiled(xbuf, (slot,), slice(0, bt))).astype(BF16)
        g = _dot(x, wg_b[...]) + bg_ref[...]
        u = _dot(x, wu_b[...]) + bu_ref[...]
        g = jnp.minimum(g, SWIGLU_LIMIT)
        u = jnp.clip(u, -SWIGLU_LIMIT, SWIGLU_LIMIT)
        act = (u + 1.0) * (g * jax.nn.sigmoid(SWIGLU_ALPHA * g))
        _to_tiled(obuf, (slot,), _pack_pairs(_dot(act.astype(BF16), wd_b[...]) + bd_ref[...]))

    for c in out_copies(i, slot):
        c.start()

    @pl.when(i == n - 1)
    def _():
        for c in out_copies(i - 1, 1 - slot) + out_copies(i, slot):
            c.wait()


def _experts(xs, blk_e, n_used, p):
    n_rows, chunks, _ = xs.shape
    D = 2 * chunks * LANES
    bm = MOE_BLOCK
    d_ff = p["w_gate"].shape[2]
    assert n_rows // bm >= 2
    wspec = lambda a, b: pl.BlockSpec((None, a, b), lambda i, be, nu: (be[i], 0, 0))
    by_tile = (n_rows // SUBLANES, SUBLANES, chunks, LANES)
    block_buf = pltpu.VMEM((2, bm // SUBLANES, chunks, SUBLANES, LANES), F32)
    ys = pl.pallas_call(
        _experts_kernel,
        grid_spec=pltpu.PrefetchScalarGridSpec(
            num_scalar_prefetch=2,
            grid=(n_rows // bm,),
            in_specs=[pl.BlockSpec(memory_space=pl.ANY), wspec(D, d_ff), wspec(1, d_ff), wspec(D, d_ff),
                      wspec(1, d_ff), wspec(d_ff, D), wspec(1, D)],
            out_specs=pl.BlockSpec(memory_space=pl.ANY),
            scratch_shapes=[block_buf, block_buf, pltpu.VMEM((D, d_ff), BF16), pltpu.VMEM((D, d_ff), BF16),
                            pltpu.VMEM((d_ff, D), BF16), pltpu.SemaphoreType.DMA((2,)), pltpu.SemaphoreType.DMA((2,))],
        ),
        out_shape=jax.ShapeDtypeStruct(by_tile, F32),
        compiler_params=pltpu.CompilerParams(dimension_semantics=("arbitrary",), vmem_limit_bytes=EXPERTS_VMEM_LIMIT),
        name="experts",
    )(blk_e, n_used, xs.reshape(by_tile), p["w_gate"], p["b_gate"], p["w_up"], p["b_up"], p["w_down"], p["b_down"])
    return ys.reshape(n_rows, chunks, LANES)


def _combine_kernel(gates_ref, x1_ref, yg_hbm, out_ref, buf, sem):
    tm = x1_ref.shape[0]
    bt = tm // SUBLANES
    j = pl.program_id(0)
    n = pl.num_programs(0) - 1
    tiles_per_k = yg_hbm.shape[0] // TOP_K

    def copies(tile, s):
        return [pltpu.make_async_copy(yg_hbm.at[pl.ds(k * tiles_per_k + tile * bt, bt), r],
                                      buf.at[s, pl.ds(k * bt, bt), :, r, :], sem.at[s])
                for k in range(TOP_K) for r in range(SUBLANES)]

    @pl.when(j < n)
    def _():
        for c in copies(j, j % 2):
            c.start()

    @pl.when(j >= 1)
    def _():
        done = 1 - j % 2
        for c in copies(j - 1, done):
            c.wait()
        y = gates_ref[:, 0:1] * _unpack_pairs(_from_tiled(buf, (done,), slice(0, bt)))
        for k in range(1, TOP_K):
            y = y + gates_ref[:, k:k + 1] * _unpack_pairs(_from_tiled(buf, (done,), slice(k * bt, (k + 1) * bt)))
        out_ref[...] = x1_ref[...] + y


def _combine(x1, gates_tok, yg):
    T, D = x1.shape
    tm = ROW_TILE
    n = T // tm
    chunks = D // (2 * LANES)
    lag = lambda i: (jnp.maximum(i - 1, 0), 0)
    return pl.pallas_call(
        _combine_kernel,
        grid=(n + 1,),
        in_specs=[pl.BlockSpec((tm, TOP_K), lag), pl.BlockSpec((tm, D), lag), pl.BlockSpec(memory_space=pl.ANY)],
        out_specs=pl.BlockSpec((tm, D), lag),
        out_shape=jax.ShapeDtypeStruct((T, D), F32),
        scratch_shapes=[pltpu.VMEM((2, TOP_K * tm // SUBLANES, chunks, SUBLANES, LANES), F32),
                        pltpu.SemaphoreType.DMA((2,))],
        compiler_params=pltpu.CompilerParams(dimension_semantics=("arbitrary",), vmem_limit_bytes=VMEM_LIMIT),
        name="combine",
    )(gates_tok, x1, yg.reshape(TOP_K * T // SUBLANES, SUBLANES, chunks, LANES))


def _route(eidx, rank, tile_counts):
    T = eidx.shape[1]
    bm = MOE_BLOCK
    n_tiles = T // TOKEN_TILE
    tc = tile_counts[:, :, 0]
    counts = jnp.sum(tc, axis=0)
    tile_off = jnp.cumsum(tc, axis=0) - tc
    pcounts = (counts + bm - 1) // bm * bm
    pends = jnp.cumsum(pcounts)
    pstarts = pends - pcounts
    base = pstarts[None, :] + tile_off
    sel = eidx.reshape(TOP_K, n_tiles, 1, TOKEN_TILE) == jnp.arange(N_EXPERTS, dtype=I32)[None, None, :, None]
    pos = jnp.sum(jnp.where(sel, base[None, :, :, None], 0), axis=2).reshape(TOP_K, T) + rank
    n_blocks = (TOP_K * T) // bm + N_EXPERTS
    n_rows = n_blocks * bm
    blk_start = jnp.arange(n_blocks, dtype=I32) * bm
    blk_e = jnp.minimum(jnp.sum((pends[None, :] <= blk_start[:, None]).astype(I32), axis=1), N_EXPERTS - 1)
    n_used = (pends[-1] // bm).astype(I32).reshape(1)
    first_free = jnp.concatenate([pstarts + counts, pends[-1:]])
    free_off = jnp.cumsum(jnp.concatenate([jnp.zeros((1,), counts.dtype), pcounts - counts]))
    j = jnp.arange(n_rows - TOP_K * T, dtype=I32)[:, None]
    seg_id = jnp.sum((free_off[None, 1:] <= j).astype(I32), axis=1, keepdims=True)
    onehot = seg_id == jnp.arange(N_EXPERTS + 1, dtype=I32)[None, :]
    pad_pos = jnp.sum(jnp.where(onehot, (first_free - free_off)[None, :] + j, 0), axis=1).astype(I32)
    return pos.reshape(-1).astype(I32), pad_pos, blk_e.astype(I32), n_used, n_rows


def _layer(x, p):
    batch, seq, D = x.shape
    x2 = x.reshape(batch * seq, D)
    q, k, v, zu, zv = _in_proj(x2, batch, seq, p)
    o, lse = zip(*[_attention(q[g], k[g], v[g], g) for g in range(N_GROUPS)])
    x1, h2p, gates, eidx, rank, tile_counts = _post_mix(x2, seq, o, lse, zu, zv, p)
    pos, pad_pos, blk_e, n_used, n_rows = _route(eidx, rank, tile_counts)
    xs = _dispatch_sc(h2p.reshape(batch * seq, D // (2 * LANES), LANES), pos, pad_pos, n_rows)
    ys = _experts(xs, blk_e, n_used, p)
    out = _combine(x1, gates.T, _gather_sc(ys, pos))
    return out.reshape(batch, seq, D)


def _rope_tables(seq):
    half = ROT_DIM // 2
    inv_freq = jnp.power(ROPE_THETA, -2.0 * jnp.arange(half, dtype=F32) / ROT_DIM)
    ang = jnp.arange(seq, dtype=F32)[:, None] * inv_freq[None, :]
    cos, sin = jnp.cos(ang), jnp.sin(ang)
    zeros = jnp.zeros((seq, HEAD_DIM - ROT_DIM), F32)
    zh = jnp.zeros((seq, half), F32)
    cos_t = jnp.concatenate([cos, cos, zeros + 1.0], axis=1)
    sa = jnp.concatenate([-sin, zh, zeros], axis=1)
    sb = jnp.concatenate([zh, sin, zeros], axis=1)
    rep = LANES // HEAD_DIM
    return tuple(jnp.tile(t, (1, rep)) for t in (cos_t, sa, sb))


def kernel(x_prompt, x_sample, attn_norm_w, w_in, q_norm_w, k_norm_w, sgu_ln_w, sgu_ln_b, sgu_w, sgu_b,
           att_out_norm_w, sgu_out_norm_w, w_out, ffn_norm_w, router_w, router_b,
           w_gate, b_gate, w_up, b_up, w_down, b_down):
    depth = w_in.shape[0]
    n_heads = ATT_WIDTH // HEAD_DIM
    blk = np.arange(MXU_DIM) // HEAD_DIM
    tri = np.arange(TOKEN_TILE)
    xa, xb = x_prompt, x_sample
    rope = {x.shape[1]: _rope_tables(x.shape[1]) for x in (xa, xb)}
    for l in range(depth):
        shared = dict(
            attn_norm_w=attn_norm_w[l][None], w_in=w_in[l].astype(BF16),
            q_norm_w=jnp.tile(q_norm_w[l], n_heads)[None], k_norm_w=jnp.tile(k_norm_w[l], n_heads)[None],
            head_ones=jnp.asarray(blk[:, None] == blk[None, :], BF16),
            sgu_ln_w=sgu_ln_w[l][None], sgu_ln_b=sgu_ln_b[l][None],
            sgu_w=sgu_w[l].reshape(SGU_GROUPS * SGU_CHUNK, SGU_CHUNK).astype(BF16),
            sgu_b=jnp.repeat(sgu_b[l].T, SGU_GROUP_DIM, axis=1),
            att_out_norm_w=att_out_norm_w[l][None], sgu_out_norm_w=sgu_out_norm_w[l][None],
            w_out=w_out[l].astype(BF16), ffn_norm_w=ffn_norm_w[l][None],
            router_wt=router_w[l].T.astype(BF16), router_b=router_b[l][:, None],
            tri=jnp.asarray(tri[:, None] < tri[None, :], BF16),
            w_gate=w_gate[l], b_gate=b_gate[l][:, None, :],
            w_up=w_up[l], b_up=b_up[l][:, None, :],
            w_down=w_down[l], b_down=b_down[l][:, None, :],
        )
        xa, xb = (_layer(x, dict(shared, rope=rope)) for x in (xa, xb))
    return xa, xb
```

```python
import functools

import numpy as np
import jax
import jax.numpy as jnp
from jax import lax
from jax.experimental import pallas as pl
from jax.experimental.pallas import tpu as pltpu
from jax.experimental.pallas import tpu_sc as plsc

F32 = jnp.float32
BF16 = jnp.bfloat16
I32 = jnp.int32

HEAD_DIM = 64
ATT_WINDOWS = ((128, 1), (512, 4), (2048, 16))
N_GROUPS = len(ATT_WINDOWS)
HEADS_PER_GROUP = 4
GROUP_WIDTH = HEADS_PER_GROUP * HEAD_DIM
ATT_WIDTH = N_GROUPS * GROUP_WIDTH
SGU_WIDTH = 256
SGU_GROUP_DIM = 64
SGU_GROUPS = SGU_WIDTH // SGU_GROUP_DIM
SGU_CHUNK = 128
ROT_DIM = HEAD_DIM // 4
ROPE_THETA = 500000.0
N_EXPERTS = 32
TOP_K = 4
SWIGLU_LIMIT = 7.0
SWIGLU_ALPHA = 1.702
NORM_EPS = 1e-6
LN_EPS = 1e-5
NEG_INF = -1e30

LANES = 128
SUBLANES = 8
MXU_DIM = 256
TOKEN_TILE = 512
ATT_Q_BLOCK = 2048
ATT_SUB_Q = 128
LSE_LANES = LANES // HEADS_PER_GROUP
MOE_BLOCK = 512
ROW_TILE = 512
SC_WINDOW = 64
VMEM_LIMIT = 48 * 1024 * 1024
EXPERTS_VMEM_LIMIT = 56 * 1024 * 1024


def _dot(a, b):
    return jnp.dot(a, b, preferred_element_type=F32)


def _dot_nt(a, b, precision=None):
    return lax.dot_general(a, b, (((1,), (1,)), ((), ())), precision=precision, preferred_element_type=F32)


def _rms(x, w):
    return x * lax.rsqrt(jnp.mean(x * x, axis=-1, keepdims=True) + NORM_EPS) * w


def _gelu(x):
    return 0.5 * x * (1.0 + lax.erf(x * np.float32(np.sqrt(0.5))))


def _pack_pairs(x):
    w = x.shape[1] // 2
    lo = lax.bitcast_convert_type(x[:, :w].astype(BF16).astype(F32), jnp.uint32) >> 16
    hi = lax.bitcast_convert_type(x[:, w:].astype(BF16).astype(F32), jnp.uint32) & jnp.uint32(0xFFFF0000)
    return lax.bitcast_convert_type(lo | hi, F32)


def _unpack_pairs(words):
    u = lax.bitcast_convert_type(words, jnp.uint32)
    lo = lax.bitcast_convert_type(u << 16, F32)
    hi = lax.bitcast_convert_type(u & jnp.uint32(0xFFFF0000), F32)
    return jnp.concatenate([lo, hi], axis=1)


def _to_tiled(ref, idx, value):
    rows = value.shape[0]
    for c in range(value.shape[1] // LANES):
        ref[idx + (slice(None), c)] = value[:, c * LANES:(c + 1) * LANES].reshape(rows // SUBLANES, SUBLANES, LANES)


def _from_tiled(ref, idx, row_tiles):
    chunks = ref.shape[-3]
    n = (row_tiles.stop - row_tiles.start) * SUBLANES
    return jnp.concatenate([ref[idx + (row_tiles, c)].reshape(n, LANES) for c in range(chunks)], axis=1)


def _in_proj_kernel(x_ref, anw_ref, w_ref, qnw_ref, knw_ref, ones_ref, cos_ref, sa_ref, sb_ref, lnw_ref, lnb_ref,
                    *refs):
    qkv_refs = refs[:3 * N_GROUPS]
    zu_ref, zv_ref, stage = refs[3 * N_GROUPS:]
    tm = x_ref.shape[0]
    h = _rms(x_ref[...], anw_ref[...]).astype(BF16)
    reps = GROUP_WIDTH // LANES
    cos = jnp.concatenate([cos_ref[...]] * reps, axis=1)
    sa = jnp.concatenate([sa_ref[...]] * reps, axis=1)
    sb = jnp.concatenate([sb_ref[...]] * reps, axis=1)
    ones = ones_ref[...]
    half = ROT_DIM // 2

    def head_norm_rope(t, nw):
        ssum = _dot((t * t).astype(BF16), ones)
        t = t * lax.rsqrt(ssum * (1.0 / HEAD_DIM) + NORM_EPS) * nw
        return t * cos + pltpu.roll(t, GROUP_WIDTH - half, 1) * sa + pltpu.roll(t, half, 1) * sb

    def emit(which, g, t):
        dil = ATT_WINDOWS[g][1]
        out = qkv_refs[which * N_GROUPS + g]
        if dil == 1:
            out[0] = t.astype(BF16)
            return
        first = (which * N_GROUPS + g) * reps
        for c in range(reps):
            stage[first + c] = t[:, c * LANES:(c + 1) * LANES]
        for r in range(dil):
            rows = pl.ds(r, tm // dil, stride=dil)
            out[r] = jnp.concatenate([stage[first + c, rows, :] for c in range(reps)], axis=1).astype(BF16)

    for which, nw_ref in enumerate((qnw_ref, knw_ref, None)):
        full = _dot(h, w_ref[:, which * ATT_WIDTH:(which + 1) * ATT_WIDTH])
        for g in range(N_GROUPS):
            cols = slice(g * GROUP_WIDTH, (g + 1) * GROUP_WIDTH)
            t = full[:, cols]
            if nw_ref is not None:
                t = head_norm_rope(t, nw_ref[:, cols])
            if which == 0:
                t = t * (HEAD_DIM ** -0.5)
            emit(which, g, t)
    z = _dot(h, w_ref[:, 3 * ATT_WIDTH:3 * ATT_WIDTH + 2 * SGU_WIDTH])
    zu_ref[...] = _gelu(z[:, :SGU_WIDTH])
    gv = _gelu(z[:, SGU_WIDTH:])
    mu = jnp.mean(gv, axis=-1, keepdims=True)
    var = jnp.mean(jnp.square(gv - mu), axis=-1, keepdims=True)
    zv_ref[...] = ((gv - mu) * lax.rsqrt(var + LN_EPS) * lnw_ref[...] + lnb_ref[...]).astype(BF16)


def _phase_spec(tm, dil, n_seq_tiles, width=GROUP_WIDTH):
    return pl.BlockSpec((None, dil, tm // dil, width), lambda i: (i // n_seq_tiles, 0, i % n_seq_tiles, 0))


def _in_proj(x2, batch, seq, p):
    T, D = x2.shape
    tm = TOKEN_TILE
    n_seq_tiles = seq // tm
    in_width = p["w_in"].shape[1]
    const = lambda shape: pl.BlockSpec(shape, lambda i: (0,) * len(shape))
    rope = pl.BlockSpec((tm, LANES), lambda i: (i % n_seq_tiles, 0))
    row = lambda w: pl.BlockSpec((tm, w), lambda i: (i, 0))
    qkv_specs = [_phase_spec(tm, dil, n_seq_tiles) for _ in range(3) for _, dil in ATT_WINDOWS]
    qkv_shapes = [jax.ShapeDtypeStruct((batch, dil, seq // dil, GROUP_WIDTH), BF16)
                  for _ in range(3) for _, dil in ATT_WINDOWS]
    outs = pl.pallas_call(
        _in_proj_kernel,
        grid=(T // tm,),
        in_specs=[row(D), const((1, D)), const((D, in_width)), const((1, ATT_WIDTH)), const((1, ATT_WIDTH)),
                  const((MXU_DIM, MXU_DIM)), rope, rope, rope, const((1, SGU_WIDTH)), const((1, SGU_WIDTH))],
        out_specs=qkv_specs + [row(SGU_WIDTH), row(SGU_WIDTH)],
        out_shape=qkv_shapes + [jax.ShapeDtypeStruct((T, SGU_WIDTH), F32), jax.ShapeDtypeStruct((T, SGU_WIDTH), BF16)],
        scratch_shapes=[pltpu.VMEM((3 * ATT_WIDTH // LANES, tm, LANES), F32)],
        compiler_params=pltpu.CompilerParams(dimension_semantics=("arbitrary",), vmem_limit_bytes=VMEM_LIMIT),
        name="in_proj",
    )(x2, p["attn_norm_w"], p["w_in"], p["q_norm_w"], p["k_norm_w"], p["head_ones"],
      *p["rope"][seq], p["sgu_ln_w"], p["sgu_ln_b"])
    q, k, v = (outs[i * N_GROUPS:(i + 1) * N_GROUPS] for i in range(3))
    return q, k, v, outs[-2], outs[-1]


def _attn_kernel(q_ref, kp_ref, km_ref, kn_ref, vp_ref, vm_ref, vn_ref, o_ref, lse_ref, *, sub_len, steps):
    n_phases, lq, _ = q_ref.shape
    sq = min(lq, ATT_SUB_Q)
    lk = sq + 2 * steps
    nh = HEADS_PER_GROUP
    lb = pl.program_id(2)
    head = lax.broadcasted_iota(I32, (1, GROUP_WIDTH), 1) // HEAD_DIM
    lse_head = lax.broadcasted_iota(I32, (1, LANES), 1) // LSE_LANES
    hm_b = [(head == h).astype(BF16) for h in range(nh)]
    qi = lax.broadcasted_iota(I32, (sq, 1), 0)
    kj = lax.broadcasted_iota(I32, (sq, lk), 1)
    for ph in range(n_phases):
        kk = jnp.concatenate([kp_ref[ph], km_ref[ph], kn_ref[ph]], axis=0)
        vv = jnp.concatenate([vp_ref[ph], vm_ref[ph], vn_ref[ph]], axis=0)
        for j in range(lq // sq):
            first = lb * lq + j * sq
            lo = jnp.maximum(qi, steps - first)
            hi = jnp.minimum(qi + 2 * steps, sub_len - 1 + steps - first)
            mask = ((kj - lo).astype(jnp.uint32) <= (hi - lo).astype(jnp.uint32))[None]
            qj = q_ref[ph, j * sq:(j + 1) * sq, :]
            qs = jnp.concatenate([qj * hm_b[h] for h in range(nh)], axis=0)
            s = _dot_nt(qs, kk[j * sq:j * sq + lk]).reshape(nh, sq, lk)
            s = jnp.where(mask, s, NEG_INF)
            m = jnp.max(s, axis=-1, keepdims=True)
            pr = jnp.exp(s - m)
            den = jnp.sum(pr, axis=-1, keepdims=True)
            oh = _dot(pr.reshape(nh * sq, lk).astype(BF16), vv[j * sq:j * sq + lk]).reshape(nh, sq, GROUP_WIDTH) / den
            lh = m + jnp.log(den)
            o, lse = oh[0], jnp.broadcast_to(lh[0], (sq, LANES))
            for h in range(1, nh):
                o = jnp.where(head == h, oh[h], o)
                lse = jnp.where(lse_head == h, lh[h], lse)
            o_ref[ph, j * sq:(j + 1) * sq, :] = o.astype(BF16)
            lse_ref[ph, j * sq:(j + 1) * sq, :] = lse


def _attention(q, k, v, group):
    window, dil = ATT_WINDOWS[group]
    steps = window // (2 * dil)
    batch, _, sub_len, _ = q.shape
    assert sub_len % steps == 0
    lq = min(ATT_Q_BLOCK, sub_len)
    assert lq % steps == 0 and sub_len % lq == 0
    per_q = lq // steps
    n_halo = sub_len // steps
    n_ph = min(dil, ATT_Q_BLOCK // lq)
    assert dil % n_ph == 0
    main = pl.BlockSpec((None, n_ph, lq, GROUP_WIDTH), lambda b, r, i: (b, r, i, 0))
    prev = pl.BlockSpec((None, n_ph, steps, GROUP_WIDTH), lambda b, r, i: (b, r, jnp.maximum(i * per_q - 1, 0), 0))
    nxt = pl.BlockSpec((None, n_ph, steps, GROUP_WIDTH),
                       lambda b, r, i: (b, r, jnp.minimum((i + 1) * per_q, n_halo - 1), 0))
    return pl.pallas_call(
        functools.partial(_attn_kernel, sub_len=sub_len, steps=steps),
        grid=(batch, dil // n_ph, sub_len // lq),
        in_specs=[main, prev, main, nxt, prev, main, nxt],
        out_specs=[main, pl.BlockSpec((None, n_ph, lq, LANES), lambda b, r, i: (b, r, i, 0))],
        out_shape=[jax.ShapeDtypeStruct(q.shape, BF16), jax.ShapeDtypeStruct(q.shape[:3] + (LANES,), F32)],
        compiler_params=pltpu.CompilerParams(dimension_semantics=("arbitrary",) * 3, vmem_limit_bytes=VMEM_LIMIT),
        name=f"attention_g{group}",
    )(q, k, k, k, v, v, v)


def _post_mix_kernel(x_ref, *refs):
    o_refs = refs[:N_GROUPS]
    l_refs = refs[N_GROUPS:2 * N_GROUPS]
    (zu_ref, zv_ref, sguw_ref, sgub_ref, aonw_ref, sonw_ref, wout_ref, fnw_ref, rwt_ref, rb_ref, tri_ref,
     x1_ref, h2_hbm, gates_ref, eidx_ref, rank_ref, cnt_ref, stage, hstage, hsem) = refs[2 * N_GROUPS:]
    tm = x_ref.shape[0]

    def token_order(ref, slot):
        dil = ref.shape[0]
        if dil == 1:
            return ref[0].astype(F32)
        n_chunks = ref.shape[2] // LANES
        per_group = GROUP_WIDTH // LANES
        for r in range(dil):
            for c in range(n_chunks):
                stage[slot * per_group + c, pl.ds(r, tm // dil, stride=dil), :] = (
                    ref[r, :, c * LANES:(c + 1) * LANES].astype(F32))
        return jnp.concatenate([stage[slot * per_group + c] for c in range(n_chunks)], axis=1)

    def per_column(w):
        lane = lax.broadcasted_iota(I32, (1, LANES), 1)
        r1 = pltpu.roll(w, LSE_LANES, 1)
        r2 = pltpu.roll(w, 2 * LSE_LANES, 1)
        r3 = pltpu.roll(w, 3 * LSE_LANES, 1)
        low = jnp.where(lane < LSE_LANES, w, jnp.where(lane < 3 * LSE_LANES, r1, r2))
        high = jnp.where(lane < LSE_LANES, r2, jnp.where(lane < 3 * LSE_LANES, r3, w))
        return jnp.concatenate([low, high], axis=1)

    o = [token_order(ref, g) for g, ref in enumerate(o_refs)]
    l = [token_order(ref, N_GROUPS + g) for g, ref in enumerate(l_refs)]
    lmax = jnp.maximum(jnp.maximum(l[0], l[1]), l[2])
    e = [jnp.exp(lg - lmax) for lg in l]
    esum = e[0] + e[1] + e[2]
    att = (per_column(e[0] / esum) * o[0] + per_column(e[1] / esum) * o[1] + per_column(e[2] / esum) * o[2])
    att_n = _rms(att, aonw_ref[...]).astype(BF16)
    cgrp = lax.broadcasted_iota(I32, (1, SGU_WIDTH), 1) // SGU_GROUP_DIM
    sguw = sguw_ref[...]
    gates = []
    for c in range(tm // SGU_CHUNK):
        r = _dot(sguw, zv_ref[c * SGU_CHUNK:(c + 1) * SGU_CHUNK, :])
        g = sgub_ref[...]
        for grp in range(SGU_GROUPS):
            g = g + r[grp * SGU_CHUNK:(grp + 1) * SGU_CHUNK, :] * (cgrp == grp).astype(F32)
        gates.append(g)
    sgu = zu_ref[...] * jnp.concatenate(gates, axis=0)
    sgu_n = _rms(sgu, sonw_ref[...]).astype(BF16)
    x1 = x_ref[...] + _dot(att_n, wout_ref[0:GROUP_WIDTH, :]) + _dot(sgu_n, wout_ref[GROUP_WIDTH:, :])
    x1_ref[...] = x1
    h2 = _rms(x1, fnw_ref[...])
    step = pl.program_id(0)
    slot = step % 2
    bt = tm // SUBLANES

    def h2_copies(tile, s):
        return [pltpu.make_async_copy(hstage.at[s, :, :, j, :], h2_hbm.at[pl.ds(tile * bt, bt), j], hsem.at[s])
                for j in range(SUBLANES)]

    _to_tiled(hstage, (slot,), _pack_pairs(h2))
    for c in h2_copies(step, slot):
        c.start()

    @pl.when(step > 0)
    def _():
        for c in h2_copies(step - 1, 1 - slot):
            c.wait()
    logits = _dot_nt(rwt_ref[...], h2.astype(BF16)) + rb_ref[...]
    eiota = lax.broadcasted_iota(I32, (N_EXPERTS, tm), 0)
    vals, idxs = [], []
    for _ in range(TOP_K):
        m = jnp.max(logits, axis=0, keepdims=True)
        idx = jnp.min(jnp.where(logits == m, eiota, N_EXPERTS), axis=0, keepdims=True)
        vals.append(m)
        idxs.append(idx)
        logits = jnp.where(eiota == idx, -jnp.inf, logits)
    exps = [jnp.exp(v - vals[0]) for v in vals]
    den = exps[0] + exps[1] + exps[2] + exps[3]
    gates_ref[...] = jnp.concatenate([ex / den for ex in exps], axis=0)
    eidx_ref[...] = jnp.concatenate(idxs, axis=0)
    onehots = [(eiota == idx).astype(F32) for idx in idxs]
    chosen = onehots[0] + onehots[1] + onehots[2] + onehots[3]
    before = _dot(chosen.astype(BF16), tri_ref[...])
    rank_ref[...] = jnp.concatenate(
        [jnp.sum(oh * before, axis=0, keepdims=True) for oh in onehots], axis=0).astype(I32)
    cnt_ref[...] = jnp.broadcast_to(jnp.sum(chosen, axis=1, keepdims=True), (N_EXPERTS, LANES)).astype(I32)

    @pl.when(step == pl.num_programs(0) - 1)
    def _():
        for c in h2_copies(step, slot):
            c.wait()


def _post_mix(x2, seq, o, lse, zu, zv, p):
    T, D = x2.shape
    tm = TOKEN_TILE
    n_tiles = T // tm
    n_seq_tiles = seq // tm
    chunks = D // (2 * LANES)
    const = lambda shape: pl.BlockSpec(shape, lambda i: (0,) * len(shape))
    row = lambda w: pl.BlockSpec((tm, w), lambda i: (i, 0))
    colt = pl.BlockSpec((TOP_K, tm), lambda i: (0, i))
    phase = [_phase_spec(tm, dil, n_seq_tiles) for _, dil in ATT_WINDOWS]
    lse_phase = [_phase_spec(tm, dil, n_seq_tiles, LANES) for _, dil in ATT_WINDOWS]
    return pl.pallas_call(
        _post_mix_kernel,
        grid=(n_tiles,),
        in_specs=[row(D)] + phase + lse_phase + [row(SGU_WIDTH), row(SGU_WIDTH),
                  const((SGU_GROUPS * SGU_CHUNK, SGU_CHUNK)), const((SGU_CHUNK, SGU_WIDTH)),
                  const((1, GROUP_WIDTH)), const((1, SGU_WIDTH)), const((GROUP_WIDTH + SGU_WIDTH, D)),
                  const((1, D)), const((N_EXPERTS, D)), const((N_EXPERTS, 1)), const((tm, tm))],
        out_specs=[row(D), pl.BlockSpec(memory_space=pl.ANY),
                   colt, colt, colt, pl.BlockSpec((None, N_EXPERTS, LANES), lambda i: (i, 0, 0))],
        out_shape=[jax.ShapeDtypeStruct((T, D), F32),
                   jax.ShapeDtypeStruct((T // SUBLANES, SUBLANES, chunks, LANES), F32),
                   jax.ShapeDtypeStruct((TOP_K, T), F32), jax.ShapeDtypeStruct((TOP_K, T), I32),
                   jax.ShapeDtypeStruct((TOP_K, T), I32), jax.ShapeDtypeStruct((n_tiles, N_EXPERTS, LANES), I32)],
        scratch_shapes=[pltpu.VMEM((2 * N_GROUPS * GROUP_WIDTH // LANES, tm, LANES), F32),
                        pltpu.VMEM((2, tm // SUBLANES, chunks, SUBLANES, LANES), F32), pltpu.SemaphoreType.DMA((2,))],
        compiler_params=pltpu.CompilerParams(dimension_semantics=("arbitrary",), vmem_limit_bytes=VMEM_LIMIT),
        name="post_mix",
    )(x2, *o, *lse, zu, zv, p["sgu_w"], p["sgu_b"], p["att_out_norm_w"],
      p["sgu_out_norm_w"], p["w_out"], p["ffn_norm_w"], p["router_wt"], p["router_b"], p["tri"])


def _sc_workers():
    info = plsc.get_sparse_core_info()
    return info.num_cores, info.num_cores * info.num_subcores


def _dispatch_sc(h2p, pos, pad_pos, n_rows):
    T, chunks, _ = h2p.shape
    n_cores, n_workers = _sc_workers()
    win = SC_WINDOW
    per_worker = T // n_workers
    pad_per_worker = pad_pos.shape[0] // n_workers
    assert per_worker % win == 0 and pad_per_worker % win == 0
    mesh = plsc.VectorSubcoreMesh(core_axis_name="c", subcore_axis_name="s")

    @functools.partial(
        pl.kernel, mesh=mesh, out_type=jax.ShapeDtypeStruct((n_rows, chunks, LANES), F32),
        scratch_types=[pltpu.VMEM((win,), I32), pltpu.VMEM((win, chunks, LANES), F32)],
        name="dispatch_sc")
    def scatter_rows(h_hbm, pos_hbm, pad_hbm, zeros_hbm, xs_hbm, idx_v, rows_v):
        worker = lax.axis_index("s") * n_cores + lax.axis_index("c")

        @pl.loop(0, per_worker // win)
        def _(w):
            t0 = worker * per_worker + w * win
            pltpu.sync_copy(h_hbm.at[pl.ds(t0, win)], rows_v)
            for k in range(TOP_K):
                pltpu.sync_copy(pos_hbm.at[pl.ds(k * T + t0, win)], idx_v)
                pltpu.sync_copy(rows_v, xs_hbm.at[idx_v])

        pltpu.sync_copy(zeros_hbm, rows_v)

        @pl.loop(0, pad_per_worker // win)
        def _(w):
            pltpu.sync_copy(pad_hbm.at[pl.ds(worker * pad_per_worker + w * win, win)], idx_v)
            pltpu.sync_copy(rows_v, xs_hbm.at[idx_v])

    return scatter_rows(h2p, pos, pad_pos, jnp.zeros((win, chunks, LANES), F32))


def _gather_sc(ys, pos):
    _, chunks, _ = ys.shape
    n_assign = pos.shape[0]
    n_cores, n_workers = _sc_workers()
    win = SC_WINDOW
    per_worker = n_assign // n_workers
    assert per_worker % win == 0
    mesh = plsc.VectorSubcoreMesh(core_axis_name="c", subcore_axis_name="s")

    @functools.partial(
        pl.kernel, mesh=mesh, out_type=jax.ShapeDtypeStruct((n_assign, chunks, LANES), F32),
        scratch_types=[pltpu.VMEM((win,), I32), pltpu.VMEM((win, chunks, LANES), F32)],
        name="gather_sc")
    def gather_rows(ys_hbm, pos_hbm, out_hbm, idx_v, rows_v):
        worker = lax.axis_index("s") * n_cores + lax.axis_index("c")

        @pl.loop(0, per_worker // win)
        def _(w):
            a0 = worker * per_worker + w * win
            pltpu.sync_copy(pos_hbm.at[pl.ds(a0, win)], idx_v)
            pltpu.sync_copy(ys_hbm.at[idx_v], rows_v)
            pltpu.sync_copy(rows_v, out_hbm.at[pl.ds(a0, win)])

    return gather_rows(ys, pos)


def _experts_kernel(blk_e_ref, n_used_ref, xs_hbm, wg_ref, bg_ref, wu_ref, bu_ref, wd_ref, bd_ref, ys_hbm,
                    xbuf, obuf, wg_b, wu_b, wd_b, in_sem, out_sem):
    bt = xbuf.shape[1]
    i = pl.program_id(0)
    n = pl.num_programs(0)
    n_used = n_used_ref[0]
    slot = i % 2

    def in_copies(blk, s):
        return [pltpu.make_async_copy(xs_hbm.at[pl.ds(blk * bt, bt), j], xbuf.at[s, :, :, j, :], in_sem.at[s])
                for j in range(SUBLANES)]

    def out_copies(blk, s):
        return [pltpu.make_async_copy(obuf.at[s, :, :, j, :], ys_hbm.at[pl.ds(blk * bt, bt), j], out_sem.at[s])
                for j in range(SUBLANES)]

    @pl.when(i == 0)
    def _():
        for c in in_copies(0, 0):
            c.start()

    @pl.when(i + 1 < n_used)
    def _():
        for c in in_copies(i + 1, 1 - slot):
            c.start()

    @pl.when(i >= 2)
    def _():
        for c in out_copies(i - 2, slot):
            c.wait()

    @pl.when(i >= n_used)
    def _():
        obuf[slot] = jnp.zeros(obuf.shape[1:], F32)

    @pl.when(i < n_used)
    def _():
        for c in in_copies(i, slot):
            c.wait()
        @pl.when((i == 0) | (blk_e_ref[i] != blk_e_ref[jnp.maximum(i - 1, 0)]))
        def _():
            wg_b[...] = wg_ref[...].astype(BF16)
            wu_b[...] = wu_ref[...].astype(BF16)
            wd_b[...] = wd_ref[...].astype(BF16)

        x = _unpack_pairs(_from_tiled(xbuf, (slot,), slice(0, bt))).astype(BF16)
        g = _dot(x, wg_b[...]) + bg_ref[...]
        u = _dot(x, wu_b[...]) + bu_ref[...]
        g = jnp.minimum(g, SWIGLU_LIMIT)
        u = jnp.clip(u, -SWIGLU_LIMIT, SWIGLU_LIMIT)
        act = (u + 1.0) * (g * jax.nn.sigmoid(SWIGLU_ALPHA * g))
        _to_tiled(obuf, (slot,), _pack_pairs(_dot(act.astype(BF16), wd_b[...]) + bd_ref[...]))

    for c in out_copies(i, slot):
        c.start()

    @pl.when(i == n - 1)
    def _():
        for c in out_copies(i - 1, 1 - slot) + out_copies(i, slot):
            c.wait()


def _experts(xs, blk_e, n_used, p):
    n_rows, chunks, _ = xs.shape
    D = 2 * chunks * LANES
    bm = MOE_BLOCK
    d_ff = p["w_gate"].shape[2]
    assert n_rows // bm >= 2
    wspec = lambda a, b: pl.BlockSpec((None, a, b), lambda i, be, nu: (be[i], 0, 0))
    by_tile = (n_rows // SUBLANES, SUBLANES, chunks, LANES)
    block_buf = pltpu.VMEM((2, bm // SUBLANES, chunks, SUBLANES, LANES), F32)
    ys = pl.pallas_call(
        _experts_kernel,
        grid_spec=pltpu.PrefetchScalarGridSpec(
            num_scalar_prefetch=2,
            grid=(n_rows // bm,),
            in_specs=[pl.BlockSpec(memory_space=pl.ANY), wspec(D, d_ff), wspec(1, d_ff), wspec(D, d_ff),
                      wspec(1, d_ff), wspec(d_ff, D), wspec(1, D)],
            out_specs=pl.BlockSpec(memory_space=pl.ANY),
            scratch_shapes=[block_buf, block_buf, pltpu.VMEM((D, d_ff), BF16), pltpu.VMEM((D, d_ff), BF16),
                            pltpu.VMEM((d_ff, D), BF16), pltpu.SemaphoreType.DMA((2,)), pltpu.SemaphoreType.DMA((2,))],
        ),
        out_shape=jax.ShapeDtypeStruct(by_tile, F32),
        compiler_params=pltpu.CompilerParams(dimension_semantics=("arbitrary",), vmem_limit_bytes=EXPERTS_VMEM_LIMIT),
        name="experts",
    )(blk_e, n_used, xs.reshape(by_tile), p["w_gate"], p["b_gate"], p["w_up"], p["b_up"], p["w_down"], p["b_down"])
    return ys.reshape(n_rows, chunks, LANES)


def _combine_kernel(gates_ref, x1_ref, yg_hbm, out_ref, buf, sem):
    tm = x1_ref.shape[0]
    bt = tm // SUBLANES
    j = pl.program_id(0)
    n = pl.num_programs(0) - 1
    tiles_per_k = yg_hbm.shape[0] // TOP_K

    def copies(tile, s):
        return [pltpu.make_async_copy(yg_hbm.at[pl.ds(k * tiles_per_k + tile * bt, bt), r],
                                      buf.at[s, pl.ds(k * bt, bt), :, r, :], sem.at[s])
                for k in range(TOP_K) for r in range(SUBLANES)]

    @pl.when(j < n)
    def _():
        for c in copies(j, j % 2):
            c.start()

    @pl.when(j >= 1)
    def _():
        done = 1 - j % 2
        for c in copies(j - 1, done):
            c.wait()
        y = gates_ref[:, 0:1] * _unpack_pairs(_from_tiled(buf, (done,), slice(0, bt)))
        for k in range(1, TOP_K):
            y = y + gates_ref[:, k:k + 1] * _unpack_pairs(_from_tiled(buf, (done,), slice(k * bt, (k + 1) * bt)))
        out_ref[...] = x1_ref[...] + y


def _combine(x1, gates_tok, yg):
    T, D = x1.shape
    tm = ROW_TILE
    n = T // tm
    chunks = D // (2 * LANES)
    lag = lambda i: (jnp.maximum(i - 1, 0), 0)
    return pl.pallas_call(
        _combine_kernel,
        grid=(n + 1,),
        in_specs=[pl.BlockSpec((tm, TOP_K), lag), pl.BlockSpec((tm, D), lag), pl.BlockSpec(memory_space=pl.ANY)],
        out_specs=pl.BlockSpec((tm, D), lag),
        out_shape=jax.ShapeDtypeStruct((T, D), F32),
        scratch_shapes=[pltpu.VMEM((2, TOP_K * tm // SUBLANES, chunks, SUBLANES, LANES), F32),
                        pltpu.SemaphoreType.DMA((2,))],
        compiler_params=pltpu.CompilerParams(dimension_semantics=("arbitrary",), vmem_limit_bytes=VMEM_LIMIT),
        name="combine",
    )(gates_tok, x1, yg.reshape(TOP_K * T // SUBLANES, SUBLANES, chunks, LANES))


def _route(eidx, rank, tile_counts):
    T = eidx.shape[1]
    bm = MOE_BLOCK
    n_tiles = T // TOKEN_TILE
    tc = tile_counts[:, :, 0]
    counts = jnp.sum(tc, axis=0)
    tile_off = jnp.cumsum(tc, axis=0) - tc
    pcounts = (counts + bm - 1) // bm * bm
    pends = jnp.cumsum(pcounts)
    pstarts = pends - pcounts
    base = pstarts[None, :] + tile_off
    sel = eidx.reshape(TOP_K, n_tiles, 1, TOKEN_TILE) == jnp.arange(N_EXPERTS, dtype=I32)[None, None, :, None]
    pos = jnp.sum(jnp.where(sel, base[None, :, :, None], 0), axis=2).reshape(TOP_K, T) + rank
    n_blocks = (TOP_K * T) // bm + N_EXPERTS
    n_rows = n_blocks * bm
    blk_start = jnp.arange(n_blocks, dtype=I32) * bm
    blk_e = jnp.minimum(jnp.sum((pends[None, :] <= blk_start[:, None]).astype(I32), axis=1), N_EXPERTS - 1)
    n_used = (pends[-1] // bm).astype(I32).reshape(1)
    first_free = jnp.concatenate([pstarts + counts, pends[-1:]])
    free_off = jnp.cumsum(jnp.concatenate([jnp.zeros((1,), counts.dtype), pcounts - counts]))
    j = jnp.arange(n_rows - TOP_K * T, dtype=I32)[:, None]
    seg_id = jnp.sum((free_off[None, 1:] <= j).astype(I32), axis=1, keepdims=True)
    onehot = seg_id == jnp.arange(N_EXPERTS + 1, dtype=I32)[None, :]
    pad_pos = jnp.sum(jnp.where(onehot, (first_free - free_off)[None, :] + j, 0), axis=1).astype(I32)
    return pos.reshape(-1).astype(I32), pad_pos, blk_e.astype(I32), n_used, n_rows


def _layer(x, p):
    batch, seq, D = x.shape
    x2 = x.reshape(batch * seq, D)
    q, k, v, zu, zv = _in_proj(x2, batch, seq, p)
    o, lse = zip(*[_attention(q[g], k[g], v[g], g) for g in range(N_GROUPS)])
    x1, h2p, gates, eidx, rank, tile_counts = _post_mix(x2, seq, o, lse, zu, zv, p)
    pos, pad_pos, blk_e, n_used, n_rows = _route(eidx, rank, tile_counts)
    xs = _dispatch_sc(h2p.reshape(batch * seq, D // (2 * LANES), LANES), pos, pad_pos, n_rows)
    ys = _experts(xs, blk_e, n_used, p)
    out = _combine(x1, gates.T, _gather_sc(ys, pos))
    return out.reshape(batch, seq, D)


def _rope_tables(seq):
    half = ROT_DIM // 2
    inv_freq = jnp.power(ROPE_THETA, -2.0 * jnp.arange(half, dtype=F32) / ROT_DIM)
    ang = jnp.arange(seq, dtype=F32)[:, None] * inv_freq[None, :]
    cos, sin = jnp.cos(ang), jnp.sin(ang)
    zeros = jnp.zeros((seq, HEAD_DIM - ROT_DIM), F32)
    zh = jnp.zeros((seq, half), F32)
    cos_t = jnp.concatenate([cos, cos, zeros + 1.0], axis=1)
    sa = jnp.concatenate([-sin, zh, zeros], axis=1)
    sb = jnp.concatenate([zh, sin, zeros], axis=1)
    rep = LANES // HEAD_DIM
    return tuple(jnp.tile(t, (1, rep)) for t in (cos_t, sa, sb))


def kernel(x_prompt, x_sample, attn_norm_w, w_in, q_norm_w, k_norm_w, sgu_ln_w, sgu_ln_b, sgu_w, sgu_b,
           att_out_norm_w, sgu_out_norm_w, w_out, ffn_norm_w, router_w, router_b,
           w_gate, b_gate, w_up, b_up, w_down, b_down):
    depth = w_in.shape[0]
    n_heads = ATT_WIDTH // HEAD_DIM
    blk = np.arange(MXU_DIM) // HEAD_DIM
    tri = np.arange(TOKEN_TILE)
    xa, xb = x_prompt, x_sample
    rope = {x.shape[1]: _rope_tables(x.shape[1]) for x in (xa, xb)}
    for l in range(depth):
        shared = dict(
            attn_norm_w=attn_norm_w[l][None], w_in=w_in[l].astype(BF16),
            q_norm_w=jnp.tile(q_norm_w[l], n_heads)[None], k_norm_w=jnp.tile(k_norm_w[l], n_heads)[None],
            head_ones=jnp.asarray(blk[:, None] == blk[None, :], BF16),
            sgu_ln_w=sgu_ln_w[l][None], sgu_ln_b=sgu_ln_b[l][None],
            sgu_w=sgu_w[l].reshape(SGU_GROUPS * SGU_CHUNK, SGU_CHUNK).astype(BF16),
            sgu_b=jnp.repeat(sgu_b[l].T, SGU_GROUP_DIM, axis=1),
            att_out_norm_w=att_out_norm_w[l][None], sgu_out_norm_w=sgu_out_norm_w[l][None],
            w_out=w_out[l].astype(BF16), ffn_norm_w=ffn_norm_w[l][None],
            router_wt=router_w[l].T.astype(BF16), router_b=router_b[l][:, None],
            tri=jnp.asarray(tri[:, None] < tri[None, :], BF16),
            w_gate=w_gate[l], b_gate=b_gate[l][:, None, :],
            w_up=w_up[l], b_up=b_up[l][:, None, :],
            w_down=w_down[l], b_down=b_down[l][:, None, :],
        )
        xa, xb = (_layer(x, dict(shared, rope=rope)) for x in (xa, xb))
    return xa, xb
```
